```python
import math
import jax, jax.numpy as jnp
from jax import lax
import numpy as np

D_MODEL = 1024
BATCH = 4
SEQ = 4096
DEPTH = 1

POOL_WINDOWS = (2, 4, 8, 16)
POOL_GROUPS = 4
POOL_GROUP_DIM = D_MODEL // 8
POOL_DIM = POOL_GROUPS * POOL_GROUP_DIM
N_HEADS = 8
HEAD_DIM = 64
ATTN_DIM = N_HEADS * HEAD_DIM
IDX_HEADS = 8
IDX_DIM = 64
TOP_K = 256
Q_BLOCK = 128
REL_BUCKETS = 32
REL_MAX_DIST = 128
D_FF = 2816
N_ADA = 9
LN_EPS = 1e-5
DEEPNORM_ALPHA = (2.0 * DEPTH) ** 0.25
DEEPNORM_BETA = (8.0 * DEPTH) ** -0.25
IN_SPLITS = (POOL_DIM, ATTN_DIM, ATTN_DIM, ATTN_DIM, IDX_HEADS * IDX_DIM, IDX_DIM, IDX_HEADS, D_MODEL, D_MODEL)
IN_COLS = sum(IN_SPLITS)
V_OFFSET = POOL_DIM + 2 * ATTN_DIM

kernel_name = "hybrid_pool_dsa_macaron_block"


def layer_norm(h, g, b):
    h32 = h.astype(jnp.float32)
    mu = jnp.mean(h32, axis=-1, keepdims=True)
    var = jnp.mean(jnp.square(h32 - mu), axis=-1, keepdims=True)
    return ((h32 - mu) * lax.rsqrt(var + LN_EPS) * g.astype(jnp.float32) + b.astype(jnp.float32)).astype(h.dtype)


def swiglu(u, w_gate, w_up, w_down):
    return (jax.nn.silu(u @ w_gate) * (u @ w_up)) @ w_down


def t5_bucket(n):
    max_exact = REL_BUCKETS // 2
    nf = jnp.maximum(n, 1).astype(jnp.float32)
    large = max_exact + (jnp.log(nf / max_exact) / math.log(REL_MAX_DIST / max_exact)
                         * (REL_BUCKETS - max_exact)).astype(jnp.int32)
    large = jnp.minimum(large, REL_BUCKETS - 1)
    return jnp.where(n < max_exact, n, large)


def pool_mixer(p, w_pool, pool_scale):
    B, S, _ = p.shape
    p32 = p.astype(jnp.float32)
    prefix = jnp.pad(jnp.cumsum(p32, axis=1), ((0, 0), (1, 0), (0, 0)))
    t = jnp.arange(S)
    outs = []
    for g, w in enumerate(POOL_WINDOWS):
        sl = slice(g * POOL_GROUP_DIM, (g + 1) * POOL_GROUP_DIM)
        pre = prefix[:, :, sl]
        hi = pre[:, 1:]
        lo = jnp.pad(pre[:, :S + 1 - w], ((0, 0), (w - 1, 0), (0, 0)))
        cnt = jnp.minimum(t + 1, w).astype(jnp.float32)[None, :, None]
        outs.append((hi - lo) / cnt - p32[:, :, sl])
    pooled = jnp.stack(outs, axis=2).astype(p.dtype)
    mixed = jnp.einsum('bsgc,gcd->bsgd', pooled, w_pool).reshape(B, S, POOL_DIM)
    return mixed * pool_scale


def dsa_attention(q, k, v, qi, ki, wi, rel_bias):
    B, S = q.shape[0], q.shape[1]
    topk = min(TOP_K, S // 4)
    n_blocks = S // Q_BLOCK
    key_pos = jnp.arange(S)

    def block(i):
        start = i * Q_BLOCK
        qb = lax.dynamic_slice_in_dim(q, start, Q_BLOCK, axis=1)
        qib = lax.dynamic_slice_in_dim(qi, start, Q_BLOCK, axis=1)
        wib = lax.dynamic_slice_in_dim(wi, start, Q_BLOCK, axis=1)
        t_pos = start + jnp.arange(Q_BLOCK)
        causal = key_pos[None, :] <= t_pos[:, None]
        rel = jax.nn.relu(jnp.einsum('bqhd,bsd->bqhs', qib, ki).astype(jnp.float32) * (IDX_DIM ** -0.5))
        score = jnp.einsum('bqhs,bqh->bqs', rel, wib.astype(jnp.float32)) * (IDX_HEADS ** -0.5)
        score = jnp.where(causal[None], score, -jnp.inf)
        _, idx = lax.top_k(score, topk)
        k_sel = jax.vmap(lambda kb, ib: kb[ib])(k, idx)
        v_sel = jax.vmap(lambda vb, ib: vb[ib])(v, idx)
        dist = t_pos[None, :, None] - idx
        valid = dist >= 0
        bias = rel_bias[t5_bucket(jnp.maximum(dist, 0))]
        logits = (jnp.einsum('bqhd,bqkhd->bqhk', qb, k_sel).astype(jnp.float32) * (HEAD_DIM ** -0.5)
                  + jnp.transpose(bias, (0, 1, 3, 2)).astype(jnp.float32))
        logits = jnp.where(valid[:, :, None, :], logits, -jnp.inf)
        probs = jax.nn.softmax(logits, axis=-1).astype(v.dtype)
        return jnp.einsum('bqhk,bqkhd->bqhd', probs, v_sel)

    out = lax.map(block, jnp.arange(n_blocks))
    return jnp.transpose(out, (1, 0, 2, 3, 4)).reshape(B, S, N_HEADS * HEAD_DIM)


def token_mixer(u, w_in, w_pool, pool_scale, w_a, w_b, w_out, rel_bias):
    B, S, _ = u.shape
    proj = u @ w_in
    p, q, k, v, qi, ki, wi, ga, gb = jnp.split(proj, np.cumsum(IN_SPLITS)[:-1].tolist(), axis=-1)
    y_a = pool_mixer(p, w_pool, pool_scale) @ w_a
    hs = (B, S, N_HEADS, HEAD_DIM)
    y_b = dsa_attention(q.reshape(hs), k.reshape(hs), v.reshape(hs),
                        qi.reshape(B, S, IDX_HEADS, IDX_DIM), ki, wi, rel_bias) @ w_b
    merged = jax.nn.sigmoid(ga) * y_a + jax.nn.sigmoid(gb) * y_b
    return merged @ w_out


def setup_inputs(seed: int = 0) -> dict:
    key = jax.random.key(seed)
    ks = jax.random.split(key, 22)
    L = DEPTH

    def nrm(k, shape, scale):
        return jax.random.normal(k, shape, jnp.float32) * scale

    w_in = nrm(ks[10], (L, D_MODEL, IN_COLS), D_MODEL ** -0.5)
    w_in = w_in.at[:, :, V_OFFSET:V_OFFSET + ATTN_DIM].multiply(DEEPNORM_BETA)
    return {
        "x": nrm(ks[0], (BATCH, SEQ, D_MODEL), 1.0),
        "c": nrm(ks[1], (BATCH, D_MODEL), 1.0),
        "w_ada": nrm(ks[2], (L, D_MODEL, N_ADA * D_MODEL), 0.5 * D_MODEL ** -0.5),
        "b_ada": nrm(ks[3], (L, N_ADA * D_MODEL), 0.02),
        "ln_g": 1.0 + nrm(ks[4], (L, 3, D_MODEL), 0.02),
        "ln_b": nrm(ks[5], (L, 3, D_MODEL), 0.02),
        "ffn1_w_gate": nrm(ks[6], (L, D_MODEL, D_FF), D_MODEL ** -0.5),
        "ffn1_w_up": nrm(ks[7], (L, D_MODEL, D_FF), D_MODEL ** -0.5),
        "ffn1_w_down": nrm(ks[8], (L, D_FF, D_MODEL), DEEPNORM_BETA * D_FF ** -0.5),
        "w_in": w_in,
        "w_pool": nrm(ks[11], (L, POOL_GROUPS, POOL_GROUP_DIM, POOL_GROUP_DIM), POOL_GROUP_DIM ** -0.5),
        "pool_scale": 1.0 + nrm(ks[12], (L, POOL_DIM), 0.02),
        "w_a": nrm(ks[13], (L, POOL_DIM, D_MODEL), POOL_DIM ** -0.5),
        "w_b": nrm(ks[14], (L, ATTN_DIM, D_MODEL), ATTN_DIM ** -0.5),
        "w_out": nrm(ks[15], (L, D_MODEL, D_MODEL), DEEPNORM_BETA * D_MODEL ** -0.5),
        "rel_bias": nrm(ks[16], (REL_BUCKETS, N_HEADS), 0.5),
        "ffn2_w_gate": nrm(ks[17], (L, D_MODEL, D_FF), D_MODEL ** -0.5),
        "ffn2_w_up": nrm(ks[18], (L, D_MODEL, D_FF), D_MODEL ** -0.5),
        "ffn2_w_down": nrm(ks[19], (L, D_FF, D_MODEL), DEEPNORM_BETA * D_FF ** -0.5),
    }


def reference(x, c, w_ada, b_ada, ln_g, ln_b, ffn1_w_gate, ffn1_w_up, ffn1_w_down,
              w_in, w_pool, pool_scale, w_a, w_b, w_out, rel_bias,
              ffn2_w_gate, ffn2_w_up, ffn2_w_down):
    B = x.shape[0]
    for l in range(DEPTH):
        mod = (jax.nn.silu(c) @ w_ada[l] + b_ada[l]).reshape(B, N_ADA, 1, D_MODEL)
        sh1, sc1, g1, sh2, sc2, g2, sh3, sc3, g3 = [mod[:, j] for j in range(N_ADA)]
        u = x * (1.0 + sc1) + sh1
        x = layer_norm(DEEPNORM_ALPHA * x + 0.5 * g1 * swiglu(u, ffn1_w_gate[l], ffn1_w_up[l], ffn1_w_down[l]),
                       ln_g[l, 0], ln_b[l, 0])
        u = x * (1.0 + sc2) + sh2
        y = token_mixer(u, w_in[l], w_pool[l], pool_scale[l], w_a[l], w_b[l], w_out[l], rel_bias)
        x = layer_norm(DEEPNORM_ALPHA * x + g2 * y, ln_g[l, 1], ln_b[l, 1])
        u = x * (1.0 + sc3) + sh3
        x = layer_norm(DEEPNORM_ALPHA * x + 0.5 * g3 * swiglu(u, ffn2_w_gate[l], ffn2_w_up[l], ffn2_w_down[l]),
                       ln_g[l, 2], ln_b[l, 2])
    return x
```

```python
import functools
import math

import numpy as np
import jax
import jax.numpy as jnp
from jax import lax
from jax.experimental import pallas as pl
from jax.experimental.pallas import tpu as pltpu

POOL_WINDOWS = (2, 4, 8, 16)
POOL_GROUP_DIM = 128
POOL_DIM = len(POOL_WINDOWS) * POOL_GROUP_DIM
N_HEADS = 8
HEAD_DIM = 64
ATTN_DIM = N_HEADS * HEAD_DIM
IDX_HEADS = 8
IDX_DIM = 64
TOP_K = 256
REL_BUCKETS = 32
REL_MAX_DIST = 128
N_ADA = 9
LN_EPS = 1e-5
POOL_HALO = 16

LANES = 128
SUBLANES = 8
V7X_VMEM_LIMIT_BYTES = 56 * 1024 * 1024
FFN_ROWS = 512
FFN_COLS = 256
PROJ_ROWS = 512
MIX_ROWS = 256
ATT_BLOCK = 256
ADA_COLS = 1024

LOG2E = math.log2(math.e)
BF16 = jnp.bfloat16
F32 = jnp.float32
NT_DIMS = (((1,), (1,)), ((), ()))


def _layer_norm(z, g, b):
    mu = jnp.mean(z, axis=-1, keepdims=True)
    zc = z - mu
    var = jnp.mean(zc * zc, axis=-1, keepdims=True)
    return zc * lax.rsqrt(var + LN_EPS) * g + b


def _silu(a):
    return a * jax.nn.sigmoid(a)


def _ada_kernel(c_ref, w_ref, b_ref, o_ref):
    a = _silu(c_ref[...])
    o_ref[...] = jnp.dot(a, w_ref[...], preferred_element_type=F32) + b_ref[...]


def _ada_mod(c, w_ada, b_ada):
    bsz, d = c.shape
    n = w_ada.shape[1]
    rows = -(-bsz // SUBLANES) * SUBLANES
    c_pad = jnp.pad(c, ((0, rows - bsz), (0, 0)))
    out = pl.pallas_call(
        _ada_kernel,
        grid=(n // ADA_COLS,),
        in_specs=[
            pl.BlockSpec((rows, d), lambda j: (0, 0)),
            pl.BlockSpec((d, ADA_COLS), lambda j: (0, j)),
            pl.BlockSpec((1, ADA_COLS), lambda j: (0, j)),
        ],
        out_specs=pl.BlockSpec((rows, ADA_COLS), lambda j: (0, j)),
        out_shape=jax.ShapeDtypeStruct((rows, n), F32),
        compiler_params=pltpu.CompilerParams(dimension_semantics=("arbitrary",),
                                             vmem_limit_bytes=V7X_VMEM_LIMIT_BYTES),
        name="ada_mod",
    )(c_pad, w_ada, b_ada.reshape(1, n))
    return out[:bsz]


def _ffn_kernel(x_ref, mod_ref, lng_ref, lnb_ref, wg_ref, wu_ref, wd_ref, o_ref, h_ref, *, alpha, mod_row, ln_row):
    x = x_ref[...]
    sh = mod_ref[0, mod_row:mod_row + 1, :]
    sc = mod_ref[0, mod_row + 1:mod_row + 2, :]
    gate = mod_ref[0, mod_row + 2:mod_row + 3, :]
    u = (x * (1.0 + sc) + sh).astype(BF16)
    d_ff = wg_ref.shape[1]
    for c in range(d_ff // FFN_COLS):
        sl = slice(c * FFN_COLS, (c + 1) * FFN_COLS)
        a = jnp.dot(u, wg_ref[:, sl], preferred_element_type=F32)
        b = jnp.dot(u, wu_ref[:, sl], preferred_element_type=F32)
        h_ref[:, sl] = (_silu(a) * b).astype(BF16)
    y = jnp.dot(h_ref[...], wd_ref[...], preferred_element_type=F32)
    z = alpha * x + (0.5 * gate) * y
    o_ref[...] = _layer_norm(z, lng_ref[ln_row:ln_row + 1, :], lnb_ref[ln_row:ln_row + 1, :])


def _ffn(x2d, mod, ln_g, ln_b, wg, wu, wd, *, seq, alpha, mod_row, ln_row):
    n, d = x2d.shape
    d_ff = wg.shape[1]
    tiles_per_seq = seq // FFN_ROWS
    resident = dict(pipeline_mode=pl.Buffered(1))
    return pl.pallas_call(
        functools.partial(_ffn_kernel, alpha=alpha, mod_row=mod_row, ln_row=ln_row),
        grid=(n // FFN_ROWS,),
        in_specs=[
            pl.BlockSpec((FFN_ROWS, d), lambda i: (i, 0)),
            pl.BlockSpec((1, N_ADA, d), lambda i: (i // tiles_per_seq, 0, 0)),
            pl.BlockSpec(ln_g.shape, lambda i: (0, 0)),
            pl.BlockSpec(ln_b.shape, lambda i: (0, 0)),
            pl.BlockSpec((d, d_ff), lambda i: (0, 0), **resident),
            pl.BlockSpec((d, d_ff), lambda i: (0, 0), **resident),
            pl.BlockSpec((d_ff, d), lambda i: (0, 0), **resident),
        ],
        out_specs=pl.BlockSpec((FFN_ROWS, d), lambda i: (i, 0)),
        out_shape=jax.ShapeDtypeStruct((n, d), F32),
        scratch_shapes=[pltpu.VMEM((FFN_ROWS, d_ff), BF16)],
        compiler_params=pltpu.CompilerParams(dimension_semantics=("arbitrary",),
                                             vmem_limit_bytes=V7X_VMEM_LIMIT_BYTES),
        name="ffn",
    )(x2d, mod, ln_g, ln_b, wg, wu, wd)


def _proj_kernel(x_ref, mod_ref, wqk_ref, wv_ref, wki_ref, wwi_ref,
                 q_ref, k_ref, qi_ref, vt_ref, kia_ref, kib_ref, wit_ref, *, mod_row):
    x = x_ref[...]
    sh = mod_ref[0, mod_row:mod_row + 1, :]
    sc = mod_ref[0, mod_row + 1:mod_row + 2, :]
    u = (x * (1.0 + sc) + sh).astype(BF16)
    qkq = jnp.dot(u, wqk_ref[...], preferred_element_type=F32)
    q_ref[...] = (qkq[:, :ATTN_DIM] * (HEAD_DIM ** -0.5 * LOG2E)).astype(BF16)
    k_ref[...] = qkq[:, ATTN_DIM:2 * ATTN_DIM].astype(BF16)
    qi_ref[...] = qkq[:, 2 * ATTN_DIM:].astype(BF16)
    kk = jnp.dot(u, wki_ref[...], preferred_element_type=F32)
    kia_ref[...] = kk[:, :LANES].astype(BF16)
    kib_ref[...] = kk[:, LANES:].astype(BF16)
    vt = lax.dot_general(wv_ref[...], u, NT_DIMS, preferred_element_type=F32).astype(BF16)
    for c in range(vt_ref.shape[1]):
        vt_ref[0, c] = vt[:, c * ATT_BLOCK:(c + 1) * ATT_BLOCK]
    wit = lax.dot_general(wwi_ref[...], u, NT_DIMS, preferred_element_type=F32)
    wit_ref[0] = wit[:IDX_HEADS, :]


def _attn_proj(x2d, mod, wqk, wv_t, wki, wwi_t, *, bsz, seq, mod_row):
    n, d = x2d.shape
    tiles_per_seq = seq // PROJ_ROWS
    chunks_per_tile = PROJ_ROWS // ATT_BLOCK
    resident = dict(pipeline_mode=pl.Buffered(1))
    row_spec = lambda cols: pl.BlockSpec((PROJ_ROWS, cols), lambda i: (i, 0))
    return pl.pallas_call(
        functools.partial(_proj_kernel, mod_row=mod_row),
        grid=(n // PROJ_ROWS,),
        in_specs=[
            pl.BlockSpec((PROJ_ROWS, d), lambda i: (i, 0)),
            pl.BlockSpec((1, N_ADA, d), lambda i: (i // tiles_per_seq, 0, 0)),
            pl.BlockSpec(wqk.shape, lambda i: (0, 0), **resident),
            pl.BlockSpec(wv_t.shape, lambda i: (0, 0), **resident),
            pl.BlockSpec(wki.shape, lambda i: (0, 0), **resident),
            pl.BlockSpec(wwi_t.shape, lambda i: (0, 0), **resident),
        ],
        out_specs=[
            row_spec(ATTN_DIM), row_spec(ATTN_DIM), row_spec(IDX_HEADS * IDX_DIM),
            pl.BlockSpec((1, chunks_per_tile, ATTN_DIM, ATT_BLOCK),
                         lambda i: (i // tiles_per_seq, i % tiles_per_seq, 0, 0)),
            row_spec(LANES), row_spec(LANES),
            pl.BlockSpec((1, IDX_HEADS, PROJ_ROWS), lambda i: (i // tiles_per_seq, 0, i % tiles_per_seq)),
        ],
        out_shape=[
            jax.ShapeDtypeStruct((n, ATTN_DIM), BF16),
            jax.ShapeDtypeStruct((n, ATTN_DIM), BF16),
            jax.ShapeDtypeStruct((n, IDX_HEADS * IDX_DIM), BF16),
            jax.ShapeDtypeStruct((bsz, seq // ATT_BLOCK, ATTN_DIM, ATT_BLOCK), BF16),
            jax.ShapeDtypeStruct((n, LANES), BF16),
            jax.ShapeDtypeStruct((n, LANES), BF16),
            jax.ShapeDtypeStruct((bsz, IDX_HEADS, seq), F32),
        ],
        compiler_params=pltpu.CompilerParams(dimension_semantics=("arbitrary",),
                                             vmem_limit_bytes=V7X_VMEM_LIMIT_BYTES),
        name="attn_proj",
    )(x2d, mod, wqk, wv_t, wki, wwi_t)


def _t5_bucket(n):
    max_exact = REL_BUCKETS // 2
    nf = jnp.maximum(n, 1).astype(F32)
    large = max_exact + (jnp.log(nf / max_exact) / math.log(REL_MAX_DIST / max_exact)
                         * (REL_BUCKETS - max_exact)).astype(jnp.int32)
    large = jnp.minimum(large, REL_BUCKETS - 1)
    return jnp.where(n < max_exact, n, large)


def _far_bucket(first_dist, last_dist):
    n = np.arange(first_dist, last_dist + 1, dtype=np.float32)
    max_exact = REL_BUCKETS // 2
    large = max_exact + (np.log(n / np.float32(max_exact)) / np.float32(math.log(REL_MAX_DIST / max_exact))
                         * np.float32(REL_BUCKETS - max_exact)).astype(np.int32)
    buckets = np.where(n < max_exact, n.astype(np.int32), np.minimum(large, REL_BUCKETS - 1))
    assert buckets.min() == buckets.max(), "key chunks two or more blocks away must share one bias bucket"
    return int(buckets[0])


def _bias_kernel(rb_ref, o_ref, *, far_bucket):
    o = pl.program_id(0)
    h = pl.program_id(1)
    row = lax.broadcasted_iota(jnp.int32, (ATT_BLOCK, ATT_BLOCK), 0)
    col = lax.broadcasted_iota(jnp.int32, (ATT_BLOCK, ATT_BLOCK), 1)
    dist = o * ATT_BLOCK + col - row
    bucket = _t5_bucket(jnp.maximum(dist, 0))
    tile = jnp.zeros((ATT_BLOCK, ATT_BLOCK), F32)
    for b in range(REL_BUCKETS):
        tile = jnp.where(bucket == b, rb_ref[b, h], tile)
    o_ref[0, 0] = (tile - rb_ref[far_bucket, h]) * LOG2E


def _rel_bias_tiles(rel_bias, far_bucket):
    return pl.pallas_call(
        functools.partial(_bias_kernel, far_bucket=far_bucket),
        grid=(2, N_HEADS),
        in_specs=[pl.BlockSpec(memory_space=pltpu.SMEM)],
        out_specs=pl.BlockSpec((1, 1, ATT_BLOCK, ATT_BLOCK), lambda o, h: (o, h, 0, 0)),
        out_shape=jax.ShapeDtypeStruct((2, N_HEADS, ATT_BLOCK, ATT_BLOCK), F32),
        compiler_params=pltpu.CompilerParams(dimension_semantics=("arbitrary", "arbitrary")),
        name="rel_bias",
    )(rel_bias)


def _attn_kernel(q_ref, qi_ref, wit_ref, k_ref, vt_ref, kia_ref, kib_ref, bt_ref, o_ref,
                 sc_ref, qh_ref, m_ref, l_ref, acc_ref, *, topk, seq):
    blk = ATT_BLOCK
    groups = blk // SUBLANES
    i = pl.program_id(1)
    nch = i + 1
    kf = float(topk)
    inf = jnp.inf

    def chunk_start(j):
        return pl.multiple_of(j * blk, blk)

    def as_groups(x):
        return x.reshape(groups, SUBLANES, blk)

    def lanes8(v):
        return jnp.broadcast_to(v, (SUBLANES, blk))

    def colmin(x8):
        return jnp.min(x8, axis=0, keepdims=True)

    def colmax(x8):
        return jnp.max(x8, axis=0, keepdims=True)

    def colsum(x8):
        return jnp.sum(x8, axis=0, keepdims=True)

    wf = wit_ref[0] * (IDX_DIM ** -0.5)
    qi = qi_ref[0]

    def chunk_scores(j):
        r0 = chunk_start(j)
        ka = kia_ref[0, pl.ds(r0, blk), :]
        kb = kib_ref[0, pl.ds(r0, blk), :]
        s = jnp.zeros((blk, blk), F32)
        for hp in range(IDX_HEADS // 2):
            qp = qi[:, hp * LANES:(hp + 1) * LANES]
            a0 = lax.dot_general(ka, qp, NT_DIMS, preferred_element_type=F32)
            a1 = lax.dot_general(kb, qp, NT_DIMS, preferred_element_type=F32)
            s = s + jnp.maximum(a0, 0.0) * wf[2 * hp:2 * hp + 1, :]
            s = s + jnp.maximum(a1, 0.0) * wf[2 * hp + 1:2 * hp + 2, :]
        return r0, s * (IDX_HEADS ** -0.5)

    def score_body(j, carry):
        mn8, mx8 = carry
        r0, s = chunk_scores(j)
        sc_ref[pl.ds(r0, blk), :] = s
        s3 = as_groups(s)
        return jnp.minimum(mn8, jnp.min(s3, axis=0)), jnp.maximum(mx8, jnp.max(s3, axis=0))

    mn8, mx8 = lax.fori_loop(0, i, score_body,
                             (jnp.full((SUBLANES, blk), inf, F32), jnp.full((SUBLANES, blk), -inf, F32)))
    r_diag, s_diag = chunk_scores(i)
    row = lax.broadcasted_iota(jnp.int32, (blk, blk), 0)
    col = lax.broadcasted_iota(jnp.int32, (blk, blk), 1)
    causal = row <= col
    sc_ref[pl.ds(r_diag, blk), :] = jnp.where(causal, s_diag, -inf)
    mn8 = jnp.minimum(mn8, jnp.min(as_groups(jnp.where(causal, s_diag, inf)), axis=0))
    mx8 = jnp.maximum(mx8, jnp.max(as_groups(jnp.where(causal, s_diag, -inf)), axis=0))
    lo0 = colmin(mn8)
    hi0 = colmax(mx8)

    def count_gt(thr):
        t8 = lanes8(thr)

        def body(j, acc):
            x = as_groups(sc_ref[pl.ds(chunk_start(j), blk), :])
            return acc + jnp.sum(jnp.where(x > t8[None], 1.0, 0.0), axis=0)

        return colsum(lax.fori_loop(0, nch, body, jnp.zeros((SUBLANES, blk), F32)))

    def bisect(lo, hi, low, done):
        mid = 0.5 * lo + 0.5 * hi
        c = count_gt(mid)
        live = done < 0.5
        up = jnp.logical_and(live, c >= kf)
        down = jnp.logical_and(live, c < kf)
        low = jnp.where(up, mid, low)
        lo = jnp.where(up, mid, lo)
        hi = jnp.where(down, mid, hi)
        done = jnp.where(c == kf, 1.0, done)
        return lo, hi, low, done

    tpos = i * blk + lax.broadcasted_iota(jnp.int32, (1, blk), 1)
    done0 = jnp.where(tpos + 1 <= topk, 1.0, 0.0)
    low0 = jnp.full((1, blk), -inf, F32)

    def fast_cond(st):
        it, _, _, _, done = st
        return jnp.logical_and(it < 24, jnp.min(done) < 0.5)

    def fast_body(st):
        it, lo, hi, low, done = st
        lo, hi, low, done = bisect(lo, hi, low, done)
        return it + 1, lo, hi, low, done

    _, lo, hi, low, done = lax.while_loop(fast_cond, fast_body, (jnp.int32(0), lo0, hi0, low0, done0))

    def next_value_above(thr):
        t8 = lanes8(thr)

        def body(j, acc):
            x = as_groups(sc_ref[pl.ds(chunk_start(j), blk), :])
            return jnp.minimum(acc, jnp.min(jnp.where(x > t8[None], x, inf), axis=0))

        return colmin(lax.fori_loop(0, nch, body, jnp.full((SUBLANES, blk), inf, F32)))

    def count_gt_eq(val):
        v8 = lanes8(val)

        def body(j, carry):
            g8, e8 = carry
            x = as_groups(sc_ref[pl.ds(chunk_start(j), blk), :])
            g8 = g8 + jnp.sum(jnp.where(x > v8[None], 1.0, 0.0), axis=0)
            e8 = e8 + jnp.sum(jnp.where(x == v8[None], 1.0, 0.0), axis=0)
            return g8, e8

        z = jnp.zeros((SUBLANES, blk), F32)
        g8, e8 = lax.fori_loop(0, nch, body, (z, z))
        return colsum(g8), colsum(e8)

    def count_ties_upto(val, jmax):
        v8 = lanes8(val)
        j8 = lanes8(jmax)
        sub = lax.broadcasted_iota(jnp.int32, (groups, SUBLANES, blk), 0) * SUBLANES \
            + lax.broadcasted_iota(jnp.int32, (groups, SUBLANES, blk), 1)

        def body(j, acc):
            x = as_groups(sc_ref[pl.ds(chunk_start(j), blk), :])
            idx = (sub + j * blk).astype(F32)
            hit = jnp.logical_and(x == v8[None], idx <= j8[None])
            return acc + jnp.sum(jnp.where(hit, 1.0, 0.0), axis=0)

        return colsum(lax.fori_loop(0, nch, body, jnp.zeros((SUBLANES, blk), F32)))

    def slow_cond(st):
        return jnp.min(st[3]) < 0.5

    def slow_body(st):
        lo, hi, low, done, tie, need, neq = st
        live = done < 0.5
        cand = next_value_above(low)
        cgt, ceq = count_gt_eq(cand)
        found = jnp.logical_and(live, cgt < kf)
        tie = jnp.where(found, cand, tie)
        need = jnp.where(found, kf - cgt, need)
        neq = jnp.where(found, ceq, neq)
        done = jnp.where(found, 1.0, done)
        climb = jnp.logical_and(live, cgt >= kf)
        low = jnp.where(climb, cand, low)
        lo = jnp.where(climb, jnp.maximum(lo, cand), lo)
        lo, hi, low, done = bisect(lo, hi, low, done)
        return lo, hi, low, done, tie, need, neq

    def slow_path(args):
        lo, hi, low, done = args
        tie0 = jnp.full((1, blk), inf, F32)
        zero = jnp.zeros((1, blk), F32)
        lo, hi, low, done, tie, need, neq = lax.while_loop(
            slow_cond, slow_body, (lo, hi, low, done, tie0, zero, zero))
        low = jnp.where(tie < inf, tie, low)
        jlo = jnp.full((1, blk), -1.0, F32)
        jhi = jnp.full((1, blk), float(seq - 1), F32)

        def cut_body(_, c):
            jlo, jhi = c
            jm = jnp.floor(0.5 * (jlo + jhi))
            enough = count_ties_upto(tie, jm) >= need
            return jnp.where(enough, jlo, jm), jnp.where(enough, jm, jhi)

        def cut(_):
            return lax.fori_loop(0, int(math.ceil(math.log2(seq))) + 1, cut_body, (jlo, jhi))[1]

        excess = jnp.max(jnp.where(neq > need, 1.0, 0.0)) > 0.5
        jcut = lax.cond(excess, cut, lambda _: jhi, 0)
        return low, tie, jcut

    def fast_path(args):
        _, _, low, _ = args
        return low, jnp.full((1, blk), inf, F32), jnp.full((1, blk), float(seq - 1), F32)

    low, tie, jcut = lax.cond(jnp.min(done) < 0.5, slow_path, fast_path, (lo, hi, low, done))

    low8, tie8, jcut8 = lanes8(low), lanes8(tie), lanes8(jcut)
    sub = lax.broadcasted_iota(jnp.int32, (groups, SUBLANES, blk), 0) * SUBLANES \
        + lax.broadcasted_iota(jnp.int32, (groups, SUBLANES, blk), 1)

    def mask_body(j, _):
        r0 = chunk_start(j)
        x = as_groups(sc_ref[pl.ds(r0, blk), :])
        idx = (sub + j * blk).astype(F32)
        sel = jnp.logical_or(x > low8[None], jnp.logical_and(x == tie8[None], idx <= jcut8[None]))
        sc_ref[pl.ds(r0, blk), :] = jnp.where(sel, 0.0, -inf).reshape(blk, blk)
        return 0

    lax.fori_loop(0, nch, mask_body, 0)

    q = q_ref[0]
    lane = lax.broadcasted_iota(jnp.int32, (blk, LANES), 1)
    for h in range(N_HEADS):
        hp = h // 2
        in_head = (lane < HEAD_DIM) if h % 2 == 0 else (lane >= HEAD_DIM)
        qh_ref[h] = jnp.where(in_head, q[:, hp * LANES:(hp + 1) * LANES], jnp.zeros((), BF16))
    m_ref[...] = jnp.full(m_ref.shape, -inf, F32)
    l_ref[...] = jnp.zeros(l_ref.shape, F32)
    acc_ref[...] = jnp.zeros(acc_ref.shape, F32)

    def att_body(j, _):
        r0 = chunk_start(j)
        mb = sc_ref[pl.ds(r0, blk), :]
        near = jnp.minimum(i - j, 2)
        for h in range(N_HEADS):
            hp = h // 2
            kc = k_ref[0, pl.ds(r0, blk), hp * LANES:(hp + 1) * LANES]
            lg = lax.dot_general(kc, qh_ref[h], NT_DIMS, preferred_element_type=F32) + mb + bt_ref[near, h]
            m_old = m_ref[h:h + 1, :]
            m_new = jnp.maximum(m_old, jnp.max(lg, axis=0, keepdims=True))
            m_use = jnp.where(m_new == -inf, 0.0, m_new)
            alpha = jnp.exp2(m_old - m_use)
            p = jnp.exp2(lg - m_use)
            l_ref[h:h + 1, :] = alpha * l_ref[h:h + 1, :] + jnp.sum(p, axis=0, keepdims=True)
            pv = jnp.dot(vt_ref[0, j, h * HEAD_DIM:(h + 1) * HEAD_DIM, :], p.astype(BF16),
                         preferred_element_type=F32)
            acc_ref[h * HEAD_DIM:(h + 1) * HEAD_DIM, :] = alpha * acc_ref[h * HEAD_DIM:(h + 1) * HEAD_DIM, :] + pv
            m_ref[h:h + 1, :] = m_new
        return 0

    lax.fori_loop(0, nch, att_body, 0)
    for h in range(N_HEADS):
        rows = slice(h * HEAD_DIM, (h + 1) * HEAD_DIM)
        acc_ref[rows, :] = acc_ref[rows, :] / l_ref[h:h + 1, :]
    o_ref[0] = acc_ref[...].T.astype(BF16)


def _attention(q, qi, wit, k, vt, kia, kib, btiles, *, bsz, seq, topk):
    blk = ATT_BLOCK
    nblk = seq // blk
    return pl.pallas_call(
        functools.partial(_attn_kernel, topk=topk, seq=seq),
        grid=(bsz, nblk),
        in_specs=[
            pl.BlockSpec((1, blk, ATTN_DIM), lambda b, i: (b, i, 0)),
            pl.BlockSpec((1, blk, IDX_HEADS * IDX_DIM), lambda b, i: (b, i, 0)),
            pl.BlockSpec((1, IDX_HEADS, blk), lambda b, i: (b, 0, i)),
            pl.BlockSpec((1, seq, ATTN_DIM), lambda b, i: (b, 0, 0)),
            pl.BlockSpec((1, nblk, ATTN_DIM, blk), lambda b, i: (b, 0, 0, 0)),
            pl.BlockSpec((1, seq, LANES), lambda b, i: (b, 0, 0)),
            pl.BlockSpec((1, seq, LANES), lambda b, i: (b, 0, 0)),
            pl.BlockSpec(btiles.shape, lambda b, i: (0, 0, 0, 0)),
        ],
        out_specs=pl.BlockSpec((1, blk, ATTN_DIM), lambda b, i: (b, i, 0)),
        out_shape=jax.ShapeDtypeStruct((bsz, seq, ATTN_DIM), BF16),
        scratch_shapes=[
            pltpu.VMEM((seq, blk), F32),
            pltpu.VMEM((N_HEADS, blk, LANES), BF16),
            pltpu.VMEM((N_HEADS, blk), F32),
            pltpu.VMEM((N_HEADS, blk), F32),
            pltpu.VMEM((ATTN_DIM, blk), F32),
        ],
        compiler_params=pltpu.CompilerParams(dimension_semantics=("arbitrary", "arbitrary"),
                                             vmem_limit_bytes=V7X_VMEM_LIMIT_BYTES),
        name="attention",
    )(q, qi, wit, k, vt, kia, kib, btiles)


def _mix_kernel(x_ref, xh_ref, mod_ref, att_ref, wpg_ref, wpool_ref, ps_ref, wa_ref, wb_ref, wo_ref,
                lng_ref, lnb_ref, o_ref, pe_ref, mix_ref, *, alpha, mod_row, ln_row, tiles_per_seq):
    rows = x_ref.shape[0]
    i = pl.program_id(0)
    seq_tile = i % tiles_per_seq
    x = x_ref[...]
    sh = mod_ref[0, mod_row:mod_row + 1, :]
    sc = mod_ref[0, mod_row + 1:mod_row + 2, :]
    gate = mod_ref[0, mod_row + 2:mod_row + 3, :]
    u = (x * (1.0 + sc) + sh).astype(BF16)
    uh = (xh_ref[...] * (1.0 + sc) + sh).astype(BF16)
    pg = jnp.dot(u, wpg_ref[...], preferred_element_type=F32)
    ph = jnp.dot(uh, wpg_ref[:, :POOL_DIM], preferred_element_type=F32)
    pe_ref[0:POOL_HALO, :] = jnp.where(seq_tile == 0, 0.0, ph)
    pe_ref[POOL_HALO:, :] = pg[:, :POOL_DIM]
    t = seq_tile * rows + lax.broadcasted_iota(jnp.int32, (rows, 1), 0)
    for g, w in enumerate(POOL_WINDOWS):
        cols = slice(g * POOL_GROUP_DIM, (g + 1) * POOL_GROUP_DIM)
        cur = pe_ref[POOL_HALO:, cols]
        win = cur
        for back in range(1, w):
            win = win + pe_ref[POOL_HALO - back:POOL_HALO - back + rows, cols]
        cnt = jnp.minimum(t + 1, w).astype(F32)
        pooled = (win / cnt - cur).astype(BF16)
        mixed = jnp.dot(pooled, wpool_ref[g], preferred_element_type=F32)
        mix_ref[:, cols] = (mixed * ps_ref[:, cols]).astype(BF16)
    y_a = jnp.dot(mix_ref[...], wa_ref[...], preferred_element_type=F32)
    y_b = jnp.dot(att_ref[...], wb_ref[...], preferred_element_type=F32)
    d = x.shape[1]
    ga = pg[:, POOL_DIM:POOL_DIM + d]
    gb = pg[:, POOL_DIM + d:]
    merged = (jax.nn.sigmoid(ga) * y_a + jax.nn.sigmoid(gb) * y_b).astype(BF16)
    y = jnp.dot(merged, wo_ref[...], preferred_element_type=F32)
    z = alpha * x + gate * y
    o_ref[...] = _layer_norm(z, lng_ref[ln_row:ln_row + 1, :], lnb_ref[ln_row:ln_row + 1, :])


def _mix_out(x2d, mod, att2d, wpg, wpool, pool_scale, wa, wb, wo, ln_g, ln_b, *, seq, alpha, mod_row, ln_row):
    n, d = x2d.shape
    tiles_per_seq = seq // MIX_ROWS
    halo_blocks = MIX_ROWS // POOL_HALO
    resident = dict(pipeline_mode=pl.Buffered(1))
    full = lambda a: pl.BlockSpec(a.shape, lambda i: (0,) * a.ndim, **resident)
    return pl.pallas_call(
        functools.partial(_mix_kernel, alpha=alpha, mod_row=mod_row, ln_row=ln_row, tiles_per_seq=tiles_per_seq),
        grid=(n // MIX_ROWS,),
        in_specs=[
            pl.BlockSpec((MIX_ROWS, d), lambda i: (i, 0)),
            pl.BlockSpec((POOL_HALO, d), lambda i: (jnp.maximum(i * halo_blocks - 1, 0), 0)),
            pl.BlockSpec((1, N_ADA, d), lambda i: (i // tiles_per_seq, 0, 0)),
            pl.BlockSpec((MIX_ROWS, ATTN_DIM), lambda i: (i, 0)),
            full(wpg), full(wpool), full(pool_scale), full(wa), full(wb), full(wo),
            pl.BlockSpec(ln_g.shape, lambda i: (0, 0)),
            pl.BlockSpec(ln_b.shape, lambda i: (0, 0)),
        ],
        out_specs=pl.BlockSpec((MIX_ROWS, d), lambda i: (i, 0)),
        out_shape=jax.ShapeDtypeStruct((n, d), F32),
        scratch_shapes=[
            pltpu.VMEM((POOL_HALO + MIX_ROWS, POOL_DIM), F32),
            pltpu.VMEM((MIX_ROWS, POOL_DIM), BF16),
        ],
        compiler_params=pltpu.CompilerParams(dimension_semantics=("arbitrary",),
                                             vmem_limit_bytes=V7X_VMEM_LIMIT_BYTES),
        name="mix_out",
    )(x2d, x2d, mod, att2d, wpg, wpool, pool_scale, wa, wb, wo, ln_g, ln_b)


def kernel(x, c, w_ada, b_ada, ln_g, ln_b, ffn1_w_gate, ffn1_w_up, ffn1_w_down, w_in, w_pool, pool_scale,
           w_a, w_b, w_out, rel_bias, ffn2_w_gate, ffn2_w_up, ffn2_w_down):
    bsz, seq, d = x.shape
    depth = w_ada.shape[0]
    alpha = (2.0 * depth) ** 0.25
    topk = min(TOP_K, seq // 4)
    assert seq % FFN_ROWS == 0 and seq % PROJ_ROWS == 0 and seq % MIX_ROWS == 0 and seq % ATT_BLOCK == 0
    assert PROJ_ROWS % ATT_BLOCK == 0 and MIX_ROWS % POOL_HALO == 0 and POOL_HALO >= max(POOL_WINDOWS) - 1
    far_bucket = _far_bucket(ATT_BLOCK + 1, max(seq - 1, ATT_BLOCK + 1))

    o_q = POOL_DIM
    o_k = o_q + ATTN_DIM
    o_v = o_k + ATTN_DIM
    o_qi = o_v + ATTN_DIM
    o_ki = o_qi + IDX_HEADS * IDX_DIM
    o_wi = o_ki + IDX_DIM
    o_ga = o_wi + IDX_HEADS

    btiles = _rel_bias_tiles(rel_bias, far_bucket)
    btiles = jnp.concatenate([btiles, jnp.zeros_like(btiles[:1])], axis=0)
    x2d = x.reshape(bsz * seq, d)
    for l in range(depth):
        wl = w_in[l]
        zeros_ki = jnp.zeros((d, LANES - IDX_DIM), wl.dtype)
        w_ki = wl[:, o_ki:o_wi]
        wqk = jnp.concatenate([wl[:, o_q:o_v], wl[:, o_qi:o_ki]], axis=1).astype(BF16)
        wv_t = wl[:, o_v:o_qi].T.astype(BF16)
        wki = jnp.concatenate([w_ki, zeros_ki, zeros_ki, w_ki], axis=1).astype(BF16)
        wwi_t = jnp.pad(wl[:, o_wi:o_ga].T, ((0, 2 * SUBLANES - IDX_HEADS), (0, 0))).astype(BF16)
        wpg = jnp.concatenate([wl[:, :POOL_DIM], wl[:, o_ga:]], axis=1).astype(BF16)

        mod = _ada_mod(c, w_ada[l], b_ada[l]).reshape(bsz, N_ADA, d)
        x2d = _ffn(x2d, mod, ln_g[l], ln_b[l], ffn1_w_gate[l].astype(BF16), ffn1_w_up[l].astype(BF16),
                   ffn1_w_down[l].astype(BF16), seq=seq, alpha=alpha, mod_row=0, ln_row=0)
        q, k, qi, vt, kia, kib, wit = _attn_proj(x2d, mod, wqk, wv_t, wki, wwi_t, bsz=bsz, seq=seq, mod_row=3)
        att = _attention(q.reshape(bsz, seq, ATTN_DIM), qi.reshape(bsz, seq, IDX_HEADS * IDX_DIM), wit,
                         k.reshape(bsz, seq, ATTN_DIM), vt, kia.reshape(bsz, seq, LANES),
                         kib.reshape(bsz, seq, LANES), btiles, bsz=bsz, seq=seq, topk=topk)
        x2d = _mix_out(x2d, mod, att.reshape(bsz * seq, ATTN_DIM), wpg, w_pool[l].astype(BF16),
                       pool_scale[l].reshape(1, POOL_DIM), w_a[l].astype(BF16), w_b[l].astype(BF16),
                       w_out[l].astype(BF16), ln_g[l], ln_b[l], seq=seq, alpha=alpha, mod_row=3, ln_row=1)
        x2d = _ffn(x2d, mod, ln_g[l], ln_b[l], ffn2_w_gate[l].astype(BF16), ffn2_w_up[l].astype(BF16),
                   ffn2_w_down[l].astype(BF16), seq=seq, alpha=alpha, mod_row=6, ln_row=2)
    return x2d.reshape(bsz, seq, d)
```

```python
import functools
import math

import numpy as np
import jax
import jax.numpy as jnp
from jax import lax
from jax.experimental import pallas as pl
from jax.experimental.pallas import tpu as pltpu

POOL_WINDOWS = (2, 4, 8, 16)
POOL_GROUP_DIM = 128
POOL_DIM = len(POOL_WINDOWS) * POOL_GROUP_DIM
N_HEADS = 8
HEAD_DIM = 64
ATTN_DIM = N_HEADS * HEAD_DIM
IDX_HEADS = 8
IDX_DIM = 64
TOP_K = 256
REL_BUCKETS = 32
REL_MAX_DIST = 128
N_ADA = 9
LN_EPS = 1e-5
POOL_HALO = 16

LANES = 128
SUBLANES = 8
V7X_VMEM_LIMIT_BYTES = 56 * 1024 * 1024
FFN_ROWS = 512
FFN_COLS = 256
PROJ_ROWS = 512
MIX_ROWS = 256
ATT_BLOCK = 256
ADA_COLS = 1024

LOG2E = math.log2(math.e)
BF16 = jnp.bfloat16
F32 = jnp.float32
NT_DIMS = (((1,), (1,)), ((), ()))


def _layer_norm(z, g, b):
    mu = jnp.mean(z, axis=-1, keepdims=True)
    zc = z - mu
    var = jnp.mean(zc * zc, axis=-1, keepdims=True)
    return zc * lax.rsqrt(var + LN_EPS) * g + b


def _silu(a):
    return a * jax.nn.sigmoid(a)


def _ada_kernel(c_ref, w_ref, b_ref, o_ref):
    a = _silu(c_ref[...])
    o_ref[...] = jnp.dot(a, w_ref[...], preferred_element_type=F32) + b_ref[...]


def _ada_mod(c, w_ada, b_ada):
    bsz, d = c.shape
    n = w_ada.shape[1]
    rows = -(-bsz // SUBLANES) * SUBLANES
    c_pad = jnp.pad(c, ((0, rows - bsz), (0, 0)))
    out = pl.pallas_call(
        _ada_kernel,
        grid=(n // ADA_COLS,),
        in_specs=[
            pl.BlockSpec((rows, d), lambda j: (0, 0)),
            pl.BlockSpec((d, ADA_COLS), lambda j: (0, j)),
            pl.BlockSpec((1, ADA_COLS), lambda j: (0, j)),
        ],
        out_specs=pl.BlockSpec((rows, ADA_COLS), lambda j: (0, j)),
        out_shape=jax.ShapeDtypeStruct((rows, n), F32),
        compiler_params=pltpu.CompilerParams(dimension_semantics=("arbitrary",),
                                             vmem_limit_bytes=V7X_VMEM_LIMIT_BYTES),
        name="ada_mod",
    )(c_pad, w_ada, b_ada.reshape(1, n))
    return out[:bsz]


def _ffn_kernel(x_ref, mod_ref, lng_ref, lnb_ref, wg_ref, wu_ref, wd_ref, o_ref, h_ref, *, alpha, mod_row, ln_row):
    x = x_ref[...]
    sh = mod_ref[0, mod_row:mod_row + 1, :]
    sc = mod_ref[0, mod_row + 1:mod_row + 2, :]
    gate = mod_ref[0, mod_row + 2:mod_row + 3, :]
    u = (x * (1.0 + sc) + sh).astype(BF16)
    d_ff = wg_ref.shape[1]
    for c in range(d_ff // FFN_COLS):
        sl = slice(c * FFN_COLS, (c + 1) * FFN_COLS)
        a = jnp.dot(u, wg_ref[:, sl], preferred_element_type=F32)
        b = jnp.dot(u, wu_ref[:, sl], preferred_element_type=F32)
        h_ref[:, sl] = (_silu(a) * b).astype(BF16)
    y = jnp.dot(h_ref[...], wd_ref[...], preferred_element_type=F32)
    z = alpha * x + (0.5 * gate) * y
    o_ref[...] = _layer_norm(z, lng_ref[ln_row:ln_row + 1, :], lnb_ref[ln_row:ln_row + 1, :])


def _ffn(x2d, mod, ln_g, ln_b, wg, wu, wd, *, seq, alpha, mod_row, ln_row):
    n, d = x2d.shape
    d_ff = wg.shape[1]
    tiles_per_seq = seq // FFN_ROWS
    resident = dict(pipeline_mode=pl.Buffered(1))
    return pl.pallas_call(
        functools.partial(_ffn_kernel, alpha=alpha, mod_row=mod_row, ln_row=ln_row),
        grid=(n // FFN_ROWS,),
        in_specs=[
            pl.BlockSpec((FFN_ROWS, d), lambda i: (i, 0)),
            pl.BlockSpec((1, N_ADA, d), lambda i: (i // tiles_per_seq, 0, 0)),
            pl.BlockSpec(ln_g.shape, lambda i: (0, 0)),
            pl.BlockSpec(ln_b.shape, lambda i: (0, 0)),
            pl.BlockSpec((d, d_ff), lambda i: (0, 0), **resident),
            pl.BlockSpec((d, d_ff), lambda i: (0, 0), **resident),
            pl.BlockSpec((d_ff, d), lambda i: (0, 0), **resident),
        ],
        out_specs=pl.BlockSpec((FFN_ROWS, d), lambda i: (i, 0)),
        out_shape=jax.ShapeDtypeStruct((n, d), F32),
        scratch_shapes=[pltpu.VMEM((FFN_ROWS, d_ff), BF16)],
        compiler_params=pltpu.CompilerParams(dimension_semantics=("arbitrary",),
                                             vmem_limit_bytes=V7X_VMEM_LIMIT_BYTES),
        name="ffn",
    )(x2d, mod, ln_g, ln_b, wg, wu, wd)


def _proj_kernel(x_ref, mod_ref, wqk_ref, wv_ref, wki_ref, wwi_ref,
                 q_ref, k_ref, qi_ref, vt_ref, kia_ref, kib_ref, wit_ref, *, mod_row):
    x = x_ref[...]
    sh = mod_ref[0, mod_row:mod_row + 1, :]
    sc = mod_ref[0, mod_row + 1:mod_row + 2, :]
    u = (x * (1.0 + sc) + sh).astype(BF16)
    qkq = jnp.dot(u, wqk_ref[...], preferred_element_type=F32)
    q_ref[...] = (qkq[:, :ATTN_DIM] * (HEAD_DIM ** -0.5 * LOG2E)).astype(BF16)
    k_ref[...] = qkq[:, ATTN_DIM:2 * ATTN_DIM].astype(BF16)
    qi_ref[...] = qkq[:, 2 * ATTN_DIM:].astype(BF16)
    kk = jnp.dot(u, wki_ref[...], preferred_element_type=F32)
    kia_ref[...] = kk[:, :LANES].astype(BF16)
    kib_ref[...] = kk[:, LANES:].astype(BF16)
    vt = lax.dot_general(wv_ref[...], u, NT_DIMS, preferred_element_type=F32).astype(BF16)
    for c in range(vt_ref.shape[1]):
        vt_ref[0, c] = vt[:, c * ATT_BLOCK:(c + 1) * ATT_BLOCK]
    wit = lax.dot_general(wwi_ref[...], u, NT_DIMS, preferred_element_type=F32)
    wit_ref[0] = wit[:IDX_HEADS, :]


def _attn_proj(x2d, mod, wqk, wv_t, wki, wwi_t, *, bsz, seq, mod_row):
    n, d = x2d.shape
    tiles_per_seq = seq // PROJ_ROWS
    chunks_per_tile = PROJ_ROWS // ATT_BLOCK
    resident = dict(pipeline_mode=pl.Buffered(1))
    row_spec = lambda cols: pl.BlockSpec((PROJ_ROWS, cols), lambda i: (i, 0))
    return pl.pallas_call(
        functools.partial(_proj_kernel, mod_row=mod_row),
        grid=(n // PROJ_ROWS,),
        in_specs=[
            pl.BlockSpec((PROJ_ROWS, d), lambda i: (i, 0)),
            pl.BlockSpec((1, N_ADA, d), lambda i: (i // tiles_per_seq, 0, 0)),
            pl.BlockSpec(wqk.shape, lambda i: (0, 0), **resident),
            pl.BlockSpec(wv_t.shape, lambda i: (0, 0), **resident),
            pl.BlockSpec(wki.shape, lambda i: (0, 0), **resident),
            pl.BlockSpec(wwi_t.shape, lambda i: (0, 0), **resident),
        ],
        out_specs=[
            row_spec(ATTN_DIM), row_spec(ATTN_DIM), row_spec(IDX_HEADS * IDX_DIM),
            pl.BlockSpec((1, chunks_per_tile, ATTN_DIM, ATT_BLOCK),
                         lambda i: (i // tiles_per_seq, i % tiles_per_seq, 0, 0)),
            row_spec(LANES), row_spec(LANES),
            pl.BlockSpec((1, IDX_HEADS, PROJ_ROWS), lambda i: (i // tiles_per_seq, 0, i % tiles_per_seq)),
        ],
        out_shape=[
            jax.ShapeDtypeStruct((n, ATTN_DIM), BF16),
            jax.ShapeDtypeStruct((n, ATTN_DIM), BF16),
            jax.ShapeDtypeStruct((n, IDX_HEADS * IDX_DIM), BF16),
            jax.ShapeDtypeStruct((bsz, seq // ATT_BLOCK, ATTN_DIM, ATT_BLOCK), BF16),
            jax.ShapeDtypeStruct((n, LANES), BF16),
            jax.ShapeDtypeStruct((n, LANES), BF16),
            jax.ShapeDtypeStruct((bsz, IDX_HEADS, seq), F32),
        ],
        compiler_params=pltpu.CompilerParams(dimension_semantics=("arbitrary",),
                                             vmem_limit_bytes=V7X_VMEM_LIMIT_BYTES),
        name="attn_proj",
    )(x2d, mod, wqk, wv_t, wki, wwi_t)


def _t5_bucket(n):
    max_exact = REL_BUCKETS // 2
    nf = jnp.maximum(n, 1).astype(F32)
    large = max_exact + (jnp.log(nf / max_exact) / math.log(REL_MAX_DIST / max_exact)
                         * (REL_BUCKETS - max_exact)).astype(jnp.int32)
    large = jnp.minimum(large, REL_BUCKETS - 1)
    return jnp.where(n < max_exact, n, large)


def _far_bucket(first_dist, last_dist):
    n = np.arange(first_dist, last_dist + 1, dtype=np.float32)
    max_exact = REL_BUCKETS // 2
    large = max_exact + (np.log(n / np.float32(max_exact)) / np.float32(math.log(REL_MAX_DIST / max_exact))
                         * np.float32(REL_BUCKETS - max_exact)).astype(np.int32)
    buckets = np.where(n < max_exact, n.astype(np.int32), np.minimum(large, REL_BUCKETS - 1))
    assert buckets.min() == buckets.max(), "key chunks two or more blocks away must share one bias bucket"
    return int(buckets[0])


def _bias_kernel(rb_ref, o_ref, *, far_bucket):
    o = pl.program_id(0)
    h = pl.program_id(1)
    row = lax.broadcasted_iota(jnp.int32, (ATT_BLOCK, ATT_BLOCK), 0)
    col = lax.broadcasted_iota(jnp.int32, (ATT_BLOCK, ATT_BLOCK), 1)
    dist = o * ATT_BLOCK + col - row
    bucket = _t5_bucket(jnp.maximum(dist, 0))
    tile = jnp.zeros((ATT_BLOCK, ATT_BLOCK), F32)
    for b in range(REL_BUCKETS):
        tile = jnp.where(bucket == b, rb_ref[b, h], tile)
    o_ref[0, 0] = (tile - rb_ref[far_bucket, h]) * LOG2E


def _rel_bias_tiles(rel_bias, far_bucket):
    return pl.pallas_call(
        functools.partial(_bias_kernel, far_bucket=far_bucket),
        grid=(2, N_HEADS),
        in_specs=[pl.BlockSpec(memory_space=pltpu.SMEM)],
        out_specs=pl.BlockSpec((1, 1, ATT_BLOCK, ATT_BLOCK), lambda o, h: (o, h, 0, 0)),
        out_shape=jax.ShapeDtypeStruct((2, N_HEADS, ATT_BLOCK, ATT_BLOCK), F32),
        compiler_params=pltpu.CompilerParams(dimension_semantics=("arbitrary", "arbitrary")),
        name="rel_bias",
    )(rel_bias)


def _attn_kernel(q_ref, qi_ref, wit_ref, k_ref, vt_ref, kia_ref, kib_ref, bt_ref, o_ref,
                 sc_ref, qh_ref, lg_ref, acc_ref, *, topk, seq):
    blk = ATT_BLOCK
    groups = blk // SUBLANES
    i = pl.program_id(1)
    nch = i + 1
    kf = float(topk)
    inf = jnp.inf

    def chunk_start(j):
        return pl.multiple_of(j * blk, blk)

    def as_groups(x):
        return x.reshape(groups, SUBLANES, blk)

    def lanes8(v):
        return jnp.broadcast_to(v, (SUBLANES, blk))

    def colmin(x8):
        return jnp.min(x8, axis=0, keepdims=True)

    def colmax(x8):
        return jnp.max(x8, axis=0, keepdims=True)

    def colsum(x8):
        return jnp.sum(x8, axis=0, keepdims=True)

    wf = wit_ref[0] * (IDX_DIM ** -0.5)
    qi = qi_ref[0]

    def chunk_scores(j):
        r0 = chunk_start(j)
        ka = kia_ref[0, pl.ds(r0, blk), :]
        kb = kib_ref[0, pl.ds(r0, blk), :]
        s = jnp.zeros((blk, blk), F32)
        for hp in range(IDX_HEADS // 2):
            qp = qi[:, hp * LANES:(hp + 1) * LANES]
            a0 = lax.dot_general(ka, qp, NT_DIMS, preferred_element_type=F32)
            a1 = lax.dot_general(kb, qp, NT_DIMS, preferred_element_type=F32)
            s = s + jnp.maximum(a0, 0.0) * wf[2 * hp:2 * hp + 1, :]
            s = s + jnp.maximum(a1, 0.0) * wf[2 * hp + 1:2 * hp + 2, :]
        return r0, s * (IDX_HEADS ** -0.5)

    def score_body(j, carry):
        mn8, mx8 = carry
        r0, s = chunk_scores(j)
        sc_ref[pl.ds(r0, blk), :] = s
        s3 = as_groups(s)
        return jnp.minimum(mn8, jnp.min(s3, axis=0)), jnp.maximum(mx8, jnp.max(s3, axis=0))

    mn8, mx8 = lax.fori_loop(0, i, score_body,
                             (jnp.full((SUBLANES, blk), inf, F32), jnp.full((SUBLANES, blk), -inf, F32)))
    r_diag, s_diag = chunk_scores(i)
    row = lax.broadcasted_iota(jnp.int32, (blk, blk), 0)
    col = lax.broadcasted_iota(jnp.int32, (blk, blk), 1)
    causal = row <= col
    sc_ref[pl.ds(r_diag, blk), :] = jnp.where(causal, s_diag, -inf)
    mn8 = jnp.minimum(mn8, jnp.min(as_groups(jnp.where(causal, s_diag, inf)), axis=0))
    mx8 = jnp.maximum(mx8, jnp.max(as_groups(jnp.where(causal, s_diag, -inf)), axis=0))
    lo0 = colmin(mn8)
    hi0 = colmax(mx8)

    def count_gt(thr):
        t8 = lanes8(thr)

        def body(j, acc):
            x = as_groups(sc_ref[pl.ds(chunk_start(j), blk), :])
            return acc + jnp.sum(jnp.where(x > t8[None], 1.0, 0.0), axis=0)

        return colsum(lax.fori_loop(0, nch, body, jnp.zeros((SUBLANES, blk), F32)))

    def bisect(lo, hi, low, done):
        mid = 0.5 * lo + 0.5 * hi
        c = count_gt(mid)
        live = done < 0.5
        up = jnp.logical_and(live, c >= kf)
        down = jnp.logical_and(live, c < kf)
        low = jnp.where(up, mid, low)
        lo = jnp.where(up, mid, lo)
        hi = jnp.where(down, mid, hi)
        done = jnp.where(c == kf, 1.0, done)
        return lo, hi, low, done

    tpos = i * blk + lax.broadcasted_iota(jnp.int32, (1, blk), 1)
    done0 = jnp.where(tpos + 1 <= topk, 1.0, 0.0)
    low0 = jnp.full((1, blk), -inf, F32)

    def fast_cond(st):
        it, _, _, _, done = st
        return jnp.logical_and(it < 24, jnp.min(done) < 0.5)

    def fast_body(st):
        it, lo, hi, low, done = st
        lo, hi, low, done = bisect(lo, hi, low, done)
        return it + 1, lo, hi, low, done

    _, lo, hi, low, done = lax.while_loop(fast_cond, fast_body, (jnp.int32(0), lo0, hi0, low0, done0))

    def next_value_above(thr):
        t8 = lanes8(thr)

        def body(j, acc):
            x = as_groups(sc_ref[pl.ds(chunk_start(j), blk), :])
            return jnp.minimum(acc, jnp.min(jnp.where(x > t8[None], x, inf), axis=0))

        return colmin(lax.fori_loop(0, nch, body, jnp.full((SUBLANES, blk), inf, F32)))

    def count_gt_eq(val):
        v8 = lanes8(val)

        def body(j, carry):
            g8, e8 = carry
            x = as_groups(sc_ref[pl.ds(chunk_start(j), blk), :])
            g8 = g8 + jnp.sum(jnp.where(x > v8[None], 1.0, 0.0), axis=0)
            e8 = e8 + jnp.sum(jnp.where(x == v8[None], 1.0, 0.0), axis=0)
            return g8, e8

        z = jnp.zeros((SUBLANES, blk), F32)
        g8, e8 = lax.fori_loop(0, nch, body, (z, z))
        return colsum(g8), colsum(e8)

    def count_ties_upto(val, jmax):
        v8 = lanes8(val)
        j8 = lanes8(jmax)
        sub = lax.broadcasted_iota(jnp.int32, (groups, SUBLANES, blk), 0) * SUBLANES \
            + lax.broadcasted_iota(jnp.int32, (groups, SUBLANES, blk), 1)

        def body(j, acc):
            x = as_groups(sc_ref[pl.ds(chunk_start(j), blk), :])
            idx = (sub + j * blk).astype(F32)
            hit = jnp.logical_and(x == v8[None], idx <= j8[None])
            return acc + jnp.sum(jnp.where(hit, 1.0, 0.0), axis=0)

        return colsum(lax.fori_loop(0, nch, body, jnp.zeros((SUBLANES, blk), F32)))

    def slow_cond(st):
        return jnp.min(st[3]) < 0.5

    def slow_body(st):
        lo, hi, low, done, tie, need, neq = st
        live = done < 0.5
        cand = next_value_above(low)
        cgt, ceq = count_gt_eq(cand)
        found = jnp.logical_and(live, cgt < kf)
        tie = jnp.where(found, cand, tie)
        need = jnp.where(found, kf - cgt, need)
        neq = jnp.where(found, ceq, neq)
        done = jnp.where(found, 1.0, done)
        climb = jnp.logical_and(live, cgt >= kf)
        low = jnp.where(climb, cand, low)
        lo = jnp.where(climb, jnp.maximum(lo, cand), lo)
        lo, hi, low, done = bisect(lo, hi, low, done)
        return lo, hi, low, done, tie, need, neq

    def slow_path(args):
        lo, hi, low, done = args
        tie0 = jnp.full((1, blk), inf, F32)
        zero = jnp.zeros((1, blk), F32)
        lo, hi, low, done, tie, need, neq = lax.while_loop(
            slow_cond, slow_body, (lo, hi, low, done, tie0, zero, zero))
        low = jnp.where(tie < inf, tie, low)
        jlo = jnp.full((1, blk), -1.0, F32)
        jhi = jnp.full((1, blk), float(seq - 1), F32)

        def cut_body(_, c):
            jlo, jhi = c
            jm = jnp.floor(0.5 * (jlo + jhi))
            enough = count_ties_upto(tie, jm) >= need
            return jnp.where(enough, jlo, jm), jnp.where(enough, jm, jhi)

        def cut(_):
            return lax.fori_loop(0, int(math.ceil(math.log2(seq))) + 1, cut_body, (jlo, jhi))[1]

        excess = jnp.max(jnp.where(neq > need, 1.0, 0.0)) > 0.5
        jcut = lax.cond(excess, cut, lambda _: jhi, 0)
        return low, tie, jcut

    def fast_path(args):
        _, _, low, _ = args
        return low, jnp.full((1, blk), inf, F32), jnp.full((1, blk), float(seq - 1), F32)

    low, tie, jcut = lax.cond(jnp.min(done) < 0.5, slow_path, fast_path, (lo, hi, low, done))

    low8, tie8, jcut8 = lanes8(low), lanes8(tie), lanes8(jcut)
    sub = lax.broadcasted_iota(jnp.int32, (groups, SUBLANES, blk), 0) * SUBLANES \
        + lax.broadcasted_iota(jnp.int32, (groups, SUBLANES, blk), 1)

    def mask_body(j, _):
        r0 = chunk_start(j)
        x = as_groups(sc_ref[pl.ds(r0, blk), :])
        idx = (sub + j * blk).astype(F32)
        sel = jnp.logical_or(x > low8[None], jnp.logical_and(x == tie8[None], idx <= jcut8[None]))
        sc_ref[pl.ds(r0, blk), :] = jnp.where(sel, 0.0, -inf).reshape(blk, blk)
        return 0

    lax.fori_loop(0, nch, mask_body, 0)

    q = q_ref[0]
    lane = lax.broadcasted_iota(jnp.int32, (blk, LANES), 1)
    for h in range(N_HEADS):
        hp = h // 2
        in_head = (lane < HEAD_DIM) if h % 2 == 0 else (lane >= HEAD_DIM)
        qh_ref[h] = jnp.where(in_head, q[:, hp * LANES:(hp + 1) * LANES], jnp.zeros((), BF16))
    acc_ref[...] = jnp.zeros(acc_ref.shape, F32)

    def logits(j, h, mb, near):
        hp = h // 2
        kc = k_ref[0, pl.ds(chunk_start(j), blk), hp * LANES:(hp + 1) * LANES]
        lg = lax.dot_general(kc, qh_ref[h], NT_DIMS, preferred_element_type=F32) + mb
        return lg if near is None else lg + bt_ref[near, h]

    def att_chunk(j, carry, near):
        ms, ls = carry
        mb = sc_ref[pl.ds(chunk_start(j), blk), :]
        cm8 = []
        for h in range(N_HEADS):
            lg = logits(j, h, mb, near)
            lg_ref[h] = lg
            cm8.append(jnp.max(as_groups(lg), axis=0))
        new_ms, new_ls = [], []
        for h in range(N_HEADS):
            rows = slice(h * HEAD_DIM, (h + 1) * HEAD_DIM)
            m_new = jnp.maximum(ms[h], colmax(cm8[h]))
            m_use = jnp.where(m_new == -inf, 0.0, m_new)
            alpha = jnp.exp2(ms[h] - m_use)
            p = jnp.exp2(lg_ref[h] - m_use)
            new_ls.append(alpha * ls[h] + colsum(jnp.sum(as_groups(p), axis=0)))
            pv = jnp.dot(vt_ref[0, j, rows, :], p.astype(BF16), preferred_element_type=F32)
            acc_ref[rows, :] = alpha * acc_ref[rows, :] + pv
            new_ms.append(m_new)
        return tuple(new_ms), tuple(new_ls)

    carry = (tuple(jnp.full((1, blk), -inf, F32) for _ in range(N_HEADS)),
             tuple(jnp.zeros((1, blk), F32) for _ in range(N_HEADS)))
    carry = lax.fori_loop(0, jnp.maximum(i - 1, 0), lambda j, c: att_chunk(j, c, None), carry)
    carry = lax.cond(i >= 1, lambda c: att_chunk(i - 1, c, 1), lambda c: c, carry)
    _, ls = att_chunk(i, carry, 0)
    for h in range(N_HEADS):
        rows = slice(h * HEAD_DIM, (h + 1) * HEAD_DIM)
        acc_ref[rows, :] = acc_ref[rows, :] / ls[h]
    o_ref[0] = acc_ref[...].T.astype(BF16)


def _attention(q, qi, wit, k, vt, kia, kib, btiles, *, bsz, seq, topk):
    blk = ATT_BLOCK
    nblk = seq // blk
    return pl.pallas_call(
        functools.partial(_attn_kernel, topk=topk, seq=seq),
        grid=(bsz, nblk),
        in_specs=[
            pl.BlockSpec((1, blk, ATTN_DIM), lambda b, i: (b, i, 0)),
            pl.BlockSpec((1, blk, IDX_HEADS * IDX_DIM), lambda b, i: (b, i, 0)),
            pl.BlockSpec((1, IDX_HEADS, blk), lambda b, i: (b, 0, i)),
            pl.BlockSpec((1, seq, ATTN_DIM), lambda b, i: (b, 0, 0)),
            pl.BlockSpec((1, nblk, ATTN_DIM, blk), lambda b, i: (b, 0, 0, 0)),
            pl.BlockSpec((1, seq, LANES), lambda b, i: (b, 0, 0)),
            pl.BlockSpec((1, seq, LANES), lambda b, i: (b, 0, 0)),
            pl.BlockSpec(btiles.shape, lambda b, i: (0, 0, 0, 0)),
        ],
        out_specs=pl.BlockSpec((1, blk, ATTN_DIM), lambda b, i: (b, i, 0)),
        out_shape=jax.ShapeDtypeStruct((bsz, seq, ATTN_DIM), BF16),
        scratch_shapes=[
            pltpu.VMEM((seq, blk), F32),
            pltpu.VMEM((N_HEADS, blk, LANES), BF16),
            pltpu.VMEM((N_HEADS, blk, blk), F32),
            pltpu.VMEM((ATTN_DIM, blk), F32),
        ],
        compiler_params=pltpu.CompilerParams(dimension_semantics=("arbitrary", "arbitrary"),
                                             vmem_limit_bytes=V7X_VMEM_LIMIT_BYTES),
        name="attention",
    )(q, qi, wit, k, vt, kia, kib, btiles)


def _mix_kernel(x_ref, xh_ref, mod_ref, att_ref, wpg_ref, wpool_ref, ps_ref, wa_ref, wb_ref, wo_ref,
                lng_ref, lnb_ref, o_ref, pe_ref, mix_ref, *, alpha, mod_row, ln_row, tiles_per_seq):
    rows = x_ref.shape[0]
    i = pl.program_id(0)
    seq_tile = i % tiles_per_seq
    x = x_ref[...]
    sh = mod_ref[0, mod_row:mod_row + 1, :]
    sc = mod_ref[0, mod_row + 1:mod_row + 2, :]
    gate = mod_ref[0, mod_row + 2:mod_row + 3, :]
    u = (x * (1.0 + sc) + sh).astype(BF16)
    uh = (xh_ref[...] * (1.0 + sc) + sh).astype(BF16)
    pg = jnp.dot(u, wpg_ref[...], preferred_element_type=F32)
    ph = jnp.dot(uh, wpg_ref[:, :POOL_DIM], preferred_element_type=F32)
    pe_ref[0:POOL_HALO, :] = jnp.where(seq_tile == 0, 0.0, ph)
    pe_ref[POOL_HALO:, :] = pg[:, :POOL_DIM]
    t = seq_tile * rows + lax.broadcasted_iota(jnp.int32, (rows, 1), 0)
    for g, w in enumerate(POOL_WINDOWS):
        cols = slice(g * POOL_GROUP_DIM, (g + 1) * POOL_GROUP_DIM)
        cur = pe_ref[POOL_HALO:, cols]
        win = cur
        for back in range(1, w):
            win = win + pe_ref[POOL_HALO - back:POOL_HALO - back + rows, cols]
        cnt = jnp.minimum(t + 1, w).astype(F32)
        pooled = (win / cnt - cur).astype(BF16)
        mixed = jnp.dot(pooled, wpool_ref[g], preferred_element_type=F32)
        mix_ref[:, cols] = (mixed * ps_ref[:, cols]).astype(BF16)
    y_a = jnp.dot(mix_ref[...], wa_ref[...], preferred_element_type=F32)
    y_b = jnp.dot(att_ref[...], wb_ref[...], preferred_element_type=F32)
    d = x.shape[1]
    ga = pg[:, POOL_DIM:POOL_DIM + d]
    gb = pg[:, POOL_DIM + d:]
    merged = (jax.nn.sigmoid(ga) * y_a + jax.nn.sigmoid(gb) * y_b).astype(BF16)
    y = jnp.dot(merged, wo_ref[...], preferred_element_type=F32)
    z = alpha * x + gate * y
    o_ref[...] = _layer_norm(z, lng_ref[ln_row:ln_row + 1, :], lnb_ref[ln_row:ln_row + 1, :])


def _mix_out(x2d, mod, att2d, wpg, wpool, pool_scale, wa, wb, wo, ln_g, ln_b, *, seq, alpha, mod_row, ln_row):
    n, d = x2d.shape
    tiles_per_seq = seq // MIX_ROWS
    halo_blocks = MIX_ROWS // POOL_HALO
    resident = dict(pipeline_mode=pl.Buffered(1))
    full = lambda a: pl.BlockSpec(a.shape, lambda i: (0,) * a.ndim, **resident)
    return pl.pallas_call(
        functools.partial(_mix_kernel, alpha=alpha, mod_row=mod_row, ln_row=ln_row, tiles_per_seq=tiles_per_seq),
        grid=(n // MIX_ROWS,),
        in_specs=[
            pl.BlockSpec((MIX_ROWS, d), lambda i: (i, 0)),
            pl.BlockSpec((POOL_HALO, d), lambda i: (jnp.maximum(i * halo_blocks - 1, 0), 0)),
            pl.BlockSpec((1, N_ADA, d), lambda i: (i // tiles_per_seq, 0, 0)),
            pl.BlockSpec((MIX_ROWS, ATTN_DIM), lambda i: (i, 0)),
            full(wpg), full(wpool), full(pool_scale), full(wa), full(wb), full(wo),
            pl.BlockSpec(ln_g.shape, lambda i: (0, 0)),
            pl.BlockSpec(ln_b.shape, lambda i: (0, 0)),
        ],
        out_specs=pl.BlockSpec((MIX_ROWS, d), lambda i: (i, 0)),
        out_shape=jax.ShapeDtypeStruct((n, d), F32),
        scratch_shapes=[
            pltpu.VMEM((POOL_HALO + MIX_ROWS, POOL_DIM), F32),
            pltpu.VMEM((MIX_ROWS, POOL_DIM), BF16),
        ],
        compiler_params=pltpu.CompilerParams(dimension_semantics=("arbitrary",),
                                             vmem_limit_bytes=V7X_VMEM_LIMIT_BYTES),
        name="mix_out",
    )(x2d, x2d, mod, att2d, wpg, wpool, pool_scale, wa, wb, wo, ln_g, ln_b)


def kernel(x, c, w_ada, b_ada, ln_g, ln_b, ffn1_w_gate, ffn1_w_up, ffn1_w_down, w_in, w_pool, pool_scale,
           w_a, w_b, w_out, rel_bias, ffn2_w_gate, ffn2_w_up, ffn2_w_down):
    bsz, seq, d = x.shape
    depth = w_ada.shape[0]
    alpha = (2.0 * depth) ** 0.25
    topk = min(TOP_K, seq // 4)
    assert seq % FFN_ROWS == 0 and seq % PROJ_ROWS == 0 and seq % MIX_ROWS == 0 and seq % ATT_BLOCK == 0
    assert PROJ_ROWS % ATT_BLOCK == 0 and MIX_ROWS % POOL_HALO == 0 and POOL_HALO >= max(POOL_WINDOWS) - 1
    far_bucket = _far_bucket(ATT_BLOCK + 1, max(seq - 1, ATT_BLOCK + 1))

    o_q = POOL_DIM
    o_k = o_q + ATTN_DIM
    o_v = o_k + ATTN_DIM
    o_qi = o_v + ATTN_DIM
    o_ki = o_qi + IDX_HEADS * IDX_DIM
    o_wi = o_ki + IDX_DIM
    o_ga = o_wi + IDX_HEADS

    btiles = _rel_bias_tiles(rel_bias, far_bucket)
    x2d = x.reshape(bsz * seq, d)
    for l in range(depth):
        wl = w_in[l]
        zeros_ki = jnp.zeros((d, LANES - IDX_DIM), wl.dtype)
        w_ki = wl[:, o_ki:o_wi]
        wqk = jnp.concatenate([wl[:, o_q:o_v], wl[:, o_qi:o_ki]], axis=1).astype(BF16)
        wv_t = wl[:, o_v:o_qi].T.astype(BF16)
        wki = jnp.concatenate([w_ki, zeros_ki, zeros_ki, w_ki], axis=1).astype(BF16)
        wwi_t = jnp.pad(wl[:, o_wi:o_ga].T, ((0, 2 * SUBLANES - IDX_HEADS), (0, 0))).astype(BF16)
        wpg = jnp.concatenate([wl[:, :POOL_DIM], wl[:, o_ga:]], axis=1).astype(BF16)

        mod = _ada_mod(c, w_ada[l], b_ada[l]).reshape(bsz, N_ADA, d)
        x2d = _ffn(x2d, mod, ln_g[l], ln_b[l], ffn1_w_gate[l].astype(BF16), ffn1_w_up[l].astype(BF16),
                   ffn1_w_down[l].astype(BF16), seq=seq, alpha=alpha, mod_row=0, ln_row=0)
        q, k, qi, vt, kia, kib, wit = _attn_proj(x2d, mod, wqk, wv_t, wki, wwi_t, bsz=bsz, seq=seq, mod_row=3)
        att = _attention(q.reshape(bsz, seq, ATTN_DIM), qi.reshape(bsz, seq, IDX_HEADS * IDX_DIM), wit,
                         k.reshape(bsz, seq, ATTN_DIM), vt, kia.reshape(bsz, seq, LANES),
                         kib.reshape(bsz, seq, LANES), btiles, bsz=bsz, seq=seq, topk=topk)
        x2d = _mix_out(x2d, mod, att.reshape(bsz * seq, ATTN_DIM), wpg, w_pool[l].astype(BF16),
                       pool_scale[l].reshape(1, POOL_DIM), w_a[l].astype(BF16), w_b[l].astype(BF16),
                       w_out[l].astype(BF16), ln_g[l], ln_b[l], seq=seq, alpha=alpha, mod_row=3, ln_row=1)
        x2d = _ffn(x2d, mod, ln_g[l], ln_b[l], ffn2_w_gate[l].astype(BF16), ffn2_w_up[l].astype(BF16),
                   ffn2_w_down[l].astype(BF16), seq=seq, alpha=alpha, mod_row=6, ln_row=2)
    return x2d.reshape(bsz, seq, d)
```

```python
import functools
import math

import numpy as np
import jax
import jax.numpy as jnp
from jax import lax
from jax.experimental import pallas as pl
from jax.experimental.pallas import tpu as pltpu

POOL_WINDOWS = (2, 4, 8, 16)
POOL_GROUP_DIM = 128
POOL_DIM = len(POOL_WINDOWS) * POOL_GROUP_DIM
N_HEADS = 8
HEAD_DIM = 64
ATTN_DIM = N_HEADS * HEAD_DIM
IDX_HEADS = 8
IDX_DIM = 64
TOP_K = 256
REL_BUCKETS = 32
REL_MAX_DIST = 128
N_ADA = 9
LN_EPS = 1e-5
POOL_HALO = 16

LANES = 128
SUBLANES = 8
V7X_VMEM_LIMIT_BYTES = 56 * 1024 * 1024
FFN_ROWS = 512
FFN_COLS = 256
PROJ_ROWS = 512
MIX_ROWS = 256
ATT_BLOCK = 256
ADA_COLS = 1024
COUNT_CHAINS = 4
COUNT_ROWS = 64
STEPS_PER_ROUND = 4
FAST_ROUNDS = 7

LOG2E = math.log2(math.e)
BF16 = jnp.bfloat16
F32 = jnp.float32
NT_DIMS = (((1,), (1,)), ((), ()))


def _layer_norm(z, g, b):
    mu = jnp.mean(z, axis=-1, keepdims=True)
    zc = z - mu
    var = jnp.mean(zc * zc, axis=-1, keepdims=True)
    return zc * lax.rsqrt(var + LN_EPS) * g + b


def _silu(a):
    return a * jax.nn.sigmoid(a)


def _ada_kernel(c_ref, w_ref, b_ref, o_ref):
    a = _silu(c_ref[...])
    o_ref[...] = jnp.dot(a, w_ref[...], preferred_element_type=F32) + b_ref[...]


def _ada_mod(c, w_ada, b_ada):
    bsz, d = c.shape
    n = w_ada.shape[1]
    rows = -(-bsz // SUBLANES) * SUBLANES
    c_pad = jnp.pad(c, ((0, rows - bsz), (0, 0)))
    out = pl.pallas_call(
        _ada_kernel,
        grid=(n // ADA_COLS,),
        in_specs=[
            pl.BlockSpec((rows, d), lambda j: (0, 0)),
            pl.BlockSpec((d, ADA_COLS), lambda j: (0, j)),
            pl.BlockSpec((1, ADA_COLS), lambda j: (0, j)),
        ],
        out_specs=pl.BlockSpec((rows, ADA_COLS), lambda j: (0, j)),
        out_shape=jax.ShapeDtypeStruct((rows, n), F32),
        compiler_params=pltpu.CompilerParams(dimension_semantics=("arbitrary",),
                                             vmem_limit_bytes=V7X_VMEM_LIMIT_BYTES),
        name="ada_mod",
    )(c_pad, w_ada, b_ada.reshape(1, n))
    return out[:bsz]


def _ffn_kernel(x_ref, mod_ref, lng_ref, lnb_ref, wg_ref, wu_ref, wd_ref, o_ref, h_ref, *, alpha, mod_row, ln_row):
    x = x_ref[...]
    sh = mod_ref[0, mod_row:mod_row + 1, :]
    sc = mod_ref[0, mod_row + 1:mod_row + 2, :]
    gate = mod_ref[0, mod_row + 2:mod_row + 3, :]
    u = (x * (1.0 + sc) + sh).astype(BF16)
    d_ff = wg_ref.shape[1]
    for c in range(d_ff // FFN_COLS):
        sl = slice(c * FFN_COLS, (c + 1) * FFN_COLS)
        a = jnp.dot(u, wg_ref[:, sl], preferred_element_type=F32)
        b = jnp.dot(u, wu_ref[:, sl], preferred_element_type=F32)
        h_ref[:, sl] = (_silu(a) * b).astype(BF16)
    y = jnp.dot(h_ref[...], wd_ref[...], preferred_element_type=F32)
    z = alpha * x + (0.5 * gate) * y
    o_ref[...] = _layer_norm(z, lng_ref[ln_row:ln_row + 1, :], lnb_ref[ln_row:ln_row + 1, :])


def _ffn(x2d, mod, ln_g, ln_b, wg, wu, wd, *, seq, alpha, mod_row, ln_row):
    n, d = x2d.shape
    d_ff = wg.shape[1]
    tiles_per_seq = seq // FFN_ROWS
    resident = dict(pipeline_mode=pl.Buffered(1))
    return pl.pallas_call(
        functools.partial(_ffn_kernel, alpha=alpha, mod_row=mod_row, ln_row=ln_row),
        grid=(n // FFN_ROWS,),
        in_specs=[
            pl.BlockSpec((FFN_ROWS, d), lambda i: (i, 0)),
            pl.BlockSpec((1, N_ADA, d), lambda i: (i // tiles_per_seq, 0, 0)),
            pl.BlockSpec(ln_g.shape, lambda i: (0, 0)),
            pl.BlockSpec(ln_b.shape, lambda i: (0, 0)),
            pl.BlockSpec((d, d_ff), lambda i: (0, 0), **resident),
            pl.BlockSpec((d, d_ff), lambda i: (0, 0), **resident),
            pl.BlockSpec((d_ff, d), lambda i: (0, 0), **resident),
        ],
        out_specs=pl.BlockSpec((FFN_ROWS, d), lambda i: (i, 0)),
        out_shape=jax.ShapeDtypeStruct((n, d), F32),
        scratch_shapes=[pltpu.VMEM((FFN_ROWS, d_ff), BF16)],
        compiler_params=pltpu.CompilerParams(dimension_semantics=("arbitrary",),
                                             vmem_limit_bytes=V7X_VMEM_LIMIT_BYTES),
        name="ffn",
    )(x2d, mod, ln_g, ln_b, wg, wu, wd)


def _proj_kernel(x_ref, mod_ref, wqk_ref, wv_ref, wki_ref, wwi_ref,
                 q_ref, k_ref, qi_ref, vt_ref, kia_ref, kib_ref, wit_ref, *, mod_row):
    x = x_ref[...]
    sh = mod_ref[0, mod_row:mod_row + 1, :]
    sc = mod_ref[0, mod_row + 1:mod_row + 2, :]
    u = (x * (1.0 + sc) + sh).astype(BF16)
    qkq = jnp.dot(u, wqk_ref[...], preferred_element_type=F32)
    q_ref[...] = (qkq[:, :ATTN_DIM] * (HEAD_DIM ** -0.5 * LOG2E)).astype(BF16)
    k_ref[...] = qkq[:, ATTN_DIM:2 * ATTN_DIM].astype(BF16)
    qi_ref[...] = qkq[:, 2 * ATTN_DIM:].astype(BF16)
    kk = jnp.dot(u, wki_ref[...], preferred_element_type=F32)
    kia_ref[...] = kk[:, :LANES].astype(BF16)
    kib_ref[...] = kk[:, LANES:].astype(BF16)
    vt = lax.dot_general(wv_ref[...], u, NT_DIMS, preferred_element_type=F32).astype(BF16)
    for c in range(vt_ref.shape[1]):
        vt_ref[0, c] = vt[:, c * ATT_BLOCK:(c + 1) * ATT_BLOCK]
    wit = lax.dot_general(wwi_ref[...], u, NT_DIMS, preferred_element_type=F32)
    wit_ref[0] = wit[:IDX_HEADS, :]


def _attn_proj(x2d, mod, wqk, wv_t, wki, wwi_t, *, bsz, seq, mod_row):
    n, d = x2d.shape
    tiles_per_seq = seq // PROJ_ROWS
    chunks_per_tile = PROJ_ROWS // ATT_BLOCK
    resident = dict(pipeline_mode=pl.Buffered(1))
    row_spec = lambda cols: pl.BlockSpec((PROJ_ROWS, cols), lambda i: (i, 0))
    return pl.pallas_call(
        functools.partial(_proj_kernel, mod_row=mod_row),
        grid=(n // PROJ_ROWS,),
        in_specs=[
            pl.BlockSpec((PROJ_ROWS, d), lambda i: (i, 0)),
            pl.BlockSpec((1, N_ADA, d), lambda i: (i // tiles_per_seq, 0, 0)),
            pl.BlockSpec(wqk.shape, lambda i: (0, 0), **resident),
            pl.BlockSpec(wv_t.shape, lambda i: (0, 0), **resident),
            pl.BlockSpec(wki.shape, lambda i: (0, 0), **resident),
            pl.BlockSpec(wwi_t.shape, lambda i: (0, 0), **resident),
        ],
        out_specs=[
            row_spec(ATTN_DIM), row_spec(ATTN_DIM), row_spec(IDX_HEADS * IDX_DIM),
            pl.BlockSpec((1, chunks_per_tile, ATTN_DIM, ATT_BLOCK),
                         lambda i: (i // tiles_per_seq, i % tiles_per_seq, 0, 0)),
            row_spec(LANES), row_spec(LANES),
            pl.BlockSpec((1, IDX_HEADS, PROJ_ROWS), lambda i: (i // tiles_per_seq, 0, i % tiles_per_seq)),
        ],
        out_shape=[
            jax.ShapeDtypeStruct((n, ATTN_DIM), BF16),
            jax.ShapeDtypeStruct((n, ATTN_DIM), BF16),
            jax.ShapeDtypeStruct((n, IDX_HEADS * IDX_DIM), BF16),
            jax.ShapeDtypeStruct((bsz, seq // ATT_BLOCK, ATTN_DIM, ATT_BLOCK), BF16),
            jax.ShapeDtypeStruct((n, LANES), BF16),
            jax.ShapeDtypeStruct((n, LANES), BF16),
            jax.ShapeDtypeStruct((bsz, IDX_HEADS, seq), F32),
        ],
        compiler_params=pltpu.CompilerParams(dimension_semantics=("arbitrary",),
                                             vmem_limit_bytes=V7X_VMEM_LIMIT_BYTES),
        name="attn_proj",
    )(x2d, mod, wqk, wv_t, wki, wwi_t)


def _t5_bucket(n):
    max_exact = REL_BUCKETS // 2
    nf = jnp.maximum(n, 1).astype(F32)
    large = max_exact + (jnp.log(nf / max_exact) / math.log(REL_MAX_DIST / max_exact)
                         * (REL_BUCKETS - max_exact)).astype(jnp.int32)
    large = jnp.minimum(large, REL_BUCKETS - 1)
    return jnp.where(n < max_exact, n, large)


def _far_bucket(first_dist, last_dist):
    n = np.arange(first_dist, last_dist + 1, dtype=np.float32)
    max_exact = REL_BUCKETS // 2
    large = max_exact + (np.log(n / np.float32(max_exact)) / np.float32(math.log(REL_MAX_DIST / max_exact))
                         * np.float32(REL_BUCKETS - max_exact)).astype(np.int32)
    buckets = np.where(n < max_exact, n.astype(np.int32), np.minimum(large, REL_BUCKETS - 1))
    assert buckets.min() == buckets.max(), "key chunks two or more blocks away must share one bias bucket"
    return int(buckets[0])


def _bias_kernel(rb_ref, o_ref, *, far_bucket):
    o = pl.program_id(0)
    h = pl.program_id(1)
    row = lax.broadcasted_iota(jnp.int32, (ATT_BLOCK, ATT_BLOCK), 0)
    col = lax.broadcasted_iota(jnp.int32, (ATT_BLOCK, ATT_BLOCK), 1)
    dist = o * ATT_BLOCK + col - row
    bucket = _t5_bucket(jnp.maximum(dist, 0))
    tile = jnp.zeros((ATT_BLOCK, ATT_BLOCK), F32)
    for b in range(REL_BUCKETS):
        tile = jnp.where(bucket == b, rb_ref[b, h], tile)
    o_ref[0, 0] = (tile - rb_ref[far_bucket, h]) * LOG2E


def _rel_bias_tiles(rel_bias, far_bucket):
    return pl.pallas_call(
        functools.partial(_bias_kernel, far_bucket=far_bucket),
        grid=(2, N_HEADS),
        in_specs=[pl.BlockSpec(memory_space=pltpu.SMEM)],
        out_specs=pl.BlockSpec((1, 1, ATT_BLOCK, ATT_BLOCK), lambda o, h: (o, h, 0, 0)),
        out_shape=jax.ShapeDtypeStruct((2, N_HEADS, ATT_BLOCK, ATT_BLOCK), F32),
        compiler_params=pltpu.CompilerParams(dimension_semantics=("arbitrary", "arbitrary")),
        name="rel_bias",
    )(rel_bias)


def _attn_kernel(q_ref, qi_ref, wit_ref, k_ref, vt_ref, kia_ref, kib_ref, bt_ref, o_ref,
                 sc_ref, qh_ref, lg_ref, acc_ref, *, topk, seq):
    blk = ATT_BLOCK
    groups = blk // SUBLANES
    i = pl.program_id(1)
    nch = i + 1
    kf = float(topk)
    inf = jnp.inf

    def chunk_start(j):
        return pl.multiple_of(j * blk, blk)

    def as_groups(x):
        return x.reshape(groups, SUBLANES, blk)

    def lanes8(v):
        return jnp.broadcast_to(v, (SUBLANES, blk))

    def colmin(x8):
        return jnp.min(x8, axis=0, keepdims=True)

    def colmax(x8):
        return jnp.max(x8, axis=0, keepdims=True)

    def colsum(x8):
        return jnp.sum(x8, axis=0, keepdims=True)

    wf = wit_ref[0] * (IDX_DIM ** -0.5)
    qi = qi_ref[0]
    row = lax.broadcasted_iota(jnp.int32, (blk, blk), 0)
    col = lax.broadcasted_iota(jnp.int32, (blk, blk), 1)
    causal = row <= col

    def chunk_scores(j):
        r0 = chunk_start(j)
        ka = kia_ref[0, pl.ds(r0, blk), :]
        kb = kib_ref[0, pl.ds(r0, blk), :]
        s = jnp.zeros((blk, blk), F32)
        for hp in range(IDX_HEADS // 2):
            qp = qi[:, hp * LANES:(hp + 1) * LANES]
            a0 = lax.dot_general(ka, qp, NT_DIMS, preferred_element_type=F32)
            a1 = lax.dot_general(kb, qp, NT_DIMS, preferred_element_type=F32)
            s = s + jnp.maximum(a0, 0.0) * wf[2 * hp:2 * hp + 1, :]
            s = s + jnp.maximum(a1, 0.0) * wf[2 * hp + 1:2 * hp + 2, :]
        return r0, s * (IDX_HEADS ** -0.5)

    def score_stats(s_lo, s_hi, stats):
        mn8, mx8, pos8, zer8 = stats
        hi3 = as_groups(s_hi)
        return (jnp.minimum(mn8, jnp.min(as_groups(s_lo), axis=0)),
                jnp.maximum(mx8, jnp.max(hi3, axis=0)),
                pos8 + jnp.sum(jnp.where(hi3 > 0.0, 1.0, 0.0), axis=0),
                zer8 + jnp.sum(jnp.where(hi3 == 0.0, 1.0, 0.0), axis=0))

    def score_body(j, stats):
        r0, s = chunk_scores(j)
        sc_ref[pl.ds(r0, blk), :] = s
        return score_stats(s, s, stats)

    zeros8 = jnp.zeros((SUBLANES, blk), F32)
    stats = lax.fori_loop(0, i, score_body,
                          (jnp.full((SUBLANES, blk), inf, F32), jnp.full((SUBLANES, blk), -inf, F32), zeros8, zeros8))
    r_diag, s_diag = chunk_scores(i)
    s_diag_hi = jnp.where(causal, s_diag, -inf)
    sc_ref[pl.ds(r_diag, blk), :] = s_diag_hi
    mn8, mx8, pos8, zer8 = score_stats(jnp.where(causal, s_diag, inf), s_diag_hi, stats)

    @pl.when(nch % 2 == 1)
    def _():
        sc_ref[pl.ds(chunk_start(nch), blk), :] = jnp.full((blk, blk), -inf, F32)

    npair = (nch + 1) // 2

    def count_gt(thr):
        t8 = lanes8(thr)

        def body(jp, accs):
            base = pl.multiple_of(jp * 2 * blk, 2 * blk)
            accs = list(accs)
            for s in range(2 * blk // COUNT_ROWS):
                x = sc_ref[pl.ds(base + s * COUNT_ROWS, COUNT_ROWS), :]
                for g in range(COUNT_ROWS // SUBLANES):
                    hit = jnp.where(x[g * SUBLANES:(g + 1) * SUBLANES] > t8, 1.0, 0.0)
                    accs[g % COUNT_CHAINS] = accs[g % COUNT_CHAINS] + hit
            return tuple(accs)

        accs = lax.fori_loop(0, npair, body, tuple(jnp.zeros((SUBLANES, blk), F32) for _ in range(COUNT_CHAINS)))
        return colsum(sum(accs[1:], accs[0]))

    def bisect(lo, hi, low, done):
        mid = 0.5 * lo + 0.5 * hi
        c = count_gt(mid)
        live = done < 0.5
        up = jnp.logical_and(live, c >= kf)
        down = jnp.logical_and(live, c < kf)
        low = jnp.where(up, mid, low)
        lo = jnp.where(up, mid, lo)
        hi = jnp.where(down, mid, hi)
        done = jnp.where(c == kf, 1.0, done)
        return lo, hi, low, done

    tpos = i * blk + lax.broadcasted_iota(jnp.int32, (1, blk), 1)
    pos, zer = colsum(pos8), colsum(zer8)
    zero_tie = jnp.logical_and(pos < kf, pos + zer >= kf)
    done0 = jnp.where(jnp.logical_or(tpos + 1 <= topk, zero_tie), 1.0, 0.0)
    low0 = jnp.where(zero_tie, 0.0, -inf)
    tie0 = jnp.where(zero_tie, 0.0, inf)
    need0 = jnp.where(zero_tie, kf - pos, 0.0)

    def fast_cond(st):
        it, _, _, _, done = st
        return jnp.logical_and(it < FAST_ROUNDS, jnp.min(done) < 0.5)

    def fast_body(st):
        it, lo, hi, low, done = st
        for _ in range(STEPS_PER_ROUND):
            lo, hi, low, done = bisect(lo, hi, low, done)
        return it + 1, lo, hi, low, done

    _, lo, hi, low, done = lax.while_loop(fast_cond, fast_body,
                                          (jnp.int32(0), colmin(mn8), colmax(mx8), low0, done0))

    def next_value_above(thr):
        t8 = lanes8(thr)

        def body(j, acc):
            x = as_groups(sc_ref[pl.ds(chunk_start(j), blk), :])
            return jnp.minimum(acc, jnp.min(jnp.where(x > t8[None], x, inf), axis=0))

        return colmin(lax.fori_loop(0, nch, body, jnp.full((SUBLANES, blk), inf, F32)))

    def slow_cond(st):
        return jnp.min(st[3]) < 0.5

    def slow_body(st):
        lo, hi, low, done, tie, need = st
        live = done < 0.5
        cand = next_value_above(low)
        cgt = count_gt(cand)
        found = jnp.logical_and(live, cgt < kf)
        tie = jnp.where(found, cand, tie)
        need = jnp.where(found, kf - cgt, need)
        climb = jnp.logical_and(live, cgt >= kf)
        low = jnp.where(jnp.logical_or(found, climb), cand, low)
        lo = jnp.where(climb, jnp.maximum(lo, cand), lo)
        done = jnp.where(jnp.logical_or(found, jnp.logical_and(live, cgt == kf)), 1.0, done)
        lo, hi, low, done = bisect(lo, hi, low, done)
        return lo, hi, low, done, tie, need

    def slow_path(args):
        lo, hi, low, done = args
        _, _, low, _, tie, need = lax.while_loop(slow_cond, slow_body, (lo, hi, low, done, tie0, need0))
        return low, tie, need

    low, tie, need = lax.cond(jnp.min(done) < 0.5, slow_path, lambda a: (a[2], tie0, need0), (lo, hi, low, done))

    def plain_mask(_):
        low8 = lanes8(low)

        def body(j, _):
            r0 = chunk_start(j)
            x = as_groups(sc_ref[pl.ds(r0, blk), :])
            sc_ref[pl.ds(r0, blk), :] = jnp.where(x > low8[None], 0.0, -inf).reshape(blk, blk)
            return 0

        return lax.fori_loop(0, nch, body, 0)

    def tie_mask(_):
        tri = jnp.where(row >= col, 1.0, 0.0).astype(BF16)

        def body(j, base):
            r0 = chunk_start(j)
            x = sc_ref[pl.ds(r0, blk), :]
            eq = x == tie
            rank = jnp.dot(tri, jnp.where(eq, 1.0, 0.0).astype(BF16), preferred_element_type=F32)
            sel = jnp.logical_or(x > low, jnp.logical_and(eq, rank + base <= need))
            sc_ref[pl.ds(r0, blk), :] = jnp.where(sel, 0.0, -inf)
            return base + rank[blk - 1:blk, :]

        lax.fori_loop(0, nch, body, jnp.zeros((1, blk), F32))
        return 0

    lax.cond(jnp.max(jnp.where(tie < inf, 1.0, 0.0)) > 0.5, tie_mask, plain_mask, 0)

    q = q_ref[0]
    lane = lax.broadcasted_iota(jnp.int32, (blk, LANES), 1)
    for h in range(N_HEADS):
        hp = h // 2
        in_head = (lane < HEAD_DIM) if h % 2 == 0 else (lane >= HEAD_DIM)
        qh_ref[h] = jnp.where(in_head, q[:, hp * LANES:(hp + 1) * LANES], jnp.zeros((), BF16))
    acc_ref[...] = jnp.zeros(acc_ref.shape, F32)

    def logits(j, h, mb, near):
        hp = h // 2
        kc = k_ref[0, pl.ds(chunk_start(j), blk), hp * LANES:(hp + 1) * LANES]
        lg = lax.dot_general(kc, qh_ref[h], NT_DIMS, preferred_element_type=F32) + mb
        return lg if near is None else lg + bt_ref[near, h]

    def att_chunk(j, carry, near):
        ms, ls = carry
        mb = sc_ref[pl.ds(chunk_start(j), blk), :]
        cm8 = []
        for h in range(N_HEADS):
            lg = logits(j, h, mb, near)
            lg_ref[h] = lg
            cm8.append(jnp.max(as_groups(lg), axis=0))
        new_ms, new_ls = [], []
        for h in range(N_HEADS):
            rows = slice(h * HEAD_DIM, (h + 1) * HEAD_DIM)
            m_new = jnp.maximum(ms[h], colmax(cm8[h]))
            m_use = jnp.where(m_new == -inf, 0.0, m_new)
            alpha = jnp.exp2(ms[h] - m_use)
            p = jnp.exp2(lg_ref[h] - m_use)
            new_ls.append(alpha * ls[h] + colsum(jnp.sum(as_groups(p), axis=0)))
            pv = jnp.dot(vt_ref[0, j, rows, :], p.astype(BF16), preferred_element_type=F32)
            acc_ref[rows, :] = alpha * acc_ref[rows, :] + pv
            new_ms.append(m_new)
        return tuple(new_ms), tuple(new_ls)

    carry = (tuple(jnp.full((1, blk), -inf, F32) for _ in range(N_HEADS)),
             tuple(jnp.zeros((1, blk), F32) for _ in range(N_HEADS)))
    carry = lax.fori_loop(0, jnp.maximum(i - 1, 0), lambda j, c: att_chunk(j, c, None), carry)
    carry = lax.cond(i >= 1, lambda c: att_chunk(i - 1, c, 1), lambda c: c, carry)
    _, ls = att_chunk(i, carry, 0)
    for h in range(N_HEADS):
        rows = slice(h * HEAD_DIM, (h + 1) * HEAD_DIM)
        acc_ref[rows, :] = acc_ref[rows, :] / ls[h]
    o_ref[0] = acc_ref[...].T.astype(BF16)


def _attention(q, qi, wit, k, vt, kia, kib, btiles, *, bsz, seq, topk):
    blk = ATT_BLOCK
    nblk = seq // blk
    return pl.pallas_call(
        functools.partial(_attn_kernel, topk=topk, seq=seq),
        grid=(bsz, nblk),
        in_specs=[
            pl.BlockSpec((1, blk, ATTN_DIM), lambda b, i: (b, i, 0)),
            pl.BlockSpec((1, blk, IDX_HEADS * IDX_DIM), lambda b, i: (b, i, 0)),
            pl.BlockSpec((1, IDX_HEADS, blk), lambda b, i: (b, 0, i)),
            pl.BlockSpec((1, seq, ATTN_DIM), lambda b, i: (b, 0, 0)),
            pl.BlockSpec((1, nblk, ATTN_DIM, blk), lambda b, i: (b, 0, 0, 0)),
            pl.BlockSpec((1, seq, LANES), lambda b, i: (b, 0, 0)),
            pl.BlockSpec((1, seq, LANES), lambda b, i: (b, 0, 0)),
            pl.BlockSpec(btiles.shape, lambda b, i: (0, 0, 0, 0)),
        ],
        out_specs=pl.BlockSpec((1, blk, ATTN_DIM), lambda b, i: (b, i, 0)),
        out_shape=jax.ShapeDtypeStruct((bsz, seq, ATTN_DIM), BF16),
        scratch_shapes=[
            pltpu.VMEM((seq + blk, blk), F32),
            pltpu.VMEM((N_HEADS, blk, LANES), BF16),
            pltpu.VMEM((N_HEADS, blk, blk), F32),
            pltpu.VMEM((ATTN_DIM, blk), F32),
        ],
        compiler_params=pltpu.CompilerParams(dimension_semantics=("arbitrary", "arbitrary"),
                                             vmem_limit_bytes=V7X_VMEM_LIMIT_BYTES),
        name="attention",
    )(q, qi, wit, k, vt, kia, kib, btiles)


def _mix_kernel(x_ref, xh_ref, mod_ref, att_ref, wpg_ref, wpool_ref, ps_ref, wa_ref, wb_ref, wo_ref,
                lng_ref, lnb_ref, o_ref, pe_ref, mix_ref, *, alpha, mod_row, ln_row, tiles_per_seq):
    rows = x_ref.shape[0]
    i = pl.program_id(0)
    seq_tile = i % tiles_per_seq
    x = x_ref[...]
    sh = mod_ref[0, mod_row:mod_row + 1, :]
    sc = mod_ref[0, mod_row + 1:mod_row + 2, :]
    gate = mod_ref[0, mod_row + 2:mod_row + 3, :]
    u = (x * (1.0 + sc) + sh).astype(BF16)
    uh = (xh_ref[...] * (1.0 + sc) + sh).astype(BF16)
    pg = jnp.dot(u, wpg_ref[...], preferred_element_type=F32)
    ph = jnp.dot(uh, wpg_ref[:, :POOL_DIM], preferred_element_type=F32)
    pe_ref[0:POOL_HALO, :] = jnp.where(seq_tile == 0, 0.0, ph)
    pe_ref[POOL_HALO:, :] = pg[:, :POOL_DIM]
    t = seq_tile * rows + lax.broadcasted_iota(jnp.int32, (rows, 1), 0)
    for g, w in enumerate(POOL_WINDOWS):
        cols = slice(g * POOL_GROUP_DIM, (g + 1) * POOL_GROUP_DIM)
        cur = pe_ref[POOL_HALO:, cols]
        win = cur
        for back in range(1, w):
            win = win + pe_ref[POOL_HALO - back:POOL_HALO - back + rows, cols]
        cnt = jnp.minimum(t + 1, w).astype(F32)
        pooled = (win / cnt - cur).astype(BF16)
        mixed = jnp.dot(pooled, wpool_ref[g], preferred_element_type=F32)
        mix_ref[:, cols] = (mixed * ps_ref[:, cols]).astype(BF16)
    y_a = jnp.dot(mix_ref[...], wa_ref[...], preferred_element_type=F32)
    y_b = jnp.dot(att_ref[...], wb_ref[...], preferred_element_type=F32)
    d = x.shape[1]
    ga = pg[:, POOL_DIM:POOL_DIM + d]
    gb = pg[:, POOL_DIM + d:]
    merged = (jax.nn.sigmoid(ga) * y_a + jax.nn.sigmoid(gb) * y_b).astype(BF16)
    y = jnp.dot(merged, wo_ref[...], preferred_element_type=F32)
    z = alpha * x + gate * y
    o_ref[...] = _layer_norm(z, lng_ref[ln_row:ln_row + 1, :], lnb_ref[ln_row:ln_row + 1, :])


def _mix_out(x2d, mod, att2d, wpg, wpool, pool_scale, wa, wb, wo, ln_g, ln_b, *, seq, alpha, mod_row, ln_row):
    n, d = x2d.shape
    tiles_per_seq = seq // MIX_ROWS
    halo_blocks = MIX_ROWS // POOL_HALO
    resident = dict(pipeline_mode=pl.Buffered(1))
    full = lambda a: pl.BlockSpec(a.shape, lambda i: (0,) * a.ndim, **resident)
    return pl.pallas_call(
        functools.partial(_mix_kernel, alpha=alpha, mod_row=mod_row, ln_row=ln_row, tiles_per_seq=tiles_per_seq),
        grid=(n // MIX_ROWS,),
        in_specs=[
            pl.BlockSpec((MIX_ROWS, d), lambda i: (i, 0)),
            pl.BlockSpec((POOL_HALO, d), lambda i: (jnp.maximum(i * halo_blocks - 1, 0), 0)),
            pl.BlockSpec((1, N_ADA, d), lambda i: (i // tiles_per_seq, 0, 0)),
            pl.BlockSpec((MIX_ROWS, ATTN_DIM), lambda i: (i, 0)),
            full(wpg), full(wpool), full(pool_scale), full(wa), full(wb), full(wo),
            pl.BlockSpec(ln_g.shape, lambda i: (0, 0)),
            pl.BlockSpec(ln_b.shape, lambda i: (0, 0)),
        ],
        out_specs=pl.BlockSpec((MIX_ROWS, d), lambda i: (i, 0)),
        out_shape=jax.ShapeDtypeStruct((n, d), F32),
        scratch_shapes=[
            pltpu.VMEM((POOL_HALO + MIX_ROWS, POOL_DIM), F32),
            pltpu.VMEM((MIX_ROWS, POOL_DIM), BF16),
        ],
        compiler_params=pltpu.CompilerParams(dimension_semantics=("arbitrary",),
                                             vmem_limit_bytes=V7X_VMEM_LIMIT_BYTES),
        name="mix_out",
    )(x2d, x2d, mod, att2d, wpg, wpool, pool_scale, wa, wb, wo, ln_g, ln_b)


def kernel(x, c, w_ada, b_ada, ln_g, ln_b, ffn1_w_gate, ffn1_w_up, ffn1_w_down, w_in, w_pool, pool_scale,
           w_a, w_b, w_out, rel_bias, ffn2_w_gate, ffn2_w_up, ffn2_w_down):
    bsz, seq, d = x.shape
    depth = w_ada.shape[0]
    alpha = (2.0 * depth) ** 0.25
    topk = min(TOP_K, seq // 4)
    assert seq % FFN_ROWS == 0 and seq % PROJ_ROWS == 0 and seq % MIX_ROWS == 0 and seq % ATT_BLOCK == 0
    assert PROJ_ROWS % ATT_BLOCK == 0 and MIX_ROWS % POOL_HALO == 0 and POOL_HALO >= max(POOL_WINDOWS) - 1
    far_bucket = _far_bucket(ATT_BLOCK + 1, max(seq - 1, ATT_BLOCK + 1))

    o_q = POOL_DIM
    o_k = o_q + ATTN_DIM
    o_v = o_k + ATTN_DIM
    o_qi = o_v + ATTN_DIM
    o_ki = o_qi + IDX_HEADS * IDX_DIM
    o_wi = o_ki + IDX_DIM
    o_ga = o_wi + IDX_HEADS

    btiles = _rel_bias_tiles(rel_bias, far_bucket)
    x2d = x.reshape(bsz * seq, d)
    for l in range(depth):
        wl = w_in[l]
        zeros_ki = jnp.zeros((d, LANES - IDX_DIM), wl.dtype)
        w_ki = wl[:, o_ki:o_wi]
        wqk = jnp.concatenate([wl[:, o_q:o_v], wl[:, o_qi:o_ki]], axis=1).astype(BF16)
        wv_t = wl[:, o_v:o_qi].T.astype(BF16)
        wki = jnp.concatenate([w_ki, zeros_ki, zeros_ki, w_ki], axis=1).astype(BF16)
        wwi_t = jnp.pad(wl[:, o_wi:o_ga].T, ((0, 2 * SUBLANES - IDX_HEADS), (0, 0))).astype(BF16)
        wpg = jnp.concatenate([wl[:, :POOL_DIM], wl[:, o_ga:]], axis=1).astype(BF16)

        mod = _ada_mod(c, w_ada[l], b_ada[l]).reshape(bsz, N_ADA, d)
        x2d = _ffn(x2d, mod, ln_g[l], ln_b[l], ffn1_w_gate[l].astype(BF16), ffn1_w_up[l].astype(BF16),
                   ffn1_w_down[l].astype(BF16), seq=seq, alpha=alpha, mod_row=0, ln_row=0)
        q, k, qi, vt, kia, kib, wit = _attn_proj(x2d, mod, wqk, wv_t, wki, wwi_t, bsz=bsz, seq=seq, mod_row=3)
        att = _attention(q.reshape(bsz, seq, ATTN_DIM), qi.reshape(bsz, seq, IDX_HEADS * IDX_DIM), wit,
                         k.reshape(bsz, seq, ATTN_DIM), vt, kia.reshape(bsz, seq, LANES),
                         kib.reshape(bsz, seq, LANES), btiles, bsz=bsz, seq=seq, topk=topk)
        x2d = _mix_out(x2d, mod, att.reshape(bsz * seq, ATTN_DIM), wpg, w_pool[l].astype(BF16),
                       pool_scale[l].reshape(1, POOL_DIM), w_a[l].astype(BF16), w_b[l].astype(BF16),
                       w_out[l].astype(BF16), ln_g[l], ln_b[l], seq=seq, alpha=alpha, mod_row=3, ln_row=1)
        x2d = _ffn(x2d, mod, ln_g[l], ln_b[l], ffn2_w_gate[l].astype(BF16), ffn2_w_up[l].astype(BF16),
                   ffn2_w_down[l].astype(BF16), seq=seq, alpha=alpha, mod_row=6, ln_row=2)
    return x2d.reshape(bsz, seq, d)
```

```python
import functools
import math

import numpy as np
import jax
import jax.numpy as jnp
from jax import lax
from jax.experimental import pallas as pl
from jax.experimental.pallas import tpu as pltpu

POOL_WINDOWS = (2, 4, 8, 16)
POOL_GROUP_DIM = 128
POOL_DIM = len(POOL_WINDOWS) * POOL_GROUP_DIM
N_HEADS = 8
HEAD_DIM = 64
ATTN_DIM = N_HEADS * HEAD_DIM
IDX_HEADS = 8
IDX_DIM = 64
TOP_K = 256
REL_BUCKETS = 32
REL_MAX_DIST = 128
N_ADA = 9
LN_EPS = 1e-5
POOL_HALO = 16

LANES = 128
SUBLANES = 8
V7X_VMEM_LIMIT_BYTES = 56 * 1024 * 1024
FFN_ROWS = 512
FFN_COLS = 256
PROJ_ROWS = 512
MIX_ROWS = 256
ATT_BLOCK = 256
ADA_COLS = 1024
COUNT_CHAINS = 4
COUNT_ROWS = 64
STEPS_PER_ROUND = 4
FAST_ROUNDS = 7

LOG2E = math.log2(math.e)
BF16 = jnp.bfloat16
F32 = jnp.float32
NT_DIMS = (((1,), (1,)), ((), ()))


def _layer_norm(z, g, b):
    mu = jnp.mean(z, axis=-1, keepdims=True)
    zc = z - mu
    var = jnp.mean(zc * zc, axis=-1, keepdims=True)
    return zc * lax.rsqrt(var + LN_EPS) * g + b


def _silu(a):
    return a * jax.nn.sigmoid(a)


def _ada_kernel(c_ref, w_ref, b_ref, o_ref):
    a = _silu(c_ref[...])
    o_ref[...] = jnp.dot(a, w_ref[...], preferred_element_type=F32) + b_ref[...]


def _ada_mod(c, w_ada, b_ada):
    bsz, d = c.shape
    n = w_ada.shape[1]
    rows = -(-bsz // SUBLANES) * SUBLANES
    c_pad = jnp.pad(c, ((0, rows - bsz), (0, 0)))
    out = pl.pallas_call(
        _ada_kernel,
        grid=(n // ADA_COLS,),
        in_specs=[
            pl.BlockSpec((rows, d), lambda j: (0, 0)),
            pl.BlockSpec((d, ADA_COLS), lambda j: (0, j)),
            pl.BlockSpec((1, ADA_COLS), lambda j: (0, j)),
        ],
        out_specs=pl.BlockSpec((rows, ADA_COLS), lambda j: (0, j)),
        out_shape=jax.ShapeDtypeStruct((rows, n), F32),
        compiler_params=pltpu.CompilerParams(dimension_semantics=("arbitrary",),
                                             vmem_limit_bytes=V7X_VMEM_LIMIT_BYTES),
        name="ada_mod",
    )(c_pad, w_ada, b_ada.reshape(1, n))
    return out[:bsz]


def _ffn_kernel(x_ref, mod_ref, lng_ref, lnb_ref, wg_ref, wu_ref, wd_ref, o_ref, h_ref, *, alpha, mod_row, ln_row):
    x = x_ref[...]
    sh = mod_ref[0, mod_row:mod_row + 1, :]
    sc = mod_ref[0, mod_row + 1:mod_row + 2, :]
    gate = mod_ref[0, mod_row + 2:mod_row + 3, :]
    u = (x * (1.0 + sc) + sh).astype(BF16)
    d_ff = wg_ref.shape[1]
    for c in range(d_ff // FFN_COLS):
        sl = slice(c * FFN_COLS, (c + 1) * FFN_COLS)
        a = jnp.dot(u, wg_ref[:, sl], preferred_element_type=F32)
        b = jnp.dot(u, wu_ref[:, sl], preferred_element_type=F32)
        h_ref[:, sl] = (_silu(a) * b).astype(BF16)
    y = jnp.dot(h_ref[...], wd_ref[...], preferred_element_type=F32)
    z = alpha * x + (0.5 * gate) * y
    o_ref[...] = _layer_norm(z, lng_ref[ln_row:ln_row + 1, :], lnb_ref[ln_row:ln_row + 1, :])


def _ffn(x2d, mod, ln_g, ln_b, wg, wu, wd, *, seq, alpha, mod_row, ln_row):
    n, d = x2d.shape
    d_ff = wg.shape[1]
    tiles_per_seq = seq // FFN_ROWS
    resident = dict(pipeline_mode=pl.Buffered(1))
    return pl.pallas_call(
        functools.partial(_ffn_kernel, alpha=alpha, mod_row=mod_row, ln_row=ln_row),
        grid=(n // FFN_ROWS,),
        in_specs=[
            pl.BlockSpec((FFN_ROWS, d), lambda i: (i, 0)),
            pl.BlockSpec((1, N_ADA, d), lambda i: (i // tiles_per_seq, 0, 0)),
            pl.BlockSpec(ln_g.shape, lambda i: (0, 0)),
            pl.BlockSpec(ln_b.shape, lambda i: (0, 0)),
            pl.BlockSpec((d, d_ff), lambda i: (0, 0), **resident),
            pl.BlockSpec((d, d_ff), lambda i: (0, 0), **resident),
            pl.BlockSpec((d_ff, d), lambda i: (0, 0), **resident),
        ],
        out_specs=pl.BlockSpec((FFN_ROWS, d), lambda i: (i, 0)),
        out_shape=jax.ShapeDtypeStruct((n, d), F32),
        scratch_shapes=[pltpu.VMEM((FFN_ROWS, d_ff), BF16)],
        compiler_params=pltpu.CompilerParams(dimension_semantics=("arbitrary",),
                                             vmem_limit_bytes=V7X_VMEM_LIMIT_BYTES),
        name="ffn",
    )(x2d, mod, ln_g, ln_b, wg, wu, wd)


def _proj_kernel(x_ref, mod_ref, wqk_ref, wv_ref, wki_ref, wwi_ref,
                 q_ref, k_ref, qi_ref, vt_ref, kia_ref, kib_ref, wit_ref, *, mod_row):
    x = x_ref[...]
    sh = mod_ref[0, mod_row:mod_row + 1, :]
    sc = mod_ref[0, mod_row + 1:mod_row + 2, :]
    u = (x * (1.0 + sc) + sh).astype(BF16)
    qkq = jnp.dot(u, wqk_ref[...], preferred_element_type=F32)
    q_ref[...] = (qkq[:, :ATTN_DIM] * (HEAD_DIM ** -0.5 * LOG2E)).astype(BF16)
    k_ref[...] = qkq[:, ATTN_DIM:2 * ATTN_DIM].astype(BF16)
    qi_ref[...] = qkq[:, 2 * ATTN_DIM:].astype(BF16)
    kk = jnp.dot(u, wki_ref[...], preferred_element_type=F32)
    kia_ref[...] = kk[:, :LANES].astype(BF16)
    kib_ref[...] = kk[:, LANES:].astype(BF16)
    vt = lax.dot_general(wv_ref[...], u, NT_DIMS, preferred_element_type=F32).astype(BF16)
    for c in range(vt_ref.shape[1]):
        vt_ref[0, c] = vt[:, c * ATT_BLOCK:(c + 1) * ATT_BLOCK]
    wit = lax.dot_general(wwi_ref[...], u, NT_DIMS, preferred_element_type=F32)
    wit_ref[0] = wit[:IDX_HEADS, :]


def _attn_proj(x2d, mod, wqk, wv_t, wki, wwi_t, *, bsz, seq, mod_row):
    n, d = x2d.shape
    tiles_per_seq = seq // PROJ_ROWS
    chunks_per_tile = PROJ_ROWS // ATT_BLOCK
    resident = dict(pipeline_mode=pl.Buffered(1))
    row_spec = lambda cols: pl.BlockSpec((PROJ_ROWS, cols), lambda i: (i, 0))
    return pl.pallas_call(
        functools.partial(_proj_kernel, mod_row=mod_row),
        grid=(n // PROJ_ROWS,),
        in_specs=[
            pl.BlockSpec((PROJ_ROWS, d), lambda i: (i, 0)),
            pl.BlockSpec((1, N_ADA, d), lambda i: (i // tiles_per_seq, 0, 0)),
            pl.BlockSpec(wqk.shape, lambda i: (0, 0), **resident),
            pl.BlockSpec(wv_t.shape, lambda i: (0, 0), **resident),
            pl.BlockSpec(wki.shape, lambda i: (0, 0), **resident),
            pl.BlockSpec(wwi_t.shape, lambda i: (0, 0), **resident),
        ],
        out_specs=[
            row_spec(ATTN_DIM), row_spec(ATTN_DIM), row_spec(IDX_HEADS * IDX_DIM),
            pl.BlockSpec((1, chunks_per_tile, ATTN_DIM, ATT_BLOCK),
                         lambda i: (i // tiles_per_seq, i % tiles_per_seq, 0, 0)),
            row_spec(LANES), row_spec(LANES),
            pl.BlockSpec((1, IDX_HEADS, PROJ_ROWS), lambda i: (i // tiles_per_seq, 0, i % tiles_per_seq)),
        ],
        out_shape=[
            jax.ShapeDtypeStruct((n, ATTN_DIM), BF16),
            jax.ShapeDtypeStruct((n, ATTN_DIM), BF16),
            jax.ShapeDtypeStruct((n, IDX_HEADS * IDX_DIM), BF16),
            jax.ShapeDtypeStruct((bsz, seq // ATT_BLOCK, ATTN_DIM, ATT_BLOCK), BF16),
            jax.ShapeDtypeStruct((n, LANES), BF16),
            jax.ShapeDtypeStruct((n, LANES), BF16),
            jax.ShapeDtypeStruct((bsz, IDX_HEADS, seq), F32),
        ],
        compiler_params=pltpu.CompilerParams(dimension_semantics=("arbitrary",),
                                             vmem_limit_bytes=V7X_VMEM_LIMIT_BYTES),
        name="attn_proj",
    )(x2d, mod, wqk, wv_t, wki, wwi_t)


def _t5_bucket(n):
    max_exact = REL_BUCKETS // 2
    nf = jnp.maximum(n, 1).astype(F32)
    large = max_exact + (jnp.log(nf / max_exact) / math.log(REL_MAX_DIST / max_exact)
                         * (REL_BUCKETS - max_exact)).astype(jnp.int32)
    large = jnp.minimum(large, REL_BUCKETS - 1)
    return jnp.where(n < max_exact, n, large)


def _far_bucket(first_dist, last_dist):
    n = np.arange(first_dist, last_dist + 1, dtype=np.float32)
    max_exact = REL_BUCKETS // 2
    large = max_exact + (np.log(n / np.float32(max_exact)) / np.float32(math.log(REL_MAX_DIST / max_exact))
                         * np.float32(REL_BUCKETS - max_exact)).astype(np.int32)
    buckets = np.where(n < max_exact, n.astype(np.int32), np.minimum(large, REL_BUCKETS - 1))
    assert buckets.min() == buckets.max(), "key chunks two or more blocks away must share one bias bucket"
    return int(buckets[0])


def _bias_kernel(rb_ref, o_ref, *, far_bucket):
    o = pl.program_id(0)
    h = pl.program_id(1)
    row = lax.broadcasted_iota(jnp.int32, (ATT_BLOCK, ATT_BLOCK), 0)
    col = lax.broadcasted_iota(jnp.int32, (ATT_BLOCK, ATT_BLOCK), 1)
    dist = o * ATT_BLOCK + col - row
    bucket = _t5_bucket(jnp.maximum(dist, 0))
    tile = jnp.zeros((ATT_BLOCK, ATT_BLOCK), F32)
    for b in range(REL_BUCKETS):
        tile = jnp.where(bucket == b, rb_ref[b, h], tile)
    o_ref[0, 0] = (tile - rb_ref[far_bucket, h]) * LOG2E


def _rel_bias_tiles(rel_bias, far_bucket):
    return pl.pallas_call(
        functools.partial(_bias_kernel, far_bucket=far_bucket),
        grid=(2, N_HEADS),
        in_specs=[pl.BlockSpec(memory_space=pltpu.SMEM)],
        out_specs=pl.BlockSpec((1, 1, ATT_BLOCK, ATT_BLOCK), lambda o, h: (o, h, 0, 0)),
        out_shape=jax.ShapeDtypeStruct((2, N_HEADS, ATT_BLOCK, ATT_BLOCK), F32),
        compiler_params=pltpu.CompilerParams(dimension_semantics=("arbitrary", "arbitrary")),
        name="rel_bias",
    )(rel_bias)


def _attn_kernel(q_ref, qi_ref, wit_ref, k_ref, vt_ref, kia_ref, kib_ref, bt_ref, o_ref,
                 sc_ref, qh_ref, lg_ref, p_ref, st_ref, acc_ref, *, topk, seq):
    blk = ATT_BLOCK
    groups = blk // SUBLANES
    i = pl.program_id(1)
    nch = i + 1
    kf = float(topk)
    inf = jnp.inf

    def chunk_start(j):
        return j * blk if isinstance(j, int) else pl.multiple_of(j * blk, blk)

    def as_groups(x):
        return x.reshape(groups, SUBLANES, blk)

    def lanes8(v):
        return jnp.broadcast_to(v, (SUBLANES, blk))

    def colmin(x8):
        return jnp.min(x8, axis=0, keepdims=True)

    def colmax(x8):
        return jnp.max(x8, axis=0, keepdims=True)

    def colsum(x8):
        return jnp.sum(x8, axis=0, keepdims=True)

    wf = wit_ref[0] * (IDX_DIM ** -0.5)
    qi = qi_ref[0]
    row = lax.broadcasted_iota(jnp.int32, (blk, blk), 0)
    col = lax.broadcasted_iota(jnp.int32, (blk, blk), 1)
    causal = row <= col

    def chunk_scores(j):
        r0 = chunk_start(j)
        ka = kia_ref[0, pl.ds(r0, blk), :]
        kb = kib_ref[0, pl.ds(r0, blk), :]
        s = jnp.zeros((blk, blk), F32)
        for hp in range(IDX_HEADS // 2):
            qp = qi[:, hp * LANES:(hp + 1) * LANES]
            a0 = lax.dot_general(ka, qp, NT_DIMS, preferred_element_type=F32)
            a1 = lax.dot_general(kb, qp, NT_DIMS, preferred_element_type=F32)
            s = s + jnp.maximum(a0, 0.0) * wf[2 * hp:2 * hp + 1, :]
            s = s + jnp.maximum(a1, 0.0) * wf[2 * hp + 1:2 * hp + 2, :]
        return r0, s * (IDX_HEADS ** -0.5)

    def score_stats(s_lo, s_hi, stats):
        mn8, mx8, pos8, zer8 = stats
        hi3 = as_groups(s_hi)
        return (jnp.minimum(mn8, jnp.min(as_groups(s_lo), axis=0)),
                jnp.maximum(mx8, jnp.max(hi3, axis=0)),
                pos8 + jnp.sum(jnp.where(hi3 > 0.0, 1.0, 0.0), axis=0),
                zer8 + jnp.sum(jnp.where(hi3 == 0.0, 1.0, 0.0), axis=0))

    def score_body(j, stats):
        r0, s = chunk_scores(j)
        sc_ref[pl.ds(r0, blk), :] = s
        return score_stats(s, s, stats)

    zeros8 = jnp.zeros((SUBLANES, blk), F32)
    stats = lax.fori_loop(0, i, score_body,
                          (jnp.full((SUBLANES, blk), inf, F32), jnp.full((SUBLANES, blk), -inf, F32), zeros8, zeros8))
    r_diag, s_diag = chunk_scores(i)
    s_diag_hi = jnp.where(causal, s_diag, -inf)
    sc_ref[pl.ds(r_diag, blk), :] = s_diag_hi
    mn8, mx8, pos8, zer8 = score_stats(jnp.where(causal, s_diag, inf), s_diag_hi, stats)

    @pl.when(nch % 2 == 1)
    def _():
        sc_ref[pl.ds(chunk_start(nch), blk), :] = jnp.full((blk, blk), -inf, F32)

    npair = (nch + 1) // 2

    def count_gt(thr):
        t8 = lanes8(thr)

        def body(jp, accs):
            base = pl.multiple_of(jp * 2 * blk, 2 * blk)
            accs = list(accs)
            for s in range(2 * blk // COUNT_ROWS):
                x = sc_ref[pl.ds(base + s * COUNT_ROWS, COUNT_ROWS), :]
                for g in range(COUNT_ROWS // SUBLANES):
                    hit = jnp.where(x[g * SUBLANES:(g + 1) * SUBLANES] > t8, 1.0, 0.0)
                    accs[g % COUNT_CHAINS] = accs[g % COUNT_CHAINS] + hit
            return tuple(accs)

        accs = lax.fori_loop(0, npair, body, tuple(jnp.zeros((SUBLANES, blk), F32) for _ in range(COUNT_CHAINS)))
        return colsum(sum(accs[1:], accs[0]))

    def bisect(lo, hi, low, done):
        mid = 0.5 * lo + 0.5 * hi
        c = count_gt(mid)
        live = done < 0.5
        up = jnp.logical_and(live, c >= kf)
        down = jnp.logical_and(live, c < kf)
        low = jnp.where(up, mid, low)
        lo = jnp.where(up, mid, lo)
        hi = jnp.where(down, mid, hi)
        done = jnp.where(c == kf, 1.0, done)
        return lo, hi, low, done

    tpos = i * blk + lax.broadcasted_iota(jnp.int32, (1, blk), 1)
    pos, zer = colsum(pos8), colsum(zer8)
    zero_tie = jnp.logical_and(pos < kf, pos + zer >= kf)
    done0 = jnp.where(jnp.logical_or(tpos + 1 <= topk, zero_tie), 1.0, 0.0)
    low0 = jnp.where(zero_tie, 0.0, -inf)
    tie0 = jnp.where(zero_tie, 0.0, inf)
    need0 = jnp.where(zero_tie, kf - pos, 0.0)

    def fast_cond(st):
        it, _, _, _, done = st
        return jnp.logical_and(it < FAST_ROUNDS, jnp.min(done) < 0.5)

    def fast_body(st):
        it, lo, hi, low, done = st
        for _ in range(STEPS_PER_ROUND):
            lo, hi, low, done = bisect(lo, hi, low, done)
        return it + 1, lo, hi, low, done

    _, lo, hi, low, done = lax.while_loop(fast_cond, fast_body,
                                          (jnp.int32(0), colmin(mn8), colmax(mx8), low0, done0))

    def next_value_above(thr):
        t8 = lanes8(thr)

        def body(j, acc):
            x = as_groups(sc_ref[pl.ds(chunk_start(j), blk), :])
            return jnp.minimum(acc, jnp.min(jnp.where(x > t8[None], x, inf), axis=0))

        return colmin(lax.fori_loop(0, nch, body, jnp.full((SUBLANES, blk), inf, F32)))

    def slow_cond(st):
        return jnp.min(st[3]) < 0.5

    def slow_body(st):
        lo, hi, low, done, tie, need = st
        live = done < 0.5
        cand = next_value_above(low)
        cgt = count_gt(cand)
        found = jnp.logical_and(live, cgt < kf)
        tie = jnp.where(found, cand, tie)
        need = jnp.where(found, kf - cgt, need)
        climb = jnp.logical_and(live, cgt >= kf)
        low = jnp.where(jnp.logical_or(found, climb), cand, low)
        lo = jnp.where(climb, jnp.maximum(lo, cand), lo)
        done = jnp.where(jnp.logical_or(found, jnp.logical_and(live, cgt == kf)), 1.0, done)
        lo, hi, low, done = bisect(lo, hi, low, done)
        return lo, hi, low, done, tie, need

    def slow_path(args):
        lo, hi, low, done = args
        _, _, low, _, tie, need = lax.while_loop(slow_cond, slow_body, (lo, hi, low, done, tie0, need0))
        return low, tie, need

    low, tie, need = lax.cond(jnp.min(done) < 0.5, slow_path, lambda a: (a[2], tie0, need0), (lo, hi, low, done))

    def plain_mask(_):
        low8 = lanes8(low)

        def body(j, _):
            r0 = chunk_start(j)
            x = as_groups(sc_ref[pl.ds(r0, blk), :])
            sc_ref[pl.ds(r0, blk), :] = jnp.where(x > low8[None], 0.0, -inf).reshape(blk, blk)
            return 0

        return lax.fori_loop(0, nch, body, 0)

    def tie_mask(_):
        tri = jnp.where(row >= col, 1.0, 0.0).astype(BF16)

        def body(j, base):
            r0 = chunk_start(j)
            x = sc_ref[pl.ds(r0, blk), :]
            eq = x == tie
            rank = jnp.dot(tri, jnp.where(eq, 1.0, 0.0).astype(BF16), preferred_element_type=F32)
            sel = jnp.logical_or(x > low, jnp.logical_and(eq, rank + base <= need))
            sc_ref[pl.ds(r0, blk), :] = jnp.where(sel, 0.0, -inf)
            return base + rank[blk - 1:blk, :]

        lax.fori_loop(0, nch, body, jnp.zeros((1, blk), F32))
        return 0

    lax.cond(jnp.max(jnp.where(tie < inf, 1.0, 0.0)) > 0.5, tie_mask, plain_mask, 0)

    q = q_ref[0]
    lane = lax.broadcasted_iota(jnp.int32, (blk, LANES), 1)
    for h in range(N_HEADS):
        hp = h // 2
        in_head = (lane < HEAD_DIM) if h % 2 == 0 else (lane >= HEAD_DIM)
        qh_ref[h] = jnp.where(in_head, q[:, hp * LANES:(hp + 1) * LANES], jnp.zeros((), BF16))
    acc_ref[...] = jnp.zeros(acc_ref.shape, F32)

    def logits(j, h, mb, near):
        hp = h // 2
        kc = k_ref[0, pl.ds(chunk_start(j), blk), hp * LANES:(hp + 1) * LANES]
        lg = lax.dot_general(kc, qh_ref[h], NT_DIMS, preferred_element_type=F32) + mb
        return lg if near is None else lg + bt_ref[near, h]

    CMAX, RMAX, RSUM, RESC = 0, 1, 2, 3

    def stage_logits(j, near):
        mb = sc_ref[pl.ds(chunk_start(j), blk), :]
        cm8 = []
        for h in range(N_HEADS):
            lg = logits(j, h, mb, near)
            lg_ref[h] = lg
            cm8.append(jnp.max(as_groups(lg), axis=0))
        for h in range(N_HEADS):
            st_ref[CMAX, h] = lanes8(colmax(cm8[h]))

    def stage_exp():
        for h in range(N_HEADS):
            m_old = st_ref[RMAX, h]
            m_new = jnp.maximum(m_old, st_ref[CMAX, h])
            m_use = jnp.where(m_new == -inf, 0.0, m_new)
            alpha = jnp.exp2(m_old - m_use)
            p = jnp.exp2(as_groups(lg_ref[h]) - m_use[None])
            p_ref[h] = p.reshape(blk, blk).astype(BF16)
            st_ref[RSUM, h] = alpha * st_ref[RSUM, h] + jnp.sum(p, axis=0)
            st_ref[RMAX, h] = m_new
            st_ref[RESC, h] = alpha

    def stage_pv(j):
        for h in range(N_HEADS):
            rows = slice(h * HEAD_DIM, (h + 1) * HEAD_DIM)
            pv = jnp.dot(vt_ref[0, j, rows, :], p_ref[h], preferred_element_type=F32)
            acc = acc_ref[rows, :].reshape(HEAD_DIM // SUBLANES, SUBLANES, blk) * st_ref[RESC, h][None]
            acc_ref[rows, :] = acc.reshape(HEAD_DIM, blk) + pv

    def pipelined(j, near_next):
        stage_exp()
        stage_logits(j + 1, near_next)
        stage_pv(j)

    st_ref[RMAX] = jnp.full((N_HEADS, SUBLANES, blk), -inf, F32)
    st_ref[RSUM] = jnp.zeros((N_HEADS, SUBLANES, blk), F32)

    @pl.when(i == 0)
    def _():
        stage_logits(0, 0)

    @pl.when(i == 1)
    def _():
        stage_logits(0, 1)

    @pl.when(i >= 2)
    def _():
        stage_logits(0, None)

    def far_body(j, _):
        pipelined(j, None)
        return 0

    lax.fori_loop(0, jnp.maximum(i - 2, 0), far_body, 0)

    @pl.when(i >= 2)
    def _():
        pipelined(i - 2, 1)

    @pl.when(i >= 1)
    def _():
        pipelined(i - 1, 0)

    stage_exp()
    stage_pv(i)
    for h in range(N_HEADS):
        rows = slice(h * HEAD_DIM, (h + 1) * HEAD_DIM)
        acc_ref[rows, :] = acc_ref[rows, :] / colsum(st_ref[RSUM, h])
    o_ref[0] = acc_ref[...].T.astype(BF16)


def _attention(q, qi, wit, k, vt, kia, kib, btiles, *, bsz, seq, topk):
    blk = ATT_BLOCK
    nblk = seq // blk
    return pl.pallas_call(
        functools.partial(_attn_kernel, topk=topk, seq=seq),
        grid=(bsz, nblk),
        in_specs=[
            pl.BlockSpec((1, blk, ATTN_DIM), lambda b, i: (b, i, 0)),
            pl.BlockSpec((1, blk, IDX_HEADS * IDX_DIM), lambda b, i: (b, i, 0)),
            pl.BlockSpec((1, IDX_HEADS, blk), lambda b, i: (b, 0, i)),
            pl.BlockSpec((1, seq, ATTN_DIM), lambda b, i: (b, 0, 0)),
            pl.BlockSpec((1, nblk, ATTN_DIM, blk), lambda b, i: (b, 0, 0, 0)),
            pl.BlockSpec((1, seq, LANES), lambda b, i: (b, 0, 0)),
            pl.BlockSpec((1, seq, LANES), lambda b, i: (b, 0, 0)),
            pl.BlockSpec(btiles.shape, lambda b, i: (0, 0, 0, 0)),
        ],
        out_specs=pl.BlockSpec((1, blk, ATTN_DIM), lambda b, i: (b, i, 0)),
        out_shape=jax.ShapeDtypeStruct((bsz, seq, ATTN_DIM), BF16),
        scratch_shapes=[
            pltpu.VMEM((seq + blk, blk), F32),
            pltpu.VMEM((N_HEADS, blk, LANES), BF16),
            pltpu.VMEM((N_HEADS, blk, blk), F32),
            pltpu.VMEM((N_HEADS, blk, blk), BF16),
            pltpu.VMEM((4, N_HEADS, SUBLANES, blk), F32),
            pltpu.VMEM((ATTN_DIM, blk), F32),
        ],
        compiler_params=pltpu.CompilerParams(dimension_semantics=("arbitrary", "arbitrary"),
                                             vmem_limit_bytes=V7X_VMEM_LIMIT_BYTES),
        name="attention",
    )(q, qi, wit, k, vt, kia, kib, btiles)


def _mix_kernel(x_ref, xh_ref, mod_ref, att_ref, wpg_ref, wpool_ref, ps_ref, wa_ref, wb_ref, wo_ref,
                lng_ref, lnb_ref, o_ref, pe_ref, mix_ref, *, alpha, mod_row, ln_row, tiles_per_seq):
    rows = x_ref.shape[0]
    i = pl.program_id(0)
    seq_tile = i % tiles_per_seq
    x = x_ref[...]
    sh = mod_ref[0, mod_row:mod_row + 1, :]
    sc = mod_ref[0, mod_row + 1:mod_row + 2, :]
    gate = mod_ref[0, mod_row + 2:mod_row + 3, :]
    u = (x * (1.0 + sc) + sh).astype(BF16)
    uh = (xh_ref[...] * (1.0 + sc) + sh).astype(BF16)
    pg = jnp.dot(u, wpg_ref[...], preferred_element_type=F32)
    ph = jnp.dot(uh, wpg_ref[:, :POOL_DIM], preferred_element_type=F32)
    pe_ref[0:POOL_HALO, :] = jnp.where(seq_tile == 0, 0.0, ph)
    pe_ref[POOL_HALO:, :] = pg[:, :POOL_DIM]
    t = seq_tile * rows + lax.broadcasted_iota(jnp.int32, (rows, 1), 0)
    for g, w in enumerate(POOL_WINDOWS):
        cols = slice(g * POOL_GROUP_DIM, (g + 1) * POOL_GROUP_DIM)
        cur = pe_ref[POOL_HALO:, cols]
        win = cur
        for back in range(1, w):
            win = win + pe_ref[POOL_HALO - back:POOL_HALO - back + rows, cols]
        cnt = jnp.minimum(t + 1, w).astype(F32)
        pooled = (win / cnt - cur).astype(BF16)
        mixed = jnp.dot(pooled, wpool_ref[g], preferred_element_type=F32)
        mix_ref[:, cols] = (mixed * ps_ref[:, cols]).astype(BF16)
    y_a = jnp.dot(mix_ref[...], wa_ref[...], preferred_element_type=F32)
    y_b = jnp.dot(att_ref[...], wb_ref[...], preferred_element_type=F32)
    d = x.shape[1]
    ga = pg[:, POOL_DIM:POOL_DIM + d]
    gb = pg[:, POOL_DIM + d:]
    merged = (jax.nn.sigmoid(ga) * y_a + jax.nn.sigmoid(gb) * y_b).astype(BF16)
    y = jnp.dot(merged, wo_ref[...], preferred_element_type=F32)
    z = alpha * x + gate * y
    o_ref[...] = _layer_norm(z, lng_ref[ln_row:ln_row + 1, :], lnb_ref[ln_row:ln_row + 1, :])


def _mix_out(x2d, mod, att2d, wpg, wpool, pool_scale, wa, wb, wo, ln_g, ln_b, *, seq, alpha, mod_row, ln_row):
    n, d = x2d.shape
    tiles_per_seq = seq // MIX_ROWS
    halo_blocks = MIX_ROWS // POOL_HALO
    resident = dict(pipeline_mode=pl.Buffered(1))
    full = lambda a: pl.BlockSpec(a.shape, lambda i: (0,) * a.ndim, **resident)
    return pl.pallas_call(
        functools.partial(_mix_kernel, alpha=alpha, mod_row=mod_row, ln_row=ln_row, tiles_per_seq=tiles_per_seq),
        grid=(n // MIX_ROWS,),
        in_specs=[
            pl.BlockSpec((MIX_ROWS, d), lambda i: (i, 0)),
            pl.BlockSpec((POOL_HALO, d), lambda i: (jnp.maximum(i * halo_blocks - 1, 0), 0)),
            pl.BlockSpec((1, N_ADA, d), lambda i: (i // tiles_per_seq, 0, 0)),
            pl.BlockSpec((MIX_ROWS, ATTN_DIM), lambda i: (i, 0)),
            full(wpg), full(wpool), full(pool_scale), full(wa), full(wb), full(wo),
            pl.BlockSpec(ln_g.shape, lambda i: (0, 0)),
            pl.BlockSpec(ln_b.shape, lambda i: (0, 0)),
        ],
        out_specs=pl.BlockSpec((MIX_ROWS, d), lambda i: (i, 0)),
        out_shape=jax.ShapeDtypeStruct((n, d), F32),
        scratch_shapes=[
            pltpu.VMEM((POOL_HALO + MIX_ROWS, POOL_DIM), F32),
            pltpu.VMEM((MIX_ROWS, POOL_DIM), BF16),
        ],
        compiler_params=pltpu.CompilerParams(dimension_semantics=("arbitrary",),
                                             vmem_limit_bytes=V7X_VMEM_LIMIT_BYTES),
        name="mix_out",
    )(x2d, x2d, mod, att2d, wpg, wpool, pool_scale, wa, wb, wo, ln_g, ln_b)


def kernel(x, c, w_ada, b_ada, ln_g, ln_b, ffn1_w_gate, ffn1_w_up, ffn1_w_down, w_in, w_pool, pool_scale,
           w_a, w_b, w_out, rel_bias, ffn2_w_gate, ffn2_w_up, ffn2_w_down):
    bsz, seq, d = x.shape
    depth = w_ada.shape[0]
    alpha = (2.0 * depth) ** 0.25
    topk = min(TOP_K, seq // 4)
    assert seq % FFN_ROWS == 0 and seq % PROJ_ROWS == 0 and seq % MIX_ROWS == 0 and seq % ATT_BLOCK == 0
    assert PROJ_ROWS % ATT_BLOCK == 0 and MIX_ROWS % POOL_HALO == 0 and POOL_HALO >= max(POOL_WINDOWS) - 1
    far_bucket = _far_bucket(ATT_BLOCK + 1, max(seq - 1, ATT_BLOCK + 1))

    o_q = POOL_DIM
    o_k = o_q + ATTN_DIM
    o_v = o_k + ATTN_DIM
    o_qi = o_v + ATTN_DIM
    o_ki = o_qi + IDX_HEADS * IDX_DIM
    o_wi = o_ki + IDX_DIM
    o_ga = o_wi + IDX_HEADS

    btiles = _rel_bias_tiles(rel_bias, far_bucket)
    x2d = x.reshape(bsz * seq, d)
    for l in range(depth):
        wl = w_in[l]
        zeros_ki = jnp.zeros((d, LANES - IDX_DIM), wl.dtype)
        w_ki = wl[:, o_ki:o_wi]
        wqk = jnp.concatenate([wl[:, o_q:o_v], wl[:, o_qi:o_ki]], axis=1).astype(BF16)
        wv_t = wl[:, o_v:o_qi].T.astype(BF16)
        wki = jnp.concatenate([w_ki, zeros_ki, zeros_ki, w_ki], axis=1).astype(BF16)
        wwi_t = jnp.pad(wl[:, o_wi:o_ga].T, ((0, 2 * SUBLANES - IDX_HEADS), (0, 0))).astype(BF16)
        wpg = jnp.concatenate([wl[:, :POOL_DIM], wl[:, o_ga:]], axis=1).astype(BF16)

        mod = _ada_mod(c, w_ada[l], b_ada[l]).reshape(bsz, N_ADA, d)
        x2d = _ffn(x2d, mod, ln_g[l], ln_b[l], ffn1_w_gate[l].astype(BF16), ffn1_w_up[l].astype(BF16),
                   ffn1_w_down[l].astype(BF16), seq=seq, alpha=alpha, mod_row=0, ln_row=0)
        q, k, qi, vt, kia, kib, wit = _attn_proj(x2d, mod, wqk, wv_t, wki, wwi_t, bsz=bsz, seq=seq, mod_row=3)
        att = _attention(q.reshape(bsz, seq, ATTN_DIM), qi.reshape(bsz, seq, IDX_HEADS * IDX_DIM), wit,
                         k.reshape(bsz, seq, ATTN_DIM), vt, kia.reshape(bsz, seq, LANES),
                         kib.reshape(bsz, seq, LANES), btiles, bsz=bsz, seq=seq, topk=topk)
        x2d = _mix_out(x2d, mod, att.reshape(bsz * seq, ATTN_DIM), wpg, w_pool[l].astype(BF16),
                       pool_scale[l].reshape(1, POOL_DIM), w_a[l].astype(BF16), w_b[l].astype(BF16),
                       w_out[l].astype(BF16), ln_g[l], ln_b[l], seq=seq, alpha=alpha, mod_row=3, ln_row=1)
        x2d = _ffn(x2d, mod, ln_g[l], ln_b[l], ffn2_w_gate[l].astype(BF16), ffn2_w_up[l].astype(BF16),
                   ffn2_w_down[l].astype(BF16), seq=seq, alpha=alpha, mod_row=6, ln_row=2)
    return x2d.reshape(bsz, seq, d)
```

```python
import functools
import math

import numpy as np
import jax
import jax.numpy as jnp
from jax import lax
from jax.experimental import pallas as pl
from jax.experimental.pallas import tpu as pltpu

POOL_WINDOWS = (2, 4, 8, 16)
POOL_GROUP_DIM = 128
POOL_DIM = len(POOL_WINDOWS) * POOL_GROUP_DIM
N_HEADS = 8
HEAD_DIM = 64
ATTN_DIM = N_HEADS * HEAD_DIM
IDX_HEADS = 8
IDX_DIM = 64
TOP_K = 256
REL_BUCKETS = 32
REL_MAX_DIST = 128
N_ADA = 9
LN_EPS = 1e-5
POOL_HALO = 16

LANES = 128
SUBLANES = 8
V7X_VMEM_LIMIT_BYTES = 56 * 1024 * 1024
FFN_ROWS = 512
FFN_COLS = 256
PROJ_ROWS = 512
MIX_ROWS = 256
ATT_BLOCK = 256
ADA_COLS = 1024
COUNT_CHAINS = 4
COUNT_ROWS = 64
BISECT_STEPS = 14

LOG2E = math.log2(math.e)
BF16 = jnp.bfloat16
F32 = jnp.float32
NT_DIMS = (((1,), (1,)), ((), ()))


def _layer_norm(z, g, b):
    mu = jnp.mean(z, axis=-1, keepdims=True)
    zc = z - mu
    var = jnp.mean(zc * zc, axis=-1, keepdims=True)
    return zc * lax.rsqrt(var + LN_EPS) * g + b


def _silu(a):
    return a * jax.nn.sigmoid(a)


def _ada_kernel(c_ref, w_ref, b_ref, o_ref):
    a = _silu(c_ref[...])
    o_ref[...] = jnp.dot(a, w_ref[...], preferred_element_type=F32) + b_ref[...]


def _ada_mod(c, w_ada, b_ada):
    bsz, d = c.shape
    n = w_ada.shape[1]
    rows = -(-bsz // SUBLANES) * SUBLANES
    c_pad = jnp.pad(c, ((0, rows - bsz), (0, 0)))
    out = pl.pallas_call(
        _ada_kernel,
        grid=(n // ADA_COLS,),
        in_specs=[
            pl.BlockSpec((rows, d), lambda j: (0, 0)),
            pl.BlockSpec((d, ADA_COLS), lambda j: (0, j)),
            pl.BlockSpec((1, ADA_COLS), lambda j: (0, j)),
        ],
        out_specs=pl.BlockSpec((rows, ADA_COLS), lambda j: (0, j)),
        out_shape=jax.ShapeDtypeStruct((rows, n), F32),
        compiler_params=pltpu.CompilerParams(dimension_semantics=("arbitrary",),
                                             vmem_limit_bytes=V7X_VMEM_LIMIT_BYTES),
        name="ada_mod",
    )(c_pad, w_ada, b_ada.reshape(1, n))
    return out[:bsz]


def _ffn_kernel(x_ref, mod_ref, lng_ref, lnb_ref, wg_ref, wu_ref, wd_ref, o_ref, h_ref, *, alpha, mod_row, ln_row):
    x = x_ref[...]
    sh = mod_ref[0, mod_row:mod_row + 1, :]
    sc = mod_ref[0, mod_row + 1:mod_row + 2, :]
    gate = mod_ref[0, mod_row + 2:mod_row + 3, :]
    u = (x * (1.0 + sc) + sh).astype(BF16)
    d_ff = wg_ref.shape[1]
    for c in range(d_ff // FFN_COLS):
        sl = slice(c * FFN_COLS, (c + 1) * FFN_COLS)
        a = jnp.dot(u, wg_ref[:, sl], preferred_element_type=F32)
        b = jnp.dot(u, wu_ref[:, sl], preferred_element_type=F32)
        h_ref[:, sl] = (_silu(a) * b).astype(BF16)
    y = jnp.dot(h_ref[...], wd_ref[...], preferred_element_type=F32)
    z = alpha * x + (0.5 * gate) * y
    o_ref[...] = _layer_norm(z, lng_ref[ln_row:ln_row + 1, :], lnb_ref[ln_row:ln_row + 1, :])


def _ffn(x2d, mod, ln_g, ln_b, wg, wu, wd, *, seq, alpha, mod_row, ln_row):
    n, d = x2d.shape
    d_ff = wg.shape[1]
    tiles_per_seq = seq // FFN_ROWS
    resident = dict(pipeline_mode=pl.Buffered(1))
    return pl.pallas_call(
        functools.partial(_ffn_kernel, alpha=alpha, mod_row=mod_row, ln_row=ln_row),
        grid=(n // FFN_ROWS,),
        in_specs=[
            pl.BlockSpec((FFN_ROWS, d), lambda i: (i, 0)),
            pl.BlockSpec((1, N_ADA, d), lambda i: (i // tiles_per_seq, 0, 0)),
            pl.BlockSpec(ln_g.shape, lambda i: (0, 0)),
            pl.BlockSpec(ln_b.shape, lambda i: (0, 0)),
            pl.BlockSpec((d, d_ff), lambda i: (0, 0), **resident),
            pl.BlockSpec((d, d_ff), lambda i: (0, 0), **resident),
            pl.BlockSpec((d_ff, d), lambda i: (0, 0), **resident),
        ],
        out_specs=pl.BlockSpec((FFN_ROWS, d), lambda i: (i, 0)),
        out_shape=jax.ShapeDtypeStruct((n, d), F32),
        scratch_shapes=[pltpu.VMEM((FFN_ROWS, d_ff), BF16)],
        compiler_params=pltpu.CompilerParams(dimension_semantics=("arbitrary",),
                                             vmem_limit_bytes=V7X_VMEM_LIMIT_BYTES),
        name="ffn",
    )(x2d, mod, ln_g, ln_b, wg, wu, wd)


def _proj_kernel(x_ref, mod_ref, wqk_ref, wv_ref, wki_ref, wwi_ref,
                 q_ref, k_ref, qi_ref, vt_ref, kia_ref, kib_ref, wit_ref, *, mod_row):
    x = x_ref[...]
    sh = mod_ref[0, mod_row:mod_row + 1, :]
    sc = mod_ref[0, mod_row + 1:mod_row + 2, :]
    u = (x * (1.0 + sc) + sh).astype(BF16)
    qkq = jnp.dot(u, wqk_ref[...], preferred_element_type=F32)
    q_ref[...] = (qkq[:, :ATTN_DIM] * (HEAD_DIM ** -0.5 * LOG2E)).astype(BF16)
    k_ref[...] = qkq[:, ATTN_DIM:2 * ATTN_DIM].astype(BF16)
    qi_ref[...] = qkq[:, 2 * ATTN_DIM:].astype(BF16)
    kk = jnp.dot(u, wki_ref[...], preferred_element_type=F32)
    kia_ref[...] = kk[:, :LANES].astype(BF16)
    kib_ref[...] = kk[:, LANES:].astype(BF16)
    vt = lax.dot_general(wv_ref[...], u, NT_DIMS, preferred_element_type=F32).astype(BF16)
    for c in range(vt_ref.shape[1]):
        vt_ref[0, c] = vt[:, c * ATT_BLOCK:(c + 1) * ATT_BLOCK]
    wit = lax.dot_general(wwi_ref[...], u, NT_DIMS, preferred_element_type=F32)
    wit_ref[0] = wit[:IDX_HEADS, :]


def _attn_proj(x2d, mod, wqk, wv_t, wki, wwi_t, *, bsz, seq, mod_row):
    n, d = x2d.shape
    tiles_per_seq = seq // PROJ_ROWS
    chunks_per_tile = PROJ_ROWS // ATT_BLOCK
    resident = dict(pipeline_mode=pl.Buffered(1))
    row_spec = lambda cols: pl.BlockSpec((PROJ_ROWS, cols), lambda i: (i, 0))
    return pl.pallas_call(
        functools.partial(_proj_kernel, mod_row=mod_row),
        grid=(n // PROJ_ROWS,),
        in_specs=[
            pl.BlockSpec((PROJ_ROWS, d), lambda i: (i, 0)),
            pl.BlockSpec((1, N_ADA, d), lambda i: (i // tiles_per_seq, 0, 0)),
            pl.BlockSpec(wqk.shape, lambda i: (0, 0), **resident),
            pl.BlockSpec(wv_t.shape, lambda i: (0, 0), **resident),
            pl.BlockSpec(wki.shape, lambda i: (0, 0), **resident),
            pl.BlockSpec(wwi_t.shape, lambda i: (0, 0), **resident),
        ],
        out_specs=[
            row_spec(ATTN_DIM), row_spec(ATTN_DIM), row_spec(IDX_HEADS * IDX_DIM),
            pl.BlockSpec((1, chunks_per_tile, ATTN_DIM, ATT_BLOCK),
                         lambda i: (i // tiles_per_seq, i % tiles_per_seq, 0, 0)),
            row_spec(LANES), row_spec(LANES),
            pl.BlockSpec((1, IDX_HEADS, PROJ_ROWS), lambda i: (i // tiles_per_seq, 0, i % tiles_per_seq)),
        ],
        out_shape=[
            jax.ShapeDtypeStruct((n, ATTN_DIM), BF16),
            jax.ShapeDtypeStruct((n, ATTN_DIM), BF16),
            jax.ShapeDtypeStruct((n, IDX_HEADS * IDX_DIM), BF16),
            jax.ShapeDtypeStruct((bsz, seq // ATT_BLOCK, ATTN_DIM, ATT_BLOCK), BF16),
            jax.ShapeDtypeStruct((n, LANES), BF16),
            jax.ShapeDtypeStruct((n, LANES), BF16),
            jax.ShapeDtypeStruct((bsz, IDX_HEADS, seq), F32),
        ],
        compiler_params=pltpu.CompilerParams(dimension_semantics=("arbitrary",),
                                             vmem_limit_bytes=V7X_VMEM_LIMIT_BYTES),
        name="attn_proj",
    )(x2d, mod, wqk, wv_t, wki, wwi_t)


def _t5_bucket(n):
    max_exact = REL_BUCKETS // 2
    nf = jnp.maximum(n, 1).astype(F32)
    large = max_exact + (jnp.log(nf / max_exact) / math.log(REL_MAX_DIST / max_exact)
                         * (REL_BUCKETS - max_exact)).astype(jnp.int32)
    large = jnp.minimum(large, REL_BUCKETS - 1)
    return jnp.where(n < max_exact, n, large)


def _far_bucket(first_dist, last_dist):
    n = np.arange(first_dist, last_dist + 1, dtype=np.float32)
    max_exact = REL_BUCKETS // 2
    large = max_exact + (np.log(n / np.float32(max_exact)) / np.float32(math.log(REL_MAX_DIST / max_exact))
                         * np.float32(REL_BUCKETS - max_exact)).astype(np.int32)
    buckets = np.where(n < max_exact, n.astype(np.int32), np.minimum(large, REL_BUCKETS - 1))
    assert buckets.min() == buckets.max(), "key chunks two or more blocks away must share one bias bucket"
    return int(buckets[0])


def _bias_kernel(rb_ref, o_ref, *, far_bucket):
    o = pl.program_id(0)
    h = pl.program_id(1)
    row = lax.broadcasted_iota(jnp.int32, (ATT_BLOCK, ATT_BLOCK), 0)
    col = lax.broadcasted_iota(jnp.int32, (ATT_BLOCK, ATT_BLOCK), 1)
    dist = o * ATT_BLOCK + col - row
    bucket = _t5_bucket(jnp.maximum(dist, 0))
    tile = jnp.zeros((ATT_BLOCK, ATT_BLOCK), F32)
    for b in range(REL_BUCKETS):
        tile = jnp.where(bucket == b, rb_ref[b, h], tile)
    o_ref[0, 0] = (tile - rb_ref[far_bucket, h]) * LOG2E


def _rel_bias_tiles(rel_bias, far_bucket):
    return pl.pallas_call(
        functools.partial(_bias_kernel, far_bucket=far_bucket),
        grid=(2, N_HEADS),
        in_specs=[pl.BlockSpec(memory_space=pltpu.SMEM)],
        out_specs=pl.BlockSpec((1, 1, ATT_BLOCK, ATT_BLOCK), lambda o, h: (o, h, 0, 0)),
        out_shape=jax.ShapeDtypeStruct((2, N_HEADS, ATT_BLOCK, ATT_BLOCK), F32),
        compiler_params=pltpu.CompilerParams(dimension_semantics=("arbitrary", "arbitrary")),
        name="rel_bias",
    )(rel_bias)


def _attn_kernel(q_ref, qi_ref, wit_ref, k_ref, vt_ref, kia_ref, kib_ref, bt_ref, o_ref,
                 sc_ref, zc_ref, ranked_ref, qh_ref, lg_ref, p_ref, st_ref, acc_ref, *, topk, seq):
    blk = ATT_BLOCK
    groups = blk // SUBLANES
    i = pl.program_id(1)
    nch = i + 1
    kf = float(topk)
    inf = jnp.inf

    def chunk_start(j):
        return j * blk if isinstance(j, int) else pl.multiple_of(j * blk, blk)

    def as_groups(x):
        return x.reshape(groups, SUBLANES, blk)

    def lanes8(v):
        return jnp.broadcast_to(v, (SUBLANES, blk))

    def colmin(x8):
        return jnp.min(x8, axis=0, keepdims=True)

    def colmax(x8):
        return jnp.max(x8, axis=0, keepdims=True)

    def colsum(x8):
        return jnp.sum(x8, axis=0, keepdims=True)

    wf = wit_ref[0] * (IDX_DIM ** -0.5)
    qi = qi_ref[0]
    row = lax.broadcasted_iota(jnp.int32, (blk, blk), 0)
    col = lax.broadcasted_iota(jnp.int32, (blk, blk), 1)
    causal = row <= col

    def chunk_scores(j):
        r0 = chunk_start(j)
        ka = kia_ref[0, pl.ds(r0, blk), :]
        kb = kib_ref[0, pl.ds(r0, blk), :]
        s = jnp.zeros((blk, blk), F32)
        for hp in range(IDX_HEADS // 2):
            qp = qi[:, hp * LANES:(hp + 1) * LANES]
            a0 = lax.dot_general(ka, qp, NT_DIMS, preferred_element_type=F32)
            a1 = lax.dot_general(kb, qp, NT_DIMS, preferred_element_type=F32)
            s = s + jnp.maximum(a0, 0.0) * wf[2 * hp:2 * hp + 1, :]
            s = s + jnp.maximum(a1, 0.0) * wf[2 * hp + 1:2 * hp + 2, :]
        return r0, s * (IDX_HEADS ** -0.5)

    def score_stats(j, s_lo, s_hi, stats):
        mn8, mx8, pos8, zer8 = stats
        hi3 = as_groups(s_hi)
        zer8 = zer8 + jnp.sum(jnp.where(hi3 == 0.0, 1.0, 0.0), axis=0)
        zc_ref[j] = zer8
        return (jnp.minimum(mn8, jnp.min(as_groups(s_lo), axis=0)),
                jnp.maximum(mx8, jnp.max(hi3, axis=0)),
                pos8 + jnp.sum(jnp.where(hi3 > 0.0, 1.0, 0.0), axis=0),
                zer8)

    def score_body(j, stats):
        r0, s = chunk_scores(j)
        sc_ref[pl.ds(r0, blk), :] = s
        return score_stats(j, s, s, stats)

    def score_pair(jp, stats):
        return score_body(2 * jp + 1, score_body(2 * jp, stats))

    zeros8 = jnp.zeros((SUBLANES, blk), F32)
    stats = (jnp.full((SUBLANES, blk), inf, F32), jnp.full((SUBLANES, blk), -inf, F32), zeros8, zeros8)
    stats = lax.fori_loop(0, i // 2, score_pair, stats)
    stats = lax.cond(i % 2 == 1, lambda st: score_body(i - 1, st), lambda st: st, stats)
    r_diag, s_diag = chunk_scores(i)
    s_diag_hi = jnp.where(causal, s_diag, -inf)
    sc_ref[pl.ds(r_diag, blk), :] = s_diag_hi
    mn8, mx8, pos8, zer8 = score_stats(i, jnp.where(causal, s_diag, inf), s_diag_hi, stats)

    @pl.when(nch % 2 == 1)
    def _():
        sc_ref[pl.ds(chunk_start(nch), blk), :] = jnp.full((blk, blk), -inf, F32)

    npair = (nch + 1) // 2

    def count_gt(thr):
        t8 = lanes8(thr)

        def body(jp, accs):
            base = pl.multiple_of(jp * 2 * blk, 2 * blk)
            accs = list(accs)
            for s in range(2 * blk // COUNT_ROWS):
                x = sc_ref[pl.ds(base + s * COUNT_ROWS, COUNT_ROWS), :]
                for g in range(COUNT_ROWS // SUBLANES):
                    hit = jnp.where(x[g * SUBLANES:(g + 1) * SUBLANES] > t8, 1.0, 0.0)
                    accs[g % COUNT_CHAINS] = accs[g % COUNT_CHAINS] + hit
            return tuple(accs)

        accs = lax.fori_loop(0, npair, body, tuple(jnp.zeros((SUBLANES, blk), F32) for _ in range(COUNT_CHAINS)))
        return colsum(sum(accs[1:], accs[0]))

    def bisect(_, st):
        lo, hi, low, done = st
        mid = 0.5 * lo + 0.5 * hi
        c = count_gt(mid)
        live = done < 0.5
        up = jnp.logical_and(live, c >= kf)
        down = jnp.logical_and(live, c < kf)
        low = jnp.where(up, mid, low)
        lo = jnp.where(up, mid, lo)
        hi = jnp.where(down, mid, hi)
        done = jnp.where(c == kf, 1.0, done)
        return lo, hi, low, done

    tpos = i * blk + lax.broadcasted_iota(jnp.int32, (1, blk), 1)
    pos, zer = colsum(pos8), colsum(zer8)
    mn, mx = colmin(mn8), colmax(mx8)
    zero_tie = jnp.logical_and(pos < kf, pos + zer >= kf)
    pos_ge = pos >= kf
    done0 = jnp.where(jnp.logical_or(jnp.logical_or(tpos + 1 <= topk, zero_tie), pos == kf), 1.0, 0.0)
    low0 = jnp.where(jnp.logical_or(zero_tie, pos_ge), 0.0, -inf)
    lo0 = jnp.where(pos_ge, jnp.maximum(mn, 0.0), mn)
    hi0 = jnp.where(pos_ge, mx, jnp.minimum(mx, 0.0))
    tie0 = jnp.where(zero_tie, 0.0, inf)
    need0 = jnp.where(zero_tie, kf - pos, 0.0)

    def next_value_above(thr):
        t8 = lanes8(thr)

        def body(j, acc):
            x = as_groups(sc_ref[pl.ds(chunk_start(j), blk), :])
            return jnp.minimum(acc, jnp.min(jnp.where(x > t8[None], x, inf), axis=0))

        return colmin(lax.fori_loop(0, nch, body, jnp.full((SUBLANES, blk), inf, F32)))

    def climb_cond(st):
        return jnp.min(st[1]) < 0.5

    def climb_body(st):
        low, done, tie, need = st
        live = done < 0.5
        cand = next_value_above(low)
        cgt = count_gt(cand)
        found = jnp.logical_and(live, cgt < kf)
        tie = jnp.where(found, cand, tie)
        need = jnp.where(found, kf - cgt, need)
        low = jnp.where(live, cand, low)
        done = jnp.where(jnp.logical_and(live, cgt <= kf), 1.0, done)
        return low, done, tie, need

    def select(_):
        _, _, low, done = lax.fori_loop(0, BISECT_STEPS, bisect, (lo0, hi0, low0, done0))
        low, _, tie, need = lax.while_loop(climb_cond, climb_body, (low, done, tie0, need0))
        return low, tie, need

    low, tie, need = lax.cond(jnp.min(done0) < 0.5, select, lambda _: (low0, tie0, need0), 0)

    def plain_mask(_):
        low8 = lanes8(low)

        def body(j, _):
            r0 = chunk_start(j)
            x = as_groups(sc_ref[pl.ds(r0, blk), :])
            sc_ref[pl.ds(r0, blk), :] = jnp.where(x > low8[None], 0.0, -inf).reshape(blk, blk)
            return 0

        return lax.fori_loop(0, nch, body, 0)

    def ranked_chunk(j, tri, base):
        r0 = chunk_start(j)
        x = sc_ref[pl.ds(r0, blk), :]
        eq = x == tie
        rank = jnp.dot(tri, jnp.where(eq, 1.0, 0.0).astype(BF16), preferred_element_type=F32)
        sel = jnp.logical_or(x > low, jnp.logical_and(eq, rank + base <= need))
        sc_ref[pl.ds(r0, blk), :] = jnp.where(sel, 0.0, -inf)
        return rank[blk - 1:blk, :]

    def tie_mask(_):
        tri = jnp.where(row >= col, 1.0, 0.0).astype(BF16)
        lax.fori_loop(0, nch, lambda j, base: base + ranked_chunk(j, tri, base), jnp.zeros((1, blk), F32))
        return 0

    def zero_tie_mask(_):
        tri = jnp.where(row >= col, 1.0, 0.0).astype(BF16)

        def cut_body(j, ncut):
            return ncut + jnp.where(colsum(zc_ref[j]) < need, 1.0, 0.0)

        ncut = lax.fori_loop(0, nch, cut_body, jnp.zeros((1, blk), F32))
        cut = jnp.where(tie < inf, ncut, inf)

        def flag_body(j, _):
            ranked_ref[j] = 0
            return 0

        lax.fori_loop(0, nch, flag_body, 0)

        def cut_cond(pending):
            return jnp.min(pending) < inf

        def cut_chunk(pending):
            jf = jnp.min(pending)
            j = jf.astype(jnp.int32)
            base = jnp.where(j > 0, colsum(zc_ref[jnp.maximum(j - 1, 0)]), 0.0)
            ranked_chunk(j, tri, base)
            ranked_ref[j] = 1
            return jnp.where(pending == jf, inf, pending)

        lax.while_loop(cut_cond, cut_chunk, cut)
        low8, cut8 = lanes8(low), lanes8(cut)

        def body(j, _):
            @pl.when(ranked_ref[j] == 0)
            def _():
                r0 = chunk_start(j)
                x = as_groups(sc_ref[pl.ds(r0, blk), :])
                at_low = jnp.where(jnp.logical_and(cut8 < inf, cut8 > j.astype(F32)), 0.0, -inf)
                out = jnp.where(x > low8[None], 0.0, jnp.where(x == low8[None], at_low[None], -inf))
                sc_ref[pl.ds(r0, blk), :] = out.reshape(blk, blk)
            return 0

        return lax.fori_loop(0, nch, body, 0)

    tied = tie < inf
    mask_kind = jnp.where(jnp.max(jnp.where(tied, 1.0, 0.0)) < 0.5, 0,
                          jnp.where(jnp.max(jnp.where(jnp.logical_and(tied, tie != 0.0), 1.0, 0.0)) < 0.5, 1, 2))
    lax.switch(mask_kind, [plain_mask, zero_tie_mask, tie_mask], 0)

    q = q_ref[0]
    lane = lax.broadcasted_iota(jnp.int32, (blk, LANES), 1)
    for h in range(N_HEADS):
        hp = h // 2
        in_head = (lane < HEAD_DIM) if h % 2 == 0 else (lane >= HEAD_DIM)
        qh_ref[h] = jnp.where(in_head, q[:, hp * LANES:(hp + 1) * LANES], jnp.zeros((), BF16))
    acc_ref[...] = jnp.zeros(acc_ref.shape, F32)

    def logits(j, h, mb, near):
        hp = h // 2
        kc = k_ref[0, pl.ds(chunk_start(j), blk), hp * LANES:(hp + 1) * LANES]
        lg = lax.dot_general(kc, qh_ref[h], NT_DIMS, preferred_element_type=F32) + mb
        return lg if near is None else lg + bt_ref[near, h]

    CMAX, RMAX, RSUM, RESC = 0, 1, 2, 3

    def stage_logits(j, near):
        mb = sc_ref[pl.ds(chunk_start(j), blk), :]
        cm8 = []
        for h in range(N_HEADS):
            lg = logits(j, h, mb, near)
            lg_ref[h] = lg
            cm8.append(jnp.max(as_groups(lg), axis=0))
        for h in range(N_HEADS):
            st_ref[CMAX, h] = lanes8(colmax(cm8[h]))

    def stage_exp():
        for h in range(N_HEADS):
            m_old = st_ref[RMAX, h]
            m_new = jnp.maximum(m_old, st_ref[CMAX, h])
            m_use = jnp.where(m_new == -inf, 0.0, m_new)
            alpha = jnp.exp2(m_old - m_use)
            p = jnp.exp2(as_groups(lg_ref[h]) - m_use[None])
            p_ref[h] = p.reshape(blk, blk).astype(BF16)
            st_ref[RSUM, h] = alpha * st_ref[RSUM, h] + jnp.sum(p, axis=0)
            st_ref[RMAX, h] = m_new
            st_ref[RESC, h] = alpha

    def stage_pv(j):
        for h in range(N_HEADS):
            rows = slice(h * HEAD_DIM, (h + 1) * HEAD_DIM)
            pv = jnp.dot(vt_ref[0, j, rows, :], p_ref[h], preferred_element_type=F32)
            acc = acc_ref[rows, :].reshape(HEAD_DIM // SUBLANES, SUBLANES, blk) * st_ref[RESC, h][None]
            acc_ref[rows, :] = acc.reshape(HEAD_DIM, blk) + pv

    def pipelined(j, near_next):
        stage_exp()
        stage_logits(j + 1, near_next)
        stage_pv(j)

    st_ref[RMAX] = jnp.full((N_HEADS, SUBLANES, blk), -inf, F32)
    st_ref[RSUM] = jnp.zeros((N_HEADS, SUBLANES, blk), F32)

    @pl.when(i == 0)
    def _():
        stage_logits(0, 0)

    @pl.when(i == 1)
    def _():
        stage_logits(0, 1)

    @pl.when(i >= 2)
    def _():
        stage_logits(0, None)

    def far_body(j, _):
        pipelined(j, None)
        return 0

    lax.fori_loop(0, jnp.maximum(i - 2, 0), far_body, 0)

    @pl.when(i >= 2)
    def _():
        pipelined(i - 2, 1)

    @pl.when(i >= 1)
    def _():
        pipelined(i - 1, 0)

    stage_exp()
    stage_pv(i)
    for h in range(N_HEADS):
        rows = slice(h * HEAD_DIM, (h + 1) * HEAD_DIM)
        acc_ref[rows, :] = acc_ref[rows, :] / colsum(st_ref[RSUM, h])
    o_ref[0] = acc_ref[...].T.astype(BF16)


def _attention(q, qi, wit, k, vt, kia, kib, btiles, *, bsz, seq, topk):
    blk = ATT_BLOCK
    nblk = seq // blk
    return pl.pallas_call(
        functools.partial(_attn_kernel, topk=topk, seq=seq),
        grid=(bsz, nblk),
        in_specs=[
            pl.BlockSpec((1, blk, ATTN_DIM), lambda b, i: (b, i, 0)),
            pl.BlockSpec((1, blk, IDX_HEADS * IDX_DIM), lambda b, i: (b, i, 0)),
            pl.BlockSpec((1, IDX_HEADS, blk), lambda b, i: (b, 0, i)),
            pl.BlockSpec((1, seq, ATTN_DIM), lambda b, i: (b, 0, 0)),
            pl.BlockSpec((1, nblk, ATTN_DIM, blk), lambda b, i: (b, 0, 0, 0)),
            pl.BlockSpec((1, seq, LANES), lambda b, i: (b, 0, 0)),
            pl.BlockSpec((1, seq, LANES), lambda b, i: (b, 0, 0)),
            pl.BlockSpec(btiles.shape, lambda b, i: (0, 0, 0, 0)),
        ],
        out_specs=pl.BlockSpec((1, blk, ATTN_DIM), lambda b, i: (b, i, 0)),
        out_shape=jax.ShapeDtypeStruct((bsz, seq, ATTN_DIM), BF16),
        scratch_shapes=[
            pltpu.VMEM((seq + blk, blk), F32),
            pltpu.VMEM((nblk, SUBLANES, blk), F32),
            pltpu.SMEM((nblk,), jnp.int32),
            pltpu.VMEM((N_HEADS, blk, LANES), BF16),
            pltpu.VMEM((N_HEADS, blk, blk), F32),
            pltpu.VMEM((N_HEADS, blk, blk), BF16),
            pltpu.VMEM((4, N_HEADS, SUBLANES, blk), F32),
            pltpu.VMEM((ATTN_DIM, blk), F32),
        ],
        compiler_params=pltpu.CompilerParams(dimension_semantics=("arbitrary", "arbitrary"),
                                             vmem_limit_bytes=V7X_VMEM_LIMIT_BYTES),
        name="attention",
    )(q, qi, wit, k, vt, kia, kib, btiles)


def _mix_kernel(x_ref, xh_ref, mod_ref, att_ref, wpg_ref, wpool_ref, ps_ref, wa_ref, wb_ref, wo_ref,
                lng_ref, lnb_ref, o_ref, pe_ref, mix_ref, *, alpha, mod_row, ln_row, tiles_per_seq):
    rows = x_ref.shape[0]
    i = pl.program_id(0)
    seq_tile = i % tiles_per_seq
    x = x_ref[...]
    sh = mod_ref[0, mod_row:mod_row + 1, :]
    sc = mod_ref[0, mod_row + 1:mod_row + 2, :]
    gate = mod_ref[0, mod_row + 2:mod_row + 3, :]
    u = (x * (1.0 + sc) + sh).astype(BF16)
    uh = (xh_ref[...] * (1.0 + sc) + sh).astype(BF16)
    pg = jnp.dot(u, wpg_ref[...], preferred_element_type=F32)
    ph = jnp.dot(uh, wpg_ref[:, :POOL_DIM], preferred_element_type=F32)
    pe_ref[0:POOL_HALO, :] = jnp.where(seq_tile == 0, 0.0, ph)
    pe_ref[POOL_HALO:, :] = pg[:, :POOL_DIM]
    t = seq_tile * rows + lax.broadcasted_iota(jnp.int32, (rows, 1), 0)
    for g, w in enumerate(POOL_WINDOWS):
        cols = slice(g * POOL_GROUP_DIM, (g + 1) * POOL_GROUP_DIM)
        cur = pe_ref[POOL_HALO:, cols]
        win = cur
        for back in range(1, w):
            win = win + pe_ref[POOL_HALO - back:POOL_HALO - back + rows, cols]
        cnt = jnp.minimum(t + 1, w).astype(F32)
        pooled = (win / cnt - cur).astype(BF16)
        mixed = jnp.dot(pooled, wpool_ref[g], preferred_element_type=F32)
        mix_ref[:, cols] = (mixed * ps_ref[:, cols]).astype(BF16)
    y_a = jnp.dot(mix_ref[...], wa_ref[...], preferred_element_type=F32)
    y_b = jnp.dot(att_ref[...], wb_ref[...], preferred_element_type=F32)
    d = x.shape[1]
    ga = pg[:, POOL_DIM:POOL_DIM + d]
    gb = pg[:, POOL_DIM + d:]
    merged = (jax.nn.sigmoid(ga) * y_a + jax.nn.sigmoid(gb) * y_b).astype(BF16)
    y = jnp.dot(merged, wo_ref[...], preferred_element_type=F32)
    z = alpha * x + gate * y
    o_ref[...] = _layer_norm(z, lng_ref[ln_row:ln_row + 1, :], lnb_ref[ln_row:ln_row + 1, :])


def _mix_out(x2d, mod, att2d, wpg, wpool, pool_scale, wa, wb, wo, ln_g, ln_b, *, seq, alpha, mod_row, ln_row):
    n, d = x2d.shape
    tiles_per_seq = seq // MIX_ROWS
    halo_blocks = MIX_ROWS // POOL_HALO
    resident = dict(pipeline_mode=pl.Buffered(1))
    full = lambda a: pl.BlockSpec(a.shape, lambda i: (0,) * a.ndim, **resident)
    return pl.pallas_call(
        functools.partial(_mix_kernel, alpha=alpha, mod_row=mod_row, ln_row=ln_row, tiles_per_seq=tiles_per_seq),
        grid=(n // MIX_ROWS,),
        in_specs=[
            pl.BlockSpec((MIX_ROWS, d), lambda i: (i, 0)),
            pl.BlockSpec((POOL_HALO, d), lambda i: (jnp.maximum(i * halo_blocks - 1, 0), 0)),
            pl.BlockSpec((1, N_ADA, d), lambda i: (i // tiles_per_seq, 0, 0)),
            pl.BlockSpec((MIX_ROWS, ATTN_DIM), lambda i: (i, 0)),
            full(wpg), full(wpool), full(pool_scale), full(wa), full(wb), full(wo),
            pl.BlockSpec(ln_g.shape, lambda i: (0, 0)),
            pl.BlockSpec(ln_b.shape, lambda i: (0, 0)),
        ],
        out_specs=pl.BlockSpec((MIX_ROWS, d), lambda i: (i, 0)),
        out_shape=jax.ShapeDtypeStruct((n, d), F32),
        scratch_shapes=[
            pltpu.VMEM((POOL_HALO + MIX_ROWS, POOL_DIM), F32),
            pltpu.VMEM((MIX_ROWS, POOL_DIM), BF16),
        ],
        compiler_params=pltpu.CompilerParams(dimension_semantics=("arbitrary",),
                                             vmem_limit_bytes=V7X_VMEM_LIMIT_BYTES),
        name="mix_out",
    )(x2d, x2d, mod, att2d, wpg, wpool, pool_scale, wa, wb, wo, ln_g, ln_b)


def kernel(x, c, w_ada, b_ada, ln_g, ln_b, ffn1_w_gate, ffn1_w_up, ffn1_w_down, w_in, w_pool, pool_scale,
           w_a, w_b, w_out, rel_bias, ffn2_w_gate, ffn2_w_up, ffn2_w_down):
    bsz, seq, d = x.shape
    depth = w_ada.shape[0]
    alpha = (2.0 * depth) ** 0.25
    topk = min(TOP_K, seq // 4)
    assert seq % FFN_ROWS == 0 and seq % PROJ_ROWS == 0 and seq % MIX_ROWS == 0 and seq % ATT_BLOCK == 0
    assert PROJ_ROWS % ATT_BLOCK == 0 and MIX_ROWS % POOL_HALO == 0 and POOL_HALO >= max(POOL_WINDOWS) - 1
    far_bucket = _far_bucket(ATT_BLOCK + 1, max(seq - 1, ATT_BLOCK + 1))

    o_q = POOL_DIM
    o_k = o_q + ATTN_DIM
    o_v = o_k + ATTN_DIM
    o_qi = o_v + ATTN_DIM
    o_ki = o_qi + IDX_HEADS * IDX_DIM
    o_wi = o_ki + IDX_DIM
    o_ga = o_wi + IDX_HEADS

    btiles = _rel_bias_tiles(rel_bias, far_bucket)
    x2d = x.reshape(bsz * seq, d)
    for l in range(depth):
        wl = w_in[l]
        zeros_ki = jnp.zeros((d, LANES - IDX_DIM), wl.dtype)
        w_ki = wl[:, o_ki:o_wi]
        wqk = jnp.concatenate([wl[:, o_q:o_v], wl[:, o_qi:o_ki]], axis=1).astype(BF16)
        wv_t = wl[:, o_v:o_qi].T.astype(BF16)
        wki = jnp.concatenate([w_ki, zeros_ki, zeros_ki, w_ki], axis=1).astype(BF16)
        wwi_t = jnp.pad(wl[:, o_wi:o_ga].T, ((0, 2 * SUBLANES - IDX_HEADS), (0, 0))).astype(BF16)
        wpg = jnp.concatenate([wl[:, :POOL_DIM], wl[:, o_ga:]], axis=1).astype(BF16)

        mod = _ada_mod(c, w_ada[l], b_ada[l]).reshape(bsz, N_ADA, d)
        x2d = _ffn(x2d, mod, ln_g[l], ln_b[l], ffn1_w_gate[l].astype(BF16), ffn1_w_up[l].astype(BF16),
                   ffn1_w_down[l].astype(BF16), seq=seq, alpha=alpha, mod_row=0, ln_row=0)
        q, k, qi, vt, kia, kib, wit = _attn_proj(x2d, mod, wqk, wv_t, wki, wwi_t, bsz=bsz, seq=seq, mod_row=3)
        att = _attention(q.reshape(bsz, seq, ATTN_DIM), qi.reshape(bsz, seq, IDX_HEADS * IDX_DIM), wit,
                         k.reshape(bsz, seq, ATTN_DIM), vt, kia.reshape(bsz, seq, LANES),
                         kib.reshape(bsz, seq, LANES), btiles, bsz=bsz, seq=seq, topk=topk)
        x2d = _mix_out(x2d, mod, att.reshape(bsz * seq, ATTN_DIM), wpg, w_pool[l].astype(BF16),
                       pool_scale[l].reshape(1, POOL_DIM), w_a[l].astype(BF16), w_b[l].astype(BF16),
                       w_out[l].astype(BF16), ln_g[l], ln_b[l], seq=seq, alpha=alpha, mod_row=3, ln_row=1)
        x2d = _ffn(x2d, mod, ln_g[l], ln_b[l], ffn2_w_gate[l].astype(BF16), ffn2_w_up[l].astype(BF16),
                   ffn2_w_down[l].astype(BF16), seq=seq, alpha=alpha, mod_row=6, ln_row=2)
    return x2d.reshape(bsz, seq, d)
```

```python
import functools
import math

import numpy as np
import jax
import jax.numpy as jnp
from jax import lax
from jax.experimental import pallas as pl
from jax.experimental.pallas import tpu as pltpu

POOL_WINDOWS = (2, 4, 8, 16)
POOL_GROUP_DIM = 128
POOL_DIM = len(POOL_WINDOWS) * POOL_GROUP_DIM
N_HEADS = 8
HEAD_DIM = 64
ATTN_DIM = N_HEADS * HEAD_DIM
HEAD_PAD = 16
HEAD_ROWS = HEAD_DIM + HEAD_PAD
IDX_HEADS = 8
IDX_DIM = 64
TOP_K = 256
REL_BUCKETS = 32
REL_MAX_DIST = 128
N_ADA = 9
LN_EPS = 1e-5
POOL_HALO = 16

LANES = 128
SUBLANES = 8
V7X_VMEM_LIMIT_BYTES = 56 * 1024 * 1024
FFN_ROWS = 512
FFN_COLS = 256
PROJ_ROWS = 512
MIX_ROWS = 512
ATT_BLOCK = 256
ADA_COLS = 1024
COUNT_CHAINS = 4
COUNT_ROWS = 64
BISECT_STEPS = 14

LOG2E = math.log2(math.e)
BF16 = jnp.bfloat16
F32 = jnp.float32
NT_DIMS = (((1,), (1,)), ((), ()))


def _layer_norm(z, g, b):
    mu = jnp.mean(z, axis=-1, keepdims=True)
    zc = z - mu
    var = jnp.mean(zc * zc, axis=-1, keepdims=True)
    return zc * lax.rsqrt(var + LN_EPS) * g + b


def _silu(a):
    return a * jax.nn.sigmoid(a)


def _ada_kernel(c_ref, w_ref, b_ref, o_ref):
    a = _silu(c_ref[...])
    o_ref[...] = jnp.dot(a, w_ref[...], preferred_element_type=F32) + b_ref[...]


def _ada_mod(c, w_ada, b_ada):
    bsz, d = c.shape
    n = w_ada.shape[1]
    rows = -(-bsz // SUBLANES) * SUBLANES
    c_pad = jnp.pad(c, ((0, rows - bsz), (0, 0)))
    out = pl.pallas_call(
        _ada_kernel,
        grid=(n // ADA_COLS,),
        in_specs=[
            pl.BlockSpec((rows, d), lambda j: (0, 0)),
            pl.BlockSpec((d, ADA_COLS), lambda j: (0, j)),
            pl.BlockSpec((1, ADA_COLS), lambda j: (0, j)),
        ],
        out_specs=pl.BlockSpec((rows, ADA_COLS), lambda j: (0, j)),
        out_shape=jax.ShapeDtypeStruct((rows, n), F32),
        compiler_params=pltpu.CompilerParams(dimension_semantics=("arbitrary",),
                                             vmem_limit_bytes=V7X_VMEM_LIMIT_BYTES),
        name="ada_mod",
    )(c_pad, w_ada, b_ada.reshape(1, n))
    return out[:bsz]


def _ffn_kernel(x_ref, mod_ref, lng_ref, lnb_ref, wg_ref, wu_ref, wd_ref, o_ref, h_ref, *, alpha, mod_row, ln_row):
    x = x_ref[...]
    sh = mod_ref[0, mod_row:mod_row + 1, :]
    sc = mod_ref[0, mod_row + 1:mod_row + 2, :]
    gate = mod_ref[0, mod_row + 2:mod_row + 3, :]
    u = (x * (1.0 + sc) + sh).astype(BF16)
    d_ff = wg_ref.shape[1]
    for c in range(d_ff // FFN_COLS):
        sl = slice(c * FFN_COLS, (c + 1) * FFN_COLS)
        a = jnp.dot(u, wg_ref[:, sl], preferred_element_type=F32)
        b = jnp.dot(u, wu_ref[:, sl], preferred_element_type=F32)
        h_ref[:, sl] = (_silu(a) * b).astype(BF16)
    y = jnp.dot(h_ref[...], wd_ref[...], preferred_element_type=F32)
    z = alpha * x + (0.5 * gate) * y
    o_ref[...] = _layer_norm(z, lng_ref[ln_row:ln_row + 1, :], lnb_ref[ln_row:ln_row + 1, :])


def _ffn(x2d, mod, ln_g, ln_b, wg, wu, wd, *, seq, alpha, mod_row, ln_row):
    n, d = x2d.shape
    d_ff = wg.shape[1]
    tiles_per_seq = seq // FFN_ROWS
    resident = dict(pipeline_mode=pl.Buffered(1))
    return pl.pallas_call(
        functools.partial(_ffn_kernel, alpha=alpha, mod_row=mod_row, ln_row=ln_row),
        grid=(n // FFN_ROWS,),
        in_specs=[
            pl.BlockSpec((FFN_ROWS, d), lambda i: (i, 0)),
            pl.BlockSpec((1, N_ADA, d), lambda i: (i // tiles_per_seq, 0, 0)),
            pl.BlockSpec(ln_g.shape, lambda i: (0, 0)),
            pl.BlockSpec(ln_b.shape, lambda i: (0, 0)),
            pl.BlockSpec((d, d_ff), lambda i: (0, 0), **resident),
            pl.BlockSpec((d, d_ff), lambda i: (0, 0), **resident),
            pl.BlockSpec((d_ff, d), lambda i: (0, 0), **resident),
        ],
        out_specs=pl.BlockSpec((FFN_ROWS, d), lambda i: (i, 0)),
        out_shape=jax.ShapeDtypeStruct((n, d), F32),
        scratch_shapes=[pltpu.VMEM((FFN_ROWS, d_ff), BF16)],
        compiler_params=pltpu.CompilerParams(dimension_semantics=("arbitrary",),
                                             vmem_limit_bytes=V7X_VMEM_LIMIT_BYTES),
        name="ffn",
    )(x2d, mod, ln_g, ln_b, wg, wu, wd)


def _proj_kernel(x_ref, mod_ref, wqk_ref, wv_ref, wki_ref, wwi_ref,
                 q_ref, k_ref, qi_ref, vt_ref, kia_ref, kib_ref, wit_ref, *, mod_row):
    x = x_ref[...]
    sh = mod_ref[0, mod_row:mod_row + 1, :]
    sc = mod_ref[0, mod_row + 1:mod_row + 2, :]
    u = (x * (1.0 + sc) + sh).astype(BF16)
    qkq = jnp.dot(u, wqk_ref[...], preferred_element_type=F32)
    q_ref[...] = (qkq[:, :ATTN_DIM] * (HEAD_DIM ** -0.5 * LOG2E)).astype(BF16)
    k_ref[...] = qkq[:, ATTN_DIM:2 * ATTN_DIM].astype(BF16)
    qi_ref[...] = qkq[:, 2 * ATTN_DIM:].astype(BF16)
    kk = jnp.dot(u, wki_ref[...], preferred_element_type=F32)
    kia_ref[...] = kk[:, :LANES].astype(BF16)
    kib_ref[...] = kk[:, LANES:].astype(BF16)
    vt = lax.dot_general(wv_ref[...], u, NT_DIMS, preferred_element_type=F32).astype(BF16)
    ones_rows = jnp.where(lax.broadcasted_iota(jnp.int32, (HEAD_PAD, ATT_BLOCK), 0) == 0, 1.0, 0.0).astype(BF16)
    for c in range(vt_ref.shape[1]):
        for h in range(N_HEADS):
            vt_ref[0, c, h * HEAD_ROWS:h * HEAD_ROWS + HEAD_DIM, :] = \
                vt[h * HEAD_DIM:(h + 1) * HEAD_DIM, c * ATT_BLOCK:(c + 1) * ATT_BLOCK]
            vt_ref[0, c, h * HEAD_ROWS + HEAD_DIM:(h + 1) * HEAD_ROWS, :] = ones_rows
    wit = lax.dot_general(wwi_ref[...], u, NT_DIMS, preferred_element_type=F32)
    wit_ref[0] = wit[:IDX_HEADS, :]


def _attn_proj(x2d, mod, wqk, wv_t, wki, wwi_t, *, bsz, seq, mod_row):
    n, d = x2d.shape
    tiles_per_seq = seq // PROJ_ROWS
    chunks_per_tile = PROJ_ROWS // ATT_BLOCK
    resident = dict(pipeline_mode=pl.Buffered(1))
    row_spec = lambda cols: pl.BlockSpec((PROJ_ROWS, cols), lambda i: (i, 0))
    return pl.pallas_call(
        functools.partial(_proj_kernel, mod_row=mod_row),
        grid=(n // PROJ_ROWS,),
        in_specs=[
            pl.BlockSpec((PROJ_ROWS, d), lambda i: (i, 0)),
            pl.BlockSpec((1, N_ADA, d), lambda i: (i // tiles_per_seq, 0, 0)),
            pl.BlockSpec(wqk.shape, lambda i: (0, 0), **resident),
            pl.BlockSpec(wv_t.shape, lambda i: (0, 0), **resident),
            pl.BlockSpec(wki.shape, lambda i: (0, 0), **resident),
            pl.BlockSpec(wwi_t.shape, lambda i: (0, 0), **resident),
        ],
        out_specs=[
            row_spec(ATTN_DIM), row_spec(ATTN_DIM), row_spec(IDX_HEADS * IDX_DIM),
            pl.BlockSpec((1, chunks_per_tile, N_HEADS * HEAD_ROWS, ATT_BLOCK),
                         lambda i: (i // tiles_per_seq, i % tiles_per_seq, 0, 0)),
            row_spec(LANES), row_spec(LANES),
            pl.BlockSpec((1, IDX_HEADS, PROJ_ROWS), lambda i: (i // tiles_per_seq, 0, i % tiles_per_seq)),
        ],
        out_shape=[
            jax.ShapeDtypeStruct((n, ATTN_DIM), BF16),
            jax.ShapeDtypeStruct((n, ATTN_DIM), BF16),
            jax.ShapeDtypeStruct((n, IDX_HEADS * IDX_DIM), BF16),
            jax.ShapeDtypeStruct((bsz, seq // ATT_BLOCK, N_HEADS * HEAD_ROWS, ATT_BLOCK), BF16),
            jax.ShapeDtypeStruct((n, LANES), BF16),
            jax.ShapeDtypeStruct((n, LANES), BF16),
            jax.ShapeDtypeStruct((bsz, IDX_HEADS, seq), F32),
        ],
        compiler_params=pltpu.CompilerParams(dimension_semantics=("arbitrary",),
                                             vmem_limit_bytes=V7X_VMEM_LIMIT_BYTES),
        name="attn_proj",
    )(x2d, mod, wqk, wv_t, wki, wwi_t)


def _t5_bucket(n):
    max_exact = REL_BUCKETS // 2
    nf = jnp.maximum(n, 1).astype(F32)
    large = max_exact + (jnp.log(nf / max_exact) / math.log(REL_MAX_DIST / max_exact)
                         * (REL_BUCKETS - max_exact)).astype(jnp.int32)
    large = jnp.minimum(large, REL_BUCKETS - 1)
    return jnp.where(n < max_exact, n, large)


def _far_bucket(first_dist, last_dist):
    n = np.arange(first_dist, last_dist + 1, dtype=np.float32)
    max_exact = REL_BUCKETS // 2
    large = max_exact + (np.log(n / np.float32(max_exact)) / np.float32(math.log(REL_MAX_DIST / max_exact))
                         * np.float32(REL_BUCKETS - max_exact)).astype(np.int32)
    buckets = np.where(n < max_exact, n.astype(np.int32), np.minimum(large, REL_BUCKETS - 1))
    assert buckets.min() == buckets.max(), "key chunks two or more blocks away must share one bias bucket"
    return int(buckets[0])


def _bias_kernel(rb_ref, o_ref, *, far_bucket):
    o = pl.program_id(0)
    row = lax.broadcasted_iota(jnp.int32, (ATT_BLOCK, ATT_BLOCK), 0)
    col = lax.broadcasted_iota(jnp.int32, (ATT_BLOCK, ATT_BLOCK), 1)
    dist = o * ATT_BLOCK + col - row
    bucket = _t5_bucket(jnp.maximum(dist, 0))
    for h in range(N_HEADS):
        tile = jnp.zeros((ATT_BLOCK, ATT_BLOCK), F32)
        for b in range(REL_BUCKETS):
            tile = jnp.where(bucket == b, rb_ref[b, h], tile)
        o_ref[0, h] = (tile - rb_ref[far_bucket, h]) * LOG2E


def _rel_bias_tiles(rel_bias, far_bucket):
    return pl.pallas_call(
        functools.partial(_bias_kernel, far_bucket=far_bucket),
        grid=(2,),
        in_specs=[pl.BlockSpec(memory_space=pltpu.SMEM)],
        out_specs=pl.BlockSpec((1, N_HEADS, ATT_BLOCK, ATT_BLOCK), lambda o: (o, 0, 0, 0)),
        out_shape=jax.ShapeDtypeStruct((2, N_HEADS, ATT_BLOCK, ATT_BLOCK), F32),
        compiler_params=pltpu.CompilerParams(dimension_semantics=("arbitrary",)),
        name="rel_bias",
    )(rel_bias)


def _attn_kernel(q_ref, qi_ref, wit_ref, k_ref, vt_ref, kia_ref, kib_ref, bt_ref, o_ref,
                 sc_ref, zc_ref, ranked_ref, qh_ref, lg_ref, p_ref, st_ref, acc_ref, *, topk, seq):
    blk = ATT_BLOCK
    groups = blk // SUBLANES
    i = pl.program_id(1)
    nch = i + 1
    kf = float(topk)
    inf = jnp.inf

    def chunk_start(j):
        return j * blk if isinstance(j, int) else pl.multiple_of(j * blk, blk)

    def as_groups(x):
        return x.reshape(groups, SUBLANES, blk)

    def lanes8(v):
        return jnp.broadcast_to(v, (SUBLANES, blk))

    def colmin(x8):
        return jnp.min(x8, axis=0, keepdims=True)

    def colmax(x8):
        return jnp.max(x8, axis=0, keepdims=True)

    def colsum(x8):
        return jnp.sum(x8, axis=0, keepdims=True)

    wf = wit_ref[0] * (IDX_DIM ** -0.5)
    qi = qi_ref[0]
    row = lax.broadcasted_iota(jnp.int32, (blk, blk), 0)
    col = lax.broadcasted_iota(jnp.int32, (blk, blk), 1)
    causal = row <= col

    def chunk_scores(j):
        r0 = chunk_start(j)
        ka = kia_ref[0, pl.ds(r0, blk), :]
        kb = kib_ref[0, pl.ds(r0, blk), :]
        s = jnp.zeros((blk, blk), F32)
        for hp in range(IDX_HEADS // 2):
            qp = qi[:, hp * LANES:(hp + 1) * LANES]
            a0 = lax.dot_general(ka, qp, NT_DIMS, preferred_element_type=F32)
            a1 = lax.dot_general(kb, qp, NT_DIMS, preferred_element_type=F32)
            s = s + jnp.maximum(a0, 0.0) * wf[2 * hp:2 * hp + 1, :]
            s = s + jnp.maximum(a1, 0.0) * wf[2 * hp + 1:2 * hp + 2, :]
        return r0, s * (IDX_HEADS ** -0.5)

    def score_stats(j, s_lo, s_hi, stats):
        mn8, mx8, pos8, zer8 = stats
        hi3 = as_groups(s_hi)
        zer8 = zer8 + jnp.sum(jnp.where(hi3 == 0.0, 1.0, 0.0), axis=0)
        zc_ref[j] = zer8
        return (jnp.minimum(mn8, jnp.min(as_groups(s_lo), axis=0)),
                jnp.maximum(mx8, jnp.max(hi3, axis=0)),
                pos8 + jnp.sum(jnp.where(hi3 > 0.0, 1.0, 0.0), axis=0),
                zer8)

    def score_body(j, stats):
        r0, s = chunk_scores(j)
        sc_ref[pl.ds(r0, blk), :] = s
        return score_stats(j, s, s, stats)

    def score_pair(jp, stats):
        return score_body(2 * jp + 1, score_body(2 * jp, stats))

    zeros8 = jnp.zeros((SUBLANES, blk), F32)
    stats = (jnp.full((SUBLANES, blk), inf, F32), jnp.full((SUBLANES, blk), -inf, F32), zeros8, zeros8)
    stats = lax.fori_loop(0, i // 2, score_pair, stats)
    stats = lax.cond(i % 2 == 1, lambda st: score_body(i - 1, st), lambda st: st, stats)
    r_diag, s_diag = chunk_scores(i)
    s_diag_hi = jnp.where(causal, s_diag, -inf)
    sc_ref[pl.ds(r_diag, blk), :] = s_diag_hi
    mn8, mx8, pos8, zer8 = score_stats(i, jnp.where(causal, s_diag, inf), s_diag_hi, stats)

    @pl.when(nch % 2 == 1)
    def _():
        sc_ref[pl.ds(chunk_start(nch), blk), :] = jnp.full((blk, blk), -inf, F32)

    npair = (nch + 1) // 2

    def count_gt(thr):
        t8 = lanes8(thr)

        def body(jp, accs):
            base = pl.multiple_of(jp * 2 * blk, 2 * blk)
            accs = list(accs)
            for s in range(2 * blk // COUNT_ROWS):
                x = sc_ref[pl.ds(base + s * COUNT_ROWS, COUNT_ROWS), :]
                for g in range(COUNT_ROWS // SUBLANES):
                    hit = jnp.where(x[g * SUBLANES:(g + 1) * SUBLANES] > t8, 1.0, 0.0)
                    accs[g % COUNT_CHAINS] = accs[g % COUNT_CHAINS] + hit
            return tuple(accs)

        accs = lax.fori_loop(0, npair, body, tuple(jnp.zeros((SUBLANES, blk), F32) for _ in range(COUNT_CHAINS)))
        return colsum(sum(accs[1:], accs[0]))

    def bisect(_, st):
        lo, hi, low, done = st
        mid = 0.5 * lo + 0.5 * hi
        c = count_gt(mid)
        live = done < 0.5
        up = jnp.logical_and(live, c >= kf)
        down = jnp.logical_and(live, c < kf)
        low = jnp.where(up, mid, low)
        lo = jnp.where(up, mid, lo)
        hi = jnp.where(down, mid, hi)
        done = jnp.where(c == kf, 1.0, done)
        return lo, hi, low, done

    tpos = i * blk + lax.broadcasted_iota(jnp.int32, (1, blk), 1)
    pos, zer = colsum(pos8), colsum(zer8)
    mn, mx = colmin(mn8), colmax(mx8)
    zero_tie = jnp.logical_and(pos < kf, pos + zer >= kf)
    pos_ge = pos >= kf
    done0 = jnp.where(jnp.logical_or(jnp.logical_or(tpos + 1 <= topk, zero_tie), pos == kf), 1.0, 0.0)
    low0 = jnp.where(jnp.logical_or(zero_tie, pos_ge), 0.0, -inf)
    lo0 = jnp.where(pos_ge, jnp.maximum(mn, 0.0), mn)
    hi0 = jnp.where(pos_ge, mx, jnp.minimum(mx, 0.0))
    tie0 = jnp.where(zero_tie, 0.0, inf)
    need0 = jnp.where(zero_tie, kf - pos, 0.0)

    def next_value_above(thr):
        t8 = lanes8(thr)

        def body(j, acc):
            x = as_groups(sc_ref[pl.ds(chunk_start(j), blk), :])
            return jnp.minimum(acc, jnp.min(jnp.where(x > t8[None], x, inf), axis=0))

        return colmin(lax.fori_loop(0, nch, body, jnp.full((SUBLANES, blk), inf, F32)))

    def climb_cond(st):
        return jnp.min(st[1]) < 0.5

    def climb_body(st):
        low, done, tie, need = st
        live = done < 0.5
        cand = next_value_above(low)
        cgt = count_gt(cand)
        found = jnp.logical_and(live, cgt < kf)
        tie = jnp.where(found, cand, tie)
        need = jnp.where(found, kf - cgt, need)
        low = jnp.where(live, cand, low)
        done = jnp.where(jnp.logical_and(live, cgt <= kf), 1.0, done)
        return low, done, tie, need

    def select(_):
        _, _, low, done = lax.fori_loop(0, BISECT_STEPS, bisect, (lo0, hi0, low0, done0))
        low, _, tie, need = lax.while_loop(climb_cond, climb_body, (low, done, tie0, need0))
        return low, tie, need

    low, tie, need = lax.cond(jnp.min(done0) < 0.5, select, lambda _: (low0, tie0, need0), 0)

    def plain_mask(_):
        low8 = lanes8(low)

        def body(j, _):
            r0 = chunk_start(j)
            x = as_groups(sc_ref[pl.ds(r0, blk), :])
            sc_ref[pl.ds(r0, blk), :] = jnp.where(x > low8[None], 0.0, -inf).reshape(blk, blk)
            return 0

        return lax.fori_loop(0, nch, body, 0)

    def ranked_chunk(j, tri, base):
        r0 = chunk_start(j)
        x = sc_ref[pl.ds(r0, blk), :]
        eq = x == tie
        rank = jnp.dot(tri, jnp.where(eq, 1.0, 0.0).astype(BF16), preferred_element_type=F32)
        sel = jnp.logical_or(x > low, jnp.logical_and(eq, rank + base <= need))
        sc_ref[pl.ds(r0, blk), :] = jnp.where(sel, 0.0, -inf)
        return rank[blk - 1:blk, :]

    def tie_mask(_):
        tri = jnp.where(row >= col, 1.0, 0.0).astype(BF16)
        lax.fori_loop(0, nch, lambda j, base: base + ranked_chunk(j, tri, base), jnp.zeros((1, blk), F32))
        return 0

    def zero_tie_mask(_):
        tri = jnp.where(row >= col, 1.0, 0.0).astype(BF16)

        def cut_body(j, ncut):
            return ncut + jnp.where(colsum(zc_ref[j]) < need, 1.0, 0.0)

        ncut = lax.fori_loop(0, nch, cut_body, jnp.zeros((1, blk), F32))
        cut = jnp.where(tie < inf, ncut, inf)

        def flag_body(j, _):
            ranked_ref[j] = 0
            return 0

        lax.fori_loop(0, nch, flag_body, 0)

        def cut_cond(pending):
            return jnp.min(pending) < inf

        def cut_chunk(pending):
            jf = jnp.min(pending)
            j = jf.astype(jnp.int32)
            base = jnp.where(j > 0, colsum(zc_ref[jnp.maximum(j - 1, 0)]), 0.0)
            ranked_chunk(j, tri, base)
            ranked_ref[j] = 1
            return jnp.where(pending == jf, inf, pending)

        lax.while_loop(cut_cond, cut_chunk, cut)
        low8, cut8 = lanes8(low), lanes8(cut)

        def body(j, _):
            @pl.when(ranked_ref[j] == 0)
            def _():
                r0 = chunk_start(j)
                x = as_groups(sc_ref[pl.ds(r0, blk), :])
                at_low = jnp.where(jnp.logical_and(cut8 < inf, cut8 > j.astype(F32)), 0.0, -inf)
                out = jnp.where(x > low8[None], 0.0, jnp.where(x == low8[None], at_low[None], -inf))
                sc_ref[pl.ds(r0, blk), :] = out.reshape(blk, blk)
            return 0

        return lax.fori_loop(0, nch, body, 0)

    tied = tie < inf
    mask_kind = jnp.where(jnp.max(jnp.where(tied, 1.0, 0.0)) < 0.5, 0,
                          jnp.where(jnp.max(jnp.where(jnp.logical_and(tied, tie != 0.0), 1.0, 0.0)) < 0.5, 1, 2))
    lax.switch(mask_kind, [plain_mask, zero_tie_mask, tie_mask], 0)

    q = q_ref[0]
    lane = lax.broadcasted_iota(jnp.int32, (blk, LANES), 1)
    for h in range(N_HEADS):
        hp = h // 2
        in_head = (lane < HEAD_DIM) if h % 2 == 0 else (lane >= HEAD_DIM)
        qh_ref[h] = jnp.where(in_head, q[:, hp * LANES:(hp + 1) * LANES], jnp.zeros((), BF16))
    acc_ref[...] = jnp.zeros(acc_ref.shape, F32)

    def logits(j, h, mb, near):
        hp = h // 2
        kc = k_ref[0, pl.ds(chunk_start(j), blk), hp * LANES:(hp + 1) * LANES]
        lg = lax.dot_general(kc, qh_ref[h], NT_DIMS, preferred_element_type=F32) + mb
        return lg if near is None else lg + bt_ref[near, h]

    CMAX, RMAX, RESC = 0, 1, 2

    def stage_logits(j, near):
        mb = sc_ref[pl.ds(chunk_start(j), blk), :]
        cm8 = []
        for h in range(N_HEADS):
            lg = logits(j, h, mb, near)
            lg_ref[h] = lg
            cm8.append(jnp.max(as_groups(lg), axis=0))
        for h in range(N_HEADS):
            st_ref[CMAX, h] = lanes8(colmax(cm8[h]))

    def stage_exp():
        for h in range(N_HEADS):
            m_old = st_ref[RMAX, h]
            m_new = jnp.maximum(m_old, st_ref[CMAX, h])
            m_use = jnp.where(m_new == -inf, 0.0, m_new)
            p = jnp.exp2(as_groups(lg_ref[h]) - m_use[None])
            p_ref[h] = p.reshape(blk, blk).astype(BF16)
            st_ref[RMAX, h] = m_new
            st_ref[RESC, h] = jnp.exp2(m_old - m_use)

    def stage_pv(j):
        for h in range(N_HEADS):
            rows = slice(h * HEAD_ROWS, (h + 1) * HEAD_ROWS)
            pv = jnp.dot(vt_ref[0, j, rows, :], p_ref[h], preferred_element_type=F32)
            acc = acc_ref[rows, :].reshape(HEAD_ROWS // SUBLANES, SUBLANES, blk) * st_ref[RESC, h][None]
            acc_ref[rows, :] = acc.reshape(HEAD_ROWS, blk) + pv

    def pipelined(j, near_next):
        stage_exp()
        stage_logits(j + 1, near_next)
        stage_pv(j)

    st_ref[RMAX] = jnp.full((N_HEADS, SUBLANES, blk), -inf, F32)

    @pl.when(i == 0)
    def _():
        stage_logits(0, 0)

    @pl.when(i == 1)
    def _():
        stage_logits(0, 1)

    @pl.when(i >= 2)
    def _():
        stage_logits(0, None)

    def far_body(j, _):
        pipelined(j, None)
        return 0

    lax.fori_loop(0, jnp.maximum(i - 2, 0), far_body, 0)

    @pl.when(i >= 2)
    def _():
        pipelined(i - 2, 1)

    @pl.when(i >= 1)
    def _():
        pipelined(i - 1, 0)

    stage_exp()
    stage_pv(i)
    heads = []
    for h in range(N_HEADS):
        r0 = h * HEAD_ROWS
        heads.append(acc_ref[r0:r0 + HEAD_DIM, :] / acc_ref[r0 + HEAD_DIM:r0 + HEAD_DIM + 1, :])
    o_ref[0] = jnp.concatenate(heads, axis=0).T.astype(BF16)


def _attention(q, qi, wit, k, vt, kia, kib, btiles, *, bsz, seq, topk):
    blk = ATT_BLOCK
    nblk = seq // blk
    return pl.pallas_call(
        functools.partial(_attn_kernel, topk=topk, seq=seq),
        grid=(bsz, nblk),
        in_specs=[
            pl.BlockSpec((1, blk, ATTN_DIM), lambda b, i: (b, i, 0)),
            pl.BlockSpec((1, blk, IDX_HEADS * IDX_DIM), lambda b, i: (b, i, 0)),
            pl.BlockSpec((1, IDX_HEADS, blk), lambda b, i: (b, 0, i)),
            pl.BlockSpec((1, seq, ATTN_DIM), lambda b, i: (b, 0, 0)),
            pl.BlockSpec((1, nblk, N_HEADS * HEAD_ROWS, blk), lambda b, i: (b, 0, 0, 0)),
            pl.BlockSpec((1, seq, LANES), lambda b, i: (b, 0, 0)),
            pl.BlockSpec((1, seq, LANES), lambda b, i: (b, 0, 0)),
            pl.BlockSpec(btiles.shape, lambda b, i: (0, 0, 0, 0)),
        ],
        out_specs=pl.BlockSpec((1, blk, ATTN_DIM), lambda b, i: (b, i, 0)),
        out_shape=jax.ShapeDtypeStruct((bsz, seq, ATTN_DIM), BF16),
        scratch_shapes=[
            pltpu.VMEM((seq + blk, blk), F32),
            pltpu.VMEM((nblk, SUBLANES, blk), F32),
            pltpu.SMEM((nblk,), jnp.int32),
            pltpu.VMEM((N_HEADS, blk, LANES), BF16),
            pltpu.VMEM((N_HEADS, blk, blk), F32),
            pltpu.VMEM((N_HEADS, blk, blk), BF16),
            pltpu.VMEM((3, N_HEADS, SUBLANES, blk), F32),
            pltpu.VMEM((N_HEADS * HEAD_ROWS, blk), F32),
        ],
        compiler_params=pltpu.CompilerParams(dimension_semantics=("arbitrary", "arbitrary"),
                                             vmem_limit_bytes=V7X_VMEM_LIMIT_BYTES),
        name="attention",
    )(q, qi, wit, k, vt, kia, kib, btiles)


def _mix_kernel(x_ref, xh_ref, mod_ref, att_ref, wpg_ref, wpool_ref, ps_ref, wa_ref, wb_ref, wo_ref,
                lng_ref, lnb_ref, o_ref, pe_ref, mix_ref, *, alpha, mod_row, ln_row, tiles_per_seq):
    rows = x_ref.shape[0]
    i = pl.program_id(0)
    seq_tile = i % tiles_per_seq
    x = x_ref[...]
    sh = mod_ref[0, mod_row:mod_row + 1, :]
    sc = mod_ref[0, mod_row + 1:mod_row + 2, :]
    gate = mod_ref[0, mod_row + 2:mod_row + 3, :]
    u = (x * (1.0 + sc) + sh).astype(BF16)
    uh = (xh_ref[...] * (1.0 + sc) + sh).astype(BF16)
    pg = jnp.dot(u, wpg_ref[...], preferred_element_type=F32)
    ph = jnp.dot(uh, wpg_ref[:, :POOL_DIM], preferred_element_type=F32)
    pe_ref[0:POOL_HALO, :] = jnp.where(seq_tile == 0, 0.0, ph)
    pe_ref[POOL_HALO:, :] = pg[:, :POOL_DIM]
    t = seq_tile * rows + lax.broadcasted_iota(jnp.int32, (rows, 1), 0)
    for g, w in enumerate(POOL_WINDOWS):
        cols = slice(g * POOL_GROUP_DIM, (g + 1) * POOL_GROUP_DIM)
        cur = pe_ref[POOL_HALO:, cols]
        win = cur
        for back in range(1, w):
            win = win + pe_ref[POOL_HALO - back:POOL_HALO - back + rows, cols]
        cnt = jnp.minimum(t + 1, w).astype(F32)
        pooled = (win / cnt - cur).astype(BF16)
        mixed = jnp.dot(pooled, wpool_ref[g], preferred_element_type=F32)
        mix_ref[:, cols] = (mixed * ps_ref[:, cols]).astype(BF16)
    y_a = jnp.dot(mix_ref[...], wa_ref[...], preferred_element_type=F32)
    y_b = jnp.dot(att_ref[...], wb_ref[...], preferred_element_type=F32)
    d = x.shape[1]
    ga = pg[:, POOL_DIM:POOL_DIM + d]
    gb = pg[:, POOL_DIM + d:]
    merged = (jax.nn.sigmoid(ga) * y_a + jax.nn.sigmoid(gb) * y_b).astype(BF16)
    y = jnp.dot(merged, wo_ref[...], preferred_element_type=F32)
    z = alpha * x + gate * y
    o_ref[...] = _layer_norm(z, lng_ref[ln_row:ln_row + 1, :], lnb_ref[ln_row:ln_row + 1, :])


def _mix_out(x2d, mod, att2d, wpg, wpool, pool_scale, wa, wb, wo, ln_g, ln_b, *, seq, alpha, mod_row, ln_row):
    n, d = x2d.shape
    tiles_per_seq = seq // MIX_ROWS
    halo_blocks = MIX_ROWS // POOL_HALO
    resident = dict(pipeline_mode=pl.Buffered(1))
    full = lambda a: pl.BlockSpec(a.shape, lambda i: (0,) * a.ndim, **resident)
    return pl.pallas_call(
        functools.partial(_mix_kernel, alpha=alpha, mod_row=mod_row, ln_row=ln_row, tiles_per_seq=tiles_per_seq),
        grid=(n // MIX_ROWS,),
        in_specs=[
            pl.BlockSpec((MIX_ROWS, d), lambda i: (i, 0)),
            pl.BlockSpec((POOL_HALO, d), lambda i: (jnp.maximum(i * halo_blocks - 1, 0), 0)),
            pl.BlockSpec((1, N_ADA, d), lambda i: (i // tiles_per_seq, 0, 0)),
            pl.BlockSpec((MIX_ROWS, ATTN_DIM), lambda i: (i, 0)),
            full(wpg), full(wpool), full(pool_scale), full(wa), full(wb), full(wo),
            pl.BlockSpec(ln_g.shape, lambda i: (0, 0)),
            pl.BlockSpec(ln_b.shape, lambda i: (0, 0)),
        ],
        out_specs=pl.BlockSpec((MIX_ROWS, d), lambda i: (i, 0)),
        out_shape=jax.ShapeDtypeStruct((n, d), F32),
        scratch_shapes=[
            pltpu.VMEM((POOL_HALO + MIX_ROWS, POOL_DIM), F32),
            pltpu.VMEM((MIX_ROWS, POOL_DIM), BF16),
        ],
        compiler_params=pltpu.CompilerParams(dimension_semantics=("arbitrary",),
                                             vmem_limit_bytes=V7X_VMEM_LIMIT_BYTES),
        name="mix_out",
    )(x2d, x2d, mod, att2d, wpg, wpool, pool_scale, wa, wb, wo, ln_g, ln_b)


def kernel(x, c, w_ada, b_ada, ln_g, ln_b, ffn1_w_gate, ffn1_w_up, ffn1_w_down, w_in, w_pool, pool_scale,
           w_a, w_b, w_out, rel_bias, ffn2_w_gate, ffn2_w_up, ffn2_w_down):
    bsz, seq, d = x.shape
    depth = w_ada.shape[0]
    alpha = (2.0 * depth) ** 0.25
    topk = min(TOP_K, seq // 4)
    assert seq % FFN_ROWS == 0 and seq % PROJ_ROWS == 0 and seq % MIX_ROWS == 0 and seq % ATT_BLOCK == 0
    assert PROJ_ROWS % ATT_BLOCK == 0 and MIX_ROWS % POOL_HALO == 0 and POOL_HALO >= max(POOL_WINDOWS) - 1
    far_bucket = _far_bucket(ATT_BLOCK + 1, max(seq - 1, ATT_BLOCK + 1))

    o_q = POOL_DIM
    o_k = o_q + ATTN_DIM
    o_v = o_k + ATTN_DIM
    o_qi = o_v + ATTN_DIM
    o_ki = o_qi + IDX_HEADS * IDX_DIM
    o_wi = o_ki + IDX_DIM
    o_ga = o_wi + IDX_HEADS

    btiles = _rel_bias_tiles(rel_bias, far_bucket)
    x2d = x.reshape(bsz * seq, d)
    for l in range(depth):
        wl = w_in[l]
        zeros_ki = jnp.zeros((d, LANES - IDX_DIM), wl.dtype)
        w_ki = wl[:, o_ki:o_wi]
        wqk = jnp.concatenate([wl[:, o_q:o_v], wl[:, o_qi:o_ki]], axis=1).astype(BF16)
        wv_t = wl[:, o_v:o_qi].T.astype(BF16)
        wki = jnp.concatenate([w_ki, zeros_ki, zeros_ki, w_ki], axis=1).astype(BF16)
        wwi_t = jnp.pad(wl[:, o_wi:o_ga].T, ((0, 2 * SUBLANES - IDX_HEADS), (0, 0))).astype(BF16)
        wpg = jnp.concatenate([wl[:, :POOL_DIM], wl[:, o_ga:]], axis=1).astype(BF16)

        mod = _ada_mod(c, w_ada[l], b_ada[l]).reshape(bsz, N_ADA, d)
        x2d = _ffn(x2d, mod, ln_g[l], ln_b[l], ffn1_w_gate[l].astype(BF16), ffn1_w_up[l].astype(BF16),
                   ffn1_w_down[l].astype(BF16), seq=seq, alpha=alpha, mod_row=0, ln_row=0)
        q, k, qi, vt, kia, kib, wit = _attn_proj(x2d, mod, wqk, wv_t, wki, wwi_t, bsz=bsz, seq=seq, mod_row=3)
        att = _attention(q.reshape(bsz, seq, ATTN_DIM), qi.reshape(bsz, seq, IDX_HEADS * IDX_DIM), wit,
                         k.reshape(bsz, seq, ATTN_DIM), vt, kia.reshape(bsz, seq, LANES),
                         kib.reshape(bsz, seq, LANES), btiles, bsz=bsz, seq=seq, topk=topk)
        x2d = _mix_out(x2d, mod, att.reshape(bsz * seq, ATTN_DIM), wpg, w_pool[l].astype(BF16),
                       pool_scale[l].reshape(1, POOL_DIM), w_a[l].astype(BF16), w_b[l].astype(BF16),
                       w_out[l].astype(BF16), ln_g[l], ln_b[l], seq=seq, alpha=alpha, mod_row=3, ln_row=1)
        x2d = _ffn(x2d, mod, ln_g[l], ln_b[l], ffn2_w_gate[l].astype(BF16), ffn2_w_up[l].astype(BF16),
                   ffn2_w_down[l].astype(BF16), seq=seq, alpha=alpha, mod_row=6, ln_row=2)
    return x2d.reshape(bsz, seq, d)
```

```python
import functools
import math

import numpy as np
import jax
import jax.numpy as jnp
from jax import lax
from jax.experimental import pallas as pl
from jax.experimental.pallas import tpu as pltpu

POOL_WINDOWS = (2, 4, 8, 16)
POOL_GROUP_DIM = 128
POOL_DIM = len(POOL_WINDOWS) * POOL_GROUP_DIM
N_HEADS = 8
HEAD_DIM = 64
ATTN_DIM = N_HEADS * HEAD_DIM
HEAD_PAD = 16
HEAD_ROWS = HEAD_DIM + HEAD_PAD
IDX_HEADS = 8
IDX_DIM = 64
TOP_K = 256
REL_BUCKETS = 32
REL_MAX_DIST = 128
N_ADA = 9
LN_EPS = 1e-5
POOL_HALO = 16

LANES = 128
SUBLANES = 8
V7X_VMEM_LIMIT_BYTES = 56 * 1024 * 1024
FFN_ROWS = 512
FFN_COLS = 256
PROJ_ROWS = 512
MIX_ROWS = 512
ATT_BLOCK = 256
ADA_COLS = 1024
COUNT_CHAINS = 4
COUNT_ROWS = 64
BISECT_STEPS = 14

LOG2E = math.log2(math.e)
BF16 = jnp.bfloat16
F32 = jnp.float32
NT_DIMS = (((1,), (1,)), ((), ()))


def _layer_norm(z, g, b):
    mu = jnp.mean(z, axis=-1, keepdims=True)
    zc = z - mu
    var = jnp.mean(zc * zc, axis=-1, keepdims=True)
    return zc * lax.rsqrt(var + LN_EPS) * g + b


def _silu(a):
    return a * jax.nn.sigmoid(a)


def _ada_kernel(c_ref, w_ref, b_ref, o_ref):
    a = _silu(c_ref[...])
    o_ref[...] = jnp.dot(a, w_ref[...], preferred_element_type=F32) + b_ref[...]


def _ada_mod(c, w_ada, b_ada):
    bsz, d = c.shape
    n = w_ada.shape[1]
    rows = -(-bsz // SUBLANES) * SUBLANES
    c_pad = jnp.pad(c, ((0, rows - bsz), (0, 0)))
    out = pl.pallas_call(
        _ada_kernel,
        grid=(n // ADA_COLS,),
        in_specs=[
            pl.BlockSpec((rows, d), lambda j: (0, 0)),
            pl.BlockSpec((d, ADA_COLS), lambda j: (0, j)),
            pl.BlockSpec((1, ADA_COLS), lambda j: (0, j)),
        ],
        out_specs=pl.BlockSpec((rows, ADA_COLS), lambda j: (0, j)),
        out_shape=jax.ShapeDtypeStruct((rows, n), F32),
        compiler_params=pltpu.CompilerParams(dimension_semantics=("arbitrary",),
                                             vmem_limit_bytes=V7X_VMEM_LIMIT_BYTES),
        name="ada_mod",
    )(c_pad, w_ada, b_ada.reshape(1, n))
    return out[:bsz]


def _ffn_kernel(x_ref, mod_ref, lng_ref, lnb_ref, wg_ref, wu_ref, wd_ref, o_ref, h_ref, *, alpha, mod_row, ln_row):
    x = x_ref[...]
    sh = mod_ref[0, mod_row:mod_row + 1, :]
    sc = mod_ref[0, mod_row + 1:mod_row + 2, :]
    gate = mod_ref[0, mod_row + 2:mod_row + 3, :]
    u = (x * (1.0 + sc) + sh).astype(BF16)
    d_ff = wg_ref.shape[1]
    for c in range(d_ff // FFN_COLS):
        sl = slice(c * FFN_COLS, (c + 1) * FFN_COLS)
        a = jnp.dot(u, wg_ref[:, sl], preferred_element_type=F32)
        b = jnp.dot(u, wu_ref[:, sl], preferred_element_type=F32)
        h_ref[:, sl] = (_silu(a) * b).astype(BF16)
    y = jnp.dot(h_ref[...], wd_ref[...], preferred_element_type=F32)
    z = alpha * x + (0.5 * gate) * y
    o_ref[...] = _layer_norm(z, lng_ref[ln_row:ln_row + 1, :], lnb_ref[ln_row:ln_row + 1, :])


def _ffn(x2d, mod, ln_g, ln_b, wg, wu, wd, *, seq, alpha, mod_row, ln_row):
    n, d = x2d.shape
    d_ff = wg.shape[1]
    tiles_per_seq = seq // FFN_ROWS
    resident = dict(pipeline_mode=pl.Buffered(1))
    return pl.pallas_call(
        functools.partial(_ffn_kernel, alpha=alpha, mod_row=mod_row, ln_row=ln_row),
        grid=(n // FFN_ROWS,),
        in_specs=[
            pl.BlockSpec((FFN_ROWS, d), lambda i: (i, 0)),
            pl.BlockSpec((1, N_ADA, d), lambda i: (i // tiles_per_seq, 0, 0)),
            pl.BlockSpec(ln_g.shape, lambda i: (0, 0)),
            pl.BlockSpec(ln_b.shape, lambda i: (0, 0)),
            pl.BlockSpec((d, d_ff), lambda i: (0, 0), **resident),
            pl.BlockSpec((d, d_ff), lambda i: (0, 0), **resident),
            pl.BlockSpec((d_ff, d), lambda i: (0, 0), **resident),
        ],
        out_specs=pl.BlockSpec((FFN_ROWS, d), lambda i: (i, 0)),
        out_shape=jax.ShapeDtypeStruct((n, d), F32),
        scratch_shapes=[pltpu.VMEM((FFN_ROWS, d_ff), BF16)],
        compiler_params=pltpu.CompilerParams(dimension_semantics=("arbitrary",),
                                             vmem_limit_bytes=V7X_VMEM_LIMIT_BYTES),
        name="ffn",
    )(x2d, mod, ln_g, ln_b, wg, wu, wd)


def _proj_kernel(x_ref, mod_ref, wqk_ref, wv_ref, wki_ref, wwi_ref,
                 q_ref, k_ref, qi_ref, vt_ref, kia_ref, kib_ref, wit_ref, *, mod_row):
    x = x_ref[...]
    sh = mod_ref[0, mod_row:mod_row + 1, :]
    sc = mod_ref[0, mod_row + 1:mod_row + 2, :]
    u = (x * (1.0 + sc) + sh).astype(BF16)
    qkq = jnp.dot(u, wqk_ref[...], preferred_element_type=F32)
    q_ref[...] = (qkq[:, :ATTN_DIM] * (HEAD_DIM ** -0.5 * LOG2E)).astype(BF16)
    k_ref[...] = qkq[:, ATTN_DIM:2 * ATTN_DIM].astype(BF16)
    qi_ref[...] = qkq[:, 2 * ATTN_DIM:].astype(BF16)
    kk = jnp.dot(u, wki_ref[...], preferred_element_type=F32)
    kia_ref[...] = kk[:, :LANES].astype(BF16)
    kib_ref[...] = kk[:, LANES:].astype(BF16)
    vt = lax.dot_general(wv_ref[...], u, NT_DIMS, preferred_element_type=F32).astype(BF16)
    ones_rows = jnp.where(lax.broadcasted_iota(jnp.int32, (HEAD_PAD, ATT_BLOCK), 0) == 0, 1.0, 0.0).astype(BF16)
    for c in range(vt_ref.shape[1]):
        for h in range(N_HEADS):
            vt_ref[0, c, h * HEAD_ROWS:h * HEAD_ROWS + HEAD_DIM, :] = \
                vt[h * HEAD_DIM:(h + 1) * HEAD_DIM, c * ATT_BLOCK:(c + 1) * ATT_BLOCK]
            vt_ref[0, c, h * HEAD_ROWS + HEAD_DIM:(h + 1) * HEAD_ROWS, :] = ones_rows
    wit = lax.dot_general(wwi_ref[...], u, NT_DIMS, preferred_element_type=F32)
    wit_ref[0] = wit[:IDX_HEADS, :]


def _attn_proj(x2d, mod, wqk, wv_t, wki, wwi_t, *, bsz, seq, mod_row):
    n, d = x2d.shape
    tiles_per_seq = seq // PROJ_ROWS
    chunks_per_tile = PROJ_ROWS // ATT_BLOCK
    resident = dict(pipeline_mode=pl.Buffered(1))
    row_spec = lambda cols: pl.BlockSpec((PROJ_ROWS, cols), lambda i: (i, 0))
    return pl.pallas_call(
        functools.partial(_proj_kernel, mod_row=mod_row),
        grid=(n // PROJ_ROWS,),
        in_specs=[
            pl.BlockSpec((PROJ_ROWS, d), lambda i: (i, 0)),
            pl.BlockSpec((1, N_ADA, d), lambda i: (i // tiles_per_seq, 0, 0)),
            pl.BlockSpec(wqk.shape, lambda i: (0, 0), **resident),
            pl.BlockSpec(wv_t.shape, lambda i: (0, 0), **resident),
            pl.BlockSpec(wki.shape, lambda i: (0, 0), **resident),
            pl.BlockSpec(wwi_t.shape, lambda i: (0, 0), **resident),
        ],
        out_specs=[
            row_spec(ATTN_DIM), row_spec(ATTN_DIM), row_spec(IDX_HEADS * IDX_DIM),
            pl.BlockSpec((1, chunks_per_tile, N_HEADS * HEAD_ROWS, ATT_BLOCK),
                         lambda i: (i // tiles_per_seq, i % tiles_per_seq, 0, 0)),
            row_spec(LANES), row_spec(LANES),
            pl.BlockSpec((1, IDX_HEADS, PROJ_ROWS), lambda i: (i // tiles_per_seq, 0, i % tiles_per_seq)),
        ],
        out_shape=[
            jax.ShapeDtypeStruct((n, ATTN_DIM), BF16),
            jax.ShapeDtypeStruct((n, ATTN_DIM), BF16),
            jax.ShapeDtypeStruct((n, IDX_HEADS * IDX_DIM), BF16),
            jax.ShapeDtypeStruct((bsz, seq // ATT_BLOCK, N_HEADS * HEAD_ROWS, ATT_BLOCK), BF16),
            jax.ShapeDtypeStruct((n, LANES), BF16),
            jax.ShapeDtypeStruct((n, LANES), BF16),
            jax.ShapeDtypeStruct((bsz, IDX_HEADS, seq), F32),
        ],
        compiler_params=pltpu.CompilerParams(dimension_semantics=("arbitrary",),
                                             vmem_limit_bytes=V7X_VMEM_LIMIT_BYTES),
        name="attn_proj",
    )(x2d, mod, wqk, wv_t, wki, wwi_t)


def _t5_bucket(n):
    max_exact = REL_BUCKETS // 2
    nf = jnp.maximum(n, 1).astype(F32)
    large = max_exact + jnp.floor(jnp.log(nf / max_exact) / math.log(REL_MAX_DIST / max_exact)
                                  * (REL_BUCKETS - max_exact)).astype(jnp.int32)
    large = jnp.minimum(large, REL_BUCKETS - 1)
    return jnp.where(n < max_exact, n, large)


def _far_bucket(first_dist, last_dist):
    n = np.arange(first_dist, last_dist + 1, dtype=np.float32)
    max_exact = REL_BUCKETS // 2
    large = max_exact + (np.log(n / np.float32(max_exact)) / np.float32(math.log(REL_MAX_DIST / max_exact))
                         * np.float32(REL_BUCKETS - max_exact)).astype(np.int32)
    buckets = np.where(n < max_exact, n.astype(np.int32), np.minimum(large, REL_BUCKETS - 1))
    assert buckets.min() == buckets.max(), "key chunks two or more blocks away must share one bias bucket"
    return int(buckets[0])


def _bias_kernel(rb_ref, o_ref, *, far_bucket):
    o = pl.program_id(0)
    row = lax.broadcasted_iota(jnp.int32, (ATT_BLOCK, ATT_BLOCK), 0)
    col = lax.broadcasted_iota(jnp.int32, (ATT_BLOCK, ATT_BLOCK), 1)
    dist = o * ATT_BLOCK + col - row
    bucket = _t5_bucket(jnp.maximum(dist, 0))
    for h in range(N_HEADS):
        tile = jnp.zeros((ATT_BLOCK, ATT_BLOCK), F32)
        for b in range(REL_BUCKETS):
            tile = jnp.where(bucket == b, rb_ref[b, h], tile)
        o_ref[0, h] = (tile - rb_ref[far_bucket, h]) * LOG2E


def _rel_bias_tiles(rel_bias, far_bucket):
    return pl.pallas_call(
        functools.partial(_bias_kernel, far_bucket=far_bucket),
        grid=(2,),
        in_specs=[pl.BlockSpec(memory_space=pltpu.SMEM)],
        out_specs=pl.BlockSpec((1, N_HEADS, ATT_BLOCK, ATT_BLOCK), lambda o: (o, 0, 0, 0)),
        out_shape=jax.ShapeDtypeStruct((2, N_HEADS, ATT_BLOCK, ATT_BLOCK), F32),
        compiler_params=pltpu.CompilerParams(dimension_semantics=("arbitrary",)),
        name="rel_bias",
    )(rel_bias)


def _attn_kernel(q_ref, qi_ref, wit_ref, k_ref, vt_ref, kia_ref, kib_ref, bt_ref, o_ref,
                 sc_ref, zc_ref, ranked_ref, qh_ref, lg_ref, p_ref, st_ref, acc_ref, *, topk, seq):
    blk = ATT_BLOCK
    groups = blk // SUBLANES
    i = pl.program_id(1)
    nch = i + 1
    kf = float(topk)
    inf = jnp.inf

    def chunk_start(j):
        return j * blk if isinstance(j, int) else pl.multiple_of(j * blk, blk)

    def as_groups(x):
        return x.reshape(groups, SUBLANES, blk)

    def lanes8(v):
        return jnp.broadcast_to(v, (SUBLANES, blk))

    def colmin(x8):
        return jnp.min(x8, axis=0, keepdims=True)

    def colmax(x8):
        return jnp.max(x8, axis=0, keepdims=True)

    def colsum(x8):
        return jnp.sum(x8, axis=0, keepdims=True)

    wf = wit_ref[0] * (IDX_DIM ** -0.5)
    qi = qi_ref[0]
    row = lax.broadcasted_iota(jnp.int32, (blk, blk), 0)
    col = lax.broadcasted_iota(jnp.int32, (blk, blk), 1)
    causal = row <= col

    def chunk_scores(j):
        r0 = chunk_start(j)
        ka = kia_ref[0, pl.ds(r0, blk), :]
        kb = kib_ref[0, pl.ds(r0, blk), :]
        s = jnp.zeros((blk, blk), F32)
        for hp in range(IDX_HEADS // 2):
            qp = qi[:, hp * LANES:(hp + 1) * LANES]
            a0 = lax.dot_general(ka, qp, NT_DIMS, preferred_element_type=F32)
            a1 = lax.dot_general(kb, qp, NT_DIMS, preferred_element_type=F32)
            s = s + jnp.maximum(a0, 0.0) * wf[2 * hp:2 * hp + 1, :]
            s = s + jnp.maximum(a1, 0.0) * wf[2 * hp + 1:2 * hp + 2, :]
        return r0, s * (IDX_HEADS ** -0.5)

    def score_stats(j, s_lo, s_hi, stats):
        mn8, mx8, pos8, zer8 = stats
        hi3 = as_groups(s_hi)
        zer8 = zer8 + jnp.sum(jnp.where(hi3 == 0.0, 1.0, 0.0), axis=0)
        zc_ref[j] = zer8
        return (jnp.minimum(mn8, jnp.min(as_groups(s_lo), axis=0)),
                jnp.maximum(mx8, jnp.max(hi3, axis=0)),
                pos8 + jnp.sum(jnp.where(hi3 > 0.0, 1.0, 0.0), axis=0),
                zer8)

    def score_body(j, stats):
        r0, s = chunk_scores(j)
        sc_ref[pl.ds(r0, blk), :] = s
        return score_stats(j, s, s, stats)

    def score_pair(jp, stats):
        return score_body(2 * jp + 1, score_body(2 * jp, stats))

    zeros8 = jnp.zeros((SUBLANES, blk), F32)
    stats = (jnp.full((SUBLANES, blk), inf, F32), jnp.full((SUBLANES, blk), -inf, F32), zeros8, zeros8)
    stats = lax.fori_loop(0, i // 2, score_pair, stats)
    def score_diag(st):
        r_diag, s_diag = chunk_scores(i)
        s_diag_hi = jnp.where(causal, s_diag, -inf)
        sc_ref[pl.ds(r_diag, blk), :] = s_diag_hi
        return score_stats(i, jnp.where(causal, s_diag, inf), s_diag_hi, st)

    mn8, mx8, pos8, zer8 = lax.cond(i % 2 == 1, lambda st: score_diag(score_body(i - 1, st)), score_diag, stats)

    @pl.when(nch % 2 == 1)
    def _():
        sc_ref[pl.ds(chunk_start(nch), blk), :] = jnp.full((blk, blk), -inf, F32)

    npair = (nch + 1) // 2

    def count_gt(thr):
        t8 = lanes8(thr)

        def body(jp, accs):
            base = pl.multiple_of(jp * 2 * blk, 2 * blk)
            accs = list(accs)
            for s in range(2 * blk // COUNT_ROWS):
                x = sc_ref[pl.ds(base + s * COUNT_ROWS, COUNT_ROWS), :]
                for g in range(COUNT_ROWS // SUBLANES):
                    hit = jnp.where(x[g * SUBLANES:(g + 1) * SUBLANES] > t8, 1.0, 0.0)
                    accs[g % COUNT_CHAINS] = accs[g % COUNT_CHAINS] + hit
            return tuple(accs)

        accs = lax.fori_loop(0, npair, body, tuple(jnp.zeros((SUBLANES, blk), F32) for _ in range(COUNT_CHAINS)))
        return colsum(sum(accs[1:], accs[0]))

    def bisect(_, st):
        lo, hi, low, done = st
        mid = 0.5 * lo + 0.5 * hi
        c = count_gt(mid)
        live = done < 0.5
        up = jnp.logical_and(live, c >= kf)
        down = jnp.logical_and(live, c < kf)
        low = jnp.where(up, mid, low)
        lo = jnp.where(up, mid, lo)
        hi = jnp.where(down, mid, hi)
        done = jnp.where(c == kf, 1.0, done)
        return lo, hi, low, done

    tpos = i * blk + lax.broadcasted_iota(jnp.int32, (1, blk), 1)
    pos, zer = colsum(pos8), colsum(zer8)
    mn, mx = colmin(mn8), colmax(mx8)
    zero_tie = jnp.logical_and(pos < kf, pos + zer >= kf)
    pos_ge = pos >= kf
    done0 = jnp.where(jnp.logical_or(jnp.logical_or(tpos + 1 <= topk, zero_tie), pos == kf), 1.0, 0.0)
    low0 = jnp.where(jnp.logical_or(zero_tie, pos_ge), 0.0, -inf)
    lo0 = jnp.where(pos_ge, jnp.maximum(mn, 0.0), mn)
    hi0 = jnp.where(pos_ge, mx, jnp.minimum(mx, 0.0))
    tie0 = jnp.where(zero_tie, 0.0, inf)
    need0 = jnp.where(zero_tie, kf - pos, 0.0)

    def next_value_above(thr):
        t8 = lanes8(thr)

        def body(jp, accs):
            base = pl.multiple_of(jp * 2 * blk, 2 * blk)
            accs = list(accs)
            for s in range(2 * blk // COUNT_ROWS):
                x = sc_ref[pl.ds(base + s * COUNT_ROWS, COUNT_ROWS), :]
                for g in range(COUNT_ROWS // SUBLANES):
                    xg = x[g * SUBLANES:(g + 1) * SUBLANES]
                    accs[g % COUNT_CHAINS] = jnp.minimum(accs[g % COUNT_CHAINS], jnp.where(xg > t8, xg, inf))
            return tuple(accs)

        accs = lax.fori_loop(0, npair, body, tuple(jnp.full((SUBLANES, blk), inf, F32) for _ in range(COUNT_CHAINS)))
        return colmin(functools.reduce(jnp.minimum, accs))

    def climb_cond(st):
        return jnp.min(st[1]) < 0.5

    def climb_body(st):
        low, done, tie, need = st
        live = done < 0.5
        cand = next_value_above(low)
        cgt = count_gt(cand)
        found = jnp.logical_and(live, cgt < kf)
        tie = jnp.where(found, cand, tie)
        need = jnp.where(found, kf - cgt, need)
        low = jnp.where(live, cand, low)
        done = jnp.where(jnp.logical_and(live, cgt <= kf), 1.0, done)
        return low, done, tie, need

    def select(_):
        _, _, low, done = lax.fori_loop(0, BISECT_STEPS, bisect, (lo0, hi0, low0, done0))
        low, _, tie, need = lax.while_loop(climb_cond, climb_body, (low, done, tie0, need0))
        return low, tie, need

    low, tie, need = lax.cond(jnp.min(done0) < 0.5, select, lambda _: (low0, tie0, need0), 0)

    def plain_mask(_):
        low8 = lanes8(low)

        def body(j, _):
            r0 = chunk_start(j)
            x = as_groups(sc_ref[pl.ds(r0, blk), :])
            sc_ref[pl.ds(r0, blk), :] = jnp.where(x > low8[None], 0.0, -inf).reshape(blk, blk)
            return 0

        return lax.fori_loop(0, nch, body, 0)

    def ranked_chunk(j, tri, base):
        r0 = chunk_start(j)
        x = sc_ref[pl.ds(r0, blk), :]
        eq = x == tie
        rank = jnp.dot(tri, jnp.where(eq, 1.0, 0.0).astype(BF16), preferred_element_type=F32)
        sel = jnp.logical_or(x > low, jnp.logical_and(eq, rank + base <= need))
        sc_ref[pl.ds(r0, blk), :] = jnp.where(sel, 0.0, -inf)
        return rank[blk - 1:blk, :]

    def tie_mask(_):
        tri = jnp.where(row >= col, 1.0, 0.0).astype(BF16)
        lax.fori_loop(0, nch, lambda j, base: base + ranked_chunk(j, tri, base), jnp.zeros((1, blk), F32))
        return 0

    def zero_tie_mask(_):
        tri = jnp.where(row >= col, 1.0, 0.0).astype(BF16)

        def cut_body(j, ncut):
            return ncut + jnp.where(colsum(zc_ref[j]) < need, 1.0, 0.0)

        ncut = lax.fori_loop(0, nch, cut_body, jnp.zeros((1, blk), F32))
        cut = jnp.where(tie < inf, ncut, inf)

        def flag_body(j, _):
            ranked_ref[j] = 0
            return 0

        lax.fori_loop(0, nch, flag_body, 0)

        def cut_cond(pending):
            return jnp.min(pending) < inf

        def cut_chunk(pending):
            jf = jnp.min(pending)
            j = jf.astype(jnp.int32)
            base = jnp.where(j > 0, colsum(zc_ref[jnp.maximum(j - 1, 0)]), 0.0)
            ranked_chunk(j, tri, base)
            ranked_ref[j] = 1
            return jnp.where(pending == jf, inf, pending)

        lax.while_loop(cut_cond, cut_chunk, cut)
        low8, cut8 = lanes8(low), lanes8(cut)

        def body(j, _):
            @pl.when(ranked_ref[j] == 0)
            def _():
                r0 = chunk_start(j)
                x = as_groups(sc_ref[pl.ds(r0, blk), :])
                at_low = jnp.where(jnp.logical_and(cut8 < inf, cut8 > j.astype(F32)), 0.0, -inf)
                out = jnp.where(x > low8[None], 0.0, jnp.where(x == low8[None], at_low[None], -inf))
                sc_ref[pl.ds(r0, blk), :] = out.reshape(blk, blk)
            return 0

        return lax.fori_loop(0, nch, body, 0)

    tied = tie < inf
    mask_kind = jnp.where(jnp.max(jnp.where(tied, 1.0, 0.0)) < 0.5, 0,
                          jnp.where(jnp.max(jnp.where(jnp.logical_and(tied, tie != 0.0), 1.0, 0.0)) < 0.5, 1, 2))
    lax.switch(mask_kind, [plain_mask, zero_tie_mask, tie_mask], 0)

    q = q_ref[0]
    lane = lax.broadcasted_iota(jnp.int32, (blk, LANES), 1)
    for h in range(N_HEADS):
        hp = h // 2
        in_head = (lane < HEAD_DIM) if h % 2 == 0 else (lane >= HEAD_DIM)
        qh_ref[h] = jnp.where(in_head, q[:, hp * LANES:(hp + 1) * LANES], jnp.zeros((), BF16))
    acc_ref[...] = jnp.zeros(acc_ref.shape, F32)

    def logits(j, h, mb, near):
        hp = h // 2
        kc = k_ref[0, pl.ds(chunk_start(j), blk), hp * LANES:(hp + 1) * LANES]
        lg = lax.dot_general(kc, qh_ref[h], NT_DIMS, preferred_element_type=F32) + mb
        return lg if near is None else lg + bt_ref[near, h]

    CMAX, RMAX, RESC = 0, 1, 2

    def stage_logits(j, near):
        mb = sc_ref[pl.ds(chunk_start(j), blk), :]
        cm8 = []
        for h in range(N_HEADS):
            lg = logits(j, h, mb, near)
            lg_ref[h] = lg
            cm8.append(jnp.max(as_groups(lg), axis=0))
        for h in range(N_HEADS):
            st_ref[CMAX, h] = lanes8(colmax(cm8[h]))

    def stage_exp():
        for h in range(N_HEADS):
            m_old = st_ref[RMAX, h]
            m_new = jnp.maximum(m_old, st_ref[CMAX, h])
            m_use = jnp.where(m_new == -inf, 0.0, m_new)
            p = jnp.exp2(as_groups(lg_ref[h]) - m_use[None])
            p_ref[h] = p.reshape(blk, blk).astype(BF16)
            st_ref[RMAX, h] = m_new
            st_ref[RESC, h] = jnp.exp2(m_old - m_use)

    def stage_pv(j):
        for h in range(N_HEADS):
            rows = slice(h * HEAD_ROWS, (h + 1) * HEAD_ROWS)
            pv = jnp.dot(vt_ref[0, j, rows, :], p_ref[h], preferred_element_type=F32)
            acc = acc_ref[rows, :].reshape(HEAD_ROWS // SUBLANES, SUBLANES, blk) * st_ref[RESC, h][None]
            acc_ref[rows, :] = acc.reshape(HEAD_ROWS, blk) + pv

    def pipelined(j, near_next):
        stage_exp()
        stage_logits(j + 1, near_next)
        stage_pv(j)

    st_ref[RMAX] = jnp.full((N_HEADS, SUBLANES, blk), -inf, F32)

    def last_chunks(first):
        for j in range(first, 0):
            pipelined(i + j, -(j + 1))
        stage_exp()
        stage_pv(i)

    @pl.when(i == 0)
    def _():
        stage_logits(0, 0)
        last_chunks(0)

    @pl.when(i == 1)
    def _():
        stage_logits(0, 1)
        last_chunks(-1)

    @pl.when(i >= 2)
    def _():
        stage_logits(0, None)

    def far_body(j, _):
        pipelined(j, None)
        return 0

    lax.fori_loop(0, jnp.maximum(i - 2, 0), far_body, 0)

    @pl.when(i >= 2)
    def _():
        last_chunks(-2)

    heads = []
    for h in range(N_HEADS):
        r0 = h * HEAD_ROWS
        heads.append(acc_ref[r0:r0 + HEAD_DIM, :] / acc_ref[r0 + HEAD_DIM:r0 + HEAD_DIM + 1, :])
    o_ref[0] = jnp.concatenate(heads, axis=0).T.astype(BF16)


def _attention(q, qi, wit, k, vt, kia, kib, btiles, *, bsz, seq, topk):
    blk = ATT_BLOCK
    nblk = seq // blk
    return pl.pallas_call(
        functools.partial(_attn_kernel, topk=topk, seq=seq),
        grid=(bsz, nblk),
        in_specs=[
            pl.BlockSpec((1, blk, ATTN_DIM), lambda b, i: (b, i, 0)),
            pl.BlockSpec((1, blk, IDX_HEADS * IDX_DIM), lambda b, i: (b, i, 0)),
            pl.BlockSpec((1, IDX_HEADS, blk), lambda b, i: (b, 0, i)),
            pl.BlockSpec((1, seq, ATTN_DIM), lambda b, i: (b, 0, 0)),
            pl.BlockSpec((1, nblk, N_HEADS * HEAD_ROWS, blk), lambda b, i: (b, 0, 0, 0)),
            pl.BlockSpec((1, seq, LANES), lambda b, i: (b, 0, 0)),
            pl.BlockSpec((1, seq, LANES), lambda b, i: (b, 0, 0)),
            pl.BlockSpec(btiles.shape, lambda b, i: (0, 0, 0, 0)),
        ],
        out_specs=pl.BlockSpec((1, blk, ATTN_DIM), lambda b, i: (b, i, 0)),
        out_shape=jax.ShapeDtypeStruct((bsz, seq, ATTN_DIM), BF16),
        scratch_shapes=[
            pltpu.VMEM((seq + blk, blk), F32),
            pltpu.VMEM((nblk, SUBLANES, blk), F32),
            pltpu.SMEM((nblk,), jnp.int32),
            pltpu.VMEM((N_HEADS, blk, LANES), BF16),
            pltpu.VMEM((N_HEADS, blk, blk), F32),
            pltpu.VMEM((N_HEADS, blk, blk), BF16),
            pltpu.VMEM((3, N_HEADS, SUBLANES, blk), F32),
            pltpu.VMEM((N_HEADS * HEAD_ROWS, blk), F32),
        ],
        compiler_params=pltpu.CompilerParams(dimension_semantics=("arbitrary", "arbitrary"),
                                             vmem_limit_bytes=V7X_VMEM_LIMIT_BYTES),
        name="attention",
    )(q, qi, wit, k, vt, kia, kib, btiles)


def _mix_kernel(x_ref, xh_ref, mod_ref, att_ref, wpg_ref, wpool_ref, ps_ref, wa_ref, wb_ref, wo_ref,
                lng_ref, lnb_ref, o_ref, pe_ref, mix_ref, *, alpha, mod_row, ln_row, tiles_per_seq):
    rows = x_ref.shape[0]
    i = pl.program_id(0)
    seq_tile = i % tiles_per_seq
    x = x_ref[...]
    sh = mod_ref[0, mod_row:mod_row + 1, :]
    sc = mod_ref[0, mod_row + 1:mod_row + 2, :]
    gate = mod_ref[0, mod_row + 2:mod_row + 3, :]
    u = (x * (1.0 + sc) + sh).astype(BF16)
    uh = (xh_ref[...] * (1.0 + sc) + sh).astype(BF16)
    pg = jnp.dot(u, wpg_ref[...], preferred_element_type=F32)
    ph = jnp.dot(uh, wpg_ref[:, :POOL_DIM], preferred_element_type=F32)
    pe_ref[0:POOL_HALO, :] = jnp.where(seq_tile == 0, 0.0, ph)
    pe_ref[POOL_HALO:, :] = pg[:, :POOL_DIM]
    t = seq_tile * rows + lax.broadcasted_iota(jnp.int32, (rows, 1), 0)
    for g, w in enumerate(POOL_WINDOWS):
        cols = slice(g * POOL_GROUP_DIM, (g + 1) * POOL_GROUP_DIM)
        cur = pe_ref[POOL_HALO:, cols]
        win = cur
        for back in range(1, w):
            win = win + pe_ref[POOL_HALO - back:POOL_HALO - back + rows, cols]
        cnt = jnp.minimum(t + 1, w).astype(F32)
        pooled = (win / cnt - cur).astype(BF16)
        mixed = jnp.dot(pooled, wpool_ref[g], preferred_element_type=F32)
        mix_ref[:, cols] = (mixed * ps_ref[:, cols]).astype(BF16)
    y_a = jnp.dot(mix_ref[...], wa_ref[...], preferred_element_type=F32)
    y_b = jnp.dot(att_ref[...], wb_ref[...], preferred_element_type=F32)
    d = x.shape[1]
    ga = pg[:, POOL_DIM:POOL_DIM + d]
    gb = pg[:, POOL_DIM + d:]
    merged = (jax.nn.sigmoid(ga) * y_a + jax.nn.sigmoid(gb) * y_b).astype(BF16)
    y = jnp.dot(merged, wo_ref[...], preferred_element_type=F32)
    z = alpha * x + gate * y
    o_ref[...] = _layer_norm(z, lng_ref[ln_row:ln_row + 1, :], lnb_ref[ln_row:ln_row + 1, :])


def _mix_out(x2d, mod, att2d, wpg, wpool, pool_scale, wa, wb, wo, ln_g, ln_b, *, seq, alpha, mod_row, ln_row):
    n, d = x2d.shape
    tiles_per_seq = seq // MIX_ROWS
    halo_blocks = MIX_ROWS // POOL_HALO
    resident = dict(pipeline_mode=pl.Buffered(1))
    full = lambda a: pl.BlockSpec(a.shape, lambda i: (0,) * a.ndim, **resident)
    return pl.pallas_call(
        functools.partial(_mix_kernel, alpha=alpha, mod_row=mod_row, ln_row=ln_row, tiles_per_seq=tiles_per_seq),
        grid=(n // MIX_ROWS,),
        in_specs=[
            pl.BlockSpec((MIX_ROWS, d), lambda i: (i, 0)),
            pl.BlockSpec((POOL_HALO, d), lambda i: (jnp.maximum(i * halo_blocks - 1, 0), 0)),
            pl.BlockSpec((1, N_ADA, d), lambda i: (i // tiles_per_seq, 0, 0)),
            pl.BlockSpec((MIX_ROWS, ATTN_DIM), lambda i: (i, 0)),
            full(wpg), full(wpool), full(pool_scale), full(wa), full(wb), full(wo),
            pl.BlockSpec(ln_g.shape, lambda i: (0, 0)),
            pl.BlockSpec(ln_b.shape, lambda i: (0, 0)),
        ],
        out_specs=pl.BlockSpec((MIX_ROWS, d), lambda i: (i, 0)),
        out_shape=jax.ShapeDtypeStruct((n, d), F32),
        scratch_shapes=[
            pltpu.VMEM((POOL_HALO + MIX_ROWS, POOL_DIM), F32),
            pltpu.VMEM((MIX_ROWS, POOL_DIM), BF16),
        ],
        compiler_params=pltpu.CompilerParams(dimension_semantics=("arbitrary",),
                                             vmem_limit_bytes=V7X_VMEM_LIMIT_BYTES),
        name="mix_out",
    )(x2d, x2d, mod, att2d, wpg, wpool, pool_scale, wa, wb, wo, ln_g, ln_b)


def kernel(x, c, w_ada, b_ada, ln_g, ln_b, ffn1_w_gate, ffn1_w_up, ffn1_w_down, w_in, w_pool, pool_scale,
           w_a, w_b, w_out, rel_bias, ffn2_w_gate, ffn2_w_up, ffn2_w_down):
    bsz, seq, d = x.shape
    depth = w_ada.shape[0]
    alpha = (2.0 * depth) ** 0.25
    topk = min(TOP_K, seq // 4)
    assert seq % FFN_ROWS == 0 and seq % PROJ_ROWS == 0 and seq % MIX_ROWS == 0 and seq % ATT_BLOCK == 0
    assert PROJ_ROWS % ATT_BLOCK == 0 and MIX_ROWS % POOL_HALO == 0 and POOL_HALO >= max(POOL_WINDOWS) - 1
    far_bucket = _far_bucket(ATT_BLOCK + 1, max(seq - 1, ATT_BLOCK + 1))

    o_q = POOL_DIM
    o_k = o_q + ATTN_DIM
    o_v = o_k + ATTN_DIM
    o_qi = o_v + ATTN_DIM
    o_ki = o_qi + IDX_HEADS * IDX_DIM
    o_wi = o_ki + IDX_DIM
    o_ga = o_wi + IDX_HEADS

    btiles = _rel_bias_tiles(rel_bias, far_bucket)
    x2d = x.reshape(bsz * seq, d)
    for l in range(depth):
        wl = w_in[l]
        zeros_ki = jnp.zeros((d, LANES - IDX_DIM), wl.dtype)
        w_ki = wl[:, o_ki:o_wi]
        wqk = jnp.concatenate([wl[:, o_q:o_v], wl[:, o_qi:o_ki]], axis=1).astype(BF16)
        wv_t = wl[:, o_v:o_qi].T.astype(BF16)
        wki = jnp.concatenate([w_ki, zeros_ki, zeros_ki, w_ki], axis=1).astype(BF16)
        wwi_t = jnp.pad(wl[:, o_wi:o_ga].T, ((0, 2 * SUBLANES - IDX_HEADS), (0, 0))).astype(BF16)
        wpg = jnp.concatenate([wl[:, :POOL_DIM], wl[:, o_ga:]], axis=1).astype(BF16)

        mod = _ada_mod(c, w_ada[l], b_ada[l]).reshape(bsz, N_ADA, d)
        x2d = _ffn(x2d, mod, ln_g[l], ln_b[l], ffn1_w_gate[l].astype(BF16), ffn1_w_up[l].astype(BF16),
                   ffn1_w_down[l].astype(BF16), seq=seq, alpha=alpha, mod_row=0, ln_row=0)
        q, k, qi, vt, kia, kib, wit = _attn_proj(x2d, mod, wqk, wv_t, wki, wwi_t, bsz=bsz, seq=seq, mod_row=3)
        att = _attention(q.reshape(bsz, seq, ATTN_DIM), qi.reshape(bsz, seq, IDX_HEADS * IDX_DIM), wit,
                         k.reshape(bsz, seq, ATTN_DIM), vt, kia.reshape(bsz, seq, LANES),
                         kib.reshape(bsz, seq, LANES), btiles, bsz=bsz, seq=seq, topk=topk)
        x2d = _mix_out(x2d, mod, att.reshape(bsz * seq, ATTN_DIM), wpg, w_pool[l].astype(BF16),
                       pool_scale[l].reshape(1, POOL_DIM), w_a[l].astype(BF16), w_b[l].astype(BF16),
                       w_out[l].astype(BF16), ln_g[l], ln_b[l], seq=seq, alpha=alpha, mod_row=3, ln_row=1)
        x2d = _ffn(x2d, mod, ln_g[l], ln_b[l], ffn2_w_gate[l].astype(BF16), ffn2_w_up[l].astype(BF16),
                   ffn2_w_down[l].astype(BF16), seq=seq, alpha=alpha, mod_row=6, ln_row=2)
    return x2d.reshape(bsz, seq, d)
```

```python
import functools
import math

import numpy as np
import jax
import jax.numpy as jnp
from jax import lax
from jax.experimental import pallas as pl
from jax.experimental.pallas import tpu as pltpu

POOL_WINDOWS = (2, 4, 8, 16)
POOL_GROUP_DIM = 128
POOL_DIM = len(POOL_WINDOWS) * POOL_GROUP_DIM
N_HEADS = 8
HEAD_DIM = 64
ATTN_DIM = N_HEADS * HEAD_DIM
HEAD_PAD = 16
HEAD_ROWS = HEAD_DIM + HEAD_PAD
IDX_HEADS = 8
IDX_DIM = 64
TOP_K = 256
REL_BUCKETS = 32
REL_MAX_DIST = 128
N_ADA = 9
LN_EPS = 1e-5
POOL_HALO = 16

LANES = 128
SUBLANES = 8
FFN_ROWS = 512
FFN_COLS = 256
PROJ_ROWS = 512
MIX_ROWS = 512
ATT_BLOCK = 256
ADA_COLS = 1024
COUNT_CHAINS = 4
COUNT_ROWS = 64
BISECT_STEPS = 14

LOG2E = math.log2(math.e)
BF16 = jnp.bfloat16
F32 = jnp.float32
NT_DIMS = (((1,), (1,)), ((), ()))


def _tile_bytes(shape, dtype):
    itemsize = jnp.dtype(dtype).itemsize
    sublanes = SUBLANES * (4 // itemsize)
    shape = (1,) * (2 - len(shape)) + tuple(shape)
    rows = -(-shape[-2] // sublanes) * sublanes
    cols = -(-shape[-1] // LANES) * LANES
    return math.prod(shape[:-2]) * rows * cols * itemsize


def _vmem_limit(pipelined, resident, temporaries):
    return (2 * sum(_tile_bytes(*b) for b in pipelined) + sum(_tile_bytes(*b) for b in resident)
            + sum(_tile_bytes(*b) for b in temporaries))


def _layer_norm(z, g, b):
    mu = jnp.mean(z, axis=-1, keepdims=True)
    zc = z - mu
    var = jnp.mean(zc * zc, axis=-1, keepdims=True)
    return zc * lax.rsqrt(var + LN_EPS) * g + b


def _silu(a):
    return a * jax.nn.sigmoid(a)


def _ada_kernel(c_ref, w_ref, b_ref, o_ref):
    a = _silu(c_ref[...])
    o_ref[...] = jnp.dot(a, w_ref[...], preferred_element_type=F32) + b_ref[...]


def _ada_mod(c, w_ada, b_ada):
    bsz, d = c.shape
    n = w_ada.shape[1]
    rows = -(-bsz // SUBLANES) * SUBLANES
    c_pad = jnp.pad(c, ((0, rows - bsz), (0, 0)))
    out = pl.pallas_call(
        _ada_kernel,
        grid=(n // ADA_COLS,),
        in_specs=[
            pl.BlockSpec((rows, d), lambda j: (0, 0)),
            pl.BlockSpec((d, ADA_COLS), lambda j: (0, j)),
            pl.BlockSpec((1, ADA_COLS), lambda j: (0, j)),
        ],
        out_specs=pl.BlockSpec((rows, ADA_COLS), lambda j: (0, j)),
        out_shape=jax.ShapeDtypeStruct((rows, n), F32),
        compiler_params=pltpu.CompilerParams(
            dimension_semantics=("arbitrary",),
            vmem_limit_bytes=_vmem_limit([((rows, d), F32), ((d, ADA_COLS), F32), ((1, ADA_COLS), F32),
                                          ((rows, ADA_COLS), F32)], [], [((rows, ADA_COLS), F32)])),
        name="ada_mod",
    )(c_pad, w_ada, b_ada.reshape(1, n))
    return out[:bsz]


def _ffn_kernel(x_ref, mod_ref, lng_ref, lnb_ref, wg_ref, wu_ref, wd_ref, o_ref, h_ref, *, alpha, mod_row, ln_row):
    x = x_ref[...]
    sh = mod_ref[0, mod_row:mod_row + 1, :]
    sc = mod_ref[0, mod_row + 1:mod_row + 2, :]
    gate = mod_ref[0, mod_row + 2:mod_row + 3, :]
    u = (x * (1.0 + sc) + sh).astype(BF16)
    d_ff = wg_ref.shape[1]
    for c in range(d_ff // FFN_COLS):
        sl = slice(c * FFN_COLS, (c + 1) * FFN_COLS)
        a = jnp.dot(u, wg_ref[:, sl], preferred_element_type=F32)
        b = jnp.dot(u, wu_ref[:, sl], preferred_element_type=F32)
        h_ref[:, sl] = (_silu(a) * b).astype(BF16)
    y = jnp.dot(h_ref[...], wd_ref[...], preferred_element_type=F32)
    z = alpha * x + (0.5 * gate) * y
    o_ref[...] = _layer_norm(z, lng_ref[ln_row:ln_row + 1, :], lnb_ref[ln_row:ln_row + 1, :])


def _ffn(x2d, mod, ln_g, ln_b, wg, wu, wd, *, seq, alpha, mod_row, ln_row):
    n, d = x2d.shape
    d_ff = wg.shape[1]
    tiles_per_seq = seq // FFN_ROWS
    resident = dict(pipeline_mode=pl.Buffered(1))
    return pl.pallas_call(
        functools.partial(_ffn_kernel, alpha=alpha, mod_row=mod_row, ln_row=ln_row),
        grid=(n // FFN_ROWS,),
        in_specs=[
            pl.BlockSpec((FFN_ROWS, d), lambda i: (i, 0)),
            pl.BlockSpec((1, N_ADA, d), lambda i: (i // tiles_per_seq, 0, 0)),
            pl.BlockSpec(ln_g.shape, lambda i: (0, 0)),
            pl.BlockSpec(ln_b.shape, lambda i: (0, 0)),
            pl.BlockSpec((d, d_ff), lambda i: (0, 0), **resident),
            pl.BlockSpec((d, d_ff), lambda i: (0, 0), **resident),
            pl.BlockSpec((d_ff, d), lambda i: (0, 0), **resident),
        ],
        out_specs=pl.BlockSpec((FFN_ROWS, d), lambda i: (i, 0)),
        out_shape=jax.ShapeDtypeStruct((n, d), F32),
        scratch_shapes=[pltpu.VMEM((FFN_ROWS, d_ff), BF16)],
        compiler_params=pltpu.CompilerParams(
            dimension_semantics=("arbitrary",),
            vmem_limit_bytes=_vmem_limit(
                [((FFN_ROWS, d), F32), ((N_ADA, d), F32), (ln_g.shape, F32), (ln_b.shape, F32), ((FFN_ROWS, d), F32)],
                [((d, d_ff), BF16), ((d, d_ff), BF16), ((d_ff, d), BF16), ((FFN_ROWS, d_ff), BF16)],
                [((FFN_ROWS, d), BF16), ((FFN_ROWS, FFN_COLS), F32), ((FFN_ROWS, FFN_COLS), F32),
                 ((FFN_ROWS, d), F32), ((FFN_ROWS, d), F32)])),
        name="ffn",
    )(x2d, mod, ln_g, ln_b, wg, wu, wd)


def _proj_kernel(x_ref, mod_ref, wqk_ref, wvw_ref, wki_ref,
                 q_ref, k_ref, qi_ref, vt_ref, kia_ref, kib_ref, wit_ref, *, mod_row):
    x = x_ref[...]
    sh = mod_ref[0, mod_row:mod_row + 1, :]
    sc = mod_ref[0, mod_row + 1:mod_row + 2, :]
    u = (x * (1.0 + sc) + sh).astype(BF16)
    qkq = jnp.dot(u, wqk_ref[...], preferred_element_type=F32)
    q_ref[...] = (qkq[:, :ATTN_DIM] * (HEAD_DIM ** -0.5 * LOG2E)).astype(BF16)
    k_ref[...] = qkq[:, ATTN_DIM:2 * ATTN_DIM].astype(BF16)
    qi_ref[...] = qkq[:, 2 * ATTN_DIM:].astype(BF16)
    kk = jnp.dot(u, wki_ref[...], preferred_element_type=F32)
    kia_ref[...] = kk[:, :LANES].astype(BF16)
    kib_ref[...] = kk[:, LANES:].astype(BF16)
    vw = lax.dot_general(wvw_ref[...], u, NT_DIMS, preferred_element_type=F32)
    vt = vw[:ATTN_DIM].astype(BF16)
    ones_rows = jnp.where(lax.broadcasted_iota(jnp.int32, (HEAD_PAD, ATT_BLOCK), 0) == 0, 1.0, 0.0).astype(BF16)
    for c in range(vt_ref.shape[1]):
        for h in range(N_HEADS):
            vt_ref[0, c, h * HEAD_ROWS:h * HEAD_ROWS + HEAD_DIM, :] = \
                vt[h * HEAD_DIM:(h + 1) * HEAD_DIM, c * ATT_BLOCK:(c + 1) * ATT_BLOCK]
            vt_ref[0, c, h * HEAD_ROWS + HEAD_DIM:(h + 1) * HEAD_ROWS, :] = ones_rows
    wit_ref[0] = vw[ATTN_DIM:ATTN_DIM + IDX_HEADS, :]


def _attn_proj(x2d, mod, wqk, wvw_t, wki, *, bsz, seq, mod_row):
    n, d = x2d.shape
    tiles_per_seq = seq // PROJ_ROWS
    chunks_per_tile = PROJ_ROWS // ATT_BLOCK
    resident = dict(pipeline_mode=pl.Buffered(1))
    row_spec = lambda cols: pl.BlockSpec((PROJ_ROWS, cols), lambda i: (i, 0))
    return pl.pallas_call(
        functools.partial(_proj_kernel, mod_row=mod_row),
        grid=(n // PROJ_ROWS,),
        in_specs=[
            pl.BlockSpec((PROJ_ROWS, d), lambda i: (i, 0)),
            pl.BlockSpec((1, N_ADA, d), lambda i: (i // tiles_per_seq, 0, 0)),
            pl.BlockSpec(wqk.shape, lambda i: (0, 0), **resident),
            pl.BlockSpec(wvw_t.shape, lambda i: (0, 0), **resident),
            pl.BlockSpec(wki.shape, lambda i: (0, 0), **resident),
        ],
        out_specs=[
            row_spec(ATTN_DIM), row_spec(ATTN_DIM), row_spec(IDX_HEADS * IDX_DIM),
            pl.BlockSpec((1, chunks_per_tile, N_HEADS * HEAD_ROWS, ATT_BLOCK),
                         lambda i: (i // tiles_per_seq, i % tiles_per_seq, 0, 0)),
            row_spec(LANES), row_spec(LANES),
            pl.BlockSpec((1, IDX_HEADS, PROJ_ROWS), lambda i: (i // tiles_per_seq, 0, i % tiles_per_seq)),
        ],
        out_shape=[
            jax.ShapeDtypeStruct((n, ATTN_DIM), BF16),
            jax.ShapeDtypeStruct((n, ATTN_DIM), BF16),
            jax.ShapeDtypeStruct((n, IDX_HEADS * IDX_DIM), BF16),
            jax.ShapeDtypeStruct((bsz, seq // ATT_BLOCK, N_HEADS * HEAD_ROWS, ATT_BLOCK), BF16),
            jax.ShapeDtypeStruct((n, LANES), BF16),
            jax.ShapeDtypeStruct((n, LANES), BF16),
            jax.ShapeDtypeStruct((bsz, IDX_HEADS, seq), F32),
        ],
        compiler_params=pltpu.CompilerParams(
            dimension_semantics=("arbitrary",),
            vmem_limit_bytes=_vmem_limit(
                [((PROJ_ROWS, d), F32), ((N_ADA, d), F32), ((PROJ_ROWS, ATTN_DIM), BF16), ((PROJ_ROWS, ATTN_DIM), BF16),
                 ((PROJ_ROWS, IDX_HEADS * IDX_DIM), BF16), ((chunks_per_tile, N_HEADS * HEAD_ROWS, ATT_BLOCK), BF16),
                 ((PROJ_ROWS, LANES), BF16), ((PROJ_ROWS, LANES), BF16), ((IDX_HEADS, PROJ_ROWS), F32)],
                [(wqk.shape, BF16), (wvw_t.shape, BF16), (wki.shape, BF16)],
                [((PROJ_ROWS, d), BF16), ((PROJ_ROWS, wqk.shape[1]), F32), ((PROJ_ROWS, wki.shape[1]), F32),
                 ((wvw_t.shape[0], PROJ_ROWS), F32)])),
        name="attn_proj",
    )(x2d, mod, wqk, wvw_t, wki)


def _t5_bucket(n):
    max_exact = REL_BUCKETS // 2
    nf = jnp.maximum(n, 1).astype(F32)
    large = max_exact + jnp.floor(jnp.log(nf / max_exact) / math.log(REL_MAX_DIST / max_exact)
                                  * (REL_BUCKETS - max_exact)).astype(jnp.int32)
    large = jnp.minimum(large, REL_BUCKETS - 1)
    return jnp.where(n < max_exact, n, large)


def _far_bucket(first_dist, last_dist):
    n = np.arange(first_dist, last_dist + 1, dtype=np.float32)
    max_exact = REL_BUCKETS // 2
    large = max_exact + (np.log(n / np.float32(max_exact)) / np.float32(math.log(REL_MAX_DIST / max_exact))
                         * np.float32(REL_BUCKETS - max_exact)).astype(np.int32)
    buckets = np.where(n < max_exact, n.astype(np.int32), np.minimum(large, REL_BUCKETS - 1))
    assert buckets.min() == buckets.max(), "key chunks two or more blocks away must share one bias bucket"
    return int(buckets[0])


def _bias_kernel(rb_ref, o_ref, *, far_bucket):
    o = pl.program_id(0)
    row = lax.broadcasted_iota(jnp.int32, (ATT_BLOCK, ATT_BLOCK), 0)
    col = lax.broadcasted_iota(jnp.int32, (ATT_BLOCK, ATT_BLOCK), 1)
    dist = o * ATT_BLOCK + col - row
    bucket = _t5_bucket(jnp.maximum(dist, 0))
    for h in range(N_HEADS):
        tile = jnp.zeros((ATT_BLOCK, ATT_BLOCK), F32)
        for b in range(REL_BUCKETS):
            tile = jnp.where(bucket == b, rb_ref[b, h], tile)
        o_ref[0, h] = (tile - rb_ref[far_bucket, h]) * LOG2E


def _rel_bias_tiles(rel_bias, far_bucket):
    return pl.pallas_call(
        functools.partial(_bias_kernel, far_bucket=far_bucket),
        grid=(2,),
        in_specs=[pl.BlockSpec(memory_space=pltpu.SMEM)],
        out_specs=pl.BlockSpec((1, N_HEADS, ATT_BLOCK, ATT_BLOCK), lambda o: (o, 0, 0, 0)),
        out_shape=jax.ShapeDtypeStruct((2, N_HEADS, ATT_BLOCK, ATT_BLOCK), F32),
        compiler_params=pltpu.CompilerParams(
            dimension_semantics=("arbitrary",),
            vmem_limit_bytes=_vmem_limit([((N_HEADS, ATT_BLOCK, ATT_BLOCK), F32)], [],
                                         [((ATT_BLOCK, ATT_BLOCK), F32)] * 3)),
        name="rel_bias",
    )(rel_bias)


def _attn_kernel(q_ref, qi_ref, wit_ref, k_ref, vt_ref, kia_ref, kib_ref, bt_ref, o_ref,
                 sc_ref, zc_ref, ranked_ref, qh_ref, lg_ref, p_ref, st_ref, acc_ref, *, topk, seq):
    blk = ATT_BLOCK
    groups = blk // SUBLANES
    i = pl.program_id(1)
    nch = i + 1
    kf = float(topk)
    inf = jnp.inf

    def chunk_start(j):
        return j * blk if isinstance(j, int) else pl.multiple_of(j * blk, blk)

    def as_groups(x):
        return x.reshape(groups, SUBLANES, blk)

    def lanes8(v):
        return jnp.broadcast_to(v, (SUBLANES, blk))

    def colmin(x8):
        return jnp.min(x8, axis=0, keepdims=True)

    def colmax(x8):
        return jnp.max(x8, axis=0, keepdims=True)

    def colsum(x8):
        return jnp.sum(x8, axis=0, keepdims=True)

    wf = wit_ref[0] * (IDX_DIM ** -0.5)
    qi = qi_ref[0]
    row = lax.broadcasted_iota(jnp.int32, (blk, blk), 0)
    col = lax.broadcasted_iota(jnp.int32, (blk, blk), 1)
    causal = row <= col

    def chunk_scores(j):
        r0 = chunk_start(j)
        ka = kia_ref[0, pl.ds(r0, blk), :]
        kb = kib_ref[0, pl.ds(r0, blk), :]
        s = jnp.zeros((blk, blk), F32)
        for hp in range(IDX_HEADS // 2):
            qp = qi[:, hp * LANES:(hp + 1) * LANES]
            a0 = lax.dot_general(ka, qp, NT_DIMS, preferred_element_type=F32)
            a1 = lax.dot_general(kb, qp, NT_DIMS, preferred_element_type=F32)
            s = s + jnp.maximum(a0, 0.0) * wf[2 * hp:2 * hp + 1, :]
            s = s + jnp.maximum(a1, 0.0) * wf[2 * hp + 1:2 * hp + 2, :]
        return r0, s * (IDX_HEADS ** -0.5)

    def score_stats(j, s_lo, s_hi, stats):
        mn8, mx8, pos8, zer8 = stats
        hi3 = as_groups(s_hi)
        zer8 = zer8 + jnp.sum(jnp.where(hi3 == 0.0, 1.0, 0.0), axis=0)
        zc_ref[j] = zer8
        return (jnp.minimum(mn8, jnp.min(as_groups(s_lo), axis=0)),
                jnp.maximum(mx8, jnp.max(hi3, axis=0)),
                pos8 + jnp.sum(jnp.where(hi3 > 0.0, 1.0, 0.0), axis=0),
                zer8)

    def score_body(j, stats):
        r0, s = chunk_scores(j)
        sc_ref[pl.ds(r0, blk), :] = s
        return score_stats(j, s, s, stats)

    def score_pair(jp, stats):
        return score_body(2 * jp + 1, score_body(2 * jp, stats))

    zeros8 = jnp.zeros((SUBLANES, blk), F32)
    stats = (jnp.full((SUBLANES, blk), inf, F32), jnp.full((SUBLANES, blk), -inf, F32), zeros8, zeros8)
    stats = lax.fori_loop(0, i // 2, score_pair, stats)
    def score_diag(st):
        r_diag, s_diag = chunk_scores(i)
        s_diag_hi = jnp.where(causal, s_diag, -inf)
        sc_ref[pl.ds(r_diag, blk), :] = s_diag_hi
        return score_stats(i, jnp.where(causal, s_diag, inf), s_diag_hi, st)

    mn8, mx8, pos8, zer8 = lax.cond(i % 2 == 1, lambda st: score_diag(score_body(i - 1, st)), score_diag, stats)

    @pl.when(nch % 2 == 1)
    def _():
        sc_ref[pl.ds(chunk_start(nch), blk), :] = jnp.full((blk, blk), -inf, F32)

    npair = (nch + 1) // 2

    def count_gt(thr):
        t8 = lanes8(thr)

        def body(jp, accs):
            base = pl.multiple_of(jp * 2 * blk, 2 * blk)
            accs = list(accs)
            for s in range(2 * blk // COUNT_ROWS):
                x = sc_ref[pl.ds(base + s * COUNT_ROWS, COUNT_ROWS), :]
                for g in range(COUNT_ROWS // SUBLANES):
                    hit = jnp.where(x[g * SUBLANES:(g + 1) * SUBLANES] > t8, 1.0, 0.0)
                    accs[g % COUNT_CHAINS] = accs[g % COUNT_CHAINS] + hit
            return tuple(accs)

        accs = lax.fori_loop(0, npair, body, tuple(jnp.zeros((SUBLANES, blk), F32) for _ in range(COUNT_CHAINS)))
        return colsum(sum(accs[1:], accs[0]))

    def bisect(_, st):
        lo, hi, low, done = st
        mid = 0.5 * lo + 0.5 * hi
        c = count_gt(mid)
        live = done < 0.5
        up = jnp.logical_and(live, c >= kf)
        down = jnp.logical_and(live, c < kf)
        low = jnp.where(up, mid, low)
        lo = jnp.where(up, mid, lo)
        hi = jnp.where(down, mid, hi)
        done = jnp.where(c == kf, 1.0, done)
        return lo, hi, low, done

    tpos = i * blk + lax.broadcasted_iota(jnp.int32, (1, blk), 1)
    pos, zer = colsum(pos8), colsum(zer8)
    mn, mx = colmin(mn8), colmax(mx8)
    zero_tie = jnp.logical_and(pos < kf, pos + zer >= kf)
    pos_ge = pos >= kf
    done0 = jnp.where(jnp.logical_or(jnp.logical_or(tpos + 1 <= topk, zero_tie), pos == kf), 1.0, 0.0)
    low0 = jnp.where(jnp.logical_or(zero_tie, pos_ge), 0.0, -inf)
    lo0 = jnp.where(pos_ge, jnp.maximum(mn, 0.0), mn)
    hi0 = jnp.where(pos_ge, mx, jnp.minimum(mx, 0.0))
    tie0 = jnp.where(zero_tie, 0.0, inf)
    need0 = jnp.where(zero_tie, kf - pos, 0.0)

    def next_value_above(thr):
        t8 = lanes8(thr)

        def body(jp, accs):
            base = pl.multiple_of(jp * 2 * blk, 2 * blk)
            accs = list(accs)
            for s in range(2 * blk // COUNT_ROWS):
                x = sc_ref[pl.ds(base + s * COUNT_ROWS, COUNT_ROWS), :]
                for g in range(COUNT_ROWS // SUBLANES):
                    xg = x[g * SUBLANES:(g + 1) * SUBLANES]
                    accs[g % COUNT_CHAINS] = jnp.minimum(accs[g % COUNT_CHAINS], jnp.where(xg > t8, xg, inf))
            return tuple(accs)

        accs = lax.fori_loop(0, npair, body, tuple(jnp.full((SUBLANES, blk), inf, F32) for _ in range(COUNT_CHAINS)))
        return colmin(functools.reduce(jnp.minimum, accs))

    def climb_cond(st):
        return jnp.min(st[1]) < 0.5

    def climb_body(st):
        low, done, tie, need = st
        live = done < 0.5
        cand = next_value_above(low)
        cgt = count_gt(cand)
        found = jnp.logical_and(live, cgt < kf)
        tie = jnp.where(found, cand, tie)
        need = jnp.where(found, kf - cgt, need)
        low = jnp.where(live, cand, low)
        done = jnp.where(jnp.logical_and(live, cgt <= kf), 1.0, done)
        return low, done, tie, need

    def select(_):
        _, _, low, done = lax.fori_loop(0, BISECT_STEPS, bisect, (lo0, hi0, low0, done0))
        low, _, tie, need = lax.while_loop(climb_cond, climb_body, (low, done, tie0, need0))
        return low, tie, need

    low, tie, need = lax.cond(jnp.min(done0) < 0.5, select, lambda _: (low0, tie0, need0), 0)

    def plain_mask(_):
        low8 = lanes8(low)

        def body(j, _):
            r0 = chunk_start(j)
            x = as_groups(sc_ref[pl.ds(r0, blk), :])
            sc_ref[pl.ds(r0, blk), :] = jnp.where(x > low8[None], 0.0, -inf).reshape(blk, blk)
            return 0

        return lax.fori_loop(0, nch, body, 0)

    def ranked_chunk(j, tri, base):
        r0 = chunk_start(j)
        x = sc_ref[pl.ds(r0, blk), :]
        eq = x == tie
        rank = jnp.dot(tri, jnp.where(eq, 1.0, 0.0).astype(BF16), preferred_element_type=F32)
        sel = jnp.logical_or(x > low, jnp.logical_and(eq, rank + base <= need))
        sc_ref[pl.ds(r0, blk), :] = jnp.where(sel, 0.0, -inf)
        return rank[blk - 1:blk, :]

    def tie_mask(_):
        tri = jnp.where(row >= col, 1.0, 0.0).astype(BF16)
        lax.fori_loop(0, nch, lambda j, base: base + ranked_chunk(j, tri, base), jnp.zeros((1, blk), F32))
        return 0

    def zero_tie_mask(_):
        tri = jnp.where(row >= col, 1.0, 0.0).astype(BF16)

        def cut_body(j, ncut):
            return ncut + jnp.where(colsum(zc_ref[j]) < need, 1.0, 0.0)

        ncut = lax.fori_loop(0, nch, cut_body, jnp.zeros((1, blk), F32))
        cut = jnp.where(tie < inf, ncut, inf)

        def flag_body(j, _):
            ranked_ref[j] = 0
            return 0

        lax.fori_loop(0, nch, flag_body, 0)

        def cut_cond(pending):
            return jnp.min(pending) < inf

        def cut_chunk(pending):
            jf = jnp.min(pending)
            j = jf.astype(jnp.int32)
            base = jnp.where(j > 0, colsum(zc_ref[jnp.maximum(j - 1, 0)]), 0.0)
            ranked_chunk(j, tri, base)
            ranked_ref[j] = 1
            return jnp.where(pending == jf, inf, pending)

        lax.while_loop(cut_cond, cut_chunk, cut)
        low8, cut8 = lanes8(low), lanes8(cut)

        def body(j, _):
            @pl.when(ranked_ref[j] == 0)
            def _():
                r0 = chunk_start(j)
                x = as_groups(sc_ref[pl.ds(r0, blk), :])
                at_low = jnp.where(jnp.logical_and(cut8 < inf, cut8 > j.astype(F32)), 0.0, -inf)
                out = jnp.where(x > low8[None], 0.0, jnp.where(x == low8[None], at_low[None], -inf))
                sc_ref[pl.ds(r0, blk), :] = out.reshape(blk, blk)
            return 0

        return lax.fori_loop(0, nch, body, 0)

    tied = tie < inf
    mask_kind = jnp.where(jnp.max(jnp.where(tied, 1.0, 0.0)) < 0.5, 0,
                          jnp.where(jnp.max(jnp.where(jnp.logical_and(tied, tie != 0.0), 1.0, 0.0)) < 0.5, 1, 2))
    lax.switch(mask_kind, [plain_mask, zero_tie_mask, tie_mask], 0)

    q = q_ref[0]
    lane = lax.broadcasted_iota(jnp.int32, (blk, LANES), 1)
    for h in range(N_HEADS):
        hp = h // 2
        in_head = (lane < HEAD_DIM) if h % 2 == 0 else (lane >= HEAD_DIM)
        qh_ref[h] = jnp.where(in_head, q[:, hp * LANES:(hp + 1) * LANES], jnp.zeros((), BF16))
    acc_ref[...] = jnp.zeros(acc_ref.shape, F32)

    def logits(j, h, mb, near):
        hp = h // 2
        kc = k_ref[0, pl.ds(chunk_start(j), blk), hp * LANES:(hp + 1) * LANES]
        lg = lax.dot_general(kc, qh_ref[h], NT_DIMS, preferred_element_type=F32) + mb
        return lg if near is None else lg + bt_ref[near, h]

    CMAX, RMAX, RESC = 0, 1, 2

    def stage_logits(j, near):
        mb = sc_ref[pl.ds(chunk_start(j), blk), :]
        cm8 = []
        for h in range(N_HEADS):
            lg = logits(j, h, mb, near)
            lg_ref[h] = lg
            cm8.append(jnp.max(as_groups(lg), axis=0))
        for h in range(N_HEADS):
            st_ref[CMAX, h] = lanes8(colmax(cm8[h]))

    def stage_exp():
        for h in range(N_HEADS):
            m_old = st_ref[RMAX, h]
            m_new = jnp.maximum(m_old, st_ref[CMAX, h])
            m_use = jnp.where(m_new == -inf, 0.0, m_new)
            p = jnp.exp2(as_groups(lg_ref[h]) - m_use[None])
            p_ref[h] = p.reshape(blk, blk).astype(BF16)
            st_ref[RMAX, h] = m_new
            st_ref[RESC, h] = jnp.exp2(m_old - m_use)

    def stage_pv(j):
        for h in range(N_HEADS):
            rows = slice(h * HEAD_ROWS, (h + 1) * HEAD_ROWS)
            pv = jnp.dot(vt_ref[0, j, rows, :], p_ref[h], preferred_element_type=F32)
            acc = acc_ref[rows, :].reshape(HEAD_ROWS // SUBLANES, SUBLANES, blk) * st_ref[RESC, h][None]
            acc_ref[rows, :] = acc.reshape(HEAD_ROWS, blk) + pv

    def pipelined(j, near_next):
        stage_exp()
        stage_logits(j + 1, near_next)
        stage_pv(j)

    st_ref[RMAX] = jnp.full((N_HEADS, SUBLANES, blk), -inf, F32)

    def last_chunks(first):
        for j in range(first, 0):
            pipelined(i + j, -(j + 1))
        stage_exp()
        stage_pv(i)

    @pl.when(i == 0)
    def _():
        stage_logits(0, 0)
        last_chunks(0)

    @pl.when(i == 1)
    def _():
        stage_logits(0, 1)
        last_chunks(-1)

    @pl.when(i >= 2)
    def _():
        stage_logits(0, None)

    def far_body(j, _):
        pipelined(j, None)
        return 0

    lax.fori_loop(0, jnp.maximum(i - 2, 0), far_body, 0)

    @pl.when(i >= 2)
    def _():
        last_chunks(-2)

    heads = []
    for h in range(N_HEADS):
        r0 = h * HEAD_ROWS
        heads.append(acc_ref[r0:r0 + HEAD_DIM, :] / acc_ref[r0 + HEAD_DIM:r0 + HEAD_DIM + 1, :])
    o_ref[0] = jnp.concatenate(heads, axis=0).T.astype(BF16)


def _attention(q, qi, wit, k, vt, kia, kib, btiles, *, bsz, seq, topk):
    blk = ATT_BLOCK
    nblk = seq // blk
    return pl.pallas_call(
        functools.partial(_attn_kernel, topk=topk, seq=seq),
        grid=(bsz, nblk),
        in_specs=[
            pl.BlockSpec((1, blk, ATTN_DIM), lambda b, i: (b, i, 0)),
            pl.BlockSpec((1, blk, IDX_HEADS * IDX_DIM), lambda b, i: (b, i, 0)),
            pl.BlockSpec((1, IDX_HEADS, blk), lambda b, i: (b, 0, i)),
            pl.BlockSpec((1, seq, ATTN_DIM), lambda b, i: (b, 0, 0)),
            pl.BlockSpec((1, nblk, N_HEADS * HEAD_ROWS, blk), lambda b, i: (b, 0, 0, 0)),
            pl.BlockSpec((1, seq, LANES), lambda b, i: (b, 0, 0)),
            pl.BlockSpec((1, seq, LANES), lambda b, i: (b, 0, 0)),
            pl.BlockSpec(btiles.shape, lambda b, i: (0, 0, 0, 0)),
        ],
        out_specs=pl.BlockSpec((1, blk, ATTN_DIM), lambda b, i: (b, i, 0)),
        out_shape=jax.ShapeDtypeStruct((bsz, seq, ATTN_DIM), BF16),
        scratch_shapes=[
            pltpu.VMEM((seq + blk, blk), F32),
            pltpu.VMEM((nblk, SUBLANES, blk), F32),
            pltpu.SMEM((nblk,), jnp.int32),
            pltpu.VMEM((N_HEADS, blk, LANES), BF16),
            pltpu.VMEM((N_HEADS, blk, blk), F32),
            pltpu.VMEM((N_HEADS, blk, blk), BF16),
            pltpu.VMEM((3, N_HEADS, SUBLANES, blk), F32),
            pltpu.VMEM((N_HEADS * HEAD_ROWS, blk), F32),
        ],
        compiler_params=pltpu.CompilerParams(
            dimension_semantics=("arbitrary", "arbitrary"),
            vmem_limit_bytes=_vmem_limit(
                [((blk, ATTN_DIM), BF16), ((blk, IDX_HEADS * IDX_DIM), BF16), ((IDX_HEADS, blk), F32),
                 ((seq, ATTN_DIM), BF16), ((nblk, N_HEADS * HEAD_ROWS, blk), BF16), ((seq, LANES), BF16),
                 ((seq, LANES), BF16), (btiles.shape, F32), ((blk, ATTN_DIM), BF16)],
                [((seq + blk, blk), F32), ((nblk, SUBLANES, blk), F32), ((N_HEADS, blk, LANES), BF16),
                 ((N_HEADS, blk, blk), F32), ((N_HEADS, blk, blk), BF16), ((3, N_HEADS, SUBLANES, blk), F32),
                 ((N_HEADS * HEAD_ROWS, blk), F32)],
                [((blk, blk), F32)] * 4 + [((blk, blk), BF16), ((ATTN_DIM, blk), F32), ((blk, ATTN_DIM), F32)])),
        name="attention",
    )(q, qi, wit, k, vt, kia, kib, btiles)


def _mix_kernel(x_ref, mod_ref, att_ref, wpg_ref, wpool_ref, ps_ref, wa_ref, wb_ref, wo_ref,
                lng_ref, lnb_ref, o_ref, pe_ref, mix_ref, *, alpha, mod_row, ln_row, tiles_per_seq):
    rows = x_ref.shape[0]
    i = pl.program_id(0)
    seq_tile = i % tiles_per_seq

    @pl.when(i == 0)
    def _():
        pe_ref[rows:, :] = jnp.zeros((POOL_HALO, POOL_DIM), F32)

    x = x_ref[...]
    sh = mod_ref[0, mod_row:mod_row + 1, :]
    sc = mod_ref[0, mod_row + 1:mod_row + 2, :]
    gate = mod_ref[0, mod_row + 2:mod_row + 3, :]
    u = (x * (1.0 + sc) + sh).astype(BF16)
    pg = jnp.dot(u, wpg_ref[...], preferred_element_type=F32)
    pe_ref[0:POOL_HALO, :] = jnp.where(seq_tile == 0, 0.0, pe_ref[rows:, :])
    pe_ref[POOL_HALO:, :] = pg[:, :POOL_DIM]
    t = seq_tile * rows + lax.broadcasted_iota(jnp.int32, (rows, 1), 0)
    for g, w in enumerate(POOL_WINDOWS):
        cols = slice(g * POOL_GROUP_DIM, (g + 1) * POOL_GROUP_DIM)
        cur = pe_ref[POOL_HALO:, cols]
        win = cur
        for back in range(1, w):
            win = win + pe_ref[POOL_HALO - back:POOL_HALO - back + rows, cols]
        cnt = jnp.minimum(t + 1, w).astype(F32)
        pooled = (win / cnt - cur).astype(BF16)
        mixed = jnp.dot(pooled, wpool_ref[g], preferred_element_type=F32)
        mix_ref[:, cols] = (mixed * ps_ref[:, cols]).astype(BF16)
    y_a = jnp.dot(mix_ref[...], wa_ref[...], preferred_element_type=F32)
    y_b = jnp.dot(att_ref[...], wb_ref[...], preferred_element_type=F32)
    d = x.shape[1]
    ga = pg[:, POOL_DIM:POOL_DIM + d]
    gb = pg[:, POOL_DIM + d:]
    merged = (jax.nn.sigmoid(ga) * y_a + jax.nn.sigmoid(gb) * y_b).astype(BF16)
    y = jnp.dot(merged, wo_ref[...], preferred_element_type=F32)
    z = alpha * x + gate * y
    o_ref[...] = _layer_norm(z, lng_ref[ln_row:ln_row + 1, :], lnb_ref[ln_row:ln_row + 1, :])


def _mix_out(x2d, mod, att2d, wpg, wpool, pool_scale, wa, wb, wo, ln_g, ln_b, *, seq, alpha, mod_row, ln_row):
    n, d = x2d.shape
    tiles_per_seq = seq // MIX_ROWS
    resident = dict(pipeline_mode=pl.Buffered(1))
    full = lambda a: pl.BlockSpec(a.shape, lambda i: (0,) * a.ndim, **resident)
    return pl.pallas_call(
        functools.partial(_mix_kernel, alpha=alpha, mod_row=mod_row, ln_row=ln_row, tiles_per_seq=tiles_per_seq),
        grid=(n // MIX_ROWS,),
        in_specs=[
            pl.BlockSpec((MIX_ROWS, d), lambda i: (i, 0)),
            pl.BlockSpec((1, N_ADA, d), lambda i: (i // tiles_per_seq, 0, 0)),
            pl.BlockSpec((MIX_ROWS, ATTN_DIM), lambda i: (i, 0)),
            full(wpg), full(wpool), full(pool_scale), full(wa), full(wb), full(wo),
            pl.BlockSpec(ln_g.shape, lambda i: (0, 0)),
            pl.BlockSpec(ln_b.shape, lambda i: (0, 0)),
        ],
        out_specs=pl.BlockSpec((MIX_ROWS, d), lambda i: (i, 0)),
        out_shape=jax.ShapeDtypeStruct((n, d), F32),
        scratch_shapes=[
            pltpu.VMEM((POOL_HALO + MIX_ROWS, POOL_DIM), F32),
            pltpu.VMEM((MIX_ROWS, POOL_DIM), BF16),
        ],
        compiler_params=pltpu.CompilerParams(
            dimension_semantics=("arbitrary",),
            vmem_limit_bytes=_vmem_limit(
                [((MIX_ROWS, d), F32), ((N_ADA, d), F32), ((MIX_ROWS, ATTN_DIM), BF16), (ln_g.shape, F32),
                 (ln_b.shape, F32), ((MIX_ROWS, d), F32)],
                [(wpg.shape, BF16), (wpool.shape, BF16), (pool_scale.shape, F32), (wa.shape, BF16), (wb.shape, BF16),
                 (wo.shape, BF16), ((POOL_HALO + MIX_ROWS, POOL_DIM), F32), ((MIX_ROWS, POOL_DIM), BF16)],
                [((MIX_ROWS, d), BF16), ((MIX_ROWS, wpg.shape[1]), F32)] + [((MIX_ROWS, d), F32)] * 5)),
        name="mix_out",
    )(x2d, mod, att2d, wpg, wpool, pool_scale, wa, wb, wo, ln_g, ln_b)


def kernel(x, c, w_ada, b_ada, ln_g, ln_b, ffn1_w_gate, ffn1_w_up, ffn1_w_down, w_in, w_pool, pool_scale,
           w_a, w_b, w_out, rel_bias, ffn2_w_gate, ffn2_w_up, ffn2_w_down):
    bsz, seq, d = x.shape
    depth = w_ada.shape[0]
    alpha = (2.0 * depth) ** 0.25
    topk = min(TOP_K, seq // 4)
    assert seq % FFN_ROWS == 0 and seq % PROJ_ROWS == 0 and seq % MIX_ROWS == 0 and seq % ATT_BLOCK == 0
    assert PROJ_ROWS % ATT_BLOCK == 0 and POOL_HALO >= max(POOL_WINDOWS) - 1
    far_bucket = _far_bucket(ATT_BLOCK + 1, max(seq - 1, ATT_BLOCK + 1))

    o_q = POOL_DIM
    o_k = o_q + ATTN_DIM
    o_v = o_k + ATTN_DIM
    o_qi = o_v + ATTN_DIM
    o_ki = o_qi + IDX_HEADS * IDX_DIM
    o_wi = o_ki + IDX_DIM
    o_ga = o_wi + IDX_HEADS

    btiles = _rel_bias_tiles(rel_bias, far_bucket)
    x2d = x.reshape(bsz * seq, d)
    for l in range(depth):
        wl = w_in[l]
        zeros_ki = jnp.zeros((d, LANES - IDX_DIM), wl.dtype)
        w_ki = wl[:, o_ki:o_wi]
        wqk = jnp.concatenate([wl[:, o_q:o_v], wl[:, o_qi:o_ki]], axis=1).astype(BF16)
        wvw_t = jnp.pad(jnp.concatenate([wl[:, o_v:o_qi], wl[:, o_wi:o_ga]], axis=1).T,
                        ((0, 2 * SUBLANES - IDX_HEADS), (0, 0))).astype(BF16)
        wki = jnp.concatenate([w_ki, zeros_ki, zeros_ki, w_ki], axis=1).astype(BF16)
        wpg = jnp.concatenate([wl[:, :POOL_DIM], wl[:, o_ga:]], axis=1).astype(BF16)

        mod = _ada_mod(c, w_ada[l], b_ada[l]).reshape(bsz, N_ADA, d)
        x2d = _ffn(x2d, mod, ln_g[l], ln_b[l], ffn1_w_gate[l].astype(BF16), ffn1_w_up[l].astype(BF16),
                   ffn1_w_down[l].astype(BF16), seq=seq, alpha=alpha, mod_row=0, ln_row=0)
        q, k, qi, vt, kia, kib, wit = _attn_proj(x2d, mod, wqk, wvw_t, wki, bsz=bsz, seq=seq, mod_row=3)
        att = _attention(q.reshape(bsz, seq, ATTN_DIM), qi.reshape(bsz, seq, IDX_HEADS * IDX_DIM), wit,
                         k.reshape(bsz, seq, ATTN_DIM), vt, kia.reshape(bsz, seq, LANES),
                         kib.reshape(bsz, seq, LANES), btiles, bsz=bsz, seq=seq, topk=topk)
        x2d = _mix_out(x2d, mod, att.reshape(bsz * seq, ATTN_DIM), wpg, w_pool[l].astype(BF16),
                       pool_scale[l].reshape(1, POOL_DIM), w_a[l].astype(BF16), w_b[l].astype(BF16),
                       w_out[l].astype(BF16), ln_g[l], ln_b[l], seq=seq, alpha=alpha, mod_row=3, ln_row=1)
        x2d = _ffn(x2d, mod, ln_g[l], ln_b[l], ffn2_w_gate[l].astype(BF16), ffn2_w_up[l].astype(BF16),
                   ffn2_w_down[l].astype(BF16), seq=seq, alpha=alpha, mod_row=6, ln_row=2)
    return x2d.reshape(bsz, seq, d)
```

```python
import functools
import math

import numpy as np
import jax
import jax.numpy as jnp
from jax import lax
from jax.experimental import pallas as pl
from jax.experimental.pallas import tpu as pltpu

POOL_WINDOWS = (2, 4, 8, 16)
POOL_GROUP_DIM = 128
POOL_DIM = len(POOL_WINDOWS) * POOL_GROUP_DIM
N_HEADS = 8
HEAD_DIM = 64
ATTN_DIM = N_HEADS * HEAD_DIM
HEAD_PAD = 16
HEAD_ROWS = HEAD_DIM + HEAD_PAD
IDX_HEADS = 8
IDX_DIM = 64
TOP_K = 256
REL_BUCKETS = 32
REL_MAX_DIST = 128
N_ADA = 9
LN_EPS = 1e-5
POOL_HALO = 16

LANES = 128
SUBLANES = 8
FFN_ROWS = 512
FFN_COLS = 256
PROJ_ROWS = 512
MIX_ROWS = 512
ATT_BLOCK = 256
ADA_COLS = 1024
SCORE_RUN = 4
COUNT_CHAINS = 4
COUNT_ROWS = 64
BISECT_STEPS = 14

LOG2E = math.log2(math.e)
BF16 = jnp.bfloat16
F32 = jnp.float32
NT_DIMS = (((1,), (1,)), ((), ()))


def _tile_bytes(shape, dtype):
    itemsize = jnp.dtype(dtype).itemsize
    sublanes = SUBLANES * (4 // itemsize)
    shape = (1,) * (2 - len(shape)) + tuple(shape)
    rows = -(-shape[-2] // sublanes) * sublanes
    cols = -(-shape[-1] // LANES) * LANES
    return math.prod(shape[:-2]) * rows * cols * itemsize


def _vmem_limit(pipelined, resident, temporaries):
    return (2 * sum(_tile_bytes(*b) for b in pipelined) + sum(_tile_bytes(*b) for b in resident)
            + sum(_tile_bytes(*b) for b in temporaries))


def _layer_norm(z, g, b):
    mu = jnp.mean(z, axis=-1, keepdims=True)
    zc = z - mu
    var = jnp.mean(zc * zc, axis=-1, keepdims=True)
    return zc * lax.rsqrt(var + LN_EPS) * g + b


def _silu(a):
    return a * jax.nn.sigmoid(a)


def _ada_kernel(c_ref, w_ref, b_ref, o_ref):
    a = _silu(c_ref[...])
    o_ref[...] = jnp.dot(a, w_ref[...], preferred_element_type=F32) + b_ref[...]


def _ada_mod(c, w_ada, b_ada):
    bsz, d = c.shape
    n = w_ada.shape[1]
    rows = -(-bsz // SUBLANES) * SUBLANES
    c_pad = jnp.pad(c, ((0, rows - bsz), (0, 0)))
    out = pl.pallas_call(
        _ada_kernel,
        grid=(n // ADA_COLS,),
        in_specs=[
            pl.BlockSpec((rows, d), lambda j: (0, 0)),
            pl.BlockSpec((d, ADA_COLS), lambda j: (0, j)),
            pl.BlockSpec((1, ADA_COLS), lambda j: (0, j)),
        ],
        out_specs=pl.BlockSpec((rows, ADA_COLS), lambda j: (0, j)),
        out_shape=jax.ShapeDtypeStruct((rows, n), F32),
        compiler_params=pltpu.CompilerParams(
            dimension_semantics=("arbitrary",),
            vmem_limit_bytes=_vmem_limit([((rows, d), F32), ((d, ADA_COLS), F32), ((1, ADA_COLS), F32),
                                          ((rows, ADA_COLS), F32)], [], [((rows, ADA_COLS), F32)])),
        name="ada_mod",
    )(c_pad, w_ada, b_ada.reshape(1, n))
    return out[:bsz]


def _ffn_kernel(x_ref, mod_ref, lng_ref, lnb_ref, wg_ref, wu_ref, wd_ref, o_ref, h_ref, *, alpha, mod_row, ln_row):
    x = x_ref[...]
    sh = mod_ref[0, mod_row:mod_row + 1, :]
    sc = mod_ref[0, mod_row + 1:mod_row + 2, :]
    gate = mod_ref[0, mod_row + 2:mod_row + 3, :]
    u = (x * (1.0 + sc) + sh).astype(BF16)
    d_ff = wg_ref.shape[1]
    for c in range(d_ff // FFN_COLS):
        sl = slice(c * FFN_COLS, (c + 1) * FFN_COLS)
        a = jnp.dot(u, wg_ref[:, sl], preferred_element_type=F32)
        b = jnp.dot(u, wu_ref[:, sl], preferred_element_type=F32)
        h_ref[:, sl] = (_silu(a) * b).astype(BF16)
    y = jnp.dot(h_ref[...], wd_ref[...], preferred_element_type=F32)
    z = alpha * x + (0.5 * gate) * y
    o_ref[...] = _layer_norm(z, lng_ref[ln_row:ln_row + 1, :], lnb_ref[ln_row:ln_row + 1, :])


def _ffn(x2d, mod, ln_g, ln_b, wg, wu, wd, *, seq, alpha, mod_row, ln_row):
    n, d = x2d.shape
    d_ff = wg.shape[1]
    tiles_per_seq = seq // FFN_ROWS
    resident = dict(pipeline_mode=pl.Buffered(1))
    return pl.pallas_call(
        functools.partial(_ffn_kernel, alpha=alpha, mod_row=mod_row, ln_row=ln_row),
        grid=(n // FFN_ROWS,),
        in_specs=[
            pl.BlockSpec((FFN_ROWS, d), lambda i: (i, 0)),
            pl.BlockSpec((1, N_ADA, d), lambda i: (i // tiles_per_seq, 0, 0)),
            pl.BlockSpec(ln_g.shape, lambda i: (0, 0)),
            pl.BlockSpec(ln_b.shape, lambda i: (0, 0)),
            pl.BlockSpec((d, d_ff), lambda i: (0, 0), **resident),
            pl.BlockSpec((d, d_ff), lambda i: (0, 0), **resident),
            pl.BlockSpec((d_ff, d), lambda i: (0, 0), **resident),
        ],
        out_specs=pl.BlockSpec((FFN_ROWS, d), lambda i: (i, 0)),
        out_shape=jax.ShapeDtypeStruct((n, d), F32),
        scratch_shapes=[pltpu.VMEM((FFN_ROWS, d_ff), BF16)],
        compiler_params=pltpu.CompilerParams(
            dimension_semantics=("arbitrary",),
            vmem_limit_bytes=_vmem_limit(
                [((FFN_ROWS, d), F32), ((N_ADA, d), F32), (ln_g.shape, F32), (ln_b.shape, F32), ((FFN_ROWS, d), F32)],
                [((d, d_ff), BF16), ((d, d_ff), BF16), ((d_ff, d), BF16), ((FFN_ROWS, d_ff), BF16)],
                [((FFN_ROWS, d), BF16), ((FFN_ROWS, FFN_COLS), F32), ((FFN_ROWS, FFN_COLS), F32),
                 ((FFN_ROWS, d), F32), ((FFN_ROWS, d), F32)])),
        name="ffn",
    )(x2d, mod, ln_g, ln_b, wg, wu, wd)


def _proj_kernel(x_ref, mod_ref, wqk_ref, wvw_ref, wki_ref,
                 q_ref, k_ref, qi_ref, vt_ref, kia_ref, kib_ref, wit_ref, *, mod_row):
    x = x_ref[...]
    sh = mod_ref[0, mod_row:mod_row + 1, :]
    sc = mod_ref[0, mod_row + 1:mod_row + 2, :]
    u = (x * (1.0 + sc) + sh).astype(BF16)
    qkq = jnp.dot(u, wqk_ref[...], preferred_element_type=F32)
    q_ref[...] = (qkq[:, :ATTN_DIM] * (HEAD_DIM ** -0.5 * LOG2E)).astype(BF16)
    k_ref[...] = qkq[:, ATTN_DIM:2 * ATTN_DIM].astype(BF16)
    qi_ref[...] = qkq[:, 2 * ATTN_DIM:].astype(BF16)
    kk = jnp.dot(u, wki_ref[...], preferred_element_type=F32)
    kia_ref[...] = kk[:, :LANES].astype(BF16)
    kib_ref[...] = kk[:, LANES:].astype(BF16)
    vw = lax.dot_general(wvw_ref[...], u, NT_DIMS, preferred_element_type=F32)
    vt = vw[:ATTN_DIM].astype(BF16)
    ones_rows = jnp.where(lax.broadcasted_iota(jnp.int32, (HEAD_PAD, ATT_BLOCK), 0) == 0, 1.0, 0.0).astype(BF16)
    for c in range(vt_ref.shape[1]):
        for h in range(N_HEADS):
            vt_ref[0, c, h * HEAD_ROWS:h * HEAD_ROWS + HEAD_DIM, :] = \
                vt[h * HEAD_DIM:(h + 1) * HEAD_DIM, c * ATT_BLOCK:(c + 1) * ATT_BLOCK]
            vt_ref[0, c, h * HEAD_ROWS + HEAD_DIM:(h + 1) * HEAD_ROWS, :] = ones_rows
    wit_ref[0] = vw[ATTN_DIM:ATTN_DIM + IDX_HEADS, :]


def _attn_proj(x2d, mod, wqk, wvw_t, wki, *, bsz, seq, mod_row):
    n, d = x2d.shape
    tiles_per_seq = seq // PROJ_ROWS
    chunks_per_tile = PROJ_ROWS // ATT_BLOCK
    resident = dict(pipeline_mode=pl.Buffered(1))
    row_spec = lambda cols: pl.BlockSpec((PROJ_ROWS, cols), lambda i: (i, 0))
    return pl.pallas_call(
        functools.partial(_proj_kernel, mod_row=mod_row),
        grid=(n // PROJ_ROWS,),
        in_specs=[
            pl.BlockSpec((PROJ_ROWS, d), lambda i: (i, 0)),
            pl.BlockSpec((1, N_ADA, d), lambda i: (i // tiles_per_seq, 0, 0)),
            pl.BlockSpec(wqk.shape, lambda i: (0, 0), **resident),
            pl.BlockSpec(wvw_t.shape, lambda i: (0, 0), **resident),
            pl.BlockSpec(wki.shape, lambda i: (0, 0), **resident),
        ],
        out_specs=[
            row_spec(ATTN_DIM), row_spec(ATTN_DIM), row_spec(IDX_HEADS * IDX_DIM),
            pl.BlockSpec((1, chunks_per_tile, N_HEADS * HEAD_ROWS, ATT_BLOCK),
                         lambda i: (i // tiles_per_seq, i % tiles_per_seq, 0, 0)),
            row_spec(LANES), row_spec(LANES),
            pl.BlockSpec((1, IDX_HEADS, PROJ_ROWS), lambda i: (i // tiles_per_seq, 0, i % tiles_per_seq)),
        ],
        out_shape=[
            jax.ShapeDtypeStruct((n, ATTN_DIM), BF16),
            jax.ShapeDtypeStruct((n, ATTN_DIM), BF16),
            jax.ShapeDtypeStruct((n, IDX_HEADS * IDX_DIM), BF16),
            jax.ShapeDtypeStruct((bsz, seq // ATT_BLOCK, N_HEADS * HEAD_ROWS, ATT_BLOCK), BF16),
            jax.ShapeDtypeStruct((n, LANES), BF16),
            jax.ShapeDtypeStruct((n, LANES), BF16),
            jax.ShapeDtypeStruct((bsz, IDX_HEADS, seq), F32),
        ],
        compiler_params=pltpu.CompilerParams(
            dimension_semantics=("arbitrary",),
            vmem_limit_bytes=_vmem_limit(
                [((PROJ_ROWS, d), F32), ((N_ADA, d), F32), ((PROJ_ROWS, ATTN_DIM), BF16), ((PROJ_ROWS, ATTN_DIM), BF16),
                 ((PROJ_ROWS, IDX_HEADS * IDX_DIM), BF16), ((chunks_per_tile, N_HEADS * HEAD_ROWS, ATT_BLOCK), BF16),
                 ((PROJ_ROWS, LANES), BF16), ((PROJ_ROWS, LANES), BF16), ((IDX_HEADS, PROJ_ROWS), F32)],
                [(wqk.shape, BF16), (wvw_t.shape, BF16), (wki.shape, BF16)],
                [((PROJ_ROWS, d), BF16), ((PROJ_ROWS, wqk.shape[1]), F32), ((PROJ_ROWS, wki.shape[1]), F32),
                 ((wvw_t.shape[0], PROJ_ROWS), F32)])),
        name="attn_proj",
    )(x2d, mod, wqk, wvw_t, wki)


def _t5_bucket(n):
    max_exact = REL_BUCKETS // 2
    nf = jnp.maximum(n, 1).astype(F32)
    large = max_exact + jnp.floor(jnp.log(nf / max_exact) / math.log(REL_MAX_DIST / max_exact)
                                  * (REL_BUCKETS - max_exact)).astype(jnp.int32)
    large = jnp.minimum(large, REL_BUCKETS - 1)
    return jnp.where(n < max_exact, n, large)


def _far_bucket(first_dist, last_dist):
    n = np.arange(first_dist, last_dist + 1, dtype=np.float32)
    max_exact = REL_BUCKETS // 2
    large = max_exact + (np.log(n / np.float32(max_exact)) / np.float32(math.log(REL_MAX_DIST / max_exact))
                         * np.float32(REL_BUCKETS - max_exact)).astype(np.int32)
    buckets = np.where(n < max_exact, n.astype(np.int32), np.minimum(large, REL_BUCKETS - 1))
    assert buckets.min() == buckets.max(), "key chunks two or more blocks away must share one bias bucket"
    return int(buckets[0])


def _bias_kernel(rb_ref, o_ref, *, far_bucket):
    o = pl.program_id(0)
    row = lax.broadcasted_iota(jnp.int32, (ATT_BLOCK, ATT_BLOCK), 0)
    col = lax.broadcasted_iota(jnp.int32, (ATT_BLOCK, ATT_BLOCK), 1)
    dist = o * ATT_BLOCK + col - row
    bucket = _t5_bucket(jnp.maximum(dist, 0))
    for h in range(N_HEADS):
        tile = jnp.zeros((ATT_BLOCK, ATT_BLOCK), F32)
        for b in range(REL_BUCKETS):
            tile = jnp.where(bucket == b, rb_ref[b, h], tile)
        o_ref[0, h] = (tile - rb_ref[far_bucket, h]) * LOG2E


def _rel_bias_tiles(rel_bias, far_bucket):
    return pl.pallas_call(
        functools.partial(_bias_kernel, far_bucket=far_bucket),
        grid=(2,),
        in_specs=[pl.BlockSpec(memory_space=pltpu.SMEM)],
        out_specs=pl.BlockSpec((1, N_HEADS, ATT_BLOCK, ATT_BLOCK), lambda o: (o, 0, 0, 0)),
        out_shape=jax.ShapeDtypeStruct((2, N_HEADS, ATT_BLOCK, ATT_BLOCK), F32),
        compiler_params=pltpu.CompilerParams(
            dimension_semantics=("arbitrary",),
            vmem_limit_bytes=_vmem_limit([((N_HEADS, ATT_BLOCK, ATT_BLOCK), F32)], [],
                                         [((ATT_BLOCK, ATT_BLOCK), F32)] * 3)),
        name="rel_bias",
    )(rel_bias)


def _attn_kernel(q_ref, qi_ref, wit_ref, k_ref, vt_ref, kia_ref, kib_ref, bt_ref, o_ref,
                 sc_ref, zc_ref, ranked_ref, qh_ref, lg_ref, p_ref, st_ref, acc_ref, *, topk, seq):
    blk = ATT_BLOCK
    groups = blk // SUBLANES
    i = pl.program_id(1)
    nch = i + 1
    kf = float(topk)
    inf = jnp.inf

    def chunk_start(j):
        return j * blk if isinstance(j, int) else pl.multiple_of(j * blk, blk)

    def as_groups(x):
        return x.reshape(groups, SUBLANES, blk)

    def lanes8(v):
        return jnp.broadcast_to(v, (SUBLANES, blk))

    def colmin(x8):
        return jnp.min(x8, axis=0, keepdims=True)

    def colmax(x8):
        return jnp.max(x8, axis=0, keepdims=True)

    def colsum(x8):
        return jnp.sum(x8, axis=0, keepdims=True)

    wf = wit_ref[0] * (IDX_DIM ** -0.5)
    qi = qi_ref[0]
    row = lax.broadcasted_iota(jnp.int32, (blk, blk), 0)
    col = lax.broadcasted_iota(jnp.int32, (blk, blk), 1)
    causal = row <= col

    def chunk_scores(j):
        r0 = chunk_start(j)
        ka = kia_ref[0, pl.ds(r0, blk), :]
        kb = kib_ref[0, pl.ds(r0, blk), :]
        s = jnp.zeros((blk, blk), F32)
        for hp in range(IDX_HEADS // 2):
            qp = qi[:, hp * LANES:(hp + 1) * LANES]
            a0 = lax.dot_general(ka, qp, NT_DIMS, preferred_element_type=F32)
            a1 = lax.dot_general(kb, qp, NT_DIMS, preferred_element_type=F32)
            s = s + jnp.maximum(a0, 0.0) * wf[2 * hp:2 * hp + 1, :]
            s = s + jnp.maximum(a1, 0.0) * wf[2 * hp + 1:2 * hp + 2, :]
        return r0, s * (IDX_HEADS ** -0.5)

    def score_stats(j, s_lo, s_hi, stats):
        mn8, mx8, pos8, zer8 = stats
        hi3 = as_groups(s_hi)
        zer8 = zer8 + jnp.sum(jnp.where(hi3 == 0.0, 1.0, 0.0), axis=0)
        zc_ref[j] = zer8
        return (jnp.minimum(mn8, jnp.min(as_groups(s_lo), axis=0)),
                jnp.maximum(mx8, jnp.max(hi3, axis=0)),
                pos8 + jnp.sum(jnp.where(hi3 > 0.0, 1.0, 0.0), axis=0),
                zer8)

    def score_body(j, stats):
        r0, s = chunk_scores(j)
        sc_ref[pl.ds(r0, blk), :] = s
        return score_stats(j, s, s, stats)

    def score_run(first, count, stats):
        for c in range(count):
            stats = score_body(first + c, stats)
        return stats

    def score_diag(st):
        r_diag, s_diag = chunk_scores(i)
        s_diag_hi = jnp.where(causal, s_diag, -inf)
        sc_ref[pl.ds(r_diag, blk), :] = s_diag_hi
        return score_stats(i, jnp.where(causal, s_diag, inf), s_diag_hi, st)

    zeros8 = jnp.zeros((SUBLANES, blk), F32)
    stats = (jnp.full((SUBLANES, blk), inf, F32), jnp.full((SUBLANES, blk), -inf, F32), zeros8, zeros8)
    stats = lax.fori_loop(0, i // SCORE_RUN, lambda jq, st: score_run(SCORE_RUN * jq, SCORE_RUN, st), stats)
    left = i % SCORE_RUN
    mn8, mx8, pos8, zer8 = lax.switch(
        left, [functools.partial(lambda r, st: score_diag(score_run(i - r, r, st)), r) for r in range(SCORE_RUN)], stats)

    @pl.when(nch % 2 == 1)
    def _():
        sc_ref[pl.ds(chunk_start(nch), blk), :] = jnp.full((blk, blk), -inf, F32)

    npair = (nch + 1) // 2

    def count_gt(thr):
        t8 = lanes8(thr)

        def body(jp, accs):
            base = pl.multiple_of(jp * 2 * blk, 2 * blk)
            accs = list(accs)
            for s in range(2 * blk // COUNT_ROWS):
                x = sc_ref[pl.ds(base + s * COUNT_ROWS, COUNT_ROWS), :]
                for g in range(COUNT_ROWS // SUBLANES):
                    hit = jnp.where(x[g * SUBLANES:(g + 1) * SUBLANES] > t8, 1.0, 0.0)
                    accs[g % COUNT_CHAINS] = accs[g % COUNT_CHAINS] + hit
            return tuple(accs)

        accs = lax.fori_loop(0, npair, body, tuple(jnp.zeros((SUBLANES, blk), F32) for _ in range(COUNT_CHAINS)))
        return colsum(sum(accs[1:], accs[0]))

    def bisect(_, st):
        lo, hi, low, done = st
        mid = 0.5 * lo + 0.5 * hi
        c = count_gt(mid)
        live = done < 0.5
        up = jnp.logical_and(live, c >= kf)
        down = jnp.logical_and(live, c < kf)
        low = jnp.where(up, mid, low)
        lo = jnp.where(up, mid, lo)
        hi = jnp.where(down, mid, hi)
        done = jnp.where(c == kf, 1.0, done)
        return lo, hi, low, done

    tpos = i * blk + lax.broadcasted_iota(jnp.int32, (1, blk), 1)
    pos, zer = colsum(pos8), colsum(zer8)
    mn, mx = colmin(mn8), colmax(mx8)
    zero_tie = jnp.logical_and(pos < kf, pos + zer >= kf)
    pos_ge = pos >= kf
    done0 = jnp.where(jnp.logical_or(jnp.logical_or(tpos + 1 <= topk, zero_tie), pos == kf), 1.0, 0.0)
    low0 = jnp.where(jnp.logical_or(zero_tie, pos_ge), 0.0, -inf)
    lo0 = jnp.where(pos_ge, jnp.maximum(mn, 0.0), mn)
    hi0 = jnp.where(pos_ge, mx, jnp.minimum(mx, 0.0))
    tie0 = jnp.where(zero_tie, 0.0, inf)
    need0 = jnp.where(zero_tie, kf - pos, 0.0)

    def next_value_above(thr):
        t8 = lanes8(thr)

        def body(jp, accs):
            base = pl.multiple_of(jp * 2 * blk, 2 * blk)
            accs = list(accs)
            for s in range(2 * blk // COUNT_ROWS):
                x = sc_ref[pl.ds(base + s * COUNT_ROWS, COUNT_ROWS), :]
                for g in range(COUNT_ROWS // SUBLANES):
                    xg = x[g * SUBLANES:(g + 1) * SUBLANES]
                    accs[g % COUNT_CHAINS] = jnp.minimum(accs[g % COUNT_CHAINS], jnp.where(xg > t8, xg, inf))
            return tuple(accs)

        accs = lax.fori_loop(0, npair, body, tuple(jnp.full((SUBLANES, blk), inf, F32) for _ in range(COUNT_CHAINS)))
        return colmin(functools.reduce(jnp.minimum, accs))

    def climb_cond(st):
        return jnp.min(st[1]) < 0.5

    def climb_body(st):
        low, done, tie, need = st
        live = done < 0.5
        cand = next_value_above(low)
        cgt = count_gt(cand)
        found = jnp.logical_and(live, cgt < kf)
        tie = jnp.where(found, cand, tie)
        need = jnp.where(found, kf - cgt, need)
        low = jnp.where(live, cand, low)
        done = jnp.where(jnp.logical_and(live, cgt <= kf), 1.0, done)
        return low, done, tie, need

    def select(_):
        _, _, low, done = lax.fori_loop(0, BISECT_STEPS, bisect, (lo0, hi0, low0, done0))
        low, _, tie, need = lax.while_loop(climb_cond, climb_body, (low, done, tie0, need0))
        return low, tie, need

    low, tie, need = lax.cond(jnp.min(done0) < 0.5, select, lambda _: (low0, tie0, need0), 0)

    def plain_mask(_):
        low8 = lanes8(low)

        def body(j, _):
            r0 = chunk_start(j)
            x = as_groups(sc_ref[pl.ds(r0, blk), :])
            sc_ref[pl.ds(r0, blk), :] = jnp.where(x > low8[None], 0.0, -inf).reshape(blk, blk)
            return 0

        return lax.fori_loop(0, nch, body, 0)

    def ranked_chunk(j, tri, base):
        r0 = chunk_start(j)
        x = sc_ref[pl.ds(r0, blk), :]
        eq = x == tie
        rank = jnp.dot(tri, jnp.where(eq, 1.0, 0.0).astype(BF16), preferred_element_type=F32)
        sel = jnp.logical_or(x > low, jnp.logical_and(eq, rank + base <= need))
        sc_ref[pl.ds(r0, blk), :] = jnp.where(sel, 0.0, -inf)
        return rank[blk - 1:blk, :]

    def tie_mask(_):
        tri = jnp.where(row >= col, 1.0, 0.0).astype(BF16)
        lax.fori_loop(0, nch, lambda j, base: base + ranked_chunk(j, tri, base), jnp.zeros((1, blk), F32))
        return 0

    def zero_tie_mask(_):
        tri = jnp.where(row >= col, 1.0, 0.0).astype(BF16)

        def cut_body(j, ncut):
            return ncut + jnp.where(colsum(zc_ref[j]) < need, 1.0, 0.0)

        ncut = lax.fori_loop(0, nch, cut_body, jnp.zeros((1, blk), F32))
        cut = jnp.where(tie < inf, ncut, inf)

        def flag_body(j, _):
            ranked_ref[j] = 0
            return 0

        lax.fori_loop(0, nch, flag_body, 0)

        def cut_cond(pending):
            return jnp.min(pending) < inf

        def cut_chunk(pending):
            jf = jnp.min(pending)
            j = jf.astype(jnp.int32)
            base = jnp.where(j > 0, colsum(zc_ref[jnp.maximum(j - 1, 0)]), 0.0)
            ranked_chunk(j, tri, base)
            ranked_ref[j] = 1
            return jnp.where(pending == jf, inf, pending)

        lax.while_loop(cut_cond, cut_chunk, cut)
        low8, cut8 = lanes8(low), lanes8(cut)

        def body(j, _):
            @pl.when(ranked_ref[j] == 0)
            def _():
                r0 = chunk_start(j)
                x = as_groups(sc_ref[pl.ds(r0, blk), :])
                at_low = jnp.where(jnp.logical_and(cut8 < inf, cut8 > j.astype(F32)), 0.0, -inf)
                out = jnp.where(x > low8[None], 0.0, jnp.where(x == low8[None], at_low[None], -inf))
                sc_ref[pl.ds(r0, blk), :] = out.reshape(blk, blk)
            return 0

        return lax.fori_loop(0, nch, body, 0)

    tied = tie < inf
    mask_kind = jnp.where(jnp.max(jnp.where(tied, 1.0, 0.0)) < 0.5, 0,
                          jnp.where(jnp.max(jnp.where(jnp.logical_and(tied, tie != 0.0), 1.0, 0.0)) < 0.5, 1, 2))
    lax.switch(mask_kind, [plain_mask, zero_tie_mask, tie_mask], 0)

    q = q_ref[0]
    lane = lax.broadcasted_iota(jnp.int32, (blk, LANES), 1)
    for h in range(N_HEADS):
        hp = h // 2
        in_head = (lane < HEAD_DIM) if h % 2 == 0 else (lane >= HEAD_DIM)
        qh_ref[h] = jnp.where(in_head, q[:, hp * LANES:(hp + 1) * LANES], jnp.zeros((), BF16))
    acc_ref[...] = jnp.zeros(acc_ref.shape, F32)

    def logits(j, h, mb, near):
        hp = h // 2
        kc = k_ref[0, pl.ds(chunk_start(j), blk), hp * LANES:(hp + 1) * LANES]
        lg = lax.dot_general(kc, qh_ref[h], NT_DIMS, preferred_element_type=F32) + mb
        return lg if near is None else lg + bt_ref[near, h]

    CMAX, RMAX, RESC = 0, 1, 2

    def stage_logits(j, near):
        mb = sc_ref[pl.ds(chunk_start(j), blk), :]
        cm8 = []
        for h in range(N_HEADS):
            lg = logits(j, h, mb, near)
            lg_ref[h] = lg
            cm8.append(jnp.max(as_groups(lg), axis=0))
        for h in range(N_HEADS):
            st_ref[CMAX, h] = lanes8(colmax(cm8[h]))

    def stage_exp():
        for h in range(N_HEADS):
            m_old = st_ref[RMAX, h]
            m_new = jnp.maximum(m_old, st_ref[CMAX, h])
            m_use = jnp.where(m_new == -inf, 0.0, m_new)
            p = jnp.exp2(as_groups(lg_ref[h]) - m_use[None])
            p_ref[h] = p.reshape(blk, blk).astype(BF16)
            st_ref[RMAX, h] = m_new
            st_ref[RESC, h] = jnp.exp2(m_old - m_use)

    def stage_pv(j):
        for h in range(N_HEADS):
            rows = slice(h * HEAD_ROWS, (h + 1) * HEAD_ROWS)
            pv = jnp.dot(vt_ref[0, j, rows, :], p_ref[h], preferred_element_type=F32)
            acc = acc_ref[rows, :].reshape(HEAD_ROWS // SUBLANES, SUBLANES, blk) * st_ref[RESC, h][None]
            acc_ref[rows, :] = acc.reshape(HEAD_ROWS, blk) + pv

    def pipelined(j, near_next):
        stage_exp()
        stage_logits(j + 1, near_next)
        stage_pv(j)

    st_ref[RMAX] = jnp.full((N_HEADS, SUBLANES, blk), -inf, F32)

    def last_chunks(first):
        for j in range(first, 0):
            pipelined(i + j, -(j + 1))
        stage_exp()
        stage_pv(i)

    @pl.when(i == 0)
    def _():
        stage_logits(0, 0)
        last_chunks(0)

    @pl.when(i == 1)
    def _():
        stage_logits(0, 1)
        last_chunks(-1)

    @pl.when(i >= 2)
    def _():
        stage_logits(0, None)

    def far_body(j, _):
        pipelined(j, None)
        return 0

    lax.fori_loop(0, jnp.maximum(i - 2, 0), far_body, 0)

    @pl.when(i >= 2)
    def _():
        last_chunks(-2)

    heads = []
    for h in range(N_HEADS):
        r0 = h * HEAD_ROWS
        heads.append(acc_ref[r0:r0 + HEAD_DIM, :] / acc_ref[r0 + HEAD_DIM:r0 + HEAD_DIM + 1, :])
    o_ref[0] = jnp.concatenate(heads, axis=0).T.astype(BF16)


def _attention(q, qi, wit, k, vt, kia, kib, btiles, *, bsz, seq, topk):
    blk = ATT_BLOCK
    nblk = seq // blk
    return pl.pallas_call(
        functools.partial(_attn_kernel, topk=topk, seq=seq),
        grid=(bsz, nblk),
        in_specs=[
            pl.BlockSpec((1, blk, ATTN_DIM), lambda b, i: (b, i, 0)),
            pl.BlockSpec((1, blk, IDX_HEADS * IDX_DIM), lambda b, i: (b, i, 0)),
            pl.BlockSpec((1, IDX_HEADS, blk), lambda b, i: (b, 0, i)),
            pl.BlockSpec((1, seq, ATTN_DIM), lambda b, i: (b, 0, 0)),
            pl.BlockSpec((1, nblk, N_HEADS * HEAD_ROWS, blk), lambda b, i: (b, 0, 0, 0)),
            pl.BlockSpec((1, seq, LANES), lambda b, i: (b, 0, 0)),
            pl.BlockSpec((1, seq, LANES), lambda b, i: (b, 0, 0)),
            pl.BlockSpec(btiles.shape, lambda b, i: (0, 0, 0, 0)),
        ],
        out_specs=pl.BlockSpec((1, blk, ATTN_DIM), lambda b, i: (b, i, 0)),
        out_shape=jax.ShapeDtypeStruct((bsz, seq, ATTN_DIM), BF16),
        scratch_shapes=[
            pltpu.VMEM((seq + blk, blk), F32),
            pltpu.VMEM((nblk, SUBLANES, blk), F32),
            pltpu.SMEM((nblk,), jnp.int32),
            pltpu.VMEM((N_HEADS, blk, LANES), BF16),
            pltpu.VMEM((N_HEADS, blk, blk), F32),
            pltpu.VMEM((N_HEADS, blk, blk), BF16),
            pltpu.VMEM((3, N_HEADS, SUBLANES, blk), F32),
            pltpu.VMEM((N_HEADS * HEAD_ROWS, blk), F32),
        ],
        compiler_params=pltpu.CompilerParams(
            dimension_semantics=("arbitrary", "arbitrary"),
            vmem_limit_bytes=_vmem_limit(
                [((blk, ATTN_DIM), BF16), ((blk, IDX_HEADS * IDX_DIM), BF16), ((IDX_HEADS, blk), F32),
                 ((seq, ATTN_DIM), BF16), ((nblk, N_HEADS * HEAD_ROWS, blk), BF16), ((seq, LANES), BF16),
                 ((seq, LANES), BF16), (btiles.shape, F32), ((blk, ATTN_DIM), BF16)],
                [((seq + blk, blk), F32), ((nblk, SUBLANES, blk), F32), ((N_HEADS, blk, LANES), BF16),
                 ((N_HEADS, blk, blk), F32), ((N_HEADS, blk, blk), BF16), ((3, N_HEADS, SUBLANES, blk), F32),
                 ((N_HEADS * HEAD_ROWS, blk), F32)],
                [((blk, blk), F32)] * 4 + [((blk, blk), BF16), ((ATTN_DIM, blk), F32), ((blk, ATTN_DIM), F32)])),
        name="attention",
    )(q, qi, wit, k, vt, kia, kib, btiles)


def _mix_kernel(x_ref, mod_ref, att_ref, wpg_ref, wpool_ref, ps_ref, wa_ref, wb_ref, wo_ref,
                lng_ref, lnb_ref, o_ref, pe_ref, mix_ref, *, alpha, mod_row, ln_row, tiles_per_seq):
    rows = x_ref.shape[0]
    i = pl.program_id(0)
    seq_tile = i % tiles_per_seq

    @pl.when(i == 0)
    def _():
        pe_ref[rows:, :] = jnp.zeros((POOL_HALO, POOL_DIM), F32)

    x = x_ref[...]
    sh = mod_ref[0, mod_row:mod_row + 1, :]
    sc = mod_ref[0, mod_row + 1:mod_row + 2, :]
    gate = mod_ref[0, mod_row + 2:mod_row + 3, :]
    u = (x * (1.0 + sc) + sh).astype(BF16)
    pg = jnp.dot(u, wpg_ref[...], preferred_element_type=F32)
    pe_ref[0:POOL_HALO, :] = jnp.where(seq_tile == 0, 0.0, pe_ref[rows:, :])
    pe_ref[POOL_HALO:, :] = pg[:, :POOL_DIM]
    t = seq_tile * rows + lax.broadcasted_iota(jnp.int32, (rows, 1), 0)
    for g, w in enumerate(POOL_WINDOWS):
        cols = slice(g * POOL_GROUP_DIM, (g + 1) * POOL_GROUP_DIM)
        cur = pe_ref[POOL_HALO:, cols]
        win = cur
        for back in range(1, w):
            win = win + pe_ref[POOL_HALO - back:POOL_HALO - back + rows, cols]
        cnt = jnp.minimum(t + 1, w).astype(F32)
        pooled = (win / cnt - cur).astype(BF16)
        mixed = jnp.dot(pooled, wpool_ref[g], preferred_element_type=F32)
        mix_ref[:, cols] = (mixed * ps_ref[:, cols]).astype(BF16)
    y_a = jnp.dot(mix_ref[...], wa_ref[...], preferred_element_type=F32)
    y_b = jnp.dot(att_ref[...], wb_ref[...], preferred_element_type=F32)
    d = x.shape[1]
    ga = pg[:, POOL_DIM:POOL_DIM + d]
    gb = pg[:, POOL_DIM + d:]
    merged = (jax.nn.sigmoid(ga) * y_a + jax.nn.sigmoid(gb) * y_b).astype(BF16)
    y = jnp.dot(merged, wo_ref[...], preferred_element_type=F32)
    z = alpha * x + gate * y
    o_ref[...] = _layer_norm(z, lng_ref[ln_row:ln_row + 1, :], lnb_ref[ln_row:ln_row + 1, :])


def _mix_out(x2d, mod, att2d, wpg, wpool, pool_scale, wa, wb, wo, ln_g, ln_b, *, seq, alpha, mod_row, ln_row):
    n, d = x2d.shape
    tiles_per_seq = seq // MIX_ROWS
    resident = dict(pipeline_mode=pl.Buffered(1))
    full = lambda a: pl.BlockSpec(a.shape, lambda i: (0,) * a.ndim, **resident)
    return pl.pallas_call(
        functools.partial(_mix_kernel, alpha=alpha, mod_row=mod_row, ln_row=ln_row, tiles_per_seq=tiles_per_seq),
        grid=(n // MIX_ROWS,),
        in_specs=[
            pl.BlockSpec((MIX_ROWS, d), lambda i: (i, 0)),
            pl.BlockSpec((1, N_ADA, d), lambda i: (i // tiles_per_seq, 0, 0)),
            pl.BlockSpec((MIX_ROWS, ATTN_DIM), lambda i: (i, 0)),
            full(wpg), full(wpool), full(pool_scale), full(wa), full(wb), full(wo),
            pl.BlockSpec(ln_g.shape, lambda i: (0, 0)),
            pl.BlockSpec(ln_b.shape, lambda i: (0, 0)),
        ],
        out_specs=pl.BlockSpec((MIX_ROWS, d), lambda i: (i, 0)),
        out_shape=jax.ShapeDtypeStruct((n, d), F32),
        scratch_shapes=[
            pltpu.VMEM((POOL_HALO + MIX_ROWS, POOL_DIM), F32),
            pltpu.VMEM((MIX_ROWS, POOL_DIM), BF16),
        ],
        compiler_params=pltpu.CompilerParams(
            dimension_semantics=("arbitrary",),
            vmem_limit_bytes=_vmem_limit(
                [((MIX_ROWS, d), F32), ((N_ADA, d), F32), ((MIX_ROWS, ATTN_DIM), BF16), (ln_g.shape, F32),
                 (ln_b.shape, F32), ((MIX_ROWS, d), F32)],
                [(wpg.shape, BF16), (wpool.shape, BF16), (pool_scale.shape, F32), (wa.shape, BF16), (wb.shape, BF16),
                 (wo.shape, BF16), ((POOL_HALO + MIX_ROWS, POOL_DIM), F32), ((MIX_ROWS, POOL_DIM), BF16)],
                [((MIX_ROWS, d), BF16), ((MIX_ROWS, wpg.shape[1]), F32)] + [((MIX_ROWS, d), F32)] * 5)),
        name="mix_out",
    )(x2d, mod, att2d, wpg, wpool, pool_scale, wa, wb, wo, ln_g, ln_b)


def kernel(x, c, w_ada, b_ada, ln_g, ln_b, ffn1_w_gate, ffn1_w_up, ffn1_w_down, w_in, w_pool, pool_scale,
           w_a, w_b, w_out, rel_bias, ffn2_w_gate, ffn2_w_up, ffn2_w_down):
    bsz, seq, d = x.shape
    depth = w_ada.shape[0]
    alpha = (2.0 * depth) ** 0.25
    topk = min(TOP_K, seq // 4)
    assert seq % FFN_ROWS == 0 and seq % PROJ_ROWS == 0 and seq % MIX_ROWS == 0 and seq % ATT_BLOCK == 0
    assert PROJ_ROWS % ATT_BLOCK == 0 and POOL_HALO >= max(POOL_WINDOWS) - 1
    far_bucket = _far_bucket(ATT_BLOCK + 1, max(seq - 1, ATT_BLOCK + 1))

    o_q = POOL_DIM
    o_k = o_q + ATTN_DIM
    o_v = o_k + ATTN_DIM
    o_qi = o_v + ATTN_DIM
    o_ki = o_qi + IDX_HEADS * IDX_DIM
    o_wi = o_ki + IDX_DIM
    o_ga = o_wi + IDX_HEADS

    btiles = _rel_bias_tiles(rel_bias, far_bucket)
    x2d = x.reshape(bsz * seq, d)
    for l in range(depth):
        wl = w_in[l]
        zeros_ki = jnp.zeros((d, LANES - IDX_DIM), wl.dtype)
        w_ki = wl[:, o_ki:o_wi]
        wqk = jnp.concatenate([wl[:, o_q:o_v], wl[:, o_qi:o_ki]], axis=1).astype(BF16)
        wvw_t = jnp.pad(jnp.concatenate([wl[:, o_v:o_qi], wl[:, o_wi:o_ga]], axis=1).T,
                        ((0, 2 * SUBLANES - IDX_HEADS), (0, 0))).astype(BF16)
        wki = jnp.concatenate([w_ki, zeros_ki, zeros_ki, w_ki], axis=1).astype(BF16)
        wpg = jnp.concatenate([wl[:, :POOL_DIM], wl[:, o_ga:]], axis=1).astype(BF16)

        mod = _ada_mod(c, w_ada[l], b_ada[l]).reshape(bsz, N_ADA, d)
        x2d = _ffn(x2d, mod, ln_g[l], ln_b[l], ffn1_w_gate[l].astype(BF16), ffn1_w_up[l].astype(BF16),
                   ffn1_w_down[l].astype(BF16), seq=seq, alpha=alpha, mod_row=0, ln_row=0)
        q, k, qi, vt, kia, kib, wit = _attn_proj(x2d, mod, wqk, wvw_t, wki, bsz=bsz, seq=seq, mod_row=3)
        att = _attention(q.reshape(bsz, seq, ATTN_DIM), qi.reshape(bsz, seq, IDX_HEADS * IDX_DIM), wit,
                         k.reshape(bsz, seq, ATTN_DIM), vt, kia.reshape(bsz, seq, LANES),
                         kib.reshape(bsz, seq, LANES), btiles, bsz=bsz, seq=seq, topk=topk)
        x2d = _mix_out(x2d, mod, att.reshape(bsz * seq, ATTN_DIM), wpg, w_pool[l].astype(BF16),
                       pool_scale[l].reshape(1, POOL_DIM), w_a[l].astype(BF16), w_b[l].astype(BF16),
                       w_out[l].astype(BF16), ln_g[l], ln_b[l], seq=seq, alpha=alpha, mod_row=3, ln_row=1)
        x2d = _ffn(x2d, mod, ln_g[l], ln_b[l], ffn2_w_gate[l].astype(BF16), ffn2_w_up[l].astype(BF16),
                   ffn2_w_down[l].astype(BF16), seq=seq, alpha=alpha, mod_row=6, ln_row=2)
    return x2d.reshape(bsz, seq, d)
```

```python
import functools
import math

import numpy as np
import jax
import jax.numpy as jnp
from jax import lax
from jax.experimental import pallas as pl
from jax.experimental.pallas import tpu as pltpu

POOL_WINDOWS = (2, 4, 8, 16)
POOL_GROUP_DIM = 128
POOL_DIM = len(POOL_WINDOWS) * POOL_GROUP_DIM
N_HEADS = 8
HEAD_DIM = 64
ATTN_DIM = N_HEADS * HEAD_DIM
HEAD_PAD = 16
HEAD_ROWS = HEAD_DIM + HEAD_PAD
IDX_HEADS = 8
IDX_DIM = 64
TOP_K = 256
REL_BUCKETS = 32
REL_MAX_DIST = 128
N_ADA = 9
LN_EPS = 1e-5
POOL_HALO = 16

LANES = 128
SUBLANES = 8
FFN_ROWS = 512
FFN_COLS = 256
PROJ_ROWS = 512
MIX_ROWS = 512
ATT_BLOCK = 256
ADA_COLS = 1024
SCORE_RUN = 4
COUNT_CHAINS = 4
COUNT_ROWS = 64
BISECT_STEPS = 14

LOG2E = math.log2(math.e)
BF16 = jnp.bfloat16
F32 = jnp.float32
NT_DIMS = (((1,), (1,)), ((), ()))


def _tile_bytes(shape, dtype):
    itemsize = jnp.dtype(dtype).itemsize
    sublanes = SUBLANES * (4 // itemsize)
    shape = (1,) * (2 - len(shape)) + tuple(shape)
    rows = -(-shape[-2] // sublanes) * sublanes
    cols = -(-shape[-1] // LANES) * LANES
    return math.prod(shape[:-2]) * rows * cols * itemsize


def _vmem_limit(pipelined, resident, temporaries):
    return (2 * sum(_tile_bytes(*b) for b in pipelined) + sum(_tile_bytes(*b) for b in resident)
            + sum(_tile_bytes(*b) for b in temporaries))


def _layer_norm(z, g, b):
    mu = jnp.mean(z, axis=-1, keepdims=True)
    zc = z - mu
    var = jnp.mean(zc * zc, axis=-1, keepdims=True)
    return zc * lax.rsqrt(var + LN_EPS) * g + b


def _silu(a):
    return a * jax.nn.sigmoid(a)


def _ada_kernel(c_ref, w_ref, b_ref, o_ref):
    a = _silu(c_ref[...])
    o_ref[...] = jnp.dot(a, w_ref[...], preferred_element_type=F32) + b_ref[...]


def _ada_mod(c, w_ada, b_ada):
    bsz, d = c.shape
    n = w_ada.shape[1]
    rows = -(-bsz // SUBLANES) * SUBLANES
    c_pad = jnp.pad(c, ((0, rows - bsz), (0, 0)))
    out = pl.pallas_call(
        _ada_kernel,
        grid=(n // ADA_COLS,),
        in_specs=[
            pl.BlockSpec((rows, d), lambda j: (0, 0)),
            pl.BlockSpec((d, ADA_COLS), lambda j: (0, j)),
            pl.BlockSpec((1, ADA_COLS), lambda j: (0, j)),
        ],
        out_specs=pl.BlockSpec((rows, ADA_COLS), lambda j: (0, j)),
        out_shape=jax.ShapeDtypeStruct((rows, n), F32),
        compiler_params=pltpu.CompilerParams(
            dimension_semantics=("arbitrary",),
            vmem_limit_bytes=_vmem_limit([((rows, d), F32), ((d, ADA_COLS), F32), ((1, ADA_COLS), F32),
                                          ((rows, ADA_COLS), F32)], [], [((rows, ADA_COLS), F32)])),
        name="ada_mod",
    )(c_pad, w_ada, b_ada.reshape(1, n))
    return out[:bsz]


def _ffn_kernel(x_ref, mod_ref, lng_ref, lnb_ref, wg_ref, wu_ref, wd_ref, o_ref, h_ref, *, alpha, mod_row, ln_row):
    x = x_ref[...]
    sh = mod_ref[0, mod_row:mod_row + 1, :]
    sc = mod_ref[0, mod_row + 1:mod_row + 2, :]
    gate = mod_ref[0, mod_row + 2:mod_row + 3, :]
    u = (x * (1.0 + sc) + sh).astype(BF16)
    d_ff = wg_ref.shape[1]
    for c in range(d_ff // FFN_COLS):
        sl = slice(c * FFN_COLS, (c + 1) * FFN_COLS)
        a = jnp.dot(u, wg_ref[:, sl], preferred_element_type=F32)
        b = jnp.dot(u, wu_ref[:, sl], preferred_element_type=F32)
        h_ref[:, sl] = (_silu(a) * b).astype(BF16)
    y = jnp.dot(h_ref[...], wd_ref[...], preferred_element_type=F32)
    z = alpha * x + (0.5 * gate) * y
    o_ref[...] = _layer_norm(z, lng_ref[ln_row:ln_row + 1, :], lnb_ref[ln_row:ln_row + 1, :])


def _ffn(x2d, mod, ln_g, ln_b, wg, wu, wd, *, seq, alpha, mod_row, ln_row):
    n, d = x2d.shape
    d_ff = wg.shape[1]
    tiles_per_seq = seq // FFN_ROWS
    resident = dict(pipeline_mode=pl.Buffered(1))
    return pl.pallas_call(
        functools.partial(_ffn_kernel, alpha=alpha, mod_row=mod_row, ln_row=ln_row),
        grid=(n // FFN_ROWS,),
        in_specs=[
            pl.BlockSpec((FFN_ROWS, d), lambda i: (i, 0)),
            pl.BlockSpec((1, N_ADA, d), lambda i: (i // tiles_per_seq, 0, 0)),
            pl.BlockSpec(ln_g.shape, lambda i: (0, 0)),
            pl.BlockSpec(ln_b.shape, lambda i: (0, 0)),
            pl.BlockSpec((d, d_ff), lambda i: (0, 0), **resident),
            pl.BlockSpec((d, d_ff), lambda i: (0, 0), **resident),
            pl.BlockSpec((d_ff, d), lambda i: (0, 0), **resident),
        ],
        out_specs=pl.BlockSpec((FFN_ROWS, d), lambda i: (i, 0)),
        out_shape=jax.ShapeDtypeStruct((n, d), F32),
        scratch_shapes=[pltpu.VMEM((FFN_ROWS, d_ff), BF16)],
        compiler_params=pltpu.CompilerParams(
            dimension_semantics=("arbitrary",),
            vmem_limit_bytes=_vmem_limit(
                [((FFN_ROWS, d), F32), ((N_ADA, d), F32), (ln_g.shape, F32), (ln_b.shape, F32), ((FFN_ROWS, d), F32)],
                [((d, d_ff), BF16), ((d, d_ff), BF16), ((d_ff, d), BF16), ((FFN_ROWS, d_ff), BF16)],
                [((FFN_ROWS, d), BF16), ((FFN_ROWS, FFN_COLS), F32), ((FFN_ROWS, FFN_COLS), F32),
                 ((FFN_ROWS, d), F32), ((FFN_ROWS, d), F32)])),
        name="ffn",
    )(x2d, mod, ln_g, ln_b, wg, wu, wd)


def _proj_kernel(x_ref, mod_ref, wqk_ref, wvw_ref, wki_ref,
                 q_ref, k_ref, qi_ref, vt_ref, kia_ref, kib_ref, wit_ref, *, mod_row):
    x = x_ref[...]
    sh = mod_ref[0, mod_row:mod_row + 1, :]
    sc = mod_ref[0, mod_row + 1:mod_row + 2, :]
    u = (x * (1.0 + sc) + sh).astype(BF16)
    qkq = jnp.dot(u, wqk_ref[...], preferred_element_type=F32)
    q_ref[...] = (qkq[:, :ATTN_DIM] * (HEAD_DIM ** -0.5 * LOG2E)).astype(BF16)
    k_ref[...] = qkq[:, ATTN_DIM:2 * ATTN_DIM].astype(BF16)
    qi_ref[...] = qkq[:, 2 * ATTN_DIM:].astype(BF16)
    kk = jnp.dot(u, wki_ref[...], preferred_element_type=F32)
    kia_ref[...] = kk[:, :LANES].astype(BF16)
    kib_ref[...] = kk[:, LANES:].astype(BF16)
    vw = lax.dot_general(wvw_ref[...], u, NT_DIMS, preferred_element_type=F32)
    vt = vw[:ATTN_DIM].astype(BF16)
    ones_rows = jnp.where(lax.broadcasted_iota(jnp.int32, (HEAD_PAD, ATT_BLOCK), 0) == 0, 1.0, 0.0).astype(BF16)
    for c in range(vt_ref.shape[1]):
        for h in range(N_HEADS):
            vt_ref[0, c, h * HEAD_ROWS:h * HEAD_ROWS + HEAD_DIM, :] = \
                vt[h * HEAD_DIM:(h + 1) * HEAD_DIM, c * ATT_BLOCK:(c + 1) * ATT_BLOCK]
            vt_ref[0, c, h * HEAD_ROWS + HEAD_DIM:(h + 1) * HEAD_ROWS, :] = ones_rows
    wit_ref[0] = vw[ATTN_DIM:ATTN_DIM + IDX_HEADS, :]


def _attn_proj(x2d, mod, wqk, wvw_t, wki, *, bsz, seq, mod_row):
    n, d = x2d.shape
    tiles_per_seq = seq // PROJ_ROWS
    chunks_per_tile = PROJ_ROWS // ATT_BLOCK
    resident = dict(pipeline_mode=pl.Buffered(1))
    row_spec = lambda cols: pl.BlockSpec((PROJ_ROWS, cols), lambda i: (i, 0))
    return pl.pallas_call(
        functools.partial(_proj_kernel, mod_row=mod_row),
        grid=(n // PROJ_ROWS,),
        in_specs=[
            pl.BlockSpec((PROJ_ROWS, d), lambda i: (i, 0)),
            pl.BlockSpec((1, N_ADA, d), lambda i: (i // tiles_per_seq, 0, 0)),
            pl.BlockSpec(wqk.shape, lambda i: (0, 0), **resident),
            pl.BlockSpec(wvw_t.shape, lambda i: (0, 0), **resident),
            pl.BlockSpec(wki.shape, lambda i: (0, 0), **resident),
        ],
        out_specs=[
            row_spec(ATTN_DIM), row_spec(ATTN_DIM), row_spec(IDX_HEADS * IDX_DIM),
            pl.BlockSpec((1, chunks_per_tile, N_HEADS * HEAD_ROWS, ATT_BLOCK),
                         lambda i: (i // tiles_per_seq, i % tiles_per_seq, 0, 0)),
            row_spec(LANES), row_spec(LANES),
            pl.BlockSpec((1, IDX_HEADS, PROJ_ROWS), lambda i: (i // tiles_per_seq, 0, i % tiles_per_seq)),
        ],
        out_shape=[
            jax.ShapeDtypeStruct((n, ATTN_DIM), BF16),
            jax.ShapeDtypeStruct((n, ATTN_DIM), BF16),
            jax.ShapeDtypeStruct((n, IDX_HEADS * IDX_DIM), BF16),
            jax.ShapeDtypeStruct((bsz, seq // ATT_BLOCK, N_HEADS * HEAD_ROWS, ATT_BLOCK), BF16),
            jax.ShapeDtypeStruct((n, LANES), BF16),
            jax.ShapeDtypeStruct((n, LANES), BF16),
            jax.ShapeDtypeStruct((bsz, IDX_HEADS, seq), F32),
        ],
        compiler_params=pltpu.CompilerParams(
            dimension_semantics=("arbitrary",),
            vmem_limit_bytes=_vmem_limit(
                [((PROJ_ROWS, d), F32), ((N_ADA, d), F32), ((PROJ_ROWS, ATTN_DIM), BF16), ((PROJ_ROWS, ATTN_DIM), BF16),
                 ((PROJ_ROWS, IDX_HEADS * IDX_DIM), BF16), ((chunks_per_tile, N_HEADS * HEAD_ROWS, ATT_BLOCK), BF16),
                 ((PROJ_ROWS, LANES), BF16), ((PROJ_ROWS, LANES), BF16), ((IDX_HEADS, PROJ_ROWS), F32)],
                [(wqk.shape, BF16), (wvw_t.shape, BF16), (wki.shape, BF16)],
                [((PROJ_ROWS, d), BF16), ((PROJ_ROWS, wqk.shape[1]), F32), ((PROJ_ROWS, wki.shape[1]), F32),
                 ((wvw_t.shape[0], PROJ_ROWS), F32)])),
        name="attn_proj",
    )(x2d, mod, wqk, wvw_t, wki)


def _t5_bucket(n):
    max_exact = REL_BUCKETS // 2
    nf = jnp.maximum(n, 1).astype(F32)
    large = max_exact + jnp.floor(jnp.log(nf / max_exact) / math.log(REL_MAX_DIST / max_exact)
                                  * (REL_BUCKETS - max_exact)).astype(jnp.int32)
    large = jnp.minimum(large, REL_BUCKETS - 1)
    return jnp.where(n < max_exact, n, large)


def _far_bucket(first_dist, last_dist):
    n = np.arange(first_dist, last_dist + 1, dtype=np.float32)
    max_exact = REL_BUCKETS // 2
    large = max_exact + (np.log(n / np.float32(max_exact)) / np.float32(math.log(REL_MAX_DIST / max_exact))
                         * np.float32(REL_BUCKETS - max_exact)).astype(np.int32)
    buckets = np.where(n < max_exact, n.astype(np.int32), np.minimum(large, REL_BUCKETS - 1))
    assert buckets.min() == buckets.max(), "key chunks two or more blocks away must share one bias bucket"
    return int(buckets[0])


def _bias_kernel(rb_ref, o_ref, *, far_bucket):
    o = pl.program_id(0)
    row = lax.broadcasted_iota(jnp.int32, (ATT_BLOCK, ATT_BLOCK), 0)
    col = lax.broadcasted_iota(jnp.int32, (ATT_BLOCK, ATT_BLOCK), 1)
    dist = o * ATT_BLOCK + col - row
    bucket = _t5_bucket(jnp.maximum(dist, 0))
    for h in range(N_HEADS):
        tile = jnp.zeros((ATT_BLOCK, ATT_BLOCK), F32)
        for b in range(REL_BUCKETS):
            tile = jnp.where(bucket == b, rb_ref[b, h], tile)
        o_ref[0, h] = (tile - rb_ref[far_bucket, h]) * LOG2E


def _rel_bias_tiles(rel_bias, far_bucket):
    return pl.pallas_call(
        functools.partial(_bias_kernel, far_bucket=far_bucket),
        grid=(2,),
        in_specs=[pl.BlockSpec(memory_space=pltpu.SMEM)],
        out_specs=pl.BlockSpec((1, N_HEADS, ATT_BLOCK, ATT_BLOCK), lambda o: (o, 0, 0, 0)),
        out_shape=jax.ShapeDtypeStruct((2, N_HEADS, ATT_BLOCK, ATT_BLOCK), F32),
        compiler_params=pltpu.CompilerParams(
            dimension_semantics=("arbitrary",),
            vmem_limit_bytes=_vmem_limit([((N_HEADS, ATT_BLOCK, ATT_BLOCK), F32)], [],
                                         [((ATT_BLOCK, ATT_BLOCK), F32)] * 3)),
        name="rel_bias",
    )(rel_bias)


def _attn_kernel(q_ref, qi_ref, wit_ref, k_ref, vt_ref, kia_ref, kib_ref, bt_ref, o_ref,
                 sc_ref, zc_ref, ranked_ref, qh_ref, lg_ref, p_ref, pp_ref, st_ref, acc_ref, *, topk, seq):
    blk = ATT_BLOCK
    groups = blk // SUBLANES
    i = pl.program_id(1)
    nch = i + 1
    kf = float(topk)
    inf = jnp.inf

    def chunk_start(j):
        return j * blk if isinstance(j, int) else pl.multiple_of(j * blk, blk)

    def as_groups(x):
        return x.reshape(groups, SUBLANES, blk)

    def lanes8(v):
        return jnp.broadcast_to(v, (SUBLANES, blk))

    def colmin(x8):
        return jnp.min(x8, axis=0, keepdims=True)

    def colmax(x8):
        return jnp.max(x8, axis=0, keepdims=True)

    def colsum(x8):
        return jnp.sum(x8, axis=0, keepdims=True)

    wf = wit_ref[0] * (IDX_DIM ** -0.5)
    qi = qi_ref[0]
    row = lax.broadcasted_iota(jnp.int32, (blk, blk), 0)
    col = lax.broadcasted_iota(jnp.int32, (blk, blk), 1)
    causal = row <= col

    def chunk_scores(j):
        r0 = chunk_start(j)
        ka = kia_ref[0, pl.ds(r0, blk), :]
        kb = kib_ref[0, pl.ds(r0, blk), :]
        s = jnp.zeros((blk, blk), F32)
        for hp in range(IDX_HEADS // 2):
            qp = qi[:, hp * LANES:(hp + 1) * LANES]
            a0 = lax.dot_general(ka, qp, NT_DIMS, preferred_element_type=F32)
            a1 = lax.dot_general(kb, qp, NT_DIMS, preferred_element_type=F32)
            s = s + jnp.maximum(a0, 0.0) * wf[2 * hp:2 * hp + 1, :]
            s = s + jnp.maximum(a1, 0.0) * wf[2 * hp + 1:2 * hp + 2, :]
        return r0, s * (IDX_HEADS ** -0.5)

    def score_stats(j, s_lo, s_hi, stats):
        mn8, mx8, pos8, zer8 = stats
        hi3 = as_groups(s_hi)
        zer8 = zer8 + jnp.sum(jnp.where(hi3 == 0.0, 1.0, 0.0), axis=0)
        zc_ref[j] = zer8
        return (jnp.minimum(mn8, jnp.min(as_groups(s_lo), axis=0)),
                jnp.maximum(mx8, jnp.max(hi3, axis=0)),
                pos8 + jnp.sum(jnp.where(hi3 > 0.0, 1.0, 0.0), axis=0),
                zer8)

    def score_body(j, stats):
        r0, s = chunk_scores(j)
        sc_ref[pl.ds(r0, blk), :] = s
        return score_stats(j, s, s, stats)

    def score_run(first, count, stats):
        for c in range(count):
            stats = score_body(first + c, stats)
        return stats

    def score_diag(st):
        r_diag, s_diag = chunk_scores(i)
        s_diag_hi = jnp.where(causal, s_diag, -inf)
        sc_ref[pl.ds(r_diag, blk), :] = s_diag_hi
        return score_stats(i, jnp.where(causal, s_diag, inf), s_diag_hi, st)

    zeros8 = jnp.zeros((SUBLANES, blk), F32)
    stats = (jnp.full((SUBLANES, blk), inf, F32), jnp.full((SUBLANES, blk), -inf, F32), zeros8, zeros8)
    stats = lax.fori_loop(0, i // SCORE_RUN, lambda jq, st: score_run(SCORE_RUN * jq, SCORE_RUN, st), stats)
    left = i % SCORE_RUN
    mn8, mx8, pos8, zer8 = lax.switch(
        left, [functools.partial(lambda r, st: score_diag(score_run(i - r, r, st)), r) for r in range(SCORE_RUN)], stats)

    @pl.when(nch % 2 == 1)
    def _():
        sc_ref[pl.ds(chunk_start(nch), blk), :] = jnp.full((blk, blk), -inf, F32)

    npair = (nch + 1) // 2

    def count_gt(thr):
        t8 = lanes8(thr)

        def body(jp, accs):
            base = pl.multiple_of(jp * 2 * blk, 2 * blk)
            accs = list(accs)
            for s in range(2 * blk // COUNT_ROWS):
                x = sc_ref[pl.ds(base + s * COUNT_ROWS, COUNT_ROWS), :]
                for g in range(COUNT_ROWS // SUBLANES):
                    hit = jnp.where(x[g * SUBLANES:(g + 1) * SUBLANES] > t8, 1.0, 0.0)
                    accs[g % COUNT_CHAINS] = accs[g % COUNT_CHAINS] + hit
            return tuple(accs)

        accs = lax.fori_loop(0, npair, body, tuple(jnp.zeros((SUBLANES, blk), F32) for _ in range(COUNT_CHAINS)))
        return colsum(sum(accs[1:], accs[0]))

    def bisect(_, st):
        lo, hi, low, done = st
        mid = 0.5 * lo + 0.5 * hi
        c = count_gt(mid)
        live = done < 0.5
        up = jnp.logical_and(live, c >= kf)
        down = jnp.logical_and(live, c < kf)
        low = jnp.where(up, mid, low)
        lo = jnp.where(up, mid, lo)
        hi = jnp.where(down, mid, hi)
        done = jnp.where(c == kf, 1.0, done)
        return lo, hi, low, done

    tpos = i * blk + lax.broadcasted_iota(jnp.int32, (1, blk), 1)
    pos, zer = colsum(pos8), colsum(zer8)
    mn, mx = colmin(mn8), colmax(mx8)
    zero_tie = jnp.logical_and(pos < kf, pos + zer >= kf)
    pos_ge = pos >= kf
    done0 = jnp.where(jnp.logical_or(jnp.logical_or(tpos + 1 <= topk, zero_tie), pos == kf), 1.0, 0.0)
    low0 = jnp.where(jnp.logical_or(zero_tie, pos_ge), 0.0, -inf)
    lo0 = jnp.where(pos_ge, jnp.maximum(mn, 0.0), mn)
    hi0 = jnp.where(pos_ge, mx, jnp.minimum(mx, 0.0))
    tie0 = jnp.where(zero_tie, 0.0, inf)
    need0 = jnp.where(zero_tie, kf - pos, 0.0)

    def next_value_above(thr):
        t8 = lanes8(thr)

        def body(jp, accs):
            base = pl.multiple_of(jp * 2 * blk, 2 * blk)
            accs = list(accs)
            for s in range(2 * blk // COUNT_ROWS):
                x = sc_ref[pl.ds(base + s * COUNT_ROWS, COUNT_ROWS), :]
                for g in range(COUNT_ROWS // SUBLANES):
                    xg = x[g * SUBLANES:(g + 1) * SUBLANES]
                    accs[g % COUNT_CHAINS] = jnp.minimum(accs[g % COUNT_CHAINS], jnp.where(xg > t8, xg, inf))
            return tuple(accs)

        accs = lax.fori_loop(0, npair, body, tuple(jnp.full((SUBLANES, blk), inf, F32) for _ in range(COUNT_CHAINS)))
        return colmin(functools.reduce(jnp.minimum, accs))

    def climb_cond(st):
        return jnp.min(st[1]) < 0.5

    def climb_body(st):
        low, done, tie, need = st
        live = done < 0.5
        cand = next_value_above(low)
        cgt = count_gt(cand)
        found = jnp.logical_and(live, cgt < kf)
        tie = jnp.where(found, cand, tie)
        need = jnp.where(found, kf - cgt, need)
        low = jnp.where(live, cand, low)
        done = jnp.where(jnp.logical_and(live, cgt <= kf), 1.0, done)
        return low, done, tie, need

    def select(_):
        _, _, low, done = lax.fori_loop(0, BISECT_STEPS, bisect, (lo0, hi0, low0, done0))
        low, _, tie, need = lax.while_loop(climb_cond, climb_body, (low, done, tie0, need0))
        return low, tie, need

    low, tie, need = lax.cond(jnp.min(done0) < 0.5, select, lambda _: (low0, tie0, need0), 0)

    def plain_mask(_):
        low8 = lanes8(low)

        def body(j, _):
            r0 = chunk_start(j)
            x = as_groups(sc_ref[pl.ds(r0, blk), :])
            sc_ref[pl.ds(r0, blk), :] = jnp.where(x > low8[None], 0.0, -inf).reshape(blk, blk)
            return 0

        return lax.fori_loop(0, nch, body, 0)

    def ranked_chunk(j, tri, base):
        r0 = chunk_start(j)
        x = sc_ref[pl.ds(r0, blk), :]
        eq = x == tie
        rank = jnp.dot(tri, jnp.where(eq, 1.0, 0.0).astype(BF16), preferred_element_type=F32)
        sel = jnp.logical_or(x > low, jnp.logical_and(eq, rank + base <= need))
        sc_ref[pl.ds(r0, blk), :] = jnp.where(sel, 0.0, -inf)
        return rank[blk - 1:blk, :]

    def tie_mask(_):
        tri = jnp.where(row >= col, 1.0, 0.0).astype(BF16)
        lax.fori_loop(0, nch, lambda j, base: base + ranked_chunk(j, tri, base), jnp.zeros((1, blk), F32))
        return 0

    def zero_tie_mask(_):
        tri = jnp.where(row >= col, 1.0, 0.0).astype(BF16)

        def cut_body(j, ncut):
            return ncut + jnp.where(colsum(zc_ref[j]) < need, 1.0, 0.0)

        ncut = lax.fori_loop(0, nch, cut_body, jnp.zeros((1, blk), F32))
        cut = jnp.where(tie < inf, ncut, inf)

        def flag_body(j, _):
            ranked_ref[j] = 0
            return 0

        lax.fori_loop(0, nch, flag_body, 0)

        def cut_cond(pending):
            return jnp.min(pending) < inf

        def cut_chunk(pending):
            jf = jnp.min(pending)
            j = jf.astype(jnp.int32)
            base = jnp.where(j > 0, colsum(zc_ref[jnp.maximum(j - 1, 0)]), 0.0)
            ranked_chunk(j, tri, base)
            ranked_ref[j] = 1
            return jnp.where(pending == jf, inf, pending)

        lax.while_loop(cut_cond, cut_chunk, cut)
        low8, cut8 = lanes8(low), lanes8(cut)

        def body(j, _):
            @pl.when(ranked_ref[j] == 0)
            def _():
                r0 = chunk_start(j)
                x = as_groups(sc_ref[pl.ds(r0, blk), :])
                at_low = jnp.where(jnp.logical_and(cut8 < inf, cut8 > j.astype(F32)), 0.0, -inf)
                out = jnp.where(x > low8[None], 0.0, jnp.where(x == low8[None], at_low[None], -inf))
                sc_ref[pl.ds(r0, blk), :] = out.reshape(blk, blk)
            return 0

        return lax.fori_loop(0, nch, body, 0)

    tied = tie < inf
    mask_kind = jnp.where(jnp.max(jnp.where(tied, 1.0, 0.0)) < 0.5, 0,
                          jnp.where(jnp.max(jnp.where(jnp.logical_and(tied, tie != 0.0), 1.0, 0.0)) < 0.5, 1, 2))
    lax.switch(mask_kind, [plain_mask, zero_tie_mask, tie_mask], 0)

    q = q_ref[0]
    lane = lax.broadcasted_iota(jnp.int32, (blk, LANES), 1)
    for h in range(N_HEADS):
        hp = h // 2
        in_head = (lane < HEAD_DIM) if h % 2 == 0 else (lane >= HEAD_DIM)
        qh_ref[h] = jnp.where(in_head, q[:, hp * LANES:(hp + 1) * LANES], jnp.zeros((), BF16))

    def logits(j, h, mb, near):
        hp = h // 2
        kc = k_ref[0, pl.ds(chunk_start(j), blk), hp * LANES:(hp + 1) * LANES]
        lg = lax.dot_general(kc, qh_ref[h], NT_DIMS, preferred_element_type=F32) + mb
        return lg if near is None else lg + bt_ref[near, h]

    CMAX, RMAX, RESC = 0, 1, 2

    def stage_logits(j, near):
        mb = sc_ref[pl.ds(chunk_start(j), blk), :]
        cm8 = []
        for h in range(N_HEADS):
            lg = logits(j, h, mb, near)
            lg_ref[h] = lg
            cm8.append(jnp.max(as_groups(lg), axis=0))
        for h in range(N_HEADS):
            st_ref[CMAX, h] = lanes8(colmax(cm8[h]))

    def stage_exp():
        for h in range(N_HEADS):
            m_old = st_ref[RMAX, h]
            m_new = jnp.maximum(m_old, st_ref[CMAX, h])
            m_use = jnp.where(m_new == -inf, 0.0, m_new)
            p = jnp.exp2(as_groups(lg_ref[h]) - m_use[None])
            p_ref[h] = p.reshape(blk, blk).astype(BF16)
            st_ref[RMAX, h] = m_new
            st_ref[RESC, h] = jnp.exp2(m_old - m_use)

    def stage_pv(j):
        for h in range(N_HEADS):
            rows = slice(h * HEAD_ROWS, (h + 1) * HEAD_ROWS)
            pv = jnp.dot(vt_ref[0, j, rows, :], p_ref[h], preferred_element_type=F32)
            acc = acc_ref[rows, :].reshape(HEAD_ROWS // SUBLANES, SUBLANES, blk) * st_ref[RESC, h][None]
            acc_ref[rows, :] = acc.reshape(HEAD_ROWS, blk) + pv

    def pipelined(j, near_next):
        stage_exp()
        stage_logits(j + 1, near_next)
        stage_pv(j)

    def exact_attention():
        acc_ref[...] = jnp.zeros(acc_ref.shape, F32)
        st_ref[RMAX] = jnp.full((N_HEADS, SUBLANES, blk), -inf, F32)

        def last_chunks(first):
            for j in range(first, 0):
                pipelined(i + j, -(j + 1))
            stage_exp()
            stage_pv(i)

        @pl.when(i == 0)
        def _():
            stage_logits(0, 0)
            last_chunks(0)

        @pl.when(i == 1)
        def _():
            stage_logits(0, 1)
            last_chunks(-1)

        @pl.when(i >= 2)
        def _():
            stage_logits(0, None)

        def far_body(j, _):
            pipelined(j, None)
            return 0

        lax.fori_loop(0, jnp.maximum(i - 2, 0), far_body, 0)

        @pl.when(i >= 2)
        def _():
            last_chunks(-2)

    def first_chunk(near):
        stage_logits(0, near)
        for h in range(N_HEADS):
            cmax = st_ref[CMAX, h]
            shift = jnp.where(cmax == -inf, 0.0, cmax)
            st_ref[RMAX, h] = shift
            p = jnp.exp2(as_groups(lg_ref[h]) - shift[None])
            p_ref[h] = p.reshape(blk, blk).astype(BF16)
        for h in range(N_HEADS):
            rows = slice(h * HEAD_ROWS, (h + 1) * HEAD_ROWS)
            acc_ref[rows, :] = jnp.dot(vt_ref[0, 0, rows, :], p_ref[h], preferred_element_type=F32)

    def fast_run(chunks):
        for c, (j, near) in enumerate(chunks):
            mb = sc_ref[pl.ds(chunk_start(j), blk), :]
            for h in range(N_HEADS):
                p = jnp.exp2(as_groups(logits(j, h, mb, near)) - st_ref[RMAX, h][None])
                pp_ref[c, h] = p.reshape(blk, blk).astype(BF16)
        for h in range(N_HEADS):
            rows = slice(h * HEAD_ROWS, (h + 1) * HEAD_ROWS)
            tot = acc_ref[rows, :]
            for c, (j, _) in enumerate(chunks):
                tot = tot + jnp.dot(vt_ref[0, j, rows, :], pp_ref[c, h], preferred_element_type=F32)
            up = lanes8(jnp.maximum(jnp.floor(jnp.log2(tot[HEAD_DIM:HEAD_DIM + 1, :])), 0.0))
            scaled = tot.reshape(HEAD_ROWS // SUBLANES, SUBLANES, blk) * jnp.exp2(-up)[None]
            acc_ref[rows, :] = scaled.reshape(HEAD_ROWS, blk)
            st_ref[RMAX, h] = st_ref[RMAX, h] + up

    @pl.when(i == 0)
    def _():
        first_chunk(0)

    @pl.when(i == 1)
    def _():
        first_chunk(1)
        fast_run([(1, 0)])

    @pl.when(i >= 2)
    def _():
        first_chunk(None)

    def far_pair(jp, _):
        fast_run([(2 * jp + 1, None), (2 * jp + 2, None)])
        return 0

    lax.fori_loop(0, jnp.maximum(i - 2, 0) // 2, far_pair, 0)

    @pl.when(jnp.logical_and(i >= 2, i % 2 == 0))
    def _():
        fast_run([(i - 1, 1), (i, 0)])

    @pl.when(jnp.logical_and(i >= 2, i % 2 == 1))
    def _():
        fast_run([(i - 2, None), (i - 1, 1)])
        fast_run([(i, 0)])

    acc = acc_ref[...]
    finite = jnp.min(jnp.where(jnp.isfinite(acc), 1.0, 0.0))
    dens = jnp.concatenate([acc[h * HEAD_ROWS + HEAD_DIM:h * HEAD_ROWS + HEAD_DIM + 1, :] for h in range(N_HEADS)], axis=0)
    usable = jnp.logical_and(finite > 0.5, jnp.min(dens) > 0.0)
    lax.cond(usable, lambda: None, exact_attention)

    heads = []
    for h in range(N_HEADS):
        r0 = h * HEAD_ROWS
        heads.append(acc_ref[r0:r0 + HEAD_DIM, :] / acc_ref[r0 + HEAD_DIM:r0 + HEAD_DIM + 1, :])
    o_ref[0] = jnp.concatenate(heads, axis=0).T.astype(BF16)


def _attention(q, qi, wit, k, vt, kia, kib, btiles, *, bsz, seq, topk):
    blk = ATT_BLOCK
    nblk = seq // blk
    return pl.pallas_call(
        functools.partial(_attn_kernel, topk=topk, seq=seq),
        grid=(bsz, nblk),
        in_specs=[
            pl.BlockSpec((1, blk, ATTN_DIM), lambda b, i: (b, i, 0)),
            pl.BlockSpec((1, blk, IDX_HEADS * IDX_DIM), lambda b, i: (b, i, 0)),
            pl.BlockSpec((1, IDX_HEADS, blk), lambda b, i: (b, 0, i)),
            pl.BlockSpec((1, seq, ATTN_DIM), lambda b, i: (b, 0, 0)),
            pl.BlockSpec((1, nblk, N_HEADS * HEAD_ROWS, blk), lambda b, i: (b, 0, 0, 0)),
            pl.BlockSpec((1, seq, LANES), lambda b, i: (b, 0, 0)),
            pl.BlockSpec((1, seq, LANES), lambda b, i: (b, 0, 0)),
            pl.BlockSpec(btiles.shape, lambda b, i: (0, 0, 0, 0)),
        ],
        out_specs=pl.BlockSpec((1, blk, ATTN_DIM), lambda b, i: (b, i, 0)),
        out_shape=jax.ShapeDtypeStruct((bsz, seq, ATTN_DIM), BF16),
        scratch_shapes=[
            pltpu.VMEM((seq + blk, blk), F32),
            pltpu.VMEM((nblk, SUBLANES, blk), F32),
            pltpu.SMEM((nblk,), jnp.int32),
            pltpu.VMEM((N_HEADS, blk, LANES), BF16),
            pltpu.VMEM((N_HEADS, blk, blk), F32),
            pltpu.VMEM((N_HEADS, blk, blk), BF16),
            pltpu.VMEM((2, N_HEADS, blk, blk), BF16),
            pltpu.VMEM((3, N_HEADS, SUBLANES, blk), F32),
            pltpu.VMEM((N_HEADS * HEAD_ROWS, blk), F32),
        ],
        compiler_params=pltpu.CompilerParams(
            dimension_semantics=("arbitrary", "arbitrary"),
            vmem_limit_bytes=_vmem_limit(
                [((blk, ATTN_DIM), BF16), ((blk, IDX_HEADS * IDX_DIM), BF16), ((IDX_HEADS, blk), F32),
                 ((seq, ATTN_DIM), BF16), ((nblk, N_HEADS * HEAD_ROWS, blk), BF16), ((seq, LANES), BF16),
                 ((seq, LANES), BF16), (btiles.shape, F32), ((blk, ATTN_DIM), BF16)],
                [((seq + blk, blk), F32), ((nblk, SUBLANES, blk), F32), ((N_HEADS, blk, LANES), BF16),
                 ((N_HEADS, blk, blk), F32), ((N_HEADS, blk, blk), BF16), ((2, N_HEADS, blk, blk), BF16),
                 ((3, N_HEADS, SUBLANES, blk), F32),
                 ((N_HEADS * HEAD_ROWS, blk), F32)],
                [((blk, blk), F32)] * 4 + [((blk, blk), BF16), ((ATTN_DIM, blk), F32), ((blk, ATTN_DIM), F32)])),
        name="attention",
    )(q, qi, wit, k, vt, kia, kib, btiles)


def _mix_kernel(x_ref, mod_ref, att_ref, wpg_ref, wpool_ref, ps_ref, wa_ref, wb_ref, wo_ref,
                lng_ref, lnb_ref, o_ref, pe_ref, mix_ref, *, alpha, mod_row, ln_row, tiles_per_seq):
    rows = x_ref.shape[0]
    i = pl.program_id(0)
    seq_tile = i % tiles_per_seq

    @pl.when(i == 0)
    def _():
        pe_ref[rows:, :] = jnp.zeros((POOL_HALO, POOL_DIM), F32)

    x = x_ref[...]
    sh = mod_ref[0, mod_row:mod_row + 1, :]
    sc = mod_ref[0, mod_row + 1:mod_row + 2, :]
    gate = mod_ref[0, mod_row + 2:mod_row + 3, :]
    u = (x * (1.0 + sc) + sh).astype(BF16)
    pg = jnp.dot(u, wpg_ref[...], preferred_element_type=F32)
    pe_ref[0:POOL_HALO, :] = jnp.where(seq_tile == 0, 0.0, pe_ref[rows:, :])
    pe_ref[POOL_HALO:, :] = pg[:, :POOL_DIM]
    t = seq_tile * rows + lax.broadcasted_iota(jnp.int32, (rows, 1), 0)
    for g, w in enumerate(POOL_WINDOWS):
        cols = slice(g * POOL_GROUP_DIM, (g + 1) * POOL_GROUP_DIM)
        cur = pe_ref[POOL_HALO:, cols]
        win = cur
        for back in range(1, w):
            win = win + pe_ref[POOL_HALO - back:POOL_HALO - back + rows, cols]
        cnt = jnp.minimum(t + 1, w).astype(F32)
        pooled = (win / cnt - cur).astype(BF16)
        mixed = jnp.dot(pooled, wpool_ref[g], preferred_element_type=F32)
        mix_ref[:, cols] = (mixed * ps_ref[:, cols]).astype(BF16)
    y_a = jnp.dot(mix_ref[...], wa_ref[...], preferred_element_type=F32)
    y_b = jnp.dot(att_ref[...], wb_ref[...], preferred_element_type=F32)
    d = x.shape[1]
    ga = pg[:, POOL_DIM:POOL_DIM + d]
    gb = pg[:, POOL_DIM + d:]
    merged = (jax.nn.sigmoid(ga) * y_a + jax.nn.sigmoid(gb) * y_b).astype(BF16)
    y = jnp.dot(merged, wo_ref[...], preferred_element_type=F32)
    z = alpha * x + gate * y
    o_ref[...] = _layer_norm(z, lng_ref[ln_row:ln_row + 1, :], lnb_ref[ln_row:ln_row + 1, :])


def _mix_out(x2d, mod, att2d, wpg, wpool, pool_scale, wa, wb, wo, ln_g, ln_b, *, seq, alpha, mod_row, ln_row):
    n, d = x2d.shape
    tiles_per_seq = seq // MIX_ROWS
    resident = dict(pipeline_mode=pl.Buffered(1))
    full = lambda a: pl.BlockSpec(a.shape, lambda i: (0,) * a.ndim, **resident)
    return pl.pallas_call(
        functools.partial(_mix_kernel, alpha=alpha, mod_row=mod_row, ln_row=ln_row, tiles_per_seq=tiles_per_seq),
        grid=(n // MIX_ROWS,),
        in_specs=[
            pl.BlockSpec((MIX_ROWS, d), lambda i: (i, 0)),
            pl.BlockSpec((1, N_ADA, d), lambda i: (i // tiles_per_seq, 0, 0)),
            pl.BlockSpec((MIX_ROWS, ATTN_DIM), lambda i: (i, 0)),
            full(wpg), full(wpool), full(pool_scale), full(wa), full(wb), full(wo),
            pl.BlockSpec(ln_g.shape, lambda i: (0, 0)),
            pl.BlockSpec(ln_b.shape, lambda i: (0, 0)),
        ],
        out_specs=pl.BlockSpec((MIX_ROWS, d), lambda i: (i, 0)),
        out_shape=jax.ShapeDtypeStruct((n, d), F32),
        scratch_shapes=[
            pltpu.VMEM((POOL_HALO + MIX_ROWS, POOL_DIM), F32),
            pltpu.VMEM((MIX_ROWS, POOL_DIM), BF16),
        ],
        compiler_params=pltpu.CompilerParams(
            dimension_semantics=("arbitrary",),
            vmem_limit_bytes=_vmem_limit(
                [((MIX_ROWS, d), F32), ((N_ADA, d), F32), ((MIX_ROWS, ATTN_DIM), BF16), (ln_g.shape, F32),
                 (ln_b.shape, F32), ((MIX_ROWS, d), F32)],
                [(wpg.shape, BF16), (wpool.shape, BF16), (pool_scale.shape, F32), (wa.shape, BF16), (wb.shape, BF16),
                 (wo.shape, BF16), ((POOL_HALO + MIX_ROWS, POOL_DIM), F32), ((MIX_ROWS, POOL_DIM), BF16)],
                [((MIX_ROWS, d), BF16), ((MIX_ROWS, wpg.shape[1]), F32)] + [((MIX_ROWS, d), F32)] * 5)),
        name="mix_out",
    )(x2d, mod, att2d, wpg, wpool, pool_scale, wa, wb, wo, ln_g, ln_b)


def kernel(x, c, w_ada, b_ada, ln_g, ln_b, ffn1_w_gate, ffn1_w_up, ffn1_w_down, w_in, w_pool, pool_scale,
           w_a, w_b, w_out, rel_bias, ffn2_w_gate, ffn2_w_up, ffn2_w_down):
    bsz, seq, d = x.shape
    depth = w_ada.shape[0]
    alpha = (2.0 * depth) ** 0.25
    topk = min(TOP_K, seq // 4)
    assert seq % FFN_ROWS == 0 and seq % PROJ_ROWS == 0 and seq % MIX_ROWS == 0 and seq % ATT_BLOCK == 0
    assert PROJ_ROWS % ATT_BLOCK == 0 and POOL_HALO >= max(POOL_WINDOWS) - 1
    far_bucket = _far_bucket(ATT_BLOCK + 1, max(seq - 1, ATT_BLOCK + 1))

    o_q = POOL_DIM
    o_k = o_q + ATTN_DIM
    o_v = o_k + ATTN_DIM
    o_qi = o_v + ATTN_DIM
    o_ki = o_qi + IDX_HEADS * IDX_DIM
    o_wi = o_ki + IDX_DIM
    o_ga = o_wi + IDX_HEADS

    btiles = _rel_bias_tiles(rel_bias, far_bucket)
    x2d = x.reshape(bsz * seq, d)
    for l in range(depth):
        wl = w_in[l]
        zeros_ki = jnp.zeros((d, LANES - IDX_DIM), wl.dtype)
        w_ki = wl[:, o_ki:o_wi]
        wqk = jnp.concatenate([wl[:, o_q:o_v], wl[:, o_qi:o_ki]], axis=1).astype(BF16)
        wvw_t = jnp.pad(jnp.concatenate([wl[:, o_v:o_qi], wl[:, o_wi:o_ga]], axis=1).T,
                        ((0, 2 * SUBLANES - IDX_HEADS), (0, 0))).astype(BF16)
        wki = jnp.concatenate([w_ki, zeros_ki, zeros_ki, w_ki], axis=1).astype(BF16)
        wpg = jnp.concatenate([wl[:, :POOL_DIM], wl[:, o_ga:]], axis=1).astype(BF16)

        mod = _ada_mod(c, w_ada[l], b_ada[l]).reshape(bsz, N_ADA, d)
        x2d = _ffn(x2d, mod, ln_g[l], ln_b[l], ffn1_w_gate[l].astype(BF16), ffn1_w_up[l].astype(BF16),
                   ffn1_w_down[l].astype(BF16), seq=seq, alpha=alpha, mod_row=0, ln_row=0)
        q, k, qi, vt, kia, kib, wit = _attn_proj(x2d, mod, wqk, wvw_t, wki, bsz=bsz, seq=seq, mod_row=3)
        att = _attention(q.reshape(bsz, seq, ATTN_DIM), qi.reshape(bsz, seq, IDX_HEADS * IDX_DIM), wit,
                         k.reshape(bsz, seq, ATTN_DIM), vt, kia.reshape(bsz, seq, LANES),
                         kib.reshape(bsz, seq, LANES), btiles, bsz=bsz, seq=seq, topk=topk)
        x2d = _mix_out(x2d, mod, att.reshape(bsz * seq, ATTN_DIM), wpg, w_pool[l].astype(BF16),
                       pool_scale[l].reshape(1, POOL_DIM), w_a[l].astype(BF16), w_b[l].astype(BF16),
                       w_out[l].astype(BF16), ln_g[l], ln_b[l], seq=seq, alpha=alpha, mod_row=3, ln_row=1)
        x2d = _ffn(x2d, mod, ln_g[l], ln_b[l], ffn2_w_gate[l].astype(BF16), ffn2_w_up[l].astype(BF16),
                   ffn2_w_down[l].astype(BF16), seq=seq, alpha=alpha, mod_row=6, ln_row=2)
    return x2d.reshape(bsz, seq, d)
```

```python
import functools
import math

import numpy as np
import jax
import jax.numpy as jnp
from jax import lax
from jax.experimental import pallas as pl
from jax.experimental.pallas import tpu as pltpu

POOL_WINDOWS = (2, 4, 8, 16)
POOL_GROUP_DIM = 128
POOL_DIM = len(POOL_WINDOWS) * POOL_GROUP_DIM
N_HEADS = 8
HEAD_DIM = 64
ATTN_DIM = N_HEADS * HEAD_DIM
HEAD_PAD = 16
HEAD_ROWS = HEAD_DIM + HEAD_PAD
IDX_HEADS = 8
IDX_DIM = 64
TOP_K = 256
REL_BUCKETS = 32
REL_MAX_DIST = 128
N_ADA = 9
LN_EPS = 1e-5
POOL_HALO = 16

LANES = 128
SUBLANES = 8
FFN_ROWS = 512
FFN_COLS = 256
PROJ_ROWS = 512
MIX_ROWS = 512
ATT_BLOCK = 256
ADA_COLS = 1024
SCORE_RUN = 4
FAST_RUN = 4
COUNT_CHAINS = 4
COUNT_ROWS = 64
BISECT_STEPS = 14

LOG2E = math.log2(math.e)
BF16 = jnp.bfloat16
F32 = jnp.float32
NT_DIMS = (((1,), (1,)), ((), ()))


def _tile_bytes(shape, dtype):
    itemsize = jnp.dtype(dtype).itemsize
    sublanes = SUBLANES * (4 // itemsize)
    shape = (1,) * (2 - len(shape)) + tuple(shape)
    rows = -(-shape[-2] // sublanes) * sublanes
    cols = -(-shape[-1] // LANES) * LANES
    return math.prod(shape[:-2]) * rows * cols * itemsize


def _vmem_limit(pipelined, resident, temporaries):
    return (2 * sum(_tile_bytes(*b) for b in pipelined) + sum(_tile_bytes(*b) for b in resident)
            + sum(_tile_bytes(*b) for b in temporaries))


def _layer_norm(z, g, b):
    mu = jnp.mean(z, axis=-1, keepdims=True)
    zc = z - mu
    var = jnp.mean(zc * zc, axis=-1, keepdims=True)
    return zc * lax.rsqrt(var + LN_EPS) * g + b


def _silu(a):
    return a * jax.nn.sigmoid(a)


def _ada_kernel(c_ref, w_ref, b_ref, o_ref):
    a = _silu(c_ref[...])
    o_ref[...] = jnp.dot(a, w_ref[...], preferred_element_type=F32) + b_ref[...]


def _ada_mod(c, w_ada, b_ada):
    bsz, d = c.shape
    n = w_ada.shape[1]
    rows = -(-bsz // SUBLANES) * SUBLANES
    c_pad = jnp.pad(c, ((0, rows - bsz), (0, 0)))
    out = pl.pallas_call(
        _ada_kernel,
        grid=(n // ADA_COLS,),
        in_specs=[
            pl.BlockSpec((rows, d), lambda j: (0, 0)),
            pl.BlockSpec((d, ADA_COLS), lambda j: (0, j)),
            pl.BlockSpec((1, ADA_COLS), lambda j: (0, j)),
        ],
        out_specs=pl.BlockSpec((rows, ADA_COLS), lambda j: (0, j)),
        out_shape=jax.ShapeDtypeStruct((rows, n), F32),
        compiler_params=pltpu.CompilerParams(
            dimension_semantics=("arbitrary",),
            vmem_limit_bytes=_vmem_limit([((rows, d), F32), ((d, ADA_COLS), F32), ((1, ADA_COLS), F32),
                                          ((rows, ADA_COLS), F32)], [], [((rows, ADA_COLS), F32)])),
        name="ada_mod",
    )(c_pad, w_ada, b_ada.reshape(1, n))
    return out[:bsz]


def _ffn_kernel(x_ref, mod_ref, lng_ref, lnb_ref, wg_ref, wu_ref, wd_ref, o_ref, h_ref, *, alpha, mod_row, ln_row):
    x = x_ref[...]
    sh = mod_ref[0, mod_row:mod_row + 1, :]
    sc = mod_ref[0, mod_row + 1:mod_row + 2, :]
    gate = mod_ref[0, mod_row + 2:mod_row + 3, :]
    u = (x * (1.0 + sc) + sh).astype(BF16)
    d_ff = wg_ref.shape[1]
    for c in range(d_ff // FFN_COLS):
        sl = slice(c * FFN_COLS, (c + 1) * FFN_COLS)
        a = jnp.dot(u, wg_ref[:, sl], preferred_element_type=F32)
        b = jnp.dot(u, wu_ref[:, sl], preferred_element_type=F32)
        h_ref[:, sl] = (_silu(a) * b).astype(BF16)
    y = jnp.dot(h_ref[...], wd_ref[...], preferred_element_type=F32)
    z = alpha * x + (0.5 * gate) * y
    o_ref[...] = _layer_norm(z, lng_ref[ln_row:ln_row + 1, :], lnb_ref[ln_row:ln_row + 1, :])


def _ffn(x2d, mod, ln_g, ln_b, wg, wu, wd, *, seq, alpha, mod_row, ln_row):
    n, d = x2d.shape
    d_ff = wg.shape[1]
    tiles_per_seq = seq // FFN_ROWS
    resident = dict(pipeline_mode=pl.Buffered(1))
    return pl.pallas_call(
        functools.partial(_ffn_kernel, alpha=alpha, mod_row=mod_row, ln_row=ln_row),
        grid=(n // FFN_ROWS,),
        in_specs=[
            pl.BlockSpec((FFN_ROWS, d), lambda i: (i, 0)),
            pl.BlockSpec((1, N_ADA, d), lambda i: (i // tiles_per_seq, 0, 0)),
            pl.BlockSpec(ln_g.shape, lambda i: (0, 0)),
            pl.BlockSpec(ln_b.shape, lambda i: (0, 0)),
            pl.BlockSpec((d, d_ff), lambda i: (0, 0), **resident),
            pl.BlockSpec((d, d_ff), lambda i: (0, 0), **resident),
            pl.BlockSpec((d_ff, d), lambda i: (0, 0), **resident),
        ],
        out_specs=pl.BlockSpec((FFN_ROWS, d), lambda i: (i, 0)),
        out_shape=jax.ShapeDtypeStruct((n, d), F32),
        scratch_shapes=[pltpu.VMEM((FFN_ROWS, d_ff), BF16)],
        compiler_params=pltpu.CompilerParams(
            dimension_semantics=("arbitrary",),
            vmem_limit_bytes=_vmem_limit(
                [((FFN_ROWS, d), F32), ((N_ADA, d), F32), (ln_g.shape, F32), (ln_b.shape, F32), ((FFN_ROWS, d), F32)],
                [((d, d_ff), BF16), ((d, d_ff), BF16), ((d_ff, d), BF16), ((FFN_ROWS, d_ff), BF16)],
                [((FFN_ROWS, d), BF16), ((FFN_ROWS, FFN_COLS), F32), ((FFN_ROWS, FFN_COLS), F32),
                 ((FFN_ROWS, d), F32), ((FFN_ROWS, d), F32)])),
        name="ffn",
    )(x2d, mod, ln_g, ln_b, wg, wu, wd)


def _proj_kernel(x_ref, mod_ref, wqk_ref, wvw_ref, wki_ref,
                 q_ref, k_ref, qi_ref, vt_ref, kia_ref, kib_ref, wit_ref, *, mod_row):
    x = x_ref[...]
    sh = mod_ref[0, mod_row:mod_row + 1, :]
    sc = mod_ref[0, mod_row + 1:mod_row + 2, :]
    u = (x * (1.0 + sc) + sh).astype(BF16)
    qkq = jnp.dot(u, wqk_ref[...], preferred_element_type=F32)
    q_ref[...] = (qkq[:, :ATTN_DIM] * (HEAD_DIM ** -0.5 * LOG2E)).astype(BF16)
    k_ref[...] = qkq[:, ATTN_DIM:2 * ATTN_DIM].astype(BF16)
    qi_ref[...] = qkq[:, 2 * ATTN_DIM:].astype(BF16)
    kk = jnp.dot(u, wki_ref[...], preferred_element_type=F32)
    kia_ref[...] = kk[:, :LANES].astype(BF16)
    kib_ref[...] = kk[:, LANES:].astype(BF16)
    vw = lax.dot_general(wvw_ref[...], u, NT_DIMS, preferred_element_type=F32)
    vt = vw[:ATTN_DIM].astype(BF16)
    ones_rows = jnp.where(lax.broadcasted_iota(jnp.int32, (HEAD_PAD, ATT_BLOCK), 0) == 0, 1.0, 0.0).astype(BF16)
    for c in range(vt_ref.shape[1]):
        for h in range(N_HEADS):
            vt_ref[0, c, h * HEAD_ROWS:h * HEAD_ROWS + HEAD_DIM, :] = \
                vt[h * HEAD_DIM:(h + 1) * HEAD_DIM, c * ATT_BLOCK:(c + 1) * ATT_BLOCK]
            vt_ref[0, c, h * HEAD_ROWS + HEAD_DIM:(h + 1) * HEAD_ROWS, :] = ones_rows
    wit_ref[0] = vw[ATTN_DIM:ATTN_DIM + IDX_HEADS, :]


def _attn_proj(x2d, mod, wqk, wvw_t, wki, *, bsz, seq, mod_row):
    n, d = x2d.shape
    tiles_per_seq = seq // PROJ_ROWS
    chunks_per_tile = PROJ_ROWS // ATT_BLOCK
    resident = dict(pipeline_mode=pl.Buffered(1))
    row_spec = lambda cols: pl.BlockSpec((PROJ_ROWS, cols), lambda i: (i, 0))
    return pl.pallas_call(
        functools.partial(_proj_kernel, mod_row=mod_row),
        grid=(n // PROJ_ROWS,),
        in_specs=[
            pl.BlockSpec((PROJ_ROWS, d), lambda i: (i, 0)),
            pl.BlockSpec((1, N_ADA, d), lambda i: (i // tiles_per_seq, 0, 0)),
            pl.BlockSpec(wqk.shape, lambda i: (0, 0), **resident),
            pl.BlockSpec(wvw_t.shape, lambda i: (0, 0), **resident),
            pl.BlockSpec(wki.shape, lambda i: (0, 0), **resident),
        ],
        out_specs=[
            row_spec(ATTN_DIM), row_spec(ATTN_DIM), row_spec(IDX_HEADS * IDX_DIM),
            pl.BlockSpec((1, chunks_per_tile, N_HEADS * HEAD_ROWS, ATT_BLOCK),
                         lambda i: (i // tiles_per_seq, i % tiles_per_seq, 0, 0)),
            row_spec(LANES), row_spec(LANES),
            pl.BlockSpec((1, IDX_HEADS, PROJ_ROWS), lambda i: (i // tiles_per_seq, 0, i % tiles_per_seq)),
        ],
        out_shape=[
            jax.ShapeDtypeStruct((n, ATTN_DIM), BF16),
            jax.ShapeDtypeStruct((n, ATTN_DIM), BF16),
            jax.ShapeDtypeStruct((n, IDX_HEADS * IDX_DIM), BF16),
            jax.ShapeDtypeStruct((bsz, seq // ATT_BLOCK, N_HEADS * HEAD_ROWS, ATT_BLOCK), BF16),
            jax.ShapeDtypeStruct((n, LANES), BF16),
            jax.ShapeDtypeStruct((n, LANES), BF16),
            jax.ShapeDtypeStruct((bsz, IDX_HEADS, seq), F32),
        ],
        compiler_params=pltpu.CompilerParams(
            dimension_semantics=("arbitrary",),
            vmem_limit_bytes=_vmem_limit(
                [((PROJ_ROWS, d), F32), ((N_ADA, d), F32), ((PROJ_ROWS, ATTN_DIM), BF16), ((PROJ_ROWS, ATTN_DIM), BF16),
                 ((PROJ_ROWS, IDX_HEADS * IDX_DIM), BF16), ((chunks_per_tile, N_HEADS * HEAD_ROWS, ATT_BLOCK), BF16),
                 ((PROJ_ROWS, LANES), BF16), ((PROJ_ROWS, LANES), BF16), ((IDX_HEADS, PROJ_ROWS), F32)],
                [(wqk.shape, BF16), (wvw_t.shape, BF16), (wki.shape, BF16)],
                [((PROJ_ROWS, d), BF16), ((PROJ_ROWS, wqk.shape[1]), F32), ((PROJ_ROWS, wki.shape[1]), F32),
                 ((wvw_t.shape[0], PROJ_ROWS), F32)])),
        name="attn_proj",
    )(x2d, mod, wqk, wvw_t, wki)


def _t5_bucket(n):
    max_exact = REL_BUCKETS // 2
    nf = jnp.maximum(n, 1).astype(F32)
    large = max_exact + jnp.floor(jnp.log(nf / max_exact) / math.log(REL_MAX_DIST / max_exact)
                                  * (REL_BUCKETS - max_exact)).astype(jnp.int32)
    large = jnp.minimum(large, REL_BUCKETS - 1)
    return jnp.where(n < max_exact, n, large)


def _far_bucket(first_dist, last_dist):
    n = np.arange(first_dist, last_dist + 1, dtype=np.float32)
    max_exact = REL_BUCKETS // 2
    large = max_exact + (np.log(n / np.float32(max_exact)) / np.float32(math.log(REL_MAX_DIST / max_exact))
                         * np.float32(REL_BUCKETS - max_exact)).astype(np.int32)
    buckets = np.where(n < max_exact, n.astype(np.int32), np.minimum(large, REL_BUCKETS - 1))
    assert buckets.min() == buckets.max(), "key chunks two or more blocks away must share one bias bucket"
    return int(buckets[0])


def _bias_kernel(rb_ref, o_ref, *, far_bucket):
    o = pl.program_id(0)
    row = lax.broadcasted_iota(jnp.int32, (ATT_BLOCK, ATT_BLOCK), 0)
    col = lax.broadcasted_iota(jnp.int32, (ATT_BLOCK, ATT_BLOCK), 1)
    dist = o * ATT_BLOCK + col - row
    bucket = _t5_bucket(jnp.maximum(dist, 0))
    for h in range(N_HEADS):
        tile = jnp.zeros((ATT_BLOCK, ATT_BLOCK), F32)
        for b in range(REL_BUCKETS):
            tile = jnp.where(bucket == b, rb_ref[b, h], tile)
        o_ref[0, h] = (tile - rb_ref[far_bucket, h]) * LOG2E


def _rel_bias_tiles(rel_bias, far_bucket):
    return pl.pallas_call(
        functools.partial(_bias_kernel, far_bucket=far_bucket),
        grid=(2,),
        in_specs=[pl.BlockSpec(memory_space=pltpu.SMEM)],
        out_specs=pl.BlockSpec((1, N_HEADS, ATT_BLOCK, ATT_BLOCK), lambda o: (o, 0, 0, 0)),
        out_shape=jax.ShapeDtypeStruct((2, N_HEADS, ATT_BLOCK, ATT_BLOCK), F32),
        compiler_params=pltpu.CompilerParams(
            dimension_semantics=("arbitrary",),
            vmem_limit_bytes=_vmem_limit([((N_HEADS, ATT_BLOCK, ATT_BLOCK), F32)], [],
                                         [((ATT_BLOCK, ATT_BLOCK), F32)] * 3)),
        name="rel_bias",
    )(rel_bias)


def _attn_kernel(q_ref, qi_ref, wit_ref, k_ref, vt_ref, kia_ref, kib_ref, bt_ref, o_ref,
                 sc_ref, zc_ref, ranked_ref, qh_ref, lg_ref, p_ref, pp_ref, st_ref, acc_ref, *, topk, seq):
    blk = ATT_BLOCK
    groups = blk // SUBLANES
    i = pl.program_id(1)
    nch = i + 1
    kf = float(topk)
    inf = jnp.inf

    def chunk_start(j):
        return j * blk if isinstance(j, int) else pl.multiple_of(j * blk, blk)

    def as_groups(x):
        return x.reshape(groups, SUBLANES, blk)

    def lanes8(v):
        return jnp.broadcast_to(v, (SUBLANES, blk))

    def colmin(x8):
        return jnp.min(x8, axis=0, keepdims=True)

    def colmax(x8):
        return jnp.max(x8, axis=0, keepdims=True)

    def colsum(x8):
        return jnp.sum(x8, axis=0, keepdims=True)

    wf = wit_ref[0] * (IDX_DIM ** -0.5)
    qi = qi_ref[0]
    row = lax.broadcasted_iota(jnp.int32, (blk, blk), 0)
    col = lax.broadcasted_iota(jnp.int32, (blk, blk), 1)
    causal = row <= col

    def chunk_scores(j):
        r0 = chunk_start(j)
        ka = kia_ref[0, pl.ds(r0, blk), :]
        kb = kib_ref[0, pl.ds(r0, blk), :]
        s = jnp.zeros((blk, blk), F32)
        for hp in range(IDX_HEADS // 2):
            qp = qi[:, hp * LANES:(hp + 1) * LANES]
            a0 = lax.dot_general(ka, qp, NT_DIMS, preferred_element_type=F32)
            a1 = lax.dot_general(kb, qp, NT_DIMS, preferred_element_type=F32)
            s = s + jnp.maximum(a0, 0.0) * wf[2 * hp:2 * hp + 1, :]
            s = s + jnp.maximum(a1, 0.0) * wf[2 * hp + 1:2 * hp + 2, :]
        return r0, s * (IDX_HEADS ** -0.5)

    def score_stats(j, s_lo, s_hi, stats):
        mn8, mx8, pos8, zer8 = stats
        hi3 = as_groups(s_hi)
        zer8 = zer8 + jnp.sum(jnp.where(hi3 == 0.0, 1.0, 0.0), axis=0)
        zc_ref[j] = zer8
        return (jnp.minimum(mn8, jnp.min(as_groups(s_lo), axis=0)),
                jnp.maximum(mx8, jnp.max(hi3, axis=0)),
                pos8 + jnp.sum(jnp.where(hi3 > 0.0, 1.0, 0.0), axis=0),
                zer8)

    def score_body(j, stats):
        r0, s = chunk_scores(j)
        sc_ref[pl.ds(r0, blk), :] = s
        return score_stats(j, s, s, stats)

    def score_run(first, count, stats):
        for c in range(count):
            stats = score_body(first + c, stats)
        return stats

    def score_diag(st):
        r_diag, s_diag = chunk_scores(i)
        s_diag_hi = jnp.where(causal, s_diag, -inf)
        sc_ref[pl.ds(r_diag, blk), :] = s_diag_hi
        return score_stats(i, jnp.where(causal, s_diag, inf), s_diag_hi, st)

    zeros8 = jnp.zeros((SUBLANES, blk), F32)
    stats = (jnp.full((SUBLANES, blk), inf, F32), jnp.full((SUBLANES, blk), -inf, F32), zeros8, zeros8)
    stats = lax.fori_loop(0, i // SCORE_RUN, lambda jq, st: score_run(SCORE_RUN * jq, SCORE_RUN, st), stats)
    left = i % SCORE_RUN
    mn8, mx8, pos8, zer8 = lax.switch(
        left, [functools.partial(lambda r, st: score_diag(score_run(i - r, r, st)), r) for r in range(SCORE_RUN)], stats)

    @pl.when(nch % 2 == 1)
    def _():
        sc_ref[pl.ds(chunk_start(nch), blk), :] = jnp.full((blk, blk), -inf, F32)

    npair = (nch + 1) // 2

    def count_gt(thr):
        t8 = lanes8(thr)

        def body(jp, accs):
            base = pl.multiple_of(jp * 2 * blk, 2 * blk)
            accs = list(accs)
            for s in range(2 * blk // COUNT_ROWS):
                x = sc_ref[pl.ds(base + s * COUNT_ROWS, COUNT_ROWS), :]
                for g in range(COUNT_ROWS // SUBLANES):
                    hit = jnp.where(x[g * SUBLANES:(g + 1) * SUBLANES] > t8, 1.0, 0.0)
                    accs[g % COUNT_CHAINS] = accs[g % COUNT_CHAINS] + hit
            return tuple(accs)

        accs = lax.fori_loop(0, npair, body, tuple(jnp.zeros((SUBLANES, blk), F32) for _ in range(COUNT_CHAINS)))
        return colsum(sum(accs[1:], accs[0]))

    def bisect(_, st):
        lo, hi, low, done = st
        mid = 0.5 * lo + 0.5 * hi
        c = count_gt(mid)
        live = done < 0.5
        up = jnp.logical_and(live, c >= kf)
        down = jnp.logical_and(live, c < kf)
        low = jnp.where(up, mid, low)
        lo = jnp.where(up, mid, lo)
        hi = jnp.where(down, mid, hi)
        done = jnp.where(c == kf, 1.0, done)
        return lo, hi, low, done

    tpos = i * blk + lax.broadcasted_iota(jnp.int32, (1, blk), 1)
    pos, zer = colsum(pos8), colsum(zer8)
    mn, mx = colmin(mn8), colmax(mx8)
    zero_tie = jnp.logical_and(pos < kf, pos + zer >= kf)
    pos_ge = pos >= kf
    done0 = jnp.where(jnp.logical_or(jnp.logical_or(tpos + 1 <= topk, zero_tie), pos == kf), 1.0, 0.0)
    low0 = jnp.where(jnp.logical_or(zero_tie, pos_ge), 0.0, -inf)
    lo0 = jnp.where(pos_ge, jnp.maximum(mn, 0.0), mn)
    hi0 = jnp.where(pos_ge, mx, jnp.minimum(mx, 0.0))
    tie0 = jnp.where(zero_tie, 0.0, inf)
    need0 = jnp.where(zero_tie, kf - pos, 0.0)

    def next_value_above(thr):
        t8 = lanes8(thr)

        def body(jp, accs):
            base = pl.multiple_of(jp * 2 * blk, 2 * blk)
            accs = list(accs)
            for s in range(2 * blk // COUNT_ROWS):
                x = sc_ref[pl.ds(base + s * COUNT_ROWS, COUNT_ROWS), :]
                for g in range(COUNT_ROWS // SUBLANES):
                    xg = x[g * SUBLANES:(g + 1) * SUBLANES]
                    accs[g % COUNT_CHAINS] = jnp.minimum(accs[g % COUNT_CHAINS], jnp.where(xg > t8, xg, inf))
            return tuple(accs)

        accs = lax.fori_loop(0, npair, body, tuple(jnp.full((SUBLANES, blk), inf, F32) for _ in range(COUNT_CHAINS)))
        return colmin(functools.reduce(jnp.minimum, accs))

    def climb_cond(st):
        return jnp.min(st[1]) < 0.5

    def climb_body(st):
        low, done, tie, need = st
        live = done < 0.5
        cand = next_value_above(low)
        cgt = count_gt(cand)
        found = jnp.logical_and(live, cgt < kf)
        tie = jnp.where(found, cand, tie)
        need = jnp.where(found, kf - cgt, need)
        low = jnp.where(live, cand, low)
        done = jnp.where(jnp.logical_and(live, cgt <= kf), 1.0, done)
        return low, done, tie, need

    def select(_):
        _, _, low, done = lax.fori_loop(0, BISECT_STEPS, bisect, (lo0, hi0, low0, done0))
        low, _, tie, need = lax.while_loop(climb_cond, climb_body, (low, done, tie0, need0))
        return low, tie, need

    low, tie, need = lax.cond(jnp.min(done0) < 0.5, select, lambda _: (low0, tie0, need0), 0)

    def plain_mask(_):
        low8 = lanes8(low)

        def body(j, _):
            r0 = chunk_start(j)
            x = as_groups(sc_ref[pl.ds(r0, blk), :])
            sc_ref[pl.ds(r0, blk), :] = jnp.where(x > low8[None], 0.0, -inf).reshape(blk, blk)
            return 0

        return lax.fori_loop(0, nch, body, 0)

    def ranked_chunk(j, tri, base):
        r0 = chunk_start(j)
        x = sc_ref[pl.ds(r0, blk), :]
        eq = x == tie
        rank = jnp.dot(tri, jnp.where(eq, 1.0, 0.0).astype(BF16), preferred_element_type=F32)
        sel = jnp.logical_or(x > low, jnp.logical_and(eq, rank + base <= need))
        sc_ref[pl.ds(r0, blk), :] = jnp.where(sel, 0.0, -inf)
        return rank[blk - 1:blk, :]

    def tie_mask(_):
        tri = jnp.where(row >= col, 1.0, 0.0).astype(BF16)
        lax.fori_loop(0, nch, lambda j, base: base + ranked_chunk(j, tri, base), jnp.zeros((1, blk), F32))
        return 0

    def zero_tie_mask(_):
        tri = jnp.where(row >= col, 1.0, 0.0).astype(BF16)

        def cut_body(j, ncut):
            return ncut + jnp.where(colsum(zc_ref[j]) < need, 1.0, 0.0)

        ncut = lax.fori_loop(0, nch, cut_body, jnp.zeros((1, blk), F32))
        cut = jnp.where(tie < inf, ncut, inf)

        def flag_body(j, _):
            ranked_ref[j] = 0
            return 0

        lax.fori_loop(0, nch, flag_body, 0)

        def cut_cond(pending):
            return jnp.min(pending) < inf

        def cut_chunk(pending):
            jf = jnp.min(pending)
            j = jf.astype(jnp.int32)
            base = jnp.where(j > 0, colsum(zc_ref[jnp.maximum(j - 1, 0)]), 0.0)
            ranked_chunk(j, tri, base)
            ranked_ref[j] = 1
            return jnp.where(pending == jf, inf, pending)

        lax.while_loop(cut_cond, cut_chunk, cut)
        low8, cut8 = lanes8(low), lanes8(cut)

        def body(j, _):
            @pl.when(ranked_ref[j] == 0)
            def _():
                r0 = chunk_start(j)
                x = as_groups(sc_ref[pl.ds(r0, blk), :])
                at_low = jnp.where(jnp.logical_and(cut8 < inf, cut8 > j.astype(F32)), 0.0, -inf)
                out = jnp.where(x > low8[None], 0.0, jnp.where(x == low8[None], at_low[None], -inf))
                sc_ref[pl.ds(r0, blk), :] = out.reshape(blk, blk)
            return 0

        return lax.fori_loop(0, nch, body, 0)

    tied = tie < inf
    mask_kind = jnp.where(jnp.max(jnp.where(tied, 1.0, 0.0)) < 0.5, 0,
                          jnp.where(jnp.max(jnp.where(jnp.logical_and(tied, tie != 0.0), 1.0, 0.0)) < 0.5, 1, 2))
    lax.switch(mask_kind, [plain_mask, zero_tie_mask, tie_mask], 0)

    q = q_ref[0]
    lane = lax.broadcasted_iota(jnp.int32, (blk, LANES), 1)
    for h in range(N_HEADS):
        hp = h // 2
        in_head = (lane < HEAD_DIM) if h % 2 == 0 else (lane >= HEAD_DIM)
        qh_ref[h] = jnp.where(in_head, q[:, hp * LANES:(hp + 1) * LANES], jnp.zeros((), BF16))

    def logits(j, h, mb, near):
        hp = h // 2
        kc = k_ref[0, pl.ds(chunk_start(j), blk), hp * LANES:(hp + 1) * LANES]
        lg = lax.dot_general(kc, qh_ref[h], NT_DIMS, preferred_element_type=F32) + mb
        return lg if near is None else lg + bt_ref[near, h]

    CMAX, RMAX, RESC = 0, 1, 2

    def stage_logits(j, near):
        mb = sc_ref[pl.ds(chunk_start(j), blk), :]
        cm8 = []
        for h in range(N_HEADS):
            lg = logits(j, h, mb, near)
            lg_ref[h] = lg
            cm8.append(jnp.max(as_groups(lg), axis=0))
        for h in range(N_HEADS):
            st_ref[CMAX, h] = lanes8(colmax(cm8[h]))

    def stage_exp():
        for h in range(N_HEADS):
            m_old = st_ref[RMAX, h]
            m_new = jnp.maximum(m_old, st_ref[CMAX, h])
            m_use = jnp.where(m_new == -inf, 0.0, m_new)
            p = jnp.exp2(as_groups(lg_ref[h]) - m_use[None])
            p_ref[h] = p.reshape(blk, blk).astype(BF16)
            st_ref[RMAX, h] = m_new
            st_ref[RESC, h] = jnp.exp2(m_old - m_use)

    def stage_pv(j):
        for h in range(N_HEADS):
            rows = slice(h * HEAD_ROWS, (h + 1) * HEAD_ROWS)
            pv = jnp.dot(vt_ref[0, j, rows, :], p_ref[h], preferred_element_type=F32)
            acc = acc_ref[rows, :].reshape(HEAD_ROWS // SUBLANES, SUBLANES, blk) * st_ref[RESC, h][None]
            acc_ref[rows, :] = acc.reshape(HEAD_ROWS, blk) + pv

    def pipelined(j, near_next):
        stage_exp()
        stage_logits(j + 1, near_next)
        stage_pv(j)

    def exact_attention():
        acc_ref[...] = jnp.zeros(acc_ref.shape, F32)
        st_ref[RMAX] = jnp.full((N_HEADS, SUBLANES, blk), -inf, F32)

        def last_chunks(first):
            for j in range(first, 0):
                pipelined(i + j, -(j + 1))
            stage_exp()
            stage_pv(i)

        @pl.when(i == 0)
        def _():
            stage_logits(0, 0)
            last_chunks(0)

        @pl.when(i == 1)
        def _():
            stage_logits(0, 1)
            last_chunks(-1)

        @pl.when(i >= 2)
        def _():
            stage_logits(0, None)

        def far_body(j, _):
            pipelined(j, None)
            return 0

        lax.fori_loop(0, jnp.maximum(i - 2, 0), far_body, 0)

        @pl.when(i >= 2)
        def _():
            last_chunks(-2)

    def first_chunk(near):
        stage_logits(0, near)
        for h in range(N_HEADS):
            cmax = st_ref[CMAX, h]
            shift = jnp.where(cmax == -inf, 0.0, cmax)
            st_ref[RMAX, h] = shift
            p = jnp.exp2(as_groups(lg_ref[h]) - shift[None])
            p_ref[h] = p.reshape(blk, blk).astype(BF16)
        for h in range(N_HEADS):
            rows = slice(h * HEAD_ROWS, (h + 1) * HEAD_ROWS)
            acc_ref[rows, :] = jnp.dot(vt_ref[0, 0, rows, :], p_ref[h], preferred_element_type=F32)

    def fast_run(chunks):
        for c, (j, near) in enumerate(chunks):
            mb = sc_ref[pl.ds(chunk_start(j), blk), :]
            for h in range(N_HEADS):
                p = jnp.exp2(as_groups(logits(j, h, mb, near)) - st_ref[RMAX, h][None])
                pp_ref[c, h] = p.reshape(blk, blk).astype(BF16)
        for h in range(N_HEADS):
            rows = slice(h * HEAD_ROWS, (h + 1) * HEAD_ROWS)
            tot = acc_ref[rows, :]
            for c, (j, _) in enumerate(chunks):
                tot = tot + jnp.dot(vt_ref[0, j, rows, :], pp_ref[c, h], preferred_element_type=F32)
            up = lanes8(jnp.maximum(jnp.floor(jnp.log2(tot[HEAD_DIM:HEAD_DIM + 1, :])), 0.0))
            scaled = tot.reshape(HEAD_ROWS // SUBLANES, SUBLANES, blk) * jnp.exp2(-up)[None]
            acc_ref[rows, :] = scaled.reshape(HEAD_ROWS, blk)
            st_ref[RMAX, h] = st_ref[RMAX, h] + up

    @pl.when(i == 0)
    def _():
        first_chunk(0)

    @pl.when(i == 1)
    def _():
        first_chunk(1)
        fast_run([(1, 0)])

    @pl.when(i >= 2)
    def _():
        first_chunk(None)

    def far_run(jr, _):
        fast_run([(FAST_RUN * jr + 1 + c, None) for c in range(FAST_RUN)])
        return 0

    n_far = jnp.maximum(i - 2, 0)
    lax.fori_loop(0, n_far // FAST_RUN, far_run, 0)
    for left in range(FAST_RUN):
        @pl.when(jnp.logical_and(i >= 2, n_far % FAST_RUN == left))
        def _():
            last = [(i - 1 - left + c, None) for c in range(left)] + [(i - 1, 1), (i, 0)]
            for first in range(0, len(last), FAST_RUN):
                fast_run(last[first:first + FAST_RUN])

    acc = acc_ref[...]
    finite = jnp.min(jnp.where(jnp.isfinite(acc), 1.0, 0.0))
    dens = jnp.concatenate([acc[h * HEAD_ROWS + HEAD_DIM:h * HEAD_ROWS + HEAD_DIM + 1, :] for h in range(N_HEADS)], axis=0)
    usable = jnp.logical_and(finite > 0.5, jnp.min(dens) > 0.0)
    lax.cond(usable, lambda: None, exact_attention)

    heads = []
    for h in range(N_HEADS):
        r0 = h * HEAD_ROWS
        heads.append(acc_ref[r0:r0 + HEAD_DIM, :] / acc_ref[r0 + HEAD_DIM:r0 + HEAD_DIM + 1, :])
    o_ref[0] = jnp.concatenate(heads, axis=0).T.astype(BF16)


def _attention(q, qi, wit, k, vt, kia, kib, btiles, *, bsz, seq, topk):
    blk = ATT_BLOCK
    nblk = seq // blk
    return pl.pallas_call(
        functools.partial(_attn_kernel, topk=topk, seq=seq),
        grid=(bsz, nblk),
        in_specs=[
            pl.BlockSpec((1, blk, ATTN_DIM), lambda b, i: (b, i, 0)),
            pl.BlockSpec((1, blk, IDX_HEADS * IDX_DIM), lambda b, i: (b, i, 0)),
            pl.BlockSpec((1, IDX_HEADS, blk), lambda b, i: (b, 0, i)),
            pl.BlockSpec((1, seq, ATTN_DIM), lambda b, i: (b, 0, 0)),
            pl.BlockSpec((1, nblk, N_HEADS * HEAD_ROWS, blk), lambda b, i: (b, 0, 0, 0)),
            pl.BlockSpec((1, seq, LANES), lambda b, i: (b, 0, 0)),
            pl.BlockSpec((1, seq, LANES), lambda b, i: (b, 0, 0)),
            pl.BlockSpec(btiles.shape, lambda b, i: (0, 0, 0, 0)),
        ],
        out_specs=pl.BlockSpec((1, blk, ATTN_DIM), lambda b, i: (b, i, 0)),
        out_shape=jax.ShapeDtypeStruct((bsz, seq, ATTN_DIM), BF16),
        scratch_shapes=[
            pltpu.VMEM((seq + blk, blk), F32),
            pltpu.VMEM((nblk, SUBLANES, blk), F32),
            pltpu.SMEM((nblk,), jnp.int32),
            pltpu.VMEM((N_HEADS, blk, LANES), BF16),
            pltpu.VMEM((N_HEADS, blk, blk), F32),
            pltpu.VMEM((N_HEADS, blk, blk), BF16),
            pltpu.VMEM((FAST_RUN, N_HEADS, blk, blk), BF16),
            pltpu.VMEM((3, N_HEADS, SUBLANES, blk), F32),
            pltpu.VMEM((N_HEADS * HEAD_ROWS, blk), F32),
        ],
        compiler_params=pltpu.CompilerParams(
            dimension_semantics=("arbitrary", "arbitrary"),
            vmem_limit_bytes=_vmem_limit(
                [((blk, ATTN_DIM), BF16), ((blk, IDX_HEADS * IDX_DIM), BF16), ((IDX_HEADS, blk), F32),
                 ((seq, ATTN_DIM), BF16), ((nblk, N_HEADS * HEAD_ROWS, blk), BF16), ((seq, LANES), BF16),
                 ((seq, LANES), BF16), (btiles.shape, F32), ((blk, ATTN_DIM), BF16)],
                [((seq + blk, blk), F32), ((nblk, SUBLANES, blk), F32), ((N_HEADS, blk, LANES), BF16),
                 ((N_HEADS, blk, blk), F32), ((N_HEADS, blk, blk), BF16), ((FAST_RUN, N_HEADS, blk, blk), BF16),
                 ((3, N_HEADS, SUBLANES, blk), F32),
                 ((N_HEADS * HEAD_ROWS, blk), F32)],
                [((blk, blk), F32)] * 4 + [((blk, blk), BF16), ((ATTN_DIM, blk), F32), ((blk, ATTN_DIM), F32)])),
        name="attention",
    )(q, qi, wit, k, vt, kia, kib, btiles)


def _mix_kernel(x_ref, mod_ref, att_ref, wpg_ref, wpool_ref, ps_ref, wa_ref, wb_ref, wo_ref,
                lng_ref, lnb_ref, o_ref, pe_ref, mix_ref, *, alpha, mod_row, ln_row, tiles_per_seq):
    rows = x_ref.shape[0]
    i = pl.program_id(0)
    seq_tile = i % tiles_per_seq

    @pl.when(i == 0)
    def _():
        pe_ref[rows:, :] = jnp.zeros((POOL_HALO, POOL_DIM), F32)

    x = x_ref[...]
    sh = mod_ref[0, mod_row:mod_row + 1, :]
    sc = mod_ref[0, mod_row + 1:mod_row + 2, :]
    gate = mod_ref[0, mod_row + 2:mod_row + 3, :]
    u = (x * (1.0 + sc) + sh).astype(BF16)
    pg = jnp.dot(u, wpg_ref[...], preferred_element_type=F32)
    pe_ref[0:POOL_HALO, :] = jnp.where(seq_tile == 0, 0.0, pe_ref[rows:, :])
    pe_ref[POOL_HALO:, :] = pg[:, :POOL_DIM]
    t = seq_tile * rows + lax.broadcasted_iota(jnp.int32, (rows, 1), 0)
    for g, w in enumerate(POOL_WINDOWS):
        cols = slice(g * POOL_GROUP_DIM, (g + 1) * POOL_GROUP_DIM)
        cur = pe_ref[POOL_HALO:, cols]
        win = cur
        for back in range(1, w):
            win = win + pe_ref[POOL_HALO - back:POOL_HALO - back + rows, cols]
        cnt = jnp.minimum(t + 1, w).astype(F32)
        pooled = (win / cnt - cur).astype(BF16)
        mixed = jnp.dot(pooled, wpool_ref[g], preferred_element_type=F32)
        mix_ref[:, cols] = (mixed * ps_ref[:, cols]).astype(BF16)
    y_a = jnp.dot(mix_ref[...], wa_ref[...], preferred_element_type=F32)
    y_b = jnp.dot(att_ref[...], wb_ref[...], preferred_element_type=F32)
    d = x.shape[1]
    ga = pg[:, POOL_DIM:POOL_DIM + d]
    gb = pg[:, POOL_DIM + d:]
    merged = (jax.nn.sigmoid(ga) * y_a + jax.nn.sigmoid(gb) * y_b).astype(BF16)
    y = jnp.dot(merged, wo_ref[...], preferred_element_type=F32)
    z = alpha * x + gate * y
    o_ref[...] = _layer_norm(z, lng_ref[ln_row:ln_row + 1, :], lnb_ref[ln_row:ln_row + 1, :])


def _mix_out(x2d, mod, att2d, wpg, wpool, pool_scale, wa, wb, wo, ln_g, ln_b, *, seq, alpha, mod_row, ln_row):
    n, d = x2d.shape
    tiles_per_seq = seq // MIX_ROWS
    resident = dict(pipeline_mode=pl.Buffered(1))
    full = lambda a: pl.BlockSpec(a.shape, lambda i: (0,) * a.ndim, **resident)
    return pl.pallas_call(
        functools.partial(_mix_kernel, alpha=alpha, mod_row=mod_row, ln_row=ln_row, tiles_per_seq=tiles_per_seq),
        grid=(n // MIX_ROWS,),
        in_specs=[
            pl.BlockSpec((MIX_ROWS, d), lambda i: (i, 0)),
            pl.BlockSpec((1, N_ADA, d), lambda i: (i // tiles_per_seq, 0, 0)),
            pl.BlockSpec((MIX_ROWS, ATTN_DIM), lambda i: (i, 0)),
            full(wpg), full(wpool), full(pool_scale), full(wa), full(wb), full(wo),
            pl.BlockSpec(ln_g.shape, lambda i: (0, 0)),
            pl.BlockSpec(ln_b.shape, lambda i: (0, 0)),
        ],
        out_specs=pl.BlockSpec((MIX_ROWS, d), lambda i: (i, 0)),
        out_shape=jax.ShapeDtypeStruct((n, d), F32),
        scratch_shapes=[
            pltpu.VMEM((POOL_HALO + MIX_ROWS, POOL_DIM), F32),
            pltpu.VMEM((MIX_ROWS, POOL_DIM), BF16),
        ],
        compiler_params=pltpu.CompilerParams(
            dimension_semantics=("arbitrary",),
            vmem_limit_bytes=_vmem_limit(
                [((MIX_ROWS, d), F32), ((N_ADA, d), F32), ((MIX_ROWS, ATTN_DIM), BF16), (ln_g.shape, F32),
                 (ln_b.shape, F32), ((MIX_ROWS, d), F32)],
                [(wpg.shape, BF16), (wpool.shape, BF16), (pool_scale.shape, F32), (wa.shape, BF16), (wb.shape, BF16),
                 (wo.shape, BF16), ((POOL_HALO + MIX_ROWS, POOL_DIM), F32), ((MIX_ROWS, POOL_DIM), BF16)],
                [((MIX_ROWS, d), BF16), ((MIX_ROWS, wpg.shape[1]), F32)] + [((MIX_ROWS, d), F32)] * 5)),
        name="mix_out",
    )(x2d, mod, att2d, wpg, wpool, pool_scale, wa, wb, wo, ln_g, ln_b)


def kernel(x, c, w_ada, b_ada, ln_g, ln_b, ffn1_w_gate, ffn1_w_up, ffn1_w_down, w_in, w_pool, pool_scale,
           w_a, w_b, w_out, rel_bias, ffn2_w_gate, ffn2_w_up, ffn2_w_down):
    bsz, seq, d = x.shape
    depth = w_ada.shape[0]
    alpha = (2.0 * depth) ** 0.25
    topk = min(TOP_K, seq // 4)
    assert seq % FFN_ROWS == 0 and seq % PROJ_ROWS == 0 and seq % MIX_ROWS == 0 and seq % ATT_BLOCK == 0
    assert PROJ_ROWS % ATT_BLOCK == 0 and POOL_HALO >= max(POOL_WINDOWS) - 1
    far_bucket = _far_bucket(ATT_BLOCK + 1, max(seq - 1, ATT_BLOCK + 1))

    o_q = POOL_DIM
    o_k = o_q + ATTN_DIM
    o_v = o_k + ATTN_DIM
    o_qi = o_v + ATTN_DIM
    o_ki = o_qi + IDX_HEADS * IDX_DIM
    o_wi = o_ki + IDX_DIM
    o_ga = o_wi + IDX_HEADS

    btiles = _rel_bias_tiles(rel_bias, far_bucket)
    x2d = x.reshape(bsz * seq, d)
    for l in range(depth):
        wl = w_in[l]
        zeros_ki = jnp.zeros((d, LANES - IDX_DIM), wl.dtype)
        w_ki = wl[:, o_ki:o_wi]
        wqk = jnp.concatenate([wl[:, o_q:o_v], wl[:, o_qi:o_ki]], axis=1).astype(BF16)
        wvw_t = jnp.pad(jnp.concatenate([wl[:, o_v:o_qi], wl[:, o_wi:o_ga]], axis=1).T,
                        ((0, 2 * SUBLANES - IDX_HEADS), (0, 0))).astype(BF16)
        wki = jnp.concatenate([w_ki, zeros_ki, zeros_ki, w_ki], axis=1).astype(BF16)
        wpg = jnp.concatenate([wl[:, :POOL_DIM], wl[:, o_ga:]], axis=1).astype(BF16)

        mod = _ada_mod(c, w_ada[l], b_ada[l]).reshape(bsz, N_ADA, d)
        x2d = _ffn(x2d, mod, ln_g[l], ln_b[l], ffn1_w_gate[l].astype(BF16), ffn1_w_up[l].astype(BF16),
                   ffn1_w_down[l].astype(BF16), seq=seq, alpha=alpha, mod_row=0, ln_row=0)
        q, k, qi, vt, kia, kib, wit = _attn_proj(x2d, mod, wqk, wvw_t, wki, bsz=bsz, seq=seq, mod_row=3)
        att = _attention(q.reshape(bsz, seq, ATTN_DIM), qi.reshape(bsz, seq, IDX_HEADS * IDX_DIM), wit,
                         k.reshape(bsz, seq, ATTN_DIM), vt, kia.reshape(bsz, seq, LANES),
                         kib.reshape(bsz, seq, LANES), btiles, bsz=bsz, seq=seq, topk=topk)
        x2d = _mix_out(x2d, mod, att.reshape(bsz * seq, ATTN_DIM), wpg, w_pool[l].astype(BF16),
                       pool_scale[l].reshape(1, POOL_DIM), w_a[l].astype(BF16), w_b[l].astype(BF16),
                       w_out[l].astype(BF16), ln_g[l], ln_b[l], seq=seq, alpha=alpha, mod_row=3, ln_row=1)
        x2d = _ffn(x2d, mod, ln_g[l], ln_b[l], ffn2_w_gate[l].astype(BF16), ffn2_w_up[l].astype(BF16),
                   ffn2_w_down[l].astype(BF16), seq=seq, alpha=alpha, mod_row=6, ln_row=2)
    return x2d.reshape(bsz, seq, d)
```

```python
import functools
import math

import numpy as np
import jax
import jax.numpy as jnp
from jax import lax
from jax.experimental import pallas as pl
from jax.experimental.pallas import tpu as pltpu

POOL_WINDOWS = (2, 4, 8, 16)
POOL_GROUP_DIM = 128
POOL_DIM = len(POOL_WINDOWS) * POOL_GROUP_DIM
N_HEADS = 8
HEAD_DIM = 64
ATTN_DIM = N_HEADS * HEAD_DIM
HEAD_PAD = 16
HEAD_ROWS = HEAD_DIM + HEAD_PAD
IDX_HEADS = 8
IDX_DIM = 64
TOP_K = 256
REL_BUCKETS = 32
REL_MAX_DIST = 128
N_ADA = 9
LN_EPS = 1e-5
POOL_HALO = 16

LANES = 128
SUBLANES = 8
FFN_ROWS = 512
FFN_COLS = 256
PROJ_ROWS = 512
MIX_ROWS = 512
ATT_BLOCK = 256
ADA_COLS = 1024
SCORE_RUN = 4
FAST_RUN = 4
COUNT_CHAINS = 4
COUNT_ROWS = 64
BISECT_STEPS = 14

LOG2E = math.log2(math.e)
BF16 = jnp.bfloat16
F32 = jnp.float32
NT_DIMS = (((1,), (1,)), ((), ()))


def _tile_bytes(shape, dtype):
    itemsize = jnp.dtype(dtype).itemsize
    sublanes = SUBLANES * (4 // itemsize)
    shape = (1,) * (2 - len(shape)) + tuple(shape)
    rows = -(-shape[-2] // sublanes) * sublanes
    cols = -(-shape[-1] // LANES) * LANES
    return math.prod(shape[:-2]) * rows * cols * itemsize


def _vmem_limit(pipelined, resident, temporaries):
    return (2 * sum(_tile_bytes(*b) for b in pipelined) + sum(_tile_bytes(*b) for b in resident)
            + sum(_tile_bytes(*b) for b in temporaries))


def _layer_norm(z, g, b):
    mu = jnp.mean(z, axis=-1, keepdims=True)
    zc = z - mu
    var = jnp.mean(zc * zc, axis=-1, keepdims=True)
    return zc * lax.rsqrt(var + LN_EPS) * g + b


def _silu(a):
    return a * jax.nn.sigmoid(a)


def _ada_kernel(c_ref, w_ref, b_ref, o_ref):
    a = _silu(c_ref[...])
    o_ref[...] = jnp.dot(a, w_ref[...], preferred_element_type=F32) + b_ref[...]


def _ada_mod(c, w_ada, b_ada):
    bsz, d = c.shape
    n = w_ada.shape[1]
    rows = -(-bsz // SUBLANES) * SUBLANES
    c_pad = jnp.pad(c, ((0, rows - bsz), (0, 0)))
    out = pl.pallas_call(
        _ada_kernel,
        grid=(n // ADA_COLS,),
        in_specs=[
            pl.BlockSpec((rows, d), lambda j: (0, 0)),
            pl.BlockSpec((d, ADA_COLS), lambda j: (0, j)),
            pl.BlockSpec((1, ADA_COLS), lambda j: (0, j)),
        ],
        out_specs=pl.BlockSpec((rows, ADA_COLS), lambda j: (0, j)),
        out_shape=jax.ShapeDtypeStruct((rows, n), F32),
        compiler_params=pltpu.CompilerParams(
            dimension_semantics=("arbitrary",),
            vmem_limit_bytes=_vmem_limit([((rows, d), F32), ((d, ADA_COLS), F32), ((1, ADA_COLS), F32),
                                          ((rows, ADA_COLS), F32)], [], [((rows, ADA_COLS), F32)])),
        name="ada_mod",
    )(c_pad, w_ada, b_ada.reshape(1, n))
    return out[:bsz]


def _ffn_kernel(x_ref, mod_ref, lng_ref, lnb_ref, wg_ref, wu_ref, wd_ref, o_ref, h_ref, *, alpha, mod_row, ln_row):
    x = x_ref[...]
    sh = mod_ref[0, mod_row:mod_row + 1, :]
    sc = mod_ref[0, mod_row + 1:mod_row + 2, :]
    gate = mod_ref[0, mod_row + 2:mod_row + 3, :]
    u = (x * (1.0 + sc) + sh).astype(BF16)
    d_ff = wg_ref.shape[1]
    for c in range(d_ff // FFN_COLS):
        sl = slice(c * FFN_COLS, (c + 1) * FFN_COLS)
        a = jnp.dot(u, wg_ref[:, sl], preferred_element_type=F32)
        b = jnp.dot(u, wu_ref[:, sl], preferred_element_type=F32)
        h_ref[:, sl] = (_silu(a) * b).astype(BF16)
    y = jnp.dot(h_ref[...], wd_ref[...], preferred_element_type=F32)
    z = alpha * x + (0.5 * gate) * y
    o_ref[...] = _layer_norm(z, lng_ref[ln_row:ln_row + 1, :], lnb_ref[ln_row:ln_row + 1, :])


def _ffn(x2d, mod, ln_g, ln_b, wg, wu, wd, *, seq, alpha, mod_row, ln_row):
    n, d = x2d.shape
    d_ff = wg.shape[1]
    tiles_per_seq = seq // FFN_ROWS
    resident = dict(pipeline_mode=pl.Buffered(1))
    return pl.pallas_call(
        functools.partial(_ffn_kernel, alpha=alpha, mod_row=mod_row, ln_row=ln_row),
        grid=(n // FFN_ROWS,),
        in_specs=[
            pl.BlockSpec((FFN_ROWS, d), lambda i: (i, 0)),
            pl.BlockSpec((1, N_ADA, d), lambda i: (i // tiles_per_seq, 0, 0)),
            pl.BlockSpec(ln_g.shape, lambda i: (0, 0)),
            pl.BlockSpec(ln_b.shape, lambda i: (0, 0)),
            pl.BlockSpec((d, d_ff), lambda i: (0, 0), **resident),
            pl.BlockSpec((d, d_ff), lambda i: (0, 0), **resident),
            pl.BlockSpec((d_ff, d), lambda i: (0, 0), **resident),
        ],
        out_specs=pl.BlockSpec((FFN_ROWS, d), lambda i: (i, 0)),
        out_shape=jax.ShapeDtypeStruct((n, d), F32),
        scratch_shapes=[pltpu.VMEM((FFN_ROWS, d_ff), BF16)],
        compiler_params=pltpu.CompilerParams(
            dimension_semantics=("arbitrary",),
            vmem_limit_bytes=_vmem_limit(
                [((FFN_ROWS, d), F32), ((N_ADA, d), F32), (ln_g.shape, F32), (ln_b.shape, F32), ((FFN_ROWS, d), F32)],
                [((d, d_ff), BF16), ((d, d_ff), BF16), ((d_ff, d), BF16), ((FFN_ROWS, d_ff), BF16)],
                [((FFN_ROWS, d), BF16), ((FFN_ROWS, FFN_COLS), F32), ((FFN_ROWS, FFN_COLS), F32),
                 ((FFN_ROWS, d), F32), ((FFN_ROWS, d), F32)])),
        name="ffn",
    )(x2d, mod, ln_g, ln_b, wg, wu, wd)


def _proj_kernel(x_ref, mod_ref, wqk_ref, wvw_ref, wki_ref,
                 q_ref, k_ref, qi_ref, vt_ref, kia_ref, kib_ref, wit_ref, *, mod_row):
    x = x_ref[...]
    sh = mod_ref[0, mod_row:mod_row + 1, :]
    sc = mod_ref[0, mod_row + 1:mod_row + 2, :]
    u = (x * (1.0 + sc) + sh).astype(BF16)
    qkq = jnp.dot(u, wqk_ref[...], preferred_element_type=F32)
    q_ref[...] = (qkq[:, :ATTN_DIM] * (HEAD_DIM ** -0.5 * LOG2E)).astype(BF16)
    k_ref[...] = qkq[:, ATTN_DIM:2 * ATTN_DIM].astype(BF16)
    qi_ref[...] = qkq[:, 2 * ATTN_DIM:].astype(BF16)
    kk = jnp.dot(u, wki_ref[...], preferred_element_type=F32)
    kia_ref[...] = kk[:, :LANES].astype(BF16)
    kib_ref[...] = kk[:, LANES:].astype(BF16)
    vw = lax.dot_general(wvw_ref[...], u, NT_DIMS, preferred_element_type=F32)
    vt = vw[:ATTN_DIM].astype(BF16)
    ones_rows = jnp.where(lax.broadcasted_iota(jnp.int32, (HEAD_PAD, ATT_BLOCK), 0) == 0, 1.0, 0.0).astype(BF16)
    for c in range(vt_ref.shape[1]):
        for h in range(N_HEADS):
            vt_ref[0, c, h * HEAD_ROWS:h * HEAD_ROWS + HEAD_DIM, :] = \
                vt[h * HEAD_DIM:(h + 1) * HEAD_DIM, c * ATT_BLOCK:(c + 1) * ATT_BLOCK]
            vt_ref[0, c, h * HEAD_ROWS + HEAD_DIM:(h + 1) * HEAD_ROWS, :] = ones_rows
    wit_ref[0] = vw[ATTN_DIM:ATTN_DIM + IDX_HEADS, :]


def _attn_proj(x2d, mod, wqk, wvw_t, wki, *, bsz, seq, mod_row):
    n, d = x2d.shape
    tiles_per_seq = seq // PROJ_ROWS
    chunks_per_tile = PROJ_ROWS // ATT_BLOCK
    resident = dict(pipeline_mode=pl.Buffered(1))
    row_spec = lambda cols: pl.BlockSpec((PROJ_ROWS, cols), lambda i: (i, 0))
    return pl.pallas_call(
        functools.partial(_proj_kernel, mod_row=mod_row),
        grid=(n // PROJ_ROWS,),
        in_specs=[
            pl.BlockSpec((PROJ_ROWS, d), lambda i: (i, 0)),
            pl.BlockSpec((1, N_ADA, d), lambda i: (i // tiles_per_seq, 0, 0)),
            pl.BlockSpec(wqk.shape, lambda i: (0, 0), **resident),
            pl.BlockSpec(wvw_t.shape, lambda i: (0, 0), **resident),
            pl.BlockSpec(wki.shape, lambda i: (0, 0), **resident),
        ],
        out_specs=[
            row_spec(ATTN_DIM), row_spec(ATTN_DIM), row_spec(IDX_HEADS * IDX_DIM),
            pl.BlockSpec((1, chunks_per_tile, N_HEADS * HEAD_ROWS, ATT_BLOCK),
                         lambda i: (i // tiles_per_seq, i % tiles_per_seq, 0, 0)),
            row_spec(LANES), row_spec(LANES),
            pl.BlockSpec((1, IDX_HEADS, PROJ_ROWS), lambda i: (i // tiles_per_seq, 0, i % tiles_per_seq)),
        ],
        out_shape=[
            jax.ShapeDtypeStruct((n, ATTN_DIM), BF16),
            jax.ShapeDtypeStruct((n, ATTN_DIM), BF16),
            jax.ShapeDtypeStruct((n, IDX_HEADS * IDX_DIM), BF16),
            jax.ShapeDtypeStruct((bsz, seq // ATT_BLOCK, N_HEADS * HEAD_ROWS, ATT_BLOCK), BF16),
            jax.ShapeDtypeStruct((n, LANES), BF16),
            jax.ShapeDtypeStruct((n, LANES), BF16),
            jax.ShapeDtypeStruct((bsz, IDX_HEADS, seq), F32),
        ],
        compiler_params=pltpu.CompilerParams(
            dimension_semantics=("arbitrary",),
            vmem_limit_bytes=_vmem_limit(
                [((PROJ_ROWS, d), F32), ((N_ADA, d), F32), ((PROJ_ROWS, ATTN_DIM), BF16), ((PROJ_ROWS, ATTN_DIM), BF16),
                 ((PROJ_ROWS, IDX_HEADS * IDX_DIM), BF16), ((chunks_per_tile, N_HEADS * HEAD_ROWS, ATT_BLOCK), BF16),
                 ((PROJ_ROWS, LANES), BF16), ((PROJ_ROWS, LANES), BF16), ((IDX_HEADS, PROJ_ROWS), F32)],
                [(wqk.shape, BF16), (wvw_t.shape, BF16), (wki.shape, BF16)],
                [((PROJ_ROWS, d), BF16), ((PROJ_ROWS, wqk.shape[1]), F32), ((PROJ_ROWS, wki.shape[1]), F32),
                 ((wvw_t.shape[0], PROJ_ROWS), F32)])),
        name="attn_proj",
    )(x2d, mod, wqk, wvw_t, wki)


def _t5_bucket(n):
    max_exact = REL_BUCKETS // 2
    nf = jnp.maximum(n, 1).astype(F32)
    large = max_exact + jnp.floor(jnp.log(nf / max_exact) / math.log(REL_MAX_DIST / max_exact)
                                  * (REL_BUCKETS - max_exact)).astype(jnp.int32)
    large = jnp.minimum(large, REL_BUCKETS - 1)
    return jnp.where(n < max_exact, n, large)


def _far_bucket(first_dist, last_dist):
    n = np.arange(first_dist, last_dist + 1, dtype=np.float32)
    max_exact = REL_BUCKETS // 2
    large = max_exact + (np.log(n / np.float32(max_exact)) / np.float32(math.log(REL_MAX_DIST / max_exact))
                         * np.float32(REL_BUCKETS - max_exact)).astype(np.int32)
    buckets = np.where(n < max_exact, n.astype(np.int32), np.minimum(large, REL_BUCKETS - 1))
    assert buckets.min() == buckets.max(), "key chunks two or more blocks away must share one bias bucket"
    return int(buckets[0])


def _bias_kernel(rb_ref, o_ref, *, far_bucket):
    o = pl.program_id(0)
    row = lax.broadcasted_iota(jnp.int32, (ATT_BLOCK, ATT_BLOCK), 0)
    col = lax.broadcasted_iota(jnp.int32, (ATT_BLOCK, ATT_BLOCK), 1)
    dist = o * ATT_BLOCK + col - row
    bucket = _t5_bucket(jnp.maximum(dist, 0))
    for h in range(N_HEADS):
        tile = jnp.zeros((ATT_BLOCK, ATT_BLOCK), F32)
        for b in range(REL_BUCKETS):
            tile = jnp.where(bucket == b, rb_ref[b, h], tile)
        o_ref[0, h] = (tile - rb_ref[far_bucket, h]) * LOG2E


def _rel_bias_tiles(rel_bias, far_bucket):
    return pl.pallas_call(
        functools.partial(_bias_kernel, far_bucket=far_bucket),
        grid=(2,),
        in_specs=[pl.BlockSpec(memory_space=pltpu.SMEM)],
        out_specs=pl.BlockSpec((1, N_HEADS, ATT_BLOCK, ATT_BLOCK), lambda o: (o, 0, 0, 0)),
        out_shape=jax.ShapeDtypeStruct((2, N_HEADS, ATT_BLOCK, ATT_BLOCK), F32),
        compiler_params=pltpu.CompilerParams(
            dimension_semantics=("arbitrary",),
            vmem_limit_bytes=_vmem_limit([((N_HEADS, ATT_BLOCK, ATT_BLOCK), F32)], [],
                                         [((ATT_BLOCK, ATT_BLOCK), F32)] * 3)),
        name="rel_bias",
    )(rel_bias)


def _attn_kernel(q_ref, qi_ref, wit_ref, k_ref, vt_ref, kia_ref, kib_ref, bt_ref, o_ref,
                 sc_ref, zc_ref, ranked_ref, qh_ref, lg_ref, p_ref, pp_ref, st_ref, acc_ref, *, topk, seq):
    blk = ATT_BLOCK
    groups = blk // SUBLANES
    i = pl.program_id(1)
    nch = i + 1
    kf = float(topk)
    inf = jnp.inf

    def chunk_start(j):
        return j * blk if isinstance(j, int) else pl.multiple_of(j * blk, blk)

    def as_groups(x):
        return x.reshape(groups, SUBLANES, blk)

    def lanes8(v):
        return jnp.broadcast_to(v, (SUBLANES, blk))

    def colmin(x8):
        return jnp.min(x8, axis=0, keepdims=True)

    def colmax(x8):
        return jnp.max(x8, axis=0, keepdims=True)

    def colsum(x8):
        return jnp.sum(x8, axis=0, keepdims=True)

    wf = wit_ref[0] * (IDX_DIM ** -0.5)
    qi = qi_ref[0]
    row = lax.broadcasted_iota(jnp.int32, (blk, blk), 0)
    col = lax.broadcasted_iota(jnp.int32, (blk, blk), 1)
    causal = row <= col

    def chunk_scores(j):
        r0 = chunk_start(j)
        ka = kia_ref[0, pl.ds(r0, blk), :]
        kb = kib_ref[0, pl.ds(r0, blk), :]
        s = jnp.zeros((blk, blk), F32)
        for hp in range(IDX_HEADS // 2):
            qp = qi[:, hp * LANES:(hp + 1) * LANES]
            a0 = lax.dot_general(ka, qp, NT_DIMS, preferred_element_type=F32)
            a1 = lax.dot_general(kb, qp, NT_DIMS, preferred_element_type=F32)
            s = s + jnp.maximum(a0, 0.0) * wf[2 * hp:2 * hp + 1, :]
            s = s + jnp.maximum(a1, 0.0) * wf[2 * hp + 1:2 * hp + 2, :]
        return r0, s * (IDX_HEADS ** -0.5)

    def score_stats(j, s_lo, s_hi, stats):
        mn8, mx8, pos8, zer8 = stats
        hi3 = as_groups(s_hi)
        zer8 = zer8 + jnp.sum(jnp.where(hi3 == 0.0, 1.0, 0.0), axis=0)
        zc_ref[j] = zer8
        return (jnp.minimum(mn8, jnp.min(as_groups(s_lo), axis=0)),
                jnp.maximum(mx8, jnp.max(hi3, axis=0)),
                pos8 + jnp.sum(jnp.where(hi3 > 0.0, 1.0, 0.0), axis=0),
                zer8)

    def score_body(j, stats):
        r0, s = chunk_scores(j)
        sc_ref[pl.ds(r0, blk), :] = s
        return score_stats(j, s, s, stats)

    def score_run(first, count, stats):
        for c in range(count):
            stats = score_body(first + c, stats)
        return stats

    def score_diag(st):
        r_diag, s_diag = chunk_scores(i)
        s_diag_hi = jnp.where(causal, s_diag, -inf)
        sc_ref[pl.ds(r_diag, blk), :] = s_diag_hi
        return score_stats(i, jnp.where(causal, s_diag, inf), s_diag_hi, st)

    zeros8 = jnp.zeros((SUBLANES, blk), F32)
    stats = (jnp.full((SUBLANES, blk), inf, F32), jnp.full((SUBLANES, blk), -inf, F32), zeros8, zeros8)
    stats = lax.fori_loop(0, i // SCORE_RUN, lambda jq, st: score_run(SCORE_RUN * jq, SCORE_RUN, st), stats)
    left = i % SCORE_RUN
    mn8, mx8, pos8, zer8 = lax.switch(
        left, [functools.partial(lambda r, st: score_diag(score_run(i - r, r, st)), r) for r in range(SCORE_RUN)], stats)

    @pl.when(nch % 2 == 1)
    def _():
        sc_ref[pl.ds(chunk_start(nch), blk), :] = jnp.full((blk, blk), -inf, F32)

    npair = (nch + 1) // 2

    def count_gt(thr):
        t8 = lanes8(thr)

        def body(jp, accs):
            base = pl.multiple_of(jp * 2 * blk, 2 * blk)
            accs = list(accs)
            for s in range(2 * blk // COUNT_ROWS):
                x = sc_ref[pl.ds(base + s * COUNT_ROWS, COUNT_ROWS), :]
                for g in range(COUNT_ROWS // SUBLANES):
                    hit = jnp.where(x[g * SUBLANES:(g + 1) * SUBLANES] > t8, 1.0, 0.0)
                    accs[g % COUNT_CHAINS] = accs[g % COUNT_CHAINS] + hit
            return tuple(accs)

        accs = lax.fori_loop(0, npair, body, tuple(jnp.zeros((SUBLANES, blk), F32) for _ in range(COUNT_CHAINS)))
        return colsum(sum(accs[1:], accs[0]))

    def bisect(_, st):
        lo, hi, low, done = st
        mid = 0.5 * lo + 0.5 * hi
        c = count_gt(mid)
        live = done < 0.5
        up = jnp.logical_and(live, c >= kf)
        down = jnp.logical_and(live, c < kf)
        low = jnp.where(up, mid, low)
        lo = jnp.where(up, mid, lo)
        hi = jnp.where(down, mid, hi)
        done = jnp.where(c == kf, 1.0, done)
        return lo, hi, low, done

    tpos = i * blk + lax.broadcasted_iota(jnp.int32, (1, blk), 1)
    pos, zer = colsum(pos8), colsum(zer8)
    mn, mx = colmin(mn8), colmax(mx8)
    zero_tie = jnp.logical_and(pos < kf, pos + zer >= kf)
    pos_ge = pos >= kf
    done0 = jnp.where(jnp.logical_or(jnp.logical_or(tpos + 1 <= topk, zero_tie), pos == kf), 1.0, 0.0)
    low0 = jnp.where(jnp.logical_or(zero_tie, pos_ge), 0.0, -inf)
    lo0 = jnp.where(pos_ge, jnp.maximum(mn, 0.0), mn)
    hi0 = jnp.where(pos_ge, mx, jnp.minimum(mx, 0.0))
    tie0 = jnp.where(zero_tie, 0.0, inf)
    need0 = jnp.where(zero_tie, kf - pos, 0.0)

    def next_value_above(thr):
        t8 = lanes8(thr)

        def body(jp, accs):
            base = pl.multiple_of(jp * 2 * blk, 2 * blk)
            accs = list(accs)
            for s in range(2 * blk // COUNT_ROWS):
                x = sc_ref[pl.ds(base + s * COUNT_ROWS, COUNT_ROWS), :]
                for g in range(COUNT_ROWS // SUBLANES):
                    xg = x[g * SUBLANES:(g + 1) * SUBLANES]
                    accs[g % COUNT_CHAINS] = jnp.minimum(accs[g % COUNT_CHAINS], jnp.where(xg > t8, xg, inf))
            return tuple(accs)

        accs = lax.fori_loop(0, npair, body, tuple(jnp.full((SUBLANES, blk), inf, F32) for _ in range(COUNT_CHAINS)))
        return colmin(functools.reduce(jnp.minimum, accs))

    def climb_cond(st):
        return jnp.min(st[1]) < 0.5

    def climb_body(st):
        low, done, tie, need = st
        live = done < 0.5
        cand = next_value_above(low)
        cgt = count_gt(cand)
        found = jnp.logical_and(live, cgt < kf)
        tie = jnp.where(found, cand, tie)
        need = jnp.where(found, kf - cgt, need)
        low = jnp.where(live, cand, low)
        done = jnp.where(jnp.logical_and(live, cgt <= kf), 1.0, done)
        return low, done, tie, need

    def select(_):
        _, _, low, done = lax.fori_loop(0, BISECT_STEPS, bisect, (lo0, hi0, low0, done0))
        low, _, tie, need = lax.while_loop(climb_cond, climb_body, (low, done, tie0, need0))
        return low, tie, need

    low, tie, need = lax.cond(jnp.min(done0) < 0.5, select, lambda _: (low0, tie0, need0), 0)

    def plain_mask(_):
        low8 = lanes8(low)

        def body(j, _):
            r0 = chunk_start(j)
            x = as_groups(sc_ref[pl.ds(r0, blk), :])
            sc_ref[pl.ds(r0, blk), :] = jnp.where(x > low8[None], 0.0, -inf).reshape(blk, blk)
            return 0

        return lax.fori_loop(0, nch, body, 0)

    def ranked_chunk(j, tri, base):
        r0 = chunk_start(j)
        x = sc_ref[pl.ds(r0, blk), :]
        eq = x == tie
        rank = jnp.dot(tri, jnp.where(eq, 1.0, 0.0).astype(BF16), preferred_element_type=F32)
        sel = jnp.logical_or(x > low, jnp.logical_and(eq, rank + base <= need))
        sc_ref[pl.ds(r0, blk), :] = jnp.where(sel, 0.0, -inf)
        return rank[blk - 1:blk, :]

    def tie_mask(_):
        tri = jnp.where(row >= col, 1.0, 0.0).astype(BF16)
        lax.fori_loop(0, nch, lambda j, base: base + ranked_chunk(j, tri, base), jnp.zeros((1, blk), F32))
        return 0

    def zero_tie_mask(_):
        tri = jnp.where(row >= col, 1.0, 0.0).astype(BF16)

        def cut_body(j, ncut):
            return ncut + jnp.where(colsum(zc_ref[j]) < need, 1.0, 0.0)

        ncut = lax.fori_loop(0, nch, cut_body, jnp.zeros((1, blk), F32))
        cut = jnp.where(tie < inf, ncut, inf)

        def flag_body(j, _):
            ranked_ref[j] = 0
            return 0

        lax.fori_loop(0, nch, flag_body, 0)

        def cut_cond(pending):
            return jnp.min(pending) < inf

        def cut_chunk(pending):
            jf = jnp.min(pending)
            j = jf.astype(jnp.int32)
            base = jnp.where(j > 0, colsum(zc_ref[jnp.maximum(j - 1, 0)]), 0.0)
            ranked_chunk(j, tri, base)
            ranked_ref[j] = 1
            return jnp.where(pending == jf, inf, pending)

        lax.while_loop(cut_cond, cut_chunk, cut)
        low8, cut8 = lanes8(low), lanes8(cut)

        def body(j, _):
            @pl.when(ranked_ref[j] == 0)
            def _():
                r0 = chunk_start(j)
                x = as_groups(sc_ref[pl.ds(r0, blk), :])
                at_low = jnp.where(jnp.logical_and(cut8 < inf, cut8 > j.astype(F32)), 0.0, -inf)
                out = jnp.where(x > low8[None], 0.0, jnp.where(x == low8[None], at_low[None], -inf))
                sc_ref[pl.ds(r0, blk), :] = out.reshape(blk, blk)
            return 0

        return lax.fori_loop(0, nch, body, 0)

    tied = tie < inf
    mask_kind = jnp.where(jnp.max(jnp.where(tied, 1.0, 0.0)) < 0.5, 0,
                          jnp.where(jnp.max(jnp.where(jnp.logical_and(tied, tie != 0.0), 1.0, 0.0)) < 0.5, 1, 2))
    lax.switch(mask_kind, [plain_mask, zero_tie_mask, tie_mask], 0)

    q = q_ref[0]
    lane = lax.broadcasted_iota(jnp.int32, (blk, LANES), 1)
    for h in range(N_HEADS):
        hp = h // 2
        in_head = (lane < HEAD_DIM) if h % 2 == 0 else (lane >= HEAD_DIM)
        qh_ref[h] = jnp.where(in_head, q[:, hp * LANES:(hp + 1) * LANES], jnp.zeros((), BF16))

    def logits(j, h, mb, near):
        hp = h // 2
        kc = k_ref[0, pl.ds(chunk_start(j), blk), hp * LANES:(hp + 1) * LANES]
        lg = lax.dot_general(kc, qh_ref[h], NT_DIMS, preferred_element_type=F32) + mb
        return lg if near is None else lg + bt_ref[near, h]

    CMAX, RMAX, RESC = 0, 1, 2

    def stage_logits(j, near):
        mb = sc_ref[pl.ds(chunk_start(j), blk), :]
        cm8 = []
        for h in range(N_HEADS):
            lg = logits(j, h, mb, near)
            lg_ref[h] = lg
            cm8.append(jnp.max(as_groups(lg), axis=0))
        for h in range(N_HEADS):
            st_ref[CMAX, h] = lanes8(colmax(cm8[h]))

    def stage_exp():
        for h in range(N_HEADS):
            m_old = st_ref[RMAX, h]
            m_new = jnp.maximum(m_old, st_ref[CMAX, h])
            m_use = jnp.where(m_new == -inf, 0.0, m_new)
            p = jnp.exp2(as_groups(lg_ref[h]) - m_use[None])
            p_ref[h] = p.reshape(blk, blk).astype(BF16)
            st_ref[RMAX, h] = m_new
            st_ref[RESC, h] = jnp.exp2(m_old - m_use)

    def stage_pv(j):
        for h in range(N_HEADS):
            rows = slice(h * HEAD_ROWS, (h + 1) * HEAD_ROWS)
            pv = jnp.dot(vt_ref[0, j, rows, :], p_ref[h], preferred_element_type=F32)
            acc = acc_ref[rows, :].reshape(HEAD_ROWS // SUBLANES, SUBLANES, blk) * st_ref[RESC, h][None]
            acc_ref[rows, :] = acc.reshape(HEAD_ROWS, blk) + pv

    def pipelined(j, near_next):
        stage_exp()
        stage_logits(j + 1, near_next)
        stage_pv(j)

    def exact_attention():
        acc_ref[...] = jnp.zeros(acc_ref.shape, F32)
        st_ref[RMAX] = jnp.full((N_HEADS, SUBLANES, blk), -inf, F32)

        def last_chunks(first):
            for j in range(first, 0):
                pipelined(i + j, -(j + 1))
            stage_exp()
            stage_pv(i)

        @pl.when(i == 0)
        def _():
            stage_logits(0, 0)
            last_chunks(0)

        @pl.when(i == 1)
        def _():
            stage_logits(0, 1)
            last_chunks(-1)

        @pl.when(i >= 2)
        def _():
            stage_logits(0, None)

        def far_body(j, _):
            pipelined(j, None)
            return 0

        lax.fori_loop(0, jnp.maximum(i - 2, 0), far_body, 0)

        @pl.when(i >= 2)
        def _():
            last_chunks(-2)

    acc_ref[...] = jnp.zeros(acc_ref.shape, F32)
    st_ref[RMAX] = jnp.zeros((N_HEADS, SUBLANES, blk), F32)

    def fast_run(chunks):
        for c, (j, near) in enumerate(chunks):
            mb = sc_ref[pl.ds(chunk_start(j), blk), :]
            for h in range(N_HEADS):
                p = jnp.exp2(as_groups(logits(j, h, mb, near)) - st_ref[RMAX, h][None])
                pp_ref[c, h] = p.reshape(blk, blk).astype(BF16)
        for h in range(N_HEADS):
            rows = slice(h * HEAD_ROWS, (h + 1) * HEAD_ROWS)
            tot = acc_ref[rows, :]
            for c, (j, _) in enumerate(chunks):
                tot = tot + jnp.dot(vt_ref[0, j, rows, :], pp_ref[c, h], preferred_element_type=F32)
            den = tot[HEAD_DIM:HEAD_DIM + 1, :]
            up = lanes8(jnp.where(den > 0.0, jnp.floor(jnp.log2(den)), 0.0))
            scaled = tot.reshape(HEAD_ROWS // SUBLANES, SUBLANES, blk) * jnp.exp2(-up)[None]
            acc_ref[rows, :] = scaled.reshape(HEAD_ROWS, blk)
            st_ref[RMAX, h] = st_ref[RMAX, h] + up

    def far_run(jr, _):
        fast_run([(FAST_RUN * jr + c, None) for c in range(FAST_RUN)])
        return 0

    n_far = jnp.maximum(i - 1, 0)
    lax.fori_loop(0, n_far // FAST_RUN, far_run, 0)

    @pl.when(i == 0)
    def _():
        fast_run([(0, 0)])

    for left in range(FAST_RUN):
        @pl.when(jnp.logical_and(i >= 1, n_far % FAST_RUN == left))
        def _():
            last = [(i - 1 - left + c, None) for c in range(left)] + [(i - 1, 1), (i, 0)]
            for first in range(0, len(last), FAST_RUN):
                fast_run(last[first:first + FAST_RUN])

    acc = acc_ref[...]
    finite = jnp.min(jnp.where(jnp.isfinite(acc), 1.0, 0.0))
    dens = jnp.concatenate([acc[h * HEAD_ROWS + HEAD_DIM:h * HEAD_ROWS + HEAD_DIM + 1, :] for h in range(N_HEADS)], axis=0)
    usable = jnp.logical_and(finite > 0.5, jnp.min(dens) > 0.0)
    lax.cond(usable, lambda: None, exact_attention)

    heads = []
    for h in range(N_HEADS):
        r0 = h * HEAD_ROWS
        heads.append(acc_ref[r0:r0 + HEAD_DIM, :] / acc_ref[r0 + HEAD_DIM:r0 + HEAD_DIM + 1, :])
    o_ref[0] = jnp.concatenate(heads, axis=0).T.astype(BF16)


def _attention(q, qi, wit, k, vt, kia, kib, btiles, *, bsz, seq, topk):
    blk = ATT_BLOCK
    nblk = seq // blk
    return pl.pallas_call(
        functools.partial(_attn_kernel, topk=topk, seq=seq),
        grid=(bsz, nblk),
        in_specs=[
            pl.BlockSpec((1, blk, ATTN_DIM), lambda b, i: (b, i, 0)),
            pl.BlockSpec((1, blk, IDX_HEADS * IDX_DIM), lambda b, i: (b, i, 0)),
            pl.BlockSpec((1, IDX_HEADS, blk), lambda b, i: (b, 0, i)),
            pl.BlockSpec((1, seq, ATTN_DIM), lambda b, i: (b, 0, 0)),
            pl.BlockSpec((1, nblk, N_HEADS * HEAD_ROWS, blk), lambda b, i: (b, 0, 0, 0)),
            pl.BlockSpec((1, seq, LANES), lambda b, i: (b, 0, 0)),
            pl.BlockSpec((1, seq, LANES), lambda b, i: (b, 0, 0)),
            pl.BlockSpec(btiles.shape, lambda b, i: (0, 0, 0, 0)),
        ],
        out_specs=pl.BlockSpec((1, blk, ATTN_DIM), lambda b, i: (b, i, 0)),
        out_shape=jax.ShapeDtypeStruct((bsz, seq, ATTN_DIM), BF16),
        scratch_shapes=[
            pltpu.VMEM((seq + blk, blk), F32),
            pltpu.VMEM((nblk, SUBLANES, blk), F32),
            pltpu.SMEM((nblk,), jnp.int32),
            pltpu.VMEM((N_HEADS, blk, LANES), BF16),
            pltpu.VMEM((N_HEADS, blk, blk), F32),
            pltpu.VMEM((N_HEADS, blk, blk), BF16),
            pltpu.VMEM((FAST_RUN, N_HEADS, blk, blk), BF16),
            pltpu.VMEM((3, N_HEADS, SUBLANES, blk), F32),
            pltpu.VMEM((N_HEADS * HEAD_ROWS, blk), F32),
        ],
        compiler_params=pltpu.CompilerParams(
            dimension_semantics=("arbitrary", "arbitrary"),
            vmem_limit_bytes=_vmem_limit(
                [((blk, ATTN_DIM), BF16), ((blk, IDX_HEADS * IDX_DIM), BF16), ((IDX_HEADS, blk), F32),
                 ((seq, ATTN_DIM), BF16), ((nblk, N_HEADS * HEAD_ROWS, blk), BF16), ((seq, LANES), BF16),
                 ((seq, LANES), BF16), (btiles.shape, F32), ((blk, ATTN_DIM), BF16)],
                [((seq + blk, blk), F32), ((nblk, SUBLANES, blk), F32), ((N_HEADS, blk, LANES), BF16),
                 ((N_HEADS, blk, blk), F32), ((N_HEADS, blk, blk), BF16), ((FAST_RUN, N_HEADS, blk, blk), BF16),
                 ((3, N_HEADS, SUBLANES, blk), F32),
                 ((N_HEADS * HEAD_ROWS, blk), F32)],
                [((blk, blk), F32)] * 4 + [((blk, blk), BF16), ((ATTN_DIM, blk), F32), ((blk, ATTN_DIM), F32)])),
        name="attention",
    )(q, qi, wit, k, vt, kia, kib, btiles)


def _mix_kernel(x_ref, mod_ref, att_ref, wpg_ref, wpool_ref, ps_ref, wa_ref, wb_ref, wo_ref,
                lng_ref, lnb_ref, o_ref, pe_ref, mix_ref, *, alpha, mod_row, ln_row, tiles_per_seq):
    rows = x_ref.shape[0]
    i = pl.program_id(0)
    seq_tile = i % tiles_per_seq

    @pl.when(i == 0)
    def _():
        pe_ref[rows:, :] = jnp.zeros((POOL_HALO, POOL_DIM), F32)

    x = x_ref[...]
    sh = mod_ref[0, mod_row:mod_row + 1, :]
    sc = mod_ref[0, mod_row + 1:mod_row + 2, :]
    gate = mod_ref[0, mod_row + 2:mod_row + 3, :]
    u = (x * (1.0 + sc) + sh).astype(BF16)
    pg = jnp.dot(u, wpg_ref[...], preferred_element_type=F32)
    pe_ref[0:POOL_HALO, :] = jnp.where(seq_tile == 0, 0.0, pe_ref[rows:, :])
    pe_ref[POOL_HALO:, :] = pg[:, :POOL_DIM]
    t = seq_tile * rows + lax.broadcasted_iota(jnp.int32, (rows, 1), 0)
    for g, w in enumerate(POOL_WINDOWS):
        cols = slice(g * POOL_GROUP_DIM, (g + 1) * POOL_GROUP_DIM)
        cur = pe_ref[POOL_HALO:, cols]
        win = cur
        for back in range(1, w):
            win = win + pe_ref[POOL_HALO - back:POOL_HALO - back + rows, cols]
        cnt = jnp.minimum(t + 1, w).astype(F32)
        pooled = (win / cnt - cur).astype(BF16)
        mixed = jnp.dot(pooled, wpool_ref[g], preferred_element_type=F32)
        mix_ref[:, cols] = (mixed * ps_ref[:, cols]).astype(BF16)
    y_a = jnp.dot(mix_ref[...], wa_ref[...], preferred_element_type=F32)
    y_b = jnp.dot(att_ref[...], wb_ref[...], preferred_element_type=F32)
    d = x.shape[1]
    ga = pg[:, POOL_DIM:POOL_DIM + d]
    gb = pg[:, POOL_DIM + d:]
    merged = (jax.nn.sigmoid(ga) * y_a + jax.nn.sigmoid(gb) * y_b).astype(BF16)
    y = jnp.dot(merged, wo_ref[...], preferred_element_type=F32)
    z = alpha * x + gate * y
    o_ref[...] = _layer_norm(z, lng_ref[ln_row:ln_row + 1, :], lnb_ref[ln_row:ln_row + 1, :])


def _mix_out(x2d, mod, att2d, wpg, wpool, pool_scale, wa, wb, wo, ln_g, ln_b, *, seq, alpha, mod_row, ln_row):
    n, d = x2d.shape
    tiles_per_seq = seq // MIX_ROWS
    resident = dict(pipeline_mode=pl.Buffered(1))
    full = lambda a: pl.BlockSpec(a.shape, lambda i: (0,) * a.ndim, **resident)
    return pl.pallas_call(
        functools.partial(_mix_kernel, alpha=alpha, mod_row=mod_row, ln_row=ln_row, tiles_per_seq=tiles_per_seq),
        grid=(n // MIX_ROWS,),
        in_specs=[
            pl.BlockSpec((MIX_ROWS, d), lambda i: (i, 0)),
            pl.BlockSpec((1, N_ADA, d), lambda i: (i // tiles_per_seq, 0, 0)),
            pl.BlockSpec((MIX_ROWS, ATTN_DIM), lambda i: (i, 0)),
            full(wpg), full(wpool), full(pool_scale), full(wa), full(wb), full(wo),
            pl.BlockSpec(ln_g.shape, lambda i: (0, 0)),
            pl.BlockSpec(ln_b.shape, lambda i: (0, 0)),
        ],
        out_specs=pl.BlockSpec((MIX_ROWS, d), lambda i: (i, 0)),
        out_shape=jax.ShapeDtypeStruct((n, d), F32),
        scratch_shapes=[
            pltpu.VMEM((POOL_HALO + MIX_ROWS, POOL_DIM), F32),
            pltpu.VMEM((MIX_ROWS, POOL_DIM), BF16),
        ],
        compiler_params=pltpu.CompilerParams(
            dimension_semantics=("arbitrary",),
            vmem_limit_bytes=_vmem_limit(
                [((MIX_ROWS, d), F32), ((N_ADA, d), F32), ((MIX_ROWS, ATTN_DIM), BF16), (ln_g.shape, F32),
                 (ln_b.shape, F32), ((MIX_ROWS, d), F32)],
                [(wpg.shape, BF16), (wpool.shape, BF16), (pool_scale.shape, F32), (wa.shape, BF16), (wb.shape, BF16),
                 (wo.shape, BF16), ((POOL_HALO + MIX_ROWS, POOL_DIM), F32), ((MIX_ROWS, POOL_DIM), BF16)],
                [((MIX_ROWS, d), BF16), ((MIX_ROWS, wpg.shape[1]), F32)] + [((MIX_ROWS, d), F32)] * 5)),
        name="mix_out",
    )(x2d, mod, att2d, wpg, wpool, pool_scale, wa, wb, wo, ln_g, ln_b)


def kernel(x, c, w_ada, b_ada, ln_g, ln_b, ffn1_w_gate, ffn1_w_up, ffn1_w_down, w_in, w_pool, pool_scale,
           w_a, w_b, w_out, rel_bias, ffn2_w_gate, ffn2_w_up, ffn2_w_down):
    bsz, seq, d = x.shape
    depth = w_ada.shape[0]
    alpha = (2.0 * depth) ** 0.25
    topk = min(TOP_K, seq // 4)
    assert seq % FFN_ROWS == 0 and seq % PROJ_ROWS == 0 and seq % MIX_ROWS == 0 and seq % ATT_BLOCK == 0
    assert PROJ_ROWS % ATT_BLOCK == 0 and POOL_HALO >= max(POOL_WINDOWS) - 1
    far_bucket = _far_bucket(ATT_BLOCK + 1, max(seq - 1, ATT_BLOCK + 1))

    o_q = POOL_DIM
    o_k = o_q + ATTN_DIM
    o_v = o_k + ATTN_DIM
    o_qi = o_v + ATTN_DIM
    o_ki = o_qi + IDX_HEADS * IDX_DIM
    o_wi = o_ki + IDX_DIM
    o_ga = o_wi + IDX_HEADS

    btiles = _rel_bias_tiles(rel_bias, far_bucket)
    x2d = x.reshape(bsz * seq, d)
    for l in range(depth):
        wl = w_in[l]
        zeros_ki = jnp.zeros((d, LANES - IDX_DIM), wl.dtype)
        w_ki = wl[:, o_ki:o_wi]
        wqk = jnp.concatenate([wl[:, o_q:o_v], wl[:, o_qi:o_ki]], axis=1).astype(BF16)
        wvw_t = jnp.pad(jnp.concatenate([wl[:, o_v:o_qi], wl[:, o_wi:o_ga]], axis=1).T,
                        ((0, 2 * SUBLANES - IDX_HEADS), (0, 0))).astype(BF16)
        wki = jnp.concatenate([w_ki, zeros_ki, zeros_ki, w_ki], axis=1).astype(BF16)
        wpg = jnp.concatenate([wl[:, :POOL_DIM], wl[:, o_ga:]], axis=1).astype(BF16)

        mod = _ada_mod(c, w_ada[l], b_ada[l]).reshape(bsz, N_ADA, d)
        x2d = _ffn(x2d, mod, ln_g[l], ln_b[l], ffn1_w_gate[l].astype(BF16), ffn1_w_up[l].astype(BF16),
                   ffn1_w_down[l].astype(BF16), seq=seq, alpha=alpha, mod_row=0, ln_row=0)
        q, k, qi, vt, kia, kib, wit = _attn_proj(x2d, mod, wqk, wvw_t, wki, bsz=bsz, seq=seq, mod_row=3)
        att = _attention(q.reshape(bsz, seq, ATTN_DIM), qi.reshape(bsz, seq, IDX_HEADS * IDX_DIM), wit,
                         k.reshape(bsz, seq, ATTN_DIM), vt, kia.reshape(bsz, seq, LANES),
                         kib.reshape(bsz, seq, LANES), btiles, bsz=bsz, seq=seq, topk=topk)
        x2d = _mix_out(x2d, mod, att.reshape(bsz * seq, ATTN_DIM), wpg, w_pool[l].astype(BF16),
                       pool_scale[l].reshape(1, POOL_DIM), w_a[l].astype(BF16), w_b[l].astype(BF16),
                       w_out[l].astype(BF16), ln_g[l], ln_b[l], seq=seq, alpha=alpha, mod_row=3, ln_row=1)
        x2d = _ffn(x2d, mod, ln_g[l], ln_b[l], ffn2_w_gate[l].astype(BF16), ffn2_w_up[l].astype(BF16),
                   ffn2_w_down[l].astype(BF16), seq=seq, alpha=alpha, mod_row=6, ln_row=2)
    return x2d.reshape(bsz, seq, d)
```

```python
import functools
import math

import numpy as np
import jax
import jax.numpy as jnp
from jax import lax
from jax.experimental import pallas as pl
from jax.experimental.pallas import tpu as pltpu

POOL_WINDOWS = (2, 4, 8, 16)
POOL_GROUP_DIM = 128
POOL_DIM = len(POOL_WINDOWS) * POOL_GROUP_DIM
N_HEADS = 8
HEAD_DIM = 64
ATTN_DIM = N_HEADS * HEAD_DIM
HEAD_PAD = 16
HEAD_ROWS = HEAD_DIM + HEAD_PAD
IDX_HEADS = 8
IDX_DIM = 64
TOP_K = 256
REL_BUCKETS = 32
REL_MAX_DIST = 128
N_ADA = 9
LN_EPS = 1e-5
POOL_HALO = 16

LANES = 128
SUBLANES = 8
FFN_ROWS = 512
FFN_COLS = 256
PROJ_ROWS = 512
MIX_ROWS = 512
ATT_BLOCK = 256
ADA_COLS = 1024
W_STAGE_CHUNKS = 8
SCORE_RUN = 4
FAST_RUN = 4
COUNT_CHAINS = 4
COUNT_ROWS = 64
BISECT_STEPS = 14

LOG2E = math.log2(math.e)
BF16 = jnp.bfloat16
F32 = jnp.float32
NT_DIMS = (((1,), (1,)), ((), ()))


def _tile_bytes(shape, dtype):
    itemsize = jnp.dtype(dtype).itemsize
    sublanes = SUBLANES * (4 // itemsize)
    shape = (1,) * (2 - len(shape)) + tuple(shape)
    rows = -(-shape[-2] // sublanes) * sublanes
    cols = -(-shape[-1] // LANES) * LANES
    return math.prod(shape[:-2]) * rows * cols * itemsize


def _vmem_limit(pipelined, resident, temporaries):
    return (2 * sum(_tile_bytes(*b) for b in pipelined) + sum(_tile_bytes(*b) for b in resident)
            + sum(_tile_bytes(*b) for b in temporaries))


def _layer_norm(z, g, b):
    mu = jnp.mean(z, axis=-1, keepdims=True)
    zc = z - mu
    var = jnp.mean(zc * zc, axis=-1, keepdims=True)
    return zc * lax.rsqrt(var + LN_EPS) * g + b


def _silu(a):
    return a * jax.nn.sigmoid(a)


def _ada_kernel(c_ref, w_ref, b_ref, o_ref):
    a = _silu(c_ref[...])
    o_ref[...] = jnp.dot(a, w_ref[...], preferred_element_type=F32) + b_ref[...]


def _ada_mod(c, w_ada, b_ada):
    bsz, d = c.shape
    n = w_ada.shape[1]
    rows = -(-bsz // SUBLANES) * SUBLANES
    c_pad = jnp.pad(c, ((0, rows - bsz), (0, 0)))
    out = pl.pallas_call(
        _ada_kernel,
        grid=(n // ADA_COLS,),
        in_specs=[
            pl.BlockSpec((rows, d), lambda j: (0, 0)),
            pl.BlockSpec((d, ADA_COLS), lambda j: (0, j)),
            pl.BlockSpec((1, ADA_COLS), lambda j: (0, j)),
        ],
        out_specs=pl.BlockSpec((rows, ADA_COLS), lambda j: (0, j)),
        out_shape=jax.ShapeDtypeStruct((rows, n), F32),
        compiler_params=pltpu.CompilerParams(
            dimension_semantics=("arbitrary",),
            vmem_limit_bytes=_vmem_limit([((rows, d), F32), ((d, ADA_COLS), F32), ((1, ADA_COLS), F32),
                                          ((rows, ADA_COLS), F32)], [], [((rows, ADA_COLS), F32)])),
        name="ada_mod",
    )(c_pad, w_ada, b_ada.reshape(1, n))
    return out[:bsz]


def _load_as_bf16(src_hbm, dst_ref, stage_ref, sem, chunk_rows):
    n_chunks = src_hbm.shape[0] // chunk_rows

    def chunk_copy(c):
        return pltpu.make_async_copy(src_hbm.at[pl.ds(c * chunk_rows, chunk_rows)], stage_ref.at[c % 2], sem.at[c % 2])

    chunk_copy(0).start()
    for c in range(n_chunks):
        if c + 1 < n_chunks:
            chunk_copy(c + 1).start()
        chunk_copy(c).wait()
        dst_ref[pl.ds(c * chunk_rows, chunk_rows), :] = stage_ref[c % 2].astype(BF16)


def _ffn_kernel(x_ref, mod_ref, lng_ref, lnb_ref, wg_hbm, wu_hbm, wd_hbm, o_ref,
                h_ref, wg_ref, wu_ref, wd_ref, stage_in_ref, stage_out_ref, sem_in, sem_out, *, alpha, mod_row, ln_row):
    @pl.when(pl.program_id(0) == 0)
    def _():
        _load_as_bf16(wg_hbm, wg_ref, stage_in_ref, sem_in, stage_in_ref.shape[1])
        _load_as_bf16(wu_hbm, wu_ref, stage_in_ref, sem_in, stage_in_ref.shape[1])
        _load_as_bf16(wd_hbm, wd_ref, stage_out_ref, sem_out, stage_out_ref.shape[1])

    x = x_ref[...]
    sh = mod_ref[0, mod_row:mod_row + 1, :]
    sc = mod_ref[0, mod_row + 1:mod_row + 2, :]
    gate = mod_ref[0, mod_row + 2:mod_row + 3, :]
    u = (x * (1.0 + sc) + sh).astype(BF16)
    d_ff = wg_ref.shape[1]
    for c in range(d_ff // FFN_COLS):
        sl = slice(c * FFN_COLS, (c + 1) * FFN_COLS)
        a = jnp.dot(u, wg_ref[:, sl], preferred_element_type=F32)
        b = jnp.dot(u, wu_ref[:, sl], preferred_element_type=F32)
        h_ref[:, sl] = (_silu(a) * b).astype(BF16)
    y = jnp.dot(h_ref[...], wd_ref[...], preferred_element_type=F32)
    z = alpha * x + (0.5 * gate) * y
    o_ref[...] = _layer_norm(z, lng_ref[ln_row:ln_row + 1, :], lnb_ref[ln_row:ln_row + 1, :])


def _ffn(x2d, mod, ln_g, ln_b, wg, wu, wd, *, seq, alpha, mod_row, ln_row):
    n, d = x2d.shape
    d_ff = wg.shape[1]
    tiles_per_seq = seq // FFN_ROWS
    stage_in = (2, d // W_STAGE_CHUNKS, d_ff)
    stage_out = (2, d_ff // W_STAGE_CHUNKS, d)
    return pl.pallas_call(
        functools.partial(_ffn_kernel, alpha=alpha, mod_row=mod_row, ln_row=ln_row),
        grid=(n // FFN_ROWS,),
        in_specs=[
            pl.BlockSpec((FFN_ROWS, d), lambda i: (i, 0)),
            pl.BlockSpec((1, N_ADA, d), lambda i: (i // tiles_per_seq, 0, 0)),
            pl.BlockSpec(ln_g.shape, lambda i: (0, 0)),
            pl.BlockSpec(ln_b.shape, lambda i: (0, 0)),
            pl.BlockSpec(memory_space=pl.ANY),
            pl.BlockSpec(memory_space=pl.ANY),
            pl.BlockSpec(memory_space=pl.ANY),
        ],
        out_specs=pl.BlockSpec((FFN_ROWS, d), lambda i: (i, 0)),
        out_shape=jax.ShapeDtypeStruct((n, d), F32),
        scratch_shapes=[pltpu.VMEM((FFN_ROWS, d_ff), BF16),
                        pltpu.VMEM((d, d_ff), BF16), pltpu.VMEM((d, d_ff), BF16), pltpu.VMEM((d_ff, d), BF16),
                        pltpu.VMEM(stage_in, F32), pltpu.VMEM(stage_out, F32),
                        pltpu.SemaphoreType.DMA((2,)), pltpu.SemaphoreType.DMA((2,))],
        compiler_params=pltpu.CompilerParams(
            dimension_semantics=("arbitrary",),
            vmem_limit_bytes=_vmem_limit(
                [((FFN_ROWS, d), F32), ((N_ADA, d), F32), (ln_g.shape, F32), (ln_b.shape, F32), ((FFN_ROWS, d), F32)],
                [((d, d_ff), BF16), ((d, d_ff), BF16), ((d_ff, d), BF16), ((FFN_ROWS, d_ff), BF16),
                 (stage_in, F32), (stage_out, F32)],
                [((FFN_ROWS, d), BF16), ((FFN_ROWS, FFN_COLS), F32), ((FFN_ROWS, FFN_COLS), F32),
                 ((FFN_ROWS, d), F32), ((FFN_ROWS, d), F32)])),
        name="ffn",
    )(x2d, mod, ln_g, ln_b, wg, wu, wd)


def _proj_kernel(x_ref, mod_ref, wqk_ref, wvw_ref, wki_ref,
                 q_ref, k_ref, qi_ref, vt_ref, kia_ref, kib_ref, wit_ref, *, mod_row):
    x = x_ref[...]
    sh = mod_ref[0, mod_row:mod_row + 1, :]
    sc = mod_ref[0, mod_row + 1:mod_row + 2, :]
    u = (x * (1.0 + sc) + sh).astype(BF16)
    qkq = jnp.dot(u, wqk_ref[...], preferred_element_type=F32)
    q_ref[...] = (qkq[:, :ATTN_DIM] * (HEAD_DIM ** -0.5 * LOG2E)).astype(BF16)
    k_ref[...] = qkq[:, ATTN_DIM:2 * ATTN_DIM].astype(BF16)
    qi_ref[...] = qkq[:, 2 * ATTN_DIM:].astype(BF16)
    kk = jnp.dot(u, wki_ref[...], preferred_element_type=F32)
    kia_ref[...] = kk[:, :LANES].astype(BF16)
    kib_ref[...] = kk[:, LANES:].astype(BF16)
    vw = lax.dot_general(wvw_ref[...], u, NT_DIMS, preferred_element_type=F32)
    vt = vw[:ATTN_DIM].astype(BF16)
    ones_rows = jnp.where(lax.broadcasted_iota(jnp.int32, (HEAD_PAD, ATT_BLOCK), 0) == 0, 1.0, 0.0).astype(BF16)
    for c in range(vt_ref.shape[1]):
        for h in range(N_HEADS):
            vt_ref[0, c, h * HEAD_ROWS:h * HEAD_ROWS + HEAD_DIM, :] = \
                vt[h * HEAD_DIM:(h + 1) * HEAD_DIM, c * ATT_BLOCK:(c + 1) * ATT_BLOCK]
            vt_ref[0, c, h * HEAD_ROWS + HEAD_DIM:(h + 1) * HEAD_ROWS, :] = ones_rows
    wit_ref[0] = vw[ATTN_DIM:ATTN_DIM + IDX_HEADS, :]


def _attn_proj(x2d, mod, wqk, wvw_t, wki, *, bsz, seq, mod_row):
    n, d = x2d.shape
    tiles_per_seq = seq // PROJ_ROWS
    chunks_per_tile = PROJ_ROWS // ATT_BLOCK
    resident = dict(pipeline_mode=pl.Buffered(1))
    row_spec = lambda cols: pl.BlockSpec((PROJ_ROWS, cols), lambda i: (i, 0))
    return pl.pallas_call(
        functools.partial(_proj_kernel, mod_row=mod_row),
        grid=(n // PROJ_ROWS,),
        in_specs=[
            pl.BlockSpec((PROJ_ROWS, d), lambda i: (i, 0)),
            pl.BlockSpec((1, N_ADA, d), lambda i: (i // tiles_per_seq, 0, 0)),
            pl.BlockSpec(wqk.shape, lambda i: (0, 0), **resident),
            pl.BlockSpec(wvw_t.shape, lambda i: (0, 0), **resident),
            pl.BlockSpec(wki.shape, lambda i: (0, 0), **resident),
        ],
        out_specs=[
            row_spec(ATTN_DIM), row_spec(ATTN_DIM), row_spec(IDX_HEADS * IDX_DIM),
            pl.BlockSpec((1, chunks_per_tile, N_HEADS * HEAD_ROWS, ATT_BLOCK),
                         lambda i: (i // tiles_per_seq, i % tiles_per_seq, 0, 0)),
            row_spec(LANES), row_spec(LANES),
            pl.BlockSpec((1, IDX_HEADS, PROJ_ROWS), lambda i: (i // tiles_per_seq, 0, i % tiles_per_seq)),
        ],
        out_shape=[
            jax.ShapeDtypeStruct((n, ATTN_DIM), BF16),
            jax.ShapeDtypeStruct((n, ATTN_DIM), BF16),
            jax.ShapeDtypeStruct((n, IDX_HEADS * IDX_DIM), BF16),
            jax.ShapeDtypeStruct((bsz, seq // ATT_BLOCK, N_HEADS * HEAD_ROWS, ATT_BLOCK), BF16),
            jax.ShapeDtypeStruct((n, LANES), BF16),
            jax.ShapeDtypeStruct((n, LANES), BF16),
            jax.ShapeDtypeStruct((bsz, IDX_HEADS, seq), F32),
        ],
        compiler_params=pltpu.CompilerParams(
            dimension_semantics=("arbitrary",),
            vmem_limit_bytes=_vmem_limit(
                [((PROJ_ROWS, d), F32), ((N_ADA, d), F32), ((PROJ_ROWS, ATTN_DIM), BF16), ((PROJ_ROWS, ATTN_DIM), BF16),
                 ((PROJ_ROWS, IDX_HEADS * IDX_DIM), BF16), ((chunks_per_tile, N_HEADS * HEAD_ROWS, ATT_BLOCK), BF16),
                 ((PROJ_ROWS, LANES), BF16), ((PROJ_ROWS, LANES), BF16), ((IDX_HEADS, PROJ_ROWS), F32)],
                [(wqk.shape, BF16), (wvw_t.shape, BF16), (wki.shape, BF16)],
                [((PROJ_ROWS, d), BF16), ((PROJ_ROWS, wqk.shape[1]), F32), ((PROJ_ROWS, wki.shape[1]), F32),
                 ((wvw_t.shape[0], PROJ_ROWS), F32)])),
        name="attn_proj",
    )(x2d, mod, wqk, wvw_t, wki)


def _t5_bucket(n):
    max_exact = REL_BUCKETS // 2
    nf = jnp.maximum(n, 1).astype(F32)
    large = max_exact + jnp.floor(jnp.log(nf / max_exact) / math.log(REL_MAX_DIST / max_exact)
                                  * (REL_BUCKETS - max_exact)).astype(jnp.int32)
    large = jnp.minimum(large, REL_BUCKETS - 1)
    return jnp.where(n < max_exact, n, large)


def _far_bucket(first_dist, last_dist):
    n = np.arange(first_dist, last_dist + 1, dtype=np.float32)
    max_exact = REL_BUCKETS // 2
    large = max_exact + (np.log(n / np.float32(max_exact)) / np.float32(math.log(REL_MAX_DIST / max_exact))
                         * np.float32(REL_BUCKETS - max_exact)).astype(np.int32)
    buckets = np.where(n < max_exact, n.astype(np.int32), np.minimum(large, REL_BUCKETS - 1))
    assert buckets.min() == buckets.max(), "key chunks two or more blocks away must share one bias bucket"
    return int(buckets[0])


def _bias_kernel(rb_ref, o_ref, *, far_bucket):
    o = pl.program_id(0)
    row = lax.broadcasted_iota(jnp.int32, (ATT_BLOCK, ATT_BLOCK), 0)
    col = lax.broadcasted_iota(jnp.int32, (ATT_BLOCK, ATT_BLOCK), 1)
    dist = o * ATT_BLOCK + col - row
    bucket = _t5_bucket(jnp.maximum(dist, 0))
    for h in range(N_HEADS):
        tile = jnp.zeros((ATT_BLOCK, ATT_BLOCK), F32)
        for b in range(REL_BUCKETS):
            tile = jnp.where(bucket == b, rb_ref[b, h], tile)
        o_ref[0, h] = (tile - rb_ref[far_bucket, h]) * LOG2E


def _rel_bias_tiles(rel_bias, far_bucket):
    return pl.pallas_call(
        functools.partial(_bias_kernel, far_bucket=far_bucket),
        grid=(2,),
        in_specs=[pl.BlockSpec(memory_space=pltpu.SMEM)],
        out_specs=pl.BlockSpec((1, N_HEADS, ATT_BLOCK, ATT_BLOCK), lambda o: (o, 0, 0, 0)),
        out_shape=jax.ShapeDtypeStruct((2, N_HEADS, ATT_BLOCK, ATT_BLOCK), F32),
        compiler_params=pltpu.CompilerParams(
            dimension_semantics=("arbitrary",),
            vmem_limit_bytes=_vmem_limit([((N_HEADS, ATT_BLOCK, ATT_BLOCK), F32)], [],
                                         [((ATT_BLOCK, ATT_BLOCK), F32)] * 3)),
        name="rel_bias",
    )(rel_bias)


def _attn_kernel(q_ref, qi_ref, wit_ref, k_ref, vt_ref, kia_ref, kib_ref, bt_ref, o_ref,
                 sc_ref, zc_ref, ranked_ref, qh_ref, lg_ref, p_ref, pp_ref, st_ref, acc_ref, *, topk, seq):
    blk = ATT_BLOCK
    groups = blk // SUBLANES
    i = pl.program_id(1)
    nch = i + 1
    kf = float(topk)
    inf = jnp.inf

    def chunk_start(j):
        return j * blk if isinstance(j, int) else pl.multiple_of(j * blk, blk)

    def as_groups(x):
        return x.reshape(groups, SUBLANES, blk)

    def lanes8(v):
        return jnp.broadcast_to(v, (SUBLANES, blk))

    def colmin(x8):
        return jnp.min(x8, axis=0, keepdims=True)

    def colmax(x8):
        return jnp.max(x8, axis=0, keepdims=True)

    def colsum(x8):
        return jnp.sum(x8, axis=0, keepdims=True)

    wf = wit_ref[0] * (IDX_DIM ** -0.5)
    qi = qi_ref[0]
    row = lax.broadcasted_iota(jnp.int32, (blk, blk), 0)
    col = lax.broadcasted_iota(jnp.int32, (blk, blk), 1)
    causal = row <= col

    def chunk_scores(j):
        r0 = chunk_start(j)
        ka = kia_ref[0, pl.ds(r0, blk), :]
        kb = kib_ref[0, pl.ds(r0, blk), :]
        s = jnp.zeros((blk, blk), F32)
        for hp in range(IDX_HEADS // 2):
            qp = qi[:, hp * LANES:(hp + 1) * LANES]
            a0 = lax.dot_general(ka, qp, NT_DIMS, preferred_element_type=F32)
            a1 = lax.dot_general(kb, qp, NT_DIMS, preferred_element_type=F32)
            s = s + jnp.maximum(a0, 0.0) * wf[2 * hp:2 * hp + 1, :]
            s = s + jnp.maximum(a1, 0.0) * wf[2 * hp + 1:2 * hp + 2, :]
        return r0, s * (IDX_HEADS ** -0.5)

    def score_stats(j, s_lo, s_hi, stats):
        mn8, mx8, pos8, zer8 = stats
        hi3 = as_groups(s_hi)
        zer8 = zer8 + jnp.sum(jnp.where(hi3 == 0.0, 1.0, 0.0), axis=0)
        zc_ref[j] = zer8
        return (jnp.minimum(mn8, jnp.min(as_groups(s_lo), axis=0)),
                jnp.maximum(mx8, jnp.max(hi3, axis=0)),
                pos8 + jnp.sum(jnp.where(hi3 > 0.0, 1.0, 0.0), axis=0),
                zer8)

    def score_body(j, stats):
        r0, s = chunk_scores(j)
        sc_ref[pl.ds(r0, blk), :] = s
        return score_stats(j, s, s, stats)

    def score_run(first, count, stats):
        for c in range(count):
            stats = score_body(first + c, stats)
        return stats

    def score_diag(st):
        r_diag, s_diag = chunk_scores(i)
        s_diag_hi = jnp.where(causal, s_diag, -inf)
        sc_ref[pl.ds(r_diag, blk), :] = s_diag_hi
        return score_stats(i, jnp.where(causal, s_diag, inf), s_diag_hi, st)

    zeros8 = jnp.zeros((SUBLANES, blk), F32)
    stats = (jnp.full((SUBLANES, blk), inf, F32), jnp.full((SUBLANES, blk), -inf, F32), zeros8, zeros8)
    stats = lax.fori_loop(0, i // SCORE_RUN, lambda jq, st: score_run(SCORE_RUN * jq, SCORE_RUN, st), stats)
    left = i % SCORE_RUN
    mn8, mx8, pos8, zer8 = lax.switch(
        left, [functools.partial(lambda r, st: score_diag(score_run(i - r, r, st)), r) for r in range(SCORE_RUN)], stats)

    @pl.when(nch % 2 == 1)
    def _():
        sc_ref[pl.ds(chunk_start(nch), blk), :] = jnp.full((blk, blk), -inf, F32)

    npair = (nch + 1) // 2

    def count_gt(thr):
        t8 = lanes8(thr)

        def body(jp, accs):
            base = pl.multiple_of(jp * 2 * blk, 2 * blk)
            accs = list(accs)
            for s in range(2 * blk // COUNT_ROWS):
                x = sc_ref[pl.ds(base + s * COUNT_ROWS, COUNT_ROWS), :]
                for g in range(COUNT_ROWS // SUBLANES):
                    hit = jnp.where(x[g * SUBLANES:(g + 1) * SUBLANES] > t8, 1.0, 0.0)
                    accs[g % COUNT_CHAINS] = accs[g % COUNT_CHAINS] + hit
            return tuple(accs)

        accs = lax.fori_loop(0, npair, body, tuple(jnp.zeros((SUBLANES, blk), F32) for _ in range(COUNT_CHAINS)))
        return colsum(sum(accs[1:], accs[0]))

    def bisect(_, st):
        lo, hi, low, done = st
        mid = 0.5 * lo + 0.5 * hi
        c = count_gt(mid)
        live = done < 0.5
        up = jnp.logical_and(live, c >= kf)
        down = jnp.logical_and(live, c < kf)
        low = jnp.where(up, mid, low)
        lo = jnp.where(up, mid, lo)
        hi = jnp.where(down, mid, hi)
        done = jnp.where(c == kf, 1.0, done)
        return lo, hi, low, done

    tpos = i * blk + lax.broadcasted_iota(jnp.int32, (1, blk), 1)
    pos, zer = colsum(pos8), colsum(zer8)
    mn, mx = colmin(mn8), colmax(mx8)
    zero_tie = jnp.logical_and(pos < kf, pos + zer >= kf)
    pos_ge = pos >= kf
    done0 = jnp.where(jnp.logical_or(jnp.logical_or(tpos + 1 <= topk, zero_tie), pos == kf), 1.0, 0.0)
    low0 = jnp.where(jnp.logical_or(zero_tie, pos_ge), 0.0, -inf)
    lo0 = jnp.where(pos_ge, jnp.maximum(mn, 0.0), mn)
    hi0 = jnp.where(pos_ge, mx, jnp.minimum(mx, 0.0))
    tie0 = jnp.where(zero_tie, 0.0, inf)
    need0 = jnp.where(zero_tie, kf - pos, 0.0)

    def next_value_above(thr):
        t8 = lanes8(thr)

        def body(jp, accs):
            base = pl.multiple_of(jp * 2 * blk, 2 * blk)
            accs = list(accs)
            for s in range(2 * blk // COUNT_ROWS):
                x = sc_ref[pl.ds(base + s * COUNT_ROWS, COUNT_ROWS), :]
                for g in range(COUNT_ROWS // SUBLANES):
                    xg = x[g * SUBLANES:(g + 1) * SUBLANES]
                    accs[g % COUNT_CHAINS] = jnp.minimum(accs[g % COUNT_CHAINS], jnp.where(xg > t8, xg, inf))
            return tuple(accs)

        accs = lax.fori_loop(0, npair, body, tuple(jnp.full((SUBLANES, blk), inf, F32) for _ in range(COUNT_CHAINS)))
        return colmin(functools.reduce(jnp.minimum, accs))

    def climb_cond(st):
        return jnp.min(st[1]) < 0.5

    def climb_body(st):
        low, done, tie, need = st
        live = done < 0.5
        cand = next_value_above(low)
        cgt = count_gt(cand)
        found = jnp.logical_and(live, cgt < kf)
        tie = jnp.where(found, cand, tie)
        need = jnp.where(found, kf - cgt, need)
        low = jnp.where(live, cand, low)
        done = jnp.where(jnp.logical_and(live, cgt <= kf), 1.0, done)
        return low, done, tie, need

    def select(_):
        _, _, low, done = lax.fori_loop(0, BISECT_STEPS, bisect, (lo0, hi0, low0, done0))
        low, _, tie, need = lax.while_loop(climb_cond, climb_body, (low, done, tie0, need0))
        return low, tie, need

    low, tie, need = lax.cond(jnp.min(done0) < 0.5, select, lambda _: (low0, tie0, need0), 0)

    def plain_mask(_):
        low8 = lanes8(low)

        def body(j, _):
            r0 = chunk_start(j)
            x = as_groups(sc_ref[pl.ds(r0, blk), :])
            sc_ref[pl.ds(r0, blk), :] = jnp.where(x > low8[None], 0.0, -inf).reshape(blk, blk)
            return 0

        return lax.fori_loop(0, nch, body, 0)

    def ranked_chunk(j, tri, base):
        r0 = chunk_start(j)
        x = sc_ref[pl.ds(r0, blk), :]
        eq = x == tie
        rank = jnp.dot(tri, jnp.where(eq, 1.0, 0.0).astype(BF16), preferred_element_type=F32)
        sel = jnp.logical_or(x > low, jnp.logical_and(eq, rank + base <= need))
        sc_ref[pl.ds(r0, blk), :] = jnp.where(sel, 0.0, -inf)
        return rank[blk - 1:blk, :]

    def tie_mask(_):
        tri = jnp.where(row >= col, 1.0, 0.0).astype(BF16)
        lax.fori_loop(0, nch, lambda j, base: base + ranked_chunk(j, tri, base), jnp.zeros((1, blk), F32))
        return 0

    def zero_tie_mask(_):
        tri = jnp.where(row >= col, 1.0, 0.0).astype(BF16)

        def cut_body(j, ncut):
            return ncut + jnp.where(colsum(zc_ref[j]) < need, 1.0, 0.0)

        ncut = lax.fori_loop(0, nch, cut_body, jnp.zeros((1, blk), F32))
        cut = jnp.where(tie < inf, ncut, inf)

        def flag_body(j, _):
            ranked_ref[j] = 0
            return 0

        lax.fori_loop(0, nch, flag_body, 0)

        def cut_cond(pending):
            return jnp.min(pending) < inf

        def cut_chunk(pending):
            jf = jnp.min(pending)
            j = jf.astype(jnp.int32)
            base = jnp.where(j > 0, colsum(zc_ref[jnp.maximum(j - 1, 0)]), 0.0)
            ranked_chunk(j, tri, base)
            ranked_ref[j] = 1
            return jnp.where(pending == jf, inf, pending)

        lax.while_loop(cut_cond, cut_chunk, cut)
        low8, cut8 = lanes8(low), lanes8(cut)

        def body(j, _):
            @pl.when(ranked_ref[j] == 0)
            def _():
                r0 = chunk_start(j)
                x = as_groups(sc_ref[pl.ds(r0, blk), :])
                at_low = jnp.where(jnp.logical_and(cut8 < inf, cut8 > j.astype(F32)), 0.0, -inf)
                out = jnp.where(x > low8[None], 0.0, jnp.where(x == low8[None], at_low[None], -inf))
                sc_ref[pl.ds(r0, blk), :] = out.reshape(blk, blk)
            return 0

        return lax.fori_loop(0, nch, body, 0)

    tied = tie < inf
    mask_kind = jnp.where(jnp.max(jnp.where(tied, 1.0, 0.0)) < 0.5, 0,
                          jnp.where(jnp.max(jnp.where(jnp.logical_and(tied, tie != 0.0), 1.0, 0.0)) < 0.5, 1, 2))
    lax.switch(mask_kind, [plain_mask, zero_tie_mask, tie_mask], 0)

    q = q_ref[0]
    lane = lax.broadcasted_iota(jnp.int32, (blk, LANES), 1)
    for h in range(N_HEADS):
        hp = h // 2
        in_head = (lane < HEAD_DIM) if h % 2 == 0 else (lane >= HEAD_DIM)
        qh_ref[h] = jnp.where(in_head, q[:, hp * LANES:(hp + 1) * LANES], jnp.zeros((), BF16))

    def logits(j, h, mb, near):
        hp = h // 2
        kc = k_ref[0, pl.ds(chunk_start(j), blk), hp * LANES:(hp + 1) * LANES]
        lg = lax.dot_general(kc, qh_ref[h], NT_DIMS, preferred_element_type=F32) + mb
        return lg if near is None else lg + bt_ref[near, h]

    CMAX, RMAX, RESC = 0, 1, 2

    def stage_logits(j, near):
        mb = sc_ref[pl.ds(chunk_start(j), blk), :]
        cm8 = []
        for h in range(N_HEADS):
            lg = logits(j, h, mb, near)
            lg_ref[h] = lg
            cm8.append(jnp.max(as_groups(lg), axis=0))
        for h in range(N_HEADS):
            st_ref[CMAX, h] = lanes8(colmax(cm8[h]))

    def stage_exp():
        for h in range(N_HEADS):
            m_old = st_ref[RMAX, h]
            m_new = jnp.maximum(m_old, st_ref[CMAX, h])
            m_use = jnp.where(m_new == -inf, 0.0, m_new)
            p = jnp.exp2(as_groups(lg_ref[h]) - m_use[None])
            p_ref[h] = p.reshape(blk, blk).astype(BF16)
            st_ref[RMAX, h] = m_new
            st_ref[RESC, h] = jnp.exp2(m_old - m_use)

    def stage_pv(j):
        for h in range(N_HEADS):
            rows = slice(h * HEAD_ROWS, (h + 1) * HEAD_ROWS)
            pv = jnp.dot(vt_ref[0, j, rows, :], p_ref[h], preferred_element_type=F32)
            acc = acc_ref[rows, :].reshape(HEAD_ROWS // SUBLANES, SUBLANES, blk) * st_ref[RESC, h][None]
            acc_ref[rows, :] = acc.reshape(HEAD_ROWS, blk) + pv

    def pipelined(j, near_next):
        stage_exp()
        stage_logits(j + 1, near_next)
        stage_pv(j)

    def exact_attention():
        acc_ref[...] = jnp.zeros(acc_ref.shape, F32)
        st_ref[RMAX] = jnp.full((N_HEADS, SUBLANES, blk), -inf, F32)

        def last_chunks(first):
            for j in range(first, 0):
                pipelined(i + j, -(j + 1))
            stage_exp()
            stage_pv(i)

        @pl.when(i == 0)
        def _():
            stage_logits(0, 0)
            last_chunks(0)

        @pl.when(i == 1)
        def _():
            stage_logits(0, 1)
            last_chunks(-1)

        @pl.when(i >= 2)
        def _():
            stage_logits(0, None)

        def far_body(j, _):
            pipelined(j, None)
            return 0

        lax.fori_loop(0, jnp.maximum(i - 2, 0), far_body, 0)

        @pl.when(i >= 2)
        def _():
            last_chunks(-2)

    acc_ref[...] = jnp.zeros(acc_ref.shape, F32)
    st_ref[RMAX] = jnp.zeros((N_HEADS, SUBLANES, blk), F32)

    def fast_run(chunks):
        for c, (j, near) in enumerate(chunks):
            mb = sc_ref[pl.ds(chunk_start(j), blk), :]
            for h in range(N_HEADS):
                p = jnp.exp2(as_groups(logits(j, h, mb, near)) - st_ref[RMAX, h][None])
                pp_ref[c, h] = p.reshape(blk, blk).astype(BF16)
        for h in range(N_HEADS):
            rows = slice(h * HEAD_ROWS, (h + 1) * HEAD_ROWS)
            tot = acc_ref[rows, :]
            for c, (j, _) in enumerate(chunks):
                tot = tot + jnp.dot(vt_ref[0, j, rows, :], pp_ref[c, h], preferred_element_type=F32)
            den = tot[HEAD_DIM:HEAD_DIM + 1, :]
            up = lanes8(jnp.where(den > 0.0, jnp.floor(jnp.log2(den)), 0.0))
            scaled = tot.reshape(HEAD_ROWS // SUBLANES, SUBLANES, blk) * jnp.exp2(-up)[None]
            acc_ref[rows, :] = scaled.reshape(HEAD_ROWS, blk)
            st_ref[RMAX, h] = st_ref[RMAX, h] + up

    def far_run(jr, _):
        fast_run([(FAST_RUN * jr + c, None) for c in range(FAST_RUN)])
        return 0

    n_far = jnp.maximum(i - 1, 0)
    lax.fori_loop(0, n_far // FAST_RUN, far_run, 0)

    @pl.when(i == 0)
    def _():
        fast_run([(0, 0)])

    for left in range(FAST_RUN):
        @pl.when(jnp.logical_and(i >= 1, n_far % FAST_RUN == left))
        def _():
            last = [(i - 1 - left + c, None) for c in range(left)] + [(i - 1, 1), (i, 0)]
            for first in range(0, len(last), FAST_RUN):
                fast_run(last[first:first + FAST_RUN])

    acc = acc_ref[...]
    finite = jnp.min(jnp.where(jnp.isfinite(acc), 1.0, 0.0))
    dens = jnp.concatenate([acc[h * HEAD_ROWS + HEAD_DIM:h * HEAD_ROWS + HEAD_DIM + 1, :] for h in range(N_HEADS)], axis=0)
    usable = jnp.logical_and(finite > 0.5, jnp.min(dens) > 0.0)
    lax.cond(usable, lambda: None, exact_attention)

    heads = []
    for h in range(N_HEADS):
        r0 = h * HEAD_ROWS
        heads.append(acc_ref[r0:r0 + HEAD_DIM, :] / acc_ref[r0 + HEAD_DIM:r0 + HEAD_DIM + 1, :])
    o_ref[0] = jnp.concatenate(heads, axis=0).T.astype(BF16)


def _attention(q, qi, wit, k, vt, kia, kib, btiles, *, bsz, seq, topk):
    blk = ATT_BLOCK
    nblk = seq // blk
    return pl.pallas_call(
        functools.partial(_attn_kernel, topk=topk, seq=seq),
        grid=(bsz, nblk),
        in_specs=[
            pl.BlockSpec((1, blk, ATTN_DIM), lambda b, i: (b, i, 0)),
            pl.BlockSpec((1, blk, IDX_HEADS * IDX_DIM), lambda b, i: (b, i, 0)),
            pl.BlockSpec((1, IDX_HEADS, blk), lambda b, i: (b, 0, i)),
            pl.BlockSpec((1, seq, ATTN_DIM), lambda b, i: (b, 0, 0)),
            pl.BlockSpec((1, nblk, N_HEADS * HEAD_ROWS, blk), lambda b, i: (b, 0, 0, 0)),
            pl.BlockSpec((1, seq, LANES), lambda b, i: (b, 0, 0)),
            pl.BlockSpec((1, seq, LANES), lambda b, i: (b, 0, 0)),
            pl.BlockSpec(btiles.shape, lambda b, i: (0, 0, 0, 0)),
        ],
        out_specs=pl.BlockSpec((1, blk, ATTN_DIM), lambda b, i: (b, i, 0)),
        out_shape=jax.ShapeDtypeStruct((bsz, seq, ATTN_DIM), BF16),
        scratch_shapes=[
            pltpu.VMEM((seq + blk, blk), F32),
            pltpu.VMEM((nblk, SUBLANES, blk), F32),
            pltpu.SMEM((nblk,), jnp.int32),
            pltpu.VMEM((N_HEADS, blk, LANES), BF16),
            pltpu.VMEM((N_HEADS, blk, blk), F32),
            pltpu.VMEM((N_HEADS, blk, blk), BF16),
            pltpu.VMEM((FAST_RUN, N_HEADS, blk, blk), BF16),
            pltpu.VMEM((3, N_HEADS, SUBLANES, blk), F32),
            pltpu.VMEM((N_HEADS * HEAD_ROWS, blk), F32),
        ],
        compiler_params=pltpu.CompilerParams(
            dimension_semantics=("arbitrary", "arbitrary"),
            vmem_limit_bytes=_vmem_limit(
                [((blk, ATTN_DIM), BF16), ((blk, IDX_HEADS * IDX_DIM), BF16), ((IDX_HEADS, blk), F32),
                 ((seq, ATTN_DIM), BF16), ((nblk, N_HEADS * HEAD_ROWS, blk), BF16), ((seq, LANES), BF16),
                 ((seq, LANES), BF16), (btiles.shape, F32), ((blk, ATTN_DIM), BF16)],
                [((seq + blk, blk), F32), ((nblk, SUBLANES, blk), F32), ((N_HEADS, blk, LANES), BF16),
                 ((N_HEADS, blk, blk), F32), ((N_HEADS, blk, blk), BF16), ((FAST_RUN, N_HEADS, blk, blk), BF16),
                 ((3, N_HEADS, SUBLANES, blk), F32),
                 ((N_HEADS * HEAD_ROWS, blk), F32)],
                [((blk, blk), F32)] * 4 + [((blk, blk), BF16), ((ATTN_DIM, blk), F32), ((blk, ATTN_DIM), F32)])),
        name="attention",
    )(q, qi, wit, k, vt, kia, kib, btiles)


def _mix_kernel(x_ref, mod_ref, att_ref, wpg_ref, wpool_ref, ps_ref, wa_ref, wb_ref, wo_ref,
                lng_ref, lnb_ref, o_ref, pe_ref, mix_ref, *, alpha, mod_row, ln_row, tiles_per_seq):
    rows = x_ref.shape[0]
    i = pl.program_id(0)
    seq_tile = i % tiles_per_seq

    @pl.when(i == 0)
    def _():
        pe_ref[rows:, :] = jnp.zeros((POOL_HALO, POOL_DIM), F32)

    x = x_ref[...]
    sh = mod_ref[0, mod_row:mod_row + 1, :]
    sc = mod_ref[0, mod_row + 1:mod_row + 2, :]
    gate = mod_ref[0, mod_row + 2:mod_row + 3, :]
    u = (x * (1.0 + sc) + sh).astype(BF16)
    pg = jnp.dot(u, wpg_ref[...], preferred_element_type=F32)
    pe_ref[0:POOL_HALO, :] = jnp.where(seq_tile == 0, 0.0, pe_ref[rows:, :])
    pe_ref[POOL_HALO:, :] = pg[:, :POOL_DIM]
    t = seq_tile * rows + lax.broadcasted_iota(jnp.int32, (rows, 1), 0)
    for g, w in enumerate(POOL_WINDOWS):
        cols = slice(g * POOL_GROUP_DIM, (g + 1) * POOL_GROUP_DIM)
        cur = pe_ref[POOL_HALO:, cols]
        win = cur
        for back in range(1, w):
            win = win + pe_ref[POOL_HALO - back:POOL_HALO - back + rows, cols]
        cnt = jnp.minimum(t + 1, w).astype(F32)
        pooled = (win / cnt - cur).astype(BF16)
        mixed = jnp.dot(pooled, wpool_ref[g], preferred_element_type=F32)
        mix_ref[:, cols] = (mixed * ps_ref[:, cols]).astype(BF16)
    y_a = jnp.dot(mix_ref[...], wa_ref[...], preferred_element_type=F32)
    y_b = jnp.dot(att_ref[...], wb_ref[...], preferred_element_type=F32)
    d = x.shape[1]
    ga = pg[:, POOL_DIM:POOL_DIM + d]
    gb = pg[:, POOL_DIM + d:]
    merged = (jax.nn.sigmoid(ga) * y_a + jax.nn.sigmoid(gb) * y_b).astype(BF16)
    y = jnp.dot(merged, wo_ref[...], preferred_element_type=F32)
    z = alpha * x + gate * y
    o_ref[...] = _layer_norm(z, lng_ref[ln_row:ln_row + 1, :], lnb_ref[ln_row:ln_row + 1, :])


def _mix_out(x2d, mod, att2d, wpg, wpool, pool_scale, wa, wb, wo, ln_g, ln_b, *, seq, alpha, mod_row, ln_row):
    n, d = x2d.shape
    tiles_per_seq = seq // MIX_ROWS
    resident = dict(pipeline_mode=pl.Buffered(1))
    full = lambda a: pl.BlockSpec(a.shape, lambda i: (0,) * a.ndim, **resident)
    return pl.pallas_call(
        functools.partial(_mix_kernel, alpha=alpha, mod_row=mod_row, ln_row=ln_row, tiles_per_seq=tiles_per_seq),
        grid=(n // MIX_ROWS,),
        in_specs=[
            pl.BlockSpec((MIX_ROWS, d), lambda i: (i, 0)),
            pl.BlockSpec((1, N_ADA, d), lambda i: (i // tiles_per_seq, 0, 0)),
            pl.BlockSpec((MIX_ROWS, ATTN_DIM), lambda i: (i, 0)),
            full(wpg), full(wpool), full(pool_scale), full(wa), full(wb), full(wo),
            pl.BlockSpec(ln_g.shape, lambda i: (0, 0)),
            pl.BlockSpec(ln_b.shape, lambda i: (0, 0)),
        ],
        out_specs=pl.BlockSpec((MIX_ROWS, d), lambda i: (i, 0)),
        out_shape=jax.ShapeDtypeStruct((n, d), F32),
        scratch_shapes=[
            pltpu.VMEM((POOL_HALO + MIX_ROWS, POOL_DIM), F32),
            pltpu.VMEM((MIX_ROWS, POOL_DIM), BF16),
        ],
        compiler_params=pltpu.CompilerParams(
            dimension_semantics=("arbitrary",),
            vmem_limit_bytes=_vmem_limit(
                [((MIX_ROWS, d), F32), ((N_ADA, d), F32), ((MIX_ROWS, ATTN_DIM), BF16), (ln_g.shape, F32),
                 (ln_b.shape, F32), ((MIX_ROWS, d), F32)],
                [(wpg.shape, BF16), (wpool.shape, BF16), (pool_scale.shape, F32), (wa.shape, BF16), (wb.shape, BF16),
                 (wo.shape, BF16), ((POOL_HALO + MIX_ROWS, POOL_DIM), F32), ((MIX_ROWS, POOL_DIM), BF16)],
                [((MIX_ROWS, d), BF16), ((MIX_ROWS, wpg.shape[1]), F32)] + [((MIX_ROWS, d), F32)] * 5)),
        name="mix_out",
    )(x2d, mod, att2d, wpg, wpool, pool_scale, wa, wb, wo, ln_g, ln_b)


def kernel(x, c, w_ada, b_ada, ln_g, ln_b, ffn1_w_gate, ffn1_w_up, ffn1_w_down, w_in, w_pool, pool_scale,
           w_a, w_b, w_out, rel_bias, ffn2_w_gate, ffn2_w_up, ffn2_w_down):
    bsz, seq, d = x.shape
    depth = w_ada.shape[0]
    alpha = (2.0 * depth) ** 0.25
    topk = min(TOP_K, seq // 4)
    assert seq % FFN_ROWS == 0 and seq % PROJ_ROWS == 0 and seq % MIX_ROWS == 0 and seq % ATT_BLOCK == 0
    assert PROJ_ROWS % ATT_BLOCK == 0 and POOL_HALO >= max(POOL_WINDOWS) - 1
    far_bucket = _far_bucket(ATT_BLOCK + 1, max(seq - 1, ATT_BLOCK + 1))

    o_q = POOL_DIM
    o_k = o_q + ATTN_DIM
    o_v = o_k + ATTN_DIM
    o_qi = o_v + ATTN_DIM
    o_ki = o_qi + IDX_HEADS * IDX_DIM
    o_wi = o_ki + IDX_DIM
    o_ga = o_wi + IDX_HEADS

    btiles = _rel_bias_tiles(rel_bias, far_bucket)
    x2d = x.reshape(bsz * seq, d)
    for l in range(depth):
        wl = w_in[l]
        zeros_ki = jnp.zeros((d, LANES - IDX_DIM), wl.dtype)
        w_ki = wl[:, o_ki:o_wi]
        wqk = jnp.concatenate([wl[:, o_q:o_v], wl[:, o_qi:o_ki]], axis=1).astype(BF16)
        wvw_t = jnp.pad(jnp.concatenate([wl[:, o_v:o_qi], wl[:, o_wi:o_ga]], axis=1).T,
                        ((0, 2 * SUBLANES - IDX_HEADS), (0, 0))).astype(BF16)
        wki = jnp.concatenate([w_ki, zeros_ki, zeros_ki, w_ki], axis=1).astype(BF16)
        wpg = jnp.concatenate([wl[:, :POOL_DIM], wl[:, o_ga:]], axis=1).astype(BF16)

        mod = _ada_mod(c, w_ada[l], b_ada[l]).reshape(bsz, N_ADA, d)
        x2d = _ffn(x2d, mod, ln_g[l], ln_b[l], ffn1_w_gate[l], ffn1_w_up[l], ffn1_w_down[l],
                   seq=seq, alpha=alpha, mod_row=0, ln_row=0)
        q, k, qi, vt, kia, kib, wit = _attn_proj(x2d, mod, wqk, wvw_t, wki, bsz=bsz, seq=seq, mod_row=3)
        att = _attention(q.reshape(bsz, seq, ATTN_DIM), qi.reshape(bsz, seq, IDX_HEADS * IDX_DIM), wit,
                         k.reshape(bsz, seq, ATTN_DIM), vt, kia.reshape(bsz, seq, LANES),
                         kib.reshape(bsz, seq, LANES), btiles, bsz=bsz, seq=seq, topk=topk)
        x2d = _mix_out(x2d, mod, att.reshape(bsz * seq, ATTN_DIM), wpg, w_pool[l].astype(BF16),
                       pool_scale[l].reshape(1, POOL_DIM), w_a[l].astype(BF16), w_b[l].astype(BF16),
                       w_out[l].astype(BF16), ln_g[l], ln_b[l], seq=seq, alpha=alpha, mod_row=3, ln_row=1)
        x2d = _ffn(x2d, mod, ln_g[l], ln_b[l], ffn2_w_gate[l], ffn2_w_up[l], ffn2_w_down[l],
                   seq=seq, alpha=alpha, mod_row=6, ln_row=2)
    return x2d.reshape(bsz, seq, d)
```

```python
import functools
import math

import numpy as np
import jax
import jax.numpy as jnp
from jax import lax
from jax.experimental import pallas as pl
from jax.experimental.pallas import tpu as pltpu

POOL_WINDOWS = (2, 4, 8, 16)
POOL_GROUP_DIM = 128
POOL_DIM = len(POOL_WINDOWS) * POOL_GROUP_DIM
N_HEADS = 8
HEAD_DIM = 64
ATTN_DIM = N_HEADS * HEAD_DIM
HEAD_PAD = 16
HEAD_ROWS = HEAD_DIM + HEAD_PAD
IDX_HEADS = 8
IDX_DIM = 64
TOP_K = 256
REL_BUCKETS = 32
REL_MAX_DIST = 128
N_ADA = 9
LN_EPS = 1e-5
POOL_HALO = 16

LANES = 128
SUBLANES = 8
FFN_ROWS = 512
FFN_COLS = 256
PROJ_ROWS = 512
MIX_ROWS = 512
ATT_BLOCK = 256
ADA_COLS = 1024
SCORE_RUN = 4
FAST_RUN = 4
COUNT_CHAINS = 4
COUNT_ROWS = 64
BISECT_STEPS = 14

LOG2E = math.log2(math.e)
BF16 = jnp.bfloat16
F32 = jnp.float32
NT_DIMS = (((1,), (1,)), ((), ()))


def _tile_bytes(shape, dtype):
    itemsize = jnp.dtype(dtype).itemsize
    sublanes = SUBLANES * (4 // itemsize)
    shape = (1,) * (2 - len(shape)) + tuple(shape)
    rows = -(-shape[-2] // sublanes) * sublanes
    cols = -(-shape[-1] // LANES) * LANES
    return math.prod(shape[:-2]) * rows * cols * itemsize


def _vmem_limit(pipelined, resident, temporaries):
    return (2 * sum(_tile_bytes(*b) for b in pipelined) + sum(_tile_bytes(*b) for b in resident)
            + sum(_tile_bytes(*b) for b in temporaries))


def _layer_norm(z, g, b):
    mu = jnp.mean(z, axis=-1, keepdims=True)
    zc = z - mu
    var = jnp.mean(zc * zc, axis=-1, keepdims=True)
    return zc * lax.rsqrt(var + LN_EPS) * g + b


def _silu(a):
    return a * jax.nn.sigmoid(a)


def _ada_kernel(c_ref, w_ref, b_ref, o_ref):
    a = _silu(c_ref[...])
    o_ref[...] = jnp.dot(a, w_ref[...], preferred_element_type=F32) + b_ref[...]


def _ada_mod(c, w_ada, b_ada):
    bsz, d = c.shape
    n = w_ada.shape[1]
    rows = -(-bsz // SUBLANES) * SUBLANES
    c_pad = jnp.pad(c, ((0, rows - bsz), (0, 0)))
    out = pl.pallas_call(
        _ada_kernel,
        grid=(n // ADA_COLS,),
        in_specs=[
            pl.BlockSpec((rows, d), lambda j: (0, 0)),
            pl.BlockSpec((d, ADA_COLS), lambda j: (0, j)),
            pl.BlockSpec((1, ADA_COLS), lambda j: (0, j)),
        ],
        out_specs=pl.BlockSpec((rows, ADA_COLS), lambda j: (0, j)),
        out_shape=jax.ShapeDtypeStruct((rows, n), F32),
        compiler_params=pltpu.CompilerParams(
            dimension_semantics=("arbitrary",),
            vmem_limit_bytes=_vmem_limit([((rows, d), F32), ((d, ADA_COLS), F32), ((1, ADA_COLS), F32),
                                          ((rows, ADA_COLS), F32)], [], [((rows, ADA_COLS), F32)])),
        name="ada_mod",
    )(c_pad, w_ada, b_ada.reshape(1, n))
    return out[:bsz]


def _ffn_kernel(x_ref, mod_ref, lng_ref, lnb_ref, wg_ref, wu_ref, wd_ref, o_ref, h_ref, *, alpha, mod_row, ln_row):
    x = x_ref[...]
    sh = mod_ref[0, mod_row:mod_row + 1, :]
    sc = mod_ref[0, mod_row + 1:mod_row + 2, :]
    gate = mod_ref[0, mod_row + 2:mod_row + 3, :]
    u = (x * (1.0 + sc) + sh).astype(BF16)
    d_ff = wg_ref.shape[1]
    for c in range(d_ff // FFN_COLS):
        sl = slice(c * FFN_COLS, (c + 1) * FFN_COLS)
        a = jnp.dot(u, wg_ref[:, sl], preferred_element_type=F32)
        b = jnp.dot(u, wu_ref[:, sl], preferred_element_type=F32)
        h_ref[:, sl] = (_silu(a) * b).astype(BF16)
    y = jnp.dot(h_ref[...], wd_ref[...], preferred_element_type=F32)
    z = alpha * x + (0.5 * gate) * y
    o_ref[...] = _layer_norm(z, lng_ref[ln_row:ln_row + 1, :], lnb_ref[ln_row:ln_row + 1, :])


def _ffn(x2d, mod, ln_g, ln_b, wg, wu, wd, *, seq, alpha, mod_row, ln_row):
    n, d = x2d.shape
    d_ff = wg.shape[1]
    tiles_per_seq = seq // FFN_ROWS
    resident = dict(pipeline_mode=pl.Buffered(1))
    return pl.pallas_call(
        functools.partial(_ffn_kernel, alpha=alpha, mod_row=mod_row, ln_row=ln_row),
        grid=(n // FFN_ROWS,),
        in_specs=[
            pl.BlockSpec((FFN_ROWS, d), lambda i: (i, 0)),
            pl.BlockSpec((1, N_ADA, d), lambda i: (i // tiles_per_seq, 0, 0)),
            pl.BlockSpec(ln_g.shape, lambda i: (0, 0)),
            pl.BlockSpec(ln_b.shape, lambda i: (0, 0)),
            pl.BlockSpec((d, d_ff), lambda i: (0, 0), **resident),
            pl.BlockSpec((d, d_ff), lambda i: (0, 0), **resident),
            pl.BlockSpec((d_ff, d), lambda i: (0, 0), **resident),
        ],
        out_specs=pl.BlockSpec((FFN_ROWS, d), lambda i: (i, 0)),
        out_shape=jax.ShapeDtypeStruct((n, d), F32),
        scratch_shapes=[pltpu.VMEM((FFN_ROWS, d_ff), BF16)],
        compiler_params=pltpu.CompilerParams(
            dimension_semantics=("arbitrary",),
            vmem_limit_bytes=_vmem_limit(
                [((FFN_ROWS, d), F32), ((N_ADA, d), F32), (ln_g.shape, F32), (ln_b.shape, F32), ((FFN_ROWS, d), F32)],
                [((d, d_ff), BF16), ((d, d_ff), BF16), ((d_ff, d), BF16), ((FFN_ROWS, d_ff), BF16)],
                [((FFN_ROWS, d), BF16), ((FFN_ROWS, FFN_COLS), F32), ((FFN_ROWS, FFN_COLS), F32),
                 ((FFN_ROWS, d), F32), ((FFN_ROWS, d), F32)])),
        name="ffn",
    )(x2d, mod, ln_g, ln_b, wg, wu, wd)


def _proj_kernel(x_ref, mod_ref, wqk_ref, wvw_ref, wki_ref,
                 q_ref, k_ref, qi_ref, vt_ref, kia_ref, kib_ref, wit_ref, *, mod_row):
    x = x_ref[...]
    sh = mod_ref[0, mod_row:mod_row + 1, :]
    sc = mod_ref[0, mod_row + 1:mod_row + 2, :]
    u = (x * (1.0 + sc) + sh).astype(BF16)
    qkq = jnp.dot(u, wqk_ref[...], preferred_element_type=F32)
    q_ref[...] = (qkq[:, :ATTN_DIM] * (HEAD_DIM ** -0.5 * LOG2E)).astype(BF16)
    k_ref[...] = qkq[:, ATTN_DIM:2 * ATTN_DIM].astype(BF16)
    qi_ref[...] = qkq[:, 2 * ATTN_DIM:].astype(BF16)
    kk = jnp.dot(u, wki_ref[...], preferred_element_type=F32)
    kia_ref[...] = kk[:, :LANES].astype(BF16)
    kib_ref[...] = kk[:, LANES:].astype(BF16)
    vw = lax.dot_general(wvw_ref[...], u, NT_DIMS, preferred_element_type=F32)
    vt = vw[:ATTN_DIM].astype(BF16)
    ones_rows = jnp.where(lax.broadcasted_iota(jnp.int32, (HEAD_PAD, ATT_BLOCK), 0) == 0, 1.0, 0.0).astype(BF16)
    for c in range(vt_ref.shape[1]):
        for h in range(N_HEADS):
            vt_ref[0, c, h * HEAD_ROWS:h * HEAD_ROWS + HEAD_DIM, :] = \
                vt[h * HEAD_DIM:(h + 1) * HEAD_DIM, c * ATT_BLOCK:(c + 1) * ATT_BLOCK]
            vt_ref[0, c, h * HEAD_ROWS + HEAD_DIM:(h + 1) * HEAD_ROWS, :] = ones_rows
    wit_ref[0] = vw[ATTN_DIM:ATTN_DIM + IDX_HEADS, :]


def _attn_proj(x2d, mod, wqk, wvw_t, wki, *, bsz, seq, mod_row):
    n, d = x2d.shape
    tiles_per_seq = seq // PROJ_ROWS
    chunks_per_tile = PROJ_ROWS // ATT_BLOCK
    resident = dict(pipeline_mode=pl.Buffered(1))
    row_spec = lambda cols: pl.BlockSpec((PROJ_ROWS, cols), lambda i: (i, 0))
    return pl.pallas_call(
        functools.partial(_proj_kernel, mod_row=mod_row),
        grid=(n // PROJ_ROWS,),
        in_specs=[
            pl.BlockSpec((PROJ_ROWS, d), lambda i: (i, 0)),
            pl.BlockSpec((1, N_ADA, d), lambda i: (i // tiles_per_seq, 0, 0)),
            pl.BlockSpec(wqk.shape, lambda i: (0, 0), **resident),
            pl.BlockSpec(wvw_t.shape, lambda i: (0, 0), **resident),
            pl.BlockSpec(wki.shape, lambda i: (0, 0), **resident),
        ],
        out_specs=[
            row_spec(ATTN_DIM), row_spec(ATTN_DIM), row_spec(IDX_HEADS * IDX_DIM),
            pl.BlockSpec((1, chunks_per_tile, N_HEADS * HEAD_ROWS, ATT_BLOCK),
                         lambda i: (i // tiles_per_seq, i % tiles_per_seq, 0, 0)),
            row_spec(LANES), row_spec(LANES),
            pl.BlockSpec((1, IDX_HEADS, PROJ_ROWS), lambda i: (i // tiles_per_seq, 0, i % tiles_per_seq)),
        ],
        out_shape=[
            jax.ShapeDtypeStruct((n, ATTN_DIM), BF16),
            jax.ShapeDtypeStruct((n, ATTN_DIM), BF16),
            jax.ShapeDtypeStruct((n, IDX_HEADS * IDX_DIM), BF16),
            jax.ShapeDtypeStruct((bsz, seq // ATT_BLOCK, N_HEADS * HEAD_ROWS, ATT_BLOCK), BF16),
            jax.ShapeDtypeStruct((n, LANES), BF16),
            jax.ShapeDtypeStruct((n, LANES), BF16),
            jax.ShapeDtypeStruct((bsz, IDX_HEADS, seq), F32),
        ],
        compiler_params=pltpu.CompilerParams(
            dimension_semantics=("arbitrary",),
            vmem_limit_bytes=_vmem_limit(
                [((PROJ_ROWS, d), F32), ((N_ADA, d), F32), ((PROJ_ROWS, ATTN_DIM), BF16), ((PROJ_ROWS, ATTN_DIM), BF16),
                 ((PROJ_ROWS, IDX_HEADS * IDX_DIM), BF16), ((chunks_per_tile, N_HEADS * HEAD_ROWS, ATT_BLOCK), BF16),
                 ((PROJ_ROWS, LANES), BF16), ((PROJ_ROWS, LANES), BF16), ((IDX_HEADS, PROJ_ROWS), F32)],
                [(wqk.shape, BF16), (wvw_t.shape, BF16), (wki.shape, BF16)],
                [((PROJ_ROWS, d), BF16), ((PROJ_ROWS, wqk.shape[1]), F32), ((PROJ_ROWS, wki.shape[1]), F32),
                 ((wvw_t.shape[0], PROJ_ROWS), F32)])),
        name="attn_proj",
    )(x2d, mod, wqk, wvw_t, wki)


def _t5_bucket(n):
    max_exact = REL_BUCKETS // 2
    nf = jnp.maximum(n, 1).astype(F32)
    large = max_exact + jnp.floor(jnp.log(nf / max_exact) / math.log(REL_MAX_DIST / max_exact)
                                  * (REL_BUCKETS - max_exact)).astype(jnp.int32)
    large = jnp.minimum(large, REL_BUCKETS - 1)
    return jnp.where(n < max_exact, n, large)


def _far_bucket(first_dist, last_dist):
    n = np.arange(first_dist, last_dist + 1, dtype=np.float32)
    max_exact = REL_BUCKETS // 2
    large = max_exact + (np.log(n / np.float32(max_exact)) / np.float32(math.log(REL_MAX_DIST / max_exact))
                         * np.float32(REL_BUCKETS - max_exact)).astype(np.int32)
    buckets = np.where(n < max_exact, n.astype(np.int32), np.minimum(large, REL_BUCKETS - 1))
    assert buckets.min() == buckets.max(), "key chunks two or more blocks away must share one bias bucket"
    return int(buckets[0])


def _bias_kernel(rb_ref, o_ref, *, far_bucket):
    o = pl.program_id(0)
    row = lax.broadcasted_iota(jnp.int32, (ATT_BLOCK, ATT_BLOCK), 0)
    col = lax.broadcasted_iota(jnp.int32, (ATT_BLOCK, ATT_BLOCK), 1)
    dist = o * ATT_BLOCK + col - row
    bucket = _t5_bucket(jnp.maximum(dist, 0))
    for h in range(N_HEADS):
        tile = jnp.zeros((ATT_BLOCK, ATT_BLOCK), F32)
        for b in range(REL_BUCKETS):
            tile = jnp.where(bucket == b, rb_ref[b, h], tile)
        o_ref[0, h] = (tile - rb_ref[far_bucket, h]) * LOG2E


def _rel_bias_tiles(rel_bias, far_bucket):
    return pl.pallas_call(
        functools.partial(_bias_kernel, far_bucket=far_bucket),
        grid=(2,),
        in_specs=[pl.BlockSpec(memory_space=pltpu.SMEM)],
        out_specs=pl.BlockSpec((1, N_HEADS, ATT_BLOCK, ATT_BLOCK), lambda o: (o, 0, 0, 0)),
        out_shape=jax.ShapeDtypeStruct((2, N_HEADS, ATT_BLOCK, ATT_BLOCK), F32),
        compiler_params=pltpu.CompilerParams(
            dimension_semantics=("arbitrary",),
            vmem_limit_bytes=_vmem_limit([((N_HEADS, ATT_BLOCK, ATT_BLOCK), F32)], [],
                                         [((ATT_BLOCK, ATT_BLOCK), F32)] * 3)),
        name="rel_bias",
    )(rel_bias)


def _attn_kernel(q_ref, qi_ref, wit_ref, k_ref, vt_ref, kia_ref, kib_ref, bt_ref, o_ref,
                 sc_ref, zc_ref, ranked_ref, qh_ref, lg_ref, p_ref, pp_ref, st_ref, acc_ref, *, topk, seq):
    blk = ATT_BLOCK
    groups = blk // SUBLANES
    i = pl.program_id(1)
    nch = i + 1
    kf = float(topk)
    inf = jnp.inf

    def chunk_start(j):
        return j * blk if isinstance(j, int) else pl.multiple_of(j * blk, blk)

    def as_groups(x):
        return x.reshape(groups, SUBLANES, blk)

    def lanes8(v):
        return jnp.broadcast_to(v, (SUBLANES, blk))

    def colmin(x8):
        return jnp.min(x8, axis=0, keepdims=True)

    def colmax(x8):
        return jnp.max(x8, axis=0, keepdims=True)

    def colsum(x8):
        return jnp.sum(x8, axis=0, keepdims=True)

    wf = wit_ref[0] * (IDX_DIM ** -0.5)
    qi = qi_ref[0]
    row = lax.broadcasted_iota(jnp.int32, (blk, blk), 0)
    col = lax.broadcasted_iota(jnp.int32, (blk, blk), 1)
    causal = row <= col

    def chunk_scores(j):
        r0 = chunk_start(j)
        ka = kia_ref[0, pl.ds(r0, blk), :]
        kb = kib_ref[0, pl.ds(r0, blk), :]
        s = jnp.zeros((blk, blk), F32)
        for hp in range(IDX_HEADS // 2):
            qp = qi[:, hp * LANES:(hp + 1) * LANES]
            a0 = lax.dot_general(ka, qp, NT_DIMS, preferred_element_type=F32)
            a1 = lax.dot_general(kb, qp, NT_DIMS, preferred_element_type=F32)
            s = s + jnp.maximum(a0, 0.0) * wf[2 * hp:2 * hp + 1, :]
            s = s + jnp.maximum(a1, 0.0) * wf[2 * hp + 1:2 * hp + 2, :]
        return r0, s * (IDX_HEADS ** -0.5)

    def score_stats(j, s_lo, s_hi, stats):
        mn8, mx8, pos8, zer8 = stats
        hi3 = as_groups(s_hi)
        zer8 = zer8 + jnp.sum(jnp.where(hi3 == 0.0, 1.0, 0.0), axis=0)
        zc_ref[j] = zer8
        return (jnp.minimum(mn8, jnp.min(as_groups(s_lo), axis=0)),
                jnp.maximum(mx8, jnp.max(hi3, axis=0)),
                pos8 + jnp.sum(jnp.where(hi3 > 0.0, 1.0, 0.0), axis=0),
                zer8)

    def score_body(j, stats):
        r0, s = chunk_scores(j)
        sc_ref[pl.ds(r0, blk), :] = s
        return score_stats(j, s, s, stats)

    def score_run(first, count, stats):
        for c in range(count):
            stats = score_body(first + c, stats)
        return stats

    def score_diag(st):
        r_diag, s_diag = chunk_scores(i)
        s_diag_hi = jnp.where(causal, s_diag, -inf)
        sc_ref[pl.ds(r_diag, blk), :] = s_diag_hi
        return score_stats(i, jnp.where(causal, s_diag, inf), s_diag_hi, st)

    zeros8 = jnp.zeros((SUBLANES, blk), F32)
    stats = (jnp.full((SUBLANES, blk), inf, F32), jnp.full((SUBLANES, blk), -inf, F32), zeros8, zeros8)
    stats = lax.fori_loop(0, i // SCORE_RUN, lambda jq, st: score_run(SCORE_RUN * jq, SCORE_RUN, st), stats)
    left = i % SCORE_RUN
    mn8, mx8, pos8, zer8 = lax.switch(
        left, [functools.partial(lambda r, st: score_diag(score_run(i - r, r, st)), r) for r in range(SCORE_RUN)], stats)

    @pl.when(nch % 2 == 1)
    def _():
        sc_ref[pl.ds(chunk_start(nch), blk), :] = jnp.full((blk, blk), -inf, F32)

    npair = (nch + 1) // 2

    def count_gt(thr):
        t8 = lanes8(thr)

        def body(jp, accs):
            base = pl.multiple_of(jp * 2 * blk, 2 * blk)
            accs = list(accs)
            for s in range(2 * blk // COUNT_ROWS):
                x = sc_ref[pl.ds(base + s * COUNT_ROWS, COUNT_ROWS), :]
                for g in range(COUNT_ROWS // SUBLANES):
                    hit = jnp.where(x[g * SUBLANES:(g + 1) * SUBLANES] > t8, 1.0, 0.0)
                    accs[g % COUNT_CHAINS] = accs[g % COUNT_CHAINS] + hit
            return tuple(accs)

        accs = lax.fori_loop(0, npair, body, tuple(jnp.zeros((SUBLANES, blk), F32) for _ in range(COUNT_CHAINS)))
        return colsum(sum(accs[1:], accs[0]))

    def bisect(_, st):
        lo, hi, low, done = st
        mid = 0.5 * lo + 0.5 * hi
        c = count_gt(mid)
        live = done < 0.5
        up = jnp.logical_and(live, c >= kf)
        down = jnp.logical_and(live, c < kf)
        low = jnp.where(up, mid, low)
        lo = jnp.where(up, mid, lo)
        hi = jnp.where(down, mid, hi)
        done = jnp.where(c == kf, 1.0, done)
        return lo, hi, low, done

    tpos = i * blk + lax.broadcasted_iota(jnp.int32, (1, blk), 1)
    pos, zer = colsum(pos8), colsum(zer8)
    mn, mx = colmin(mn8), colmax(mx8)
    zero_tie = jnp.logical_and(pos < kf, pos + zer >= kf)
    pos_ge = pos >= kf
    done0 = jnp.where(jnp.logical_or(jnp.logical_or(tpos + 1 <= topk, zero_tie), pos == kf), 1.0, 0.0)
    low0 = jnp.where(jnp.logical_or(zero_tie, pos_ge), 0.0, -inf)
    lo0 = jnp.where(pos_ge, jnp.maximum(mn, 0.0), mn)
    hi0 = jnp.where(pos_ge, mx, jnp.minimum(mx, 0.0))
    tie0 = jnp.where(zero_tie, 0.0, inf)
    need0 = jnp.where(zero_tie, kf - pos, 0.0)

    def next_value_above(thr):
        t8 = lanes8(thr)

        def body(jp, accs):
            base = pl.multiple_of(jp * 2 * blk, 2 * blk)
            accs = list(accs)
            for s in range(2 * blk // COUNT_ROWS):
                x = sc_ref[pl.ds(base + s * COUNT_ROWS, COUNT_ROWS), :]
                for g in range(COUNT_ROWS // SUBLANES):
                    xg = x[g * SUBLANES:(g + 1) * SUBLANES]
                    accs[g % COUNT_CHAINS] = jnp.minimum(accs[g % COUNT_CHAINS], jnp.where(xg > t8, xg, inf))
            return tuple(accs)

        accs = lax.fori_loop(0, npair, body, tuple(jnp.full((SUBLANES, blk), inf, F32) for _ in range(COUNT_CHAINS)))
        return colmin(functools.reduce(jnp.minimum, accs))

    def climb_cond(st):
        return jnp.min(st[1]) < 0.5

    def climb_body(st):
        low, done, tie, need = st
        live = done < 0.5
        cand = next_value_above(low)
        cgt = count_gt(cand)
        found = jnp.logical_and(live, cgt < kf)
        tie = jnp.where(found, cand, tie)
        need = jnp.where(found, kf - cgt, need)
        low = jnp.where(live, cand, low)
        done = jnp.where(jnp.logical_and(live, cgt <= kf), 1.0, done)
        return low, done, tie, need

    def select(_):
        _, _, low, done = lax.fori_loop(0, BISECT_STEPS, bisect, (lo0, hi0, low0, done0))
        low, _, tie, need = lax.while_loop(climb_cond, climb_body, (low, done, tie0, need0))
        return low, tie, need

    low, tie, need = lax.cond(jnp.min(done0) < 0.5, select, lambda _: (low0, tie0, need0), 0)

    def plain_mask(_):
        low8 = lanes8(low)

        def body(j, _):
            r0 = chunk_start(j)
            x = as_groups(sc_ref[pl.ds(r0, blk), :])
            sc_ref[pl.ds(r0, blk), :] = jnp.where(x > low8[None], 0.0, -inf).reshape(blk, blk)
            return 0

        return lax.fori_loop(0, nch, body, 0)

    def ranked_chunk(j, tri, base):
        r0 = chunk_start(j)
        x = sc_ref[pl.ds(r0, blk), :]
        eq = x == tie
        rank = jnp.dot(tri, jnp.where(eq, 1.0, 0.0).astype(BF16), preferred_element_type=F32)
        sel = jnp.logical_or(x > low, jnp.logical_and(eq, rank + base <= need))
        sc_ref[pl.ds(r0, blk), :] = jnp.where(sel, 0.0, -inf)
        return rank[blk - 1:blk, :]

    def tie_mask(_):
        tri = jnp.where(row >= col, 1.0, 0.0).astype(BF16)
        lax.fori_loop(0, nch, lambda j, base: base + ranked_chunk(j, tri, base), jnp.zeros((1, blk), F32))
        return 0

    def zero_tie_mask(_):
        tri = jnp.where(row >= col, 1.0, 0.0).astype(BF16)

        def cut_body(j, ncut):
            return ncut + jnp.where(colsum(zc_ref[j]) < need, 1.0, 0.0)

        ncut = lax.fori_loop(0, nch, cut_body, jnp.zeros((1, blk), F32))
        cut = jnp.where(tie < inf, ncut, inf)

        def flag_body(j, _):
            ranked_ref[j] = 0
            return 0

        lax.fori_loop(0, nch, flag_body, 0)

        def cut_cond(pending):
            return jnp.min(pending) < inf

        def cut_chunk(pending):
            jf = jnp.min(pending)
            j = jf.astype(jnp.int32)
            base = jnp.where(j > 0, colsum(zc_ref[jnp.maximum(j - 1, 0)]), 0.0)
            ranked_chunk(j, tri, base)
            ranked_ref[j] = 1
            return jnp.where(pending == jf, inf, pending)

        lax.while_loop(cut_cond, cut_chunk, cut)
        low8, cut8 = lanes8(low), lanes8(cut)

        def body(j, _):
            @pl.when(ranked_ref[j] == 0)
            def _():
                r0 = chunk_start(j)
                x = as_groups(sc_ref[pl.ds(r0, blk), :])
                at_low = jnp.where(jnp.logical_and(cut8 < inf, cut8 > j.astype(F32)), 0.0, -inf)
                out = jnp.where(x > low8[None], 0.0, jnp.where(x == low8[None], at_low[None], -inf))
                sc_ref[pl.ds(r0, blk), :] = out.reshape(blk, blk)
            return 0

        return lax.fori_loop(0, nch, body, 0)

    tied = tie < inf
    mask_kind = jnp.where(jnp.max(jnp.where(tied, 1.0, 0.0)) < 0.5, 0,
                          jnp.where(jnp.max(jnp.where(jnp.logical_and(tied, tie != 0.0), 1.0, 0.0)) < 0.5, 1, 2))
    lax.switch(mask_kind, [plain_mask, zero_tie_mask, tie_mask], 0)

    q = q_ref[0]
    lane = lax.broadcasted_iota(jnp.int32, (blk, LANES), 1)
    for h in range(N_HEADS):
        hp = h // 2
        in_head = (lane < HEAD_DIM) if h % 2 == 0 else (lane >= HEAD_DIM)
        qh_ref[h] = jnp.where(in_head, q[:, hp * LANES:(hp + 1) * LANES], jnp.zeros((), BF16))

    def logits(j, h, mb, near):
        hp = h // 2
        kc = k_ref[0, pl.ds(chunk_start(j), blk), hp * LANES:(hp + 1) * LANES]
        lg = lax.dot_general(kc, qh_ref[h], NT_DIMS, preferred_element_type=F32) + mb
        return lg if near is None else lg + bt_ref[near, h]

    CMAX, RMAX, RESC = 0, 1, 2

    def stage_logits(j, near):
        mb = sc_ref[pl.ds(chunk_start(j), blk), :]
        cm8 = []
        for h in range(N_HEADS):
            lg = logits(j, h, mb, near)
            lg_ref[h] = lg
            cm8.append(jnp.max(as_groups(lg), axis=0))
        for h in range(N_HEADS):
            st_ref[CMAX, h] = lanes8(colmax(cm8[h]))

    def stage_exp():
        for h in range(N_HEADS):
            m_old = st_ref[RMAX, h]
            m_new = jnp.maximum(m_old, st_ref[CMAX, h])
            m_use = jnp.where(m_new == -inf, 0.0, m_new)
            p = jnp.exp2(as_groups(lg_ref[h]) - m_use[None])
            p_ref[h] = p.reshape(blk, blk).astype(BF16)
            st_ref[RMAX, h] = m_new
            st_ref[RESC, h] = jnp.exp2(m_old - m_use)

    def stage_pv(j):
        for h in range(N_HEADS):
            rows = slice(h * HEAD_ROWS, (h + 1) * HEAD_ROWS)
            pv = jnp.dot(vt_ref[0, j, rows, :], p_ref[h], preferred_element_type=F32)
            acc = acc_ref[rows, :].reshape(HEAD_ROWS // SUBLANES, SUBLANES, blk) * st_ref[RESC, h][None]
            acc_ref[rows, :] = acc.reshape(HEAD_ROWS, blk) + pv

    def pipelined(j, near_next):
        stage_exp()
        stage_logits(j + 1, near_next)
        stage_pv(j)

    def exact_attention():
        acc_ref[...] = jnp.zeros(acc_ref.shape, F32)
        st_ref[RMAX] = jnp.full((N_HEADS, SUBLANES, blk), -inf, F32)

        def last_chunks(first):
            for j in range(first, 0):
                pipelined(i + j, -(j + 1))
            stage_exp()
            stage_pv(i)

        @pl.when(i == 0)
        def _():
            stage_logits(0, 0)
            last_chunks(0)

        @pl.when(i == 1)
        def _():
            stage_logits(0, 1)
            last_chunks(-1)

        @pl.when(i >= 2)
        def _():
            stage_logits(0, None)

        def far_body(j, _):
            pipelined(j, None)
            return 0

        lax.fori_loop(0, jnp.maximum(i - 2, 0), far_body, 0)

        @pl.when(i >= 2)
        def _():
            last_chunks(-2)

    acc_ref[...] = jnp.zeros(acc_ref.shape, F32)
    st_ref[RMAX] = jnp.zeros((N_HEADS, SUBLANES, blk), F32)

    def fast_run(chunks):
        for c, (j, near) in enumerate(chunks):
            mb = sc_ref[pl.ds(chunk_start(j), blk), :]
            for h in range(N_HEADS):
                p = jnp.exp2(as_groups(logits(j, h, mb, near)) - st_ref[RMAX, h][None])
                pp_ref[c, h] = p.reshape(blk, blk).astype(BF16)
        for h in range(N_HEADS):
            rows = slice(h * HEAD_ROWS, (h + 1) * HEAD_ROWS)
            tot = acc_ref[rows, :]
            for c, (j, _) in enumerate(chunks):
                tot = tot + jnp.dot(vt_ref[0, j, rows, :], pp_ref[c, h], preferred_element_type=F32)
            den = tot[HEAD_DIM:HEAD_DIM + 1, :]
            up = lanes8(jnp.where(den > 0.0, jnp.floor(jnp.log2(den)), 0.0))
            scaled = tot.reshape(HEAD_ROWS // SUBLANES, SUBLANES, blk) * jnp.exp2(-up)[None]
            acc_ref[rows, :] = scaled.reshape(HEAD_ROWS, blk)
            st_ref[RMAX, h] = st_ref[RMAX, h] + up

    def far_run(jr, _):
        fast_run([(FAST_RUN * jr + c, None) for c in range(FAST_RUN)])
        return 0

    n_far = jnp.maximum(i - 1, 0)
    lax.fori_loop(0, n_far // FAST_RUN, far_run, 0)

    @pl.when(i == 0)
    def _():
        fast_run([(0, 0)])

    for left in range(FAST_RUN):
        @pl.when(jnp.logical_and(i >= 1, n_far % FAST_RUN == left))
        def _():
            last = [(i - 1 - left + c, None) for c in range(left)] + [(i - 1, 1), (i, 0)]
            for first in range(0, len(last), FAST_RUN):
                fast_run(last[first:first + FAST_RUN])

    acc = acc_ref[...]
    finite = jnp.min(jnp.where(jnp.isfinite(acc), 1.0, 0.0))
    dens = jnp.concatenate([acc[h * HEAD_ROWS + HEAD_DIM:h * HEAD_ROWS + HEAD_DIM + 1, :] for h in range(N_HEADS)], axis=0)
    usable = jnp.logical_and(finite > 0.5, jnp.min(dens) > 0.0)
    lax.cond(usable, lambda: None, exact_attention)

    heads = []
    for h in range(N_HEADS):
        r0 = h * HEAD_ROWS
        heads.append(acc_ref[r0:r0 + HEAD_DIM, :] / acc_ref[r0 + HEAD_DIM:r0 + HEAD_DIM + 1, :])
    o_ref[0] = jnp.concatenate(heads, axis=0).T.astype(BF16)


def _attention(q, qi, wit, k, vt, kia, kib, btiles, *, bsz, seq, topk):
    blk = ATT_BLOCK
    nblk = seq // blk
    return pl.pallas_call(
        functools.partial(_attn_kernel, topk=topk, seq=seq),
        grid=(bsz, nblk),
        in_specs=[
            pl.BlockSpec((1, blk, ATTN_DIM), lambda b, i: (b, i, 0)),
            pl.BlockSpec((1, blk, IDX_HEADS * IDX_DIM), lambda b, i: (b, i, 0)),
            pl.BlockSpec((1, IDX_HEADS, blk), lambda b, i: (b, 0, i)),
            pl.BlockSpec((1, seq, ATTN_DIM), lambda b, i: (b, 0, 0)),
            pl.BlockSpec((1, nblk, N_HEADS * HEAD_ROWS, blk), lambda b, i: (b, 0, 0, 0)),
            pl.BlockSpec((1, seq, LANES), lambda b, i: (b, 0, 0)),
            pl.BlockSpec((1, seq, LANES), lambda b, i: (b, 0, 0)),
            pl.BlockSpec(btiles.shape, lambda b, i: (0, 0, 0, 0)),
        ],
        out_specs=pl.BlockSpec((1, blk, ATTN_DIM), lambda b, i: (b, i, 0)),
        out_shape=jax.ShapeDtypeStruct((bsz, seq, ATTN_DIM), BF16),
        scratch_shapes=[
            pltpu.VMEM((seq + blk, blk), F32),
            pltpu.VMEM((nblk, SUBLANES, blk), F32),
            pltpu.SMEM((nblk,), jnp.int32),
            pltpu.VMEM((N_HEADS, blk, LANES), BF16),
            pltpu.VMEM((N_HEADS, blk, blk), F32),
            pltpu.VMEM((N_HEADS, blk, blk), BF16),
            pltpu.VMEM((FAST_RUN, N_HEADS, blk, blk), BF16),
            pltpu.VMEM((3, N_HEADS, SUBLANES, blk), F32),
            pltpu.VMEM((N_HEADS * HEAD_ROWS, blk), F32),
        ],
        compiler_params=pltpu.CompilerParams(
            dimension_semantics=("arbitrary", "arbitrary"),
            vmem_limit_bytes=_vmem_limit(
                [((blk, ATTN_DIM), BF16), ((blk, IDX_HEADS * IDX_DIM), BF16), ((IDX_HEADS, blk), F32),
                 ((seq, ATTN_DIM), BF16), ((nblk, N_HEADS * HEAD_ROWS, blk), BF16), ((seq, LANES), BF16),
                 ((seq, LANES), BF16), (btiles.shape, F32), ((blk, ATTN_DIM), BF16)],
                [((seq + blk, blk), F32), ((nblk, SUBLANES, blk), F32), ((N_HEADS, blk, LANES), BF16),
                 ((N_HEADS, blk, blk), F32), ((N_HEADS, blk, blk), BF16), ((FAST_RUN, N_HEADS, blk, blk), BF16),
                 ((3, N_HEADS, SUBLANES, blk), F32),
                 ((N_HEADS * HEAD_ROWS, blk), F32)],
                [((blk, blk), F32)] * 4 + [((blk, blk), BF16), ((ATTN_DIM, blk), F32), ((blk, ATTN_DIM), F32)])),
        name="attention",
    )(q, qi, wit, k, vt, kia, kib, btiles)


def _mix_kernel(x_ref, mod_ref, att_ref, wpg_ref, wpool_ref, ps_ref, wa_ref, wb_ref, wo_ref,
                lng_ref, lnb_ref, o_ref, pe_ref, mix_ref, *, alpha, mod_row, ln_row, tiles_per_seq):
    rows = x_ref.shape[0]
    i = pl.program_id(0)
    seq_tile = i % tiles_per_seq

    @pl.when(i == 0)
    def _():
        pe_ref[rows:, :] = jnp.zeros((POOL_HALO, POOL_DIM), F32)

    x = x_ref[...]
    sh = mod_ref[0, mod_row:mod_row + 1, :]
    sc = mod_ref[0, mod_row + 1:mod_row + 2, :]
    gate = mod_ref[0, mod_row + 2:mod_row + 3, :]
    u = (x * (1.0 + sc) + sh).astype(BF16)
    pg = jnp.dot(u, wpg_ref[...], preferred_element_type=F32)
    pe_ref[0:POOL_HALO, :] = jnp.where(seq_tile == 0, 0.0, pe_ref[rows:, :])
    pe_ref[POOL_HALO:, :] = pg[:, :POOL_DIM]
    t = seq_tile * rows + lax.broadcasted_iota(jnp.int32, (rows, 1), 0)
    for g, w in enumerate(POOL_WINDOWS):
        cols = slice(g * POOL_GROUP_DIM, (g + 1) * POOL_GROUP_DIM)
        cur = pe_ref[POOL_HALO:, cols]
        win = cur
        for back in range(1, w):
            win = win + pe_ref[POOL_HALO - back:POOL_HALO - back + rows, cols]
        cnt = jnp.minimum(t + 1, w).astype(F32)
        pooled = (win / cnt - cur).astype(BF16)
        mixed = jnp.dot(pooled, wpool_ref[g], preferred_element_type=F32)
        mix_ref[:, cols] = (mixed * ps_ref[:, cols]).astype(BF16)
    y_a = jnp.dot(mix_ref[...], wa_ref[...], preferred_element_type=F32)
    y_b = jnp.dot(att_ref[...], wb_ref[...], preferred_element_type=F32)
    d = x.shape[1]
    ga = pg[:, POOL_DIM:POOL_DIM + d]
    gb = pg[:, POOL_DIM + d:]
    merged = (jax.nn.sigmoid(ga) * y_a + jax.nn.sigmoid(gb) * y_b).astype(BF16)
    y = jnp.dot(merged, wo_ref[...], preferred_element_type=F32)
    z = alpha * x + gate * y
    o_ref[...] = _layer_norm(z, lng_ref[ln_row:ln_row + 1, :], lnb_ref[ln_row:ln_row + 1, :])


def _mix_out(x2d, mod, att2d, wpg, wpool, pool_scale, wa, wb, wo, ln_g, ln_b, *, seq, alpha, mod_row, ln_row):
    n, d = x2d.shape
    tiles_per_seq = seq // MIX_ROWS
    resident = dict(pipeline_mode=pl.Buffered(1))
    full = lambda a: pl.BlockSpec(a.shape, lambda i: (0,) * a.ndim, **resident)
    return pl.pallas_call(
        functools.partial(_mix_kernel, alpha=alpha, mod_row=mod_row, ln_row=ln_row, tiles_per_seq=tiles_per_seq),
        grid=(n // MIX_ROWS,),
        in_specs=[
            pl.BlockSpec((MIX_ROWS, d), lambda i: (i, 0)),
            pl.BlockSpec((1, N_ADA, d), lambda i: (i // tiles_per_seq, 0, 0)),
            pl.BlockSpec((MIX_ROWS, ATTN_DIM), lambda i: (i, 0)),
            full(wpg), full(wpool), full(pool_scale), full(wa), full(wb), full(wo),
            pl.BlockSpec(ln_g.shape, lambda i: (0, 0)),
            pl.BlockSpec(ln_b.shape, lambda i: (0, 0)),
        ],
        out_specs=pl.BlockSpec((MIX_ROWS, d), lambda i: (i, 0)),
        out_shape=jax.ShapeDtypeStruct((n, d), F32),
        scratch_shapes=[
            pltpu.VMEM((POOL_HALO + MIX_ROWS, POOL_DIM), F32),
            pltpu.VMEM((MIX_ROWS, POOL_DIM), BF16),
        ],
        compiler_params=pltpu.CompilerParams(
            dimension_semantics=("arbitrary",),
            vmem_limit_bytes=_vmem_limit(
                [((MIX_ROWS, d), F32), ((N_ADA, d), F32), ((MIX_ROWS, ATTN_DIM), BF16), (ln_g.shape, F32),
                 (ln_b.shape, F32), ((MIX_ROWS, d), F32)],
                [(wpg.shape, BF16), (wpool.shape, BF16), (pool_scale.shape, F32), (wa.shape, BF16), (wb.shape, BF16),
                 (wo.shape, BF16), ((POOL_HALO + MIX_ROWS, POOL_DIM), F32), ((MIX_ROWS, POOL_DIM), BF16)],
                [((MIX_ROWS, d), BF16), ((MIX_ROWS, wpg.shape[1]), F32)] + [((MIX_ROWS, d), F32)] * 5)),
        name="mix_out",
    )(x2d, mod, att2d, wpg, wpool, pool_scale, wa, wb, wo, ln_g, ln_b)


def kernel(x, c, w_ada, b_ada, ln_g, ln_b, ffn1_w_gate, ffn1_w_up, ffn1_w_down, w_in, w_pool, pool_scale,
           w_a, w_b, w_out, rel_bias, ffn2_w_gate, ffn2_w_up, ffn2_w_down):
    bsz, seq, d = x.shape
    depth = w_ada.shape[0]
    alpha = (2.0 * depth) ** 0.25
    topk = min(TOP_K, seq // 4)
    assert seq % FFN_ROWS == 0 and seq % PROJ_ROWS == 0 and seq % MIX_ROWS == 0 and seq % ATT_BLOCK == 0
    assert PROJ_ROWS % ATT_BLOCK == 0 and POOL_HALO >= max(POOL_WINDOWS) - 1
    far_bucket = _far_bucket(ATT_BLOCK + 1, max(seq - 1, ATT_BLOCK + 1))

    o_q = POOL_DIM
    o_k = o_q + ATTN_DIM
    o_v = o_k + ATTN_DIM
    o_qi = o_v + ATTN_DIM
    o_ki = o_qi + IDX_HEADS * IDX_DIM
    o_wi = o_ki + IDX_DIM
    o_ga = o_wi + IDX_HEADS

    btiles = _rel_bias_tiles(rel_bias, far_bucket)
    x2d = x.reshape(bsz * seq, d)
    for l in range(depth):
        wl = w_in[l]
        zeros_ki = jnp.zeros((d, LANES - IDX_DIM), wl.dtype)
        w_ki = wl[:, o_ki:o_wi]
        wqk = jnp.concatenate([wl[:, o_q:o_v], wl[:, o_qi:o_ki]], axis=1).astype(BF16)
        wvw_t = jnp.pad(jnp.concatenate([wl[:, o_v:o_qi], wl[:, o_wi:o_ga]], axis=1).T,
                        ((0, 2 * SUBLANES - IDX_HEADS), (0, 0))).astype(BF16)
        wki = jnp.concatenate([w_ki, zeros_ki, zeros_ki, w_ki], axis=1).astype(BF16)
        wpg = jnp.concatenate([wl[:, :POOL_DIM], wl[:, o_ga:]], axis=1).astype(BF16)

        mod = _ada_mod(c, w_ada[l], b_ada[l]).reshape(bsz, N_ADA, d)
        x2d = _ffn(x2d, mod, ln_g[l], ln_b[l], ffn1_w_gate[l].astype(BF16), ffn1_w_up[l].astype(BF16),
                   ffn1_w_down[l].astype(BF16), seq=seq, alpha=alpha, mod_row=0, ln_row=0)
        q, k, qi, vt, kia, kib, wit = _attn_proj(x2d, mod, wqk, wvw_t, wki, bsz=bsz, seq=seq, mod_row=3)
        att = _attention(q.reshape(bsz, seq, ATTN_DIM), qi.reshape(bsz, seq, IDX_HEADS * IDX_DIM), wit,
                         k.reshape(bsz, seq, ATTN_DIM), vt, kia.reshape(bsz, seq, LANES),
                         kib.reshape(bsz, seq, LANES), btiles, bsz=bsz, seq=seq, topk=topk)
        x2d = _mix_out(x2d, mod, att.reshape(bsz * seq, ATTN_DIM), wpg, w_pool[l].astype(BF16),
                       pool_scale[l].reshape(1, POOL_DIM), w_a[l].astype(BF16), w_b[l].astype(BF16),
                       w_out[l].astype(BF16), ln_g[l], ln_b[l], seq=seq, alpha=alpha, mod_row=3, ln_row=1)
        x2d = _ffn(x2d, mod, ln_g[l], ln_b[l], ffn2_w_gate[l].astype(BF16), ffn2_w_up[l].astype(BF16),
                   ffn2_w_down[l].astype(BF16), seq=seq, alpha=alpha, mod_row=6, ln_row=2)
    return x2d.reshape(bsz, seq, d)
```

```python
import functools
import math

import numpy as np
import jax
import jax.numpy as jnp
from jax import lax
from jax.experimental import pallas as pl
from jax.experimental.pallas import tpu as pltpu

POOL_WINDOWS = (2, 4, 8, 16)
POOL_GROUP_DIM = 128
POOL_DIM = len(POOL_WINDOWS) * POOL_GROUP_DIM
N_HEADS = 8
HEAD_DIM = 64
ATTN_DIM = N_HEADS * HEAD_DIM
HEAD_PAD = 16
HEAD_ROWS = HEAD_DIM + HEAD_PAD
IDX_HEADS = 8
IDX_DIM = 64
TOP_K = 256
REL_BUCKETS = 32
REL_MAX_DIST = 128
N_ADA = 9
LN_EPS = 1e-5
POOL_HALO = 16

LANES = 128
SUBLANES = 8
V7X_VMEM_BYTES = 64 * 1024 * 1024
FFN_ROWS = 512
FFN_COLS = 256
PROJ_ROWS = 512
MIX_ROWS = 512
ATT_BLOCK = 256
ADA_COLS = 1024
SCORE_RUN = 4
FAST_RUN = 4
COUNT_CHAINS = 4
COUNT_ROWS = 64
BISECT_STEPS = 14

LOG2E = math.log2(math.e)
BF16 = jnp.bfloat16
F32 = jnp.float32
NT_DIMS = (((1,), (1,)), ((), ()))


def _tile_bytes(shape, dtype):
    itemsize = jnp.dtype(dtype).itemsize
    sublanes = SUBLANES * (4 // itemsize)
    shape = (1,) * (2 - len(shape)) + tuple(shape)
    rows = -(-shape[-2] // sublanes) * sublanes
    cols = -(-shape[-1] // LANES) * LANES
    return math.prod(shape[:-2]) * rows * cols * itemsize


def _vmem_limit(pipelined, resident, temporaries, stays_in_hbm=None):
    need = (2 * sum(_tile_bytes(*b) for b in pipelined) + sum(_tile_bytes(*b) for b in resident)
            + sum(_tile_bytes(*b) for b in temporaries))
    if stays_in_hbm is not None:
        need = max(need, V7X_VMEM_BYTES - _tile_bytes(*stays_in_hbm))
    return need


def _layer_norm(z, g, b):
    mu = jnp.mean(z, axis=-1, keepdims=True)
    zc = z - mu
    var = jnp.mean(zc * zc, axis=-1, keepdims=True)
    return zc * lax.rsqrt(var + LN_EPS) * g + b


def _silu(a):
    return a * jax.nn.sigmoid(a)


def _ada_kernel(c_ref, w_ref, b_ref, o_ref):
    a = _silu(c_ref[...])
    o_ref[...] = jnp.dot(a, w_ref[...], preferred_element_type=F32) + b_ref[...]


def _ada_mod(c, w_ada, b_ada):
    bsz, d = c.shape
    n = w_ada.shape[1]
    rows = -(-bsz // SUBLANES) * SUBLANES
    c_pad = jnp.pad(c, ((0, rows - bsz), (0, 0)))
    out = pl.pallas_call(
        _ada_kernel,
        grid=(n // ADA_COLS,),
        in_specs=[
            pl.BlockSpec((rows, d), lambda j: (0, 0)),
            pl.BlockSpec((d, ADA_COLS), lambda j: (0, j)),
            pl.BlockSpec((1, ADA_COLS), lambda j: (0, j)),
        ],
        out_specs=pl.BlockSpec((rows, ADA_COLS), lambda j: (0, j)),
        out_shape=jax.ShapeDtypeStruct((rows, n), F32),
        compiler_params=pltpu.CompilerParams(
            dimension_semantics=("arbitrary",),
            vmem_limit_bytes=_vmem_limit([((rows, d), F32), ((d, ADA_COLS), F32), ((1, ADA_COLS), F32),
                                          ((rows, ADA_COLS), F32)], [], [((rows, ADA_COLS), F32)],
                                         stays_in_hbm=(w_ada.shape, F32))),
        name="ada_mod",
    )(c_pad, w_ada, b_ada.reshape(1, n))
    return out[:bsz]


def _ffn_kernel(x_ref, mod_ref, lng_ref, lnb_ref, wg_ref, wu_ref, wd_ref, o_ref, h_ref, *, alpha, mod_row, ln_row):
    x = x_ref[...]
    sh = mod_ref[0, mod_row:mod_row + 1, :]
    sc = mod_ref[0, mod_row + 1:mod_row + 2, :]
    gate = mod_ref[0, mod_row + 2:mod_row + 3, :]
    u = (x * (1.0 + sc) + sh).astype(BF16)
    d_ff = wg_ref.shape[1]
    for c in range(d_ff // FFN_COLS):
        sl = slice(c * FFN_COLS, (c + 1) * FFN_COLS)
        a = jnp.dot(u, wg_ref[:, sl], preferred_element_type=F32)
        b = jnp.dot(u, wu_ref[:, sl], preferred_element_type=F32)
        h_ref[:, sl] = (_silu(a) * b).astype(BF16)
    y = jnp.dot(h_ref[...], wd_ref[...], preferred_element_type=F32)
    z = alpha * x + (0.5 * gate) * y
    o_ref[...] = _layer_norm(z, lng_ref[ln_row:ln_row + 1, :], lnb_ref[ln_row:ln_row + 1, :])


def _ffn(x2d, mod, ln_g, ln_b, wg, wu, wd, *, seq, alpha, mod_row, ln_row):
    n, d = x2d.shape
    d_ff = wg.shape[1]
    tiles_per_seq = seq // FFN_ROWS
    resident = dict(pipeline_mode=pl.Buffered(1))
    return pl.pallas_call(
        functools.partial(_ffn_kernel, alpha=alpha, mod_row=mod_row, ln_row=ln_row),
        grid=(n // FFN_ROWS,),
        in_specs=[
            pl.BlockSpec((FFN_ROWS, d), lambda i: (i, 0)),
            pl.BlockSpec((1, N_ADA, d), lambda i: (i // tiles_per_seq, 0, 0)),
            pl.BlockSpec(ln_g.shape, lambda i: (0, 0)),
            pl.BlockSpec(ln_b.shape, lambda i: (0, 0)),
            pl.BlockSpec((d, d_ff), lambda i: (0, 0), **resident),
            pl.BlockSpec((d, d_ff), lambda i: (0, 0), **resident),
            pl.BlockSpec((d_ff, d), lambda i: (0, 0), **resident),
        ],
        out_specs=pl.BlockSpec((FFN_ROWS, d), lambda i: (i, 0)),
        out_shape=jax.ShapeDtypeStruct((n, d), F32),
        scratch_shapes=[pltpu.VMEM((FFN_ROWS, d_ff), BF16)],
        compiler_params=pltpu.CompilerParams(
            dimension_semantics=("arbitrary",),
            vmem_limit_bytes=_vmem_limit(
                [((FFN_ROWS, d), F32), ((N_ADA, d), F32), (ln_g.shape, F32), (ln_b.shape, F32), ((FFN_ROWS, d), F32)],
                [((d, d_ff), BF16), ((d, d_ff), BF16), ((d_ff, d), BF16), ((FFN_ROWS, d_ff), BF16)],
                [((FFN_ROWS, d), BF16), ((FFN_ROWS, FFN_COLS), F32), ((FFN_ROWS, FFN_COLS), F32),
                 ((FFN_ROWS, d), F32), ((FFN_ROWS, d), F32)])),
        name="ffn",
    )(x2d, mod, ln_g, ln_b, wg, wu, wd)


def _proj_kernel(x_ref, mod_ref, wqk_ref, wvw_ref, wki_ref,
                 q_ref, k_ref, qi_ref, vt_ref, kia_ref, kib_ref, wit_ref, *, mod_row):
    x = x_ref[...]
    sh = mod_ref[0, mod_row:mod_row + 1, :]
    sc = mod_ref[0, mod_row + 1:mod_row + 2, :]
    u = (x * (1.0 + sc) + sh).astype(BF16)
    qkq = jnp.dot(u, wqk_ref[...], preferred_element_type=F32)
    q_ref[...] = (qkq[:, :ATTN_DIM] * (HEAD_DIM ** -0.5 * LOG2E)).astype(BF16)
    k_ref[...] = qkq[:, ATTN_DIM:2 * ATTN_DIM].astype(BF16)
    qi_ref[...] = qkq[:, 2 * ATTN_DIM:].astype(BF16)
    kk = jnp.dot(u, wki_ref[...], preferred_element_type=F32)
    kia_ref[...] = kk[:, :LANES].astype(BF16)
    kib_ref[...] = kk[:, LANES:].astype(BF16)
    vw = lax.dot_general(wvw_ref[...], u, NT_DIMS, preferred_element_type=F32)
    vt = vw[:ATTN_DIM].astype(BF16)
    ones_rows = jnp.where(lax.broadcasted_iota(jnp.int32, (HEAD_PAD, ATT_BLOCK), 0) == 0, 1.0, 0.0).astype(BF16)
    for c in range(vt_ref.shape[1]):
        for h in range(N_HEADS):
            vt_ref[0, c, h * HEAD_ROWS:h * HEAD_ROWS + HEAD_DIM, :] = \
                vt[h * HEAD_DIM:(h + 1) * HEAD_DIM, c * ATT_BLOCK:(c + 1) * ATT_BLOCK]
            vt_ref[0, c, h * HEAD_ROWS + HEAD_DIM:(h + 1) * HEAD_ROWS, :] = ones_rows
    wit_ref[0] = vw[ATTN_DIM:ATTN_DIM + IDX_HEADS, :]


def _attn_proj(x2d, mod, wqk, wvw_t, wki, *, bsz, seq, mod_row):
    n, d = x2d.shape
    tiles_per_seq = seq // PROJ_ROWS
    chunks_per_tile = PROJ_ROWS // ATT_BLOCK
    resident = dict(pipeline_mode=pl.Buffered(1))
    row_spec = lambda cols: pl.BlockSpec((PROJ_ROWS, cols), lambda i: (i, 0))
    return pl.pallas_call(
        functools.partial(_proj_kernel, mod_row=mod_row),
        grid=(n // PROJ_ROWS,),
        in_specs=[
            pl.BlockSpec((PROJ_ROWS, d), lambda i: (i, 0)),
            pl.BlockSpec((1, N_ADA, d), lambda i: (i // tiles_per_seq, 0, 0)),
            pl.BlockSpec(wqk.shape, lambda i: (0, 0), **resident),
            pl.BlockSpec(wvw_t.shape, lambda i: (0, 0), **resident),
            pl.BlockSpec(wki.shape, lambda i: (0, 0), **resident),
        ],
        out_specs=[
            row_spec(ATTN_DIM), row_spec(ATTN_DIM), row_spec(IDX_HEADS * IDX_DIM),
            pl.BlockSpec((1, chunks_per_tile, N_HEADS * HEAD_ROWS, ATT_BLOCK),
                         lambda i: (i // tiles_per_seq, i % tiles_per_seq, 0, 0)),
            row_spec(LANES), row_spec(LANES),
            pl.BlockSpec((1, IDX_HEADS, PROJ_ROWS), lambda i: (i // tiles_per_seq, 0, i % tiles_per_seq)),
        ],
        out_shape=[
            jax.ShapeDtypeStruct((n, ATTN_DIM), BF16),
            jax.ShapeDtypeStruct((n, ATTN_DIM), BF16),
            jax.ShapeDtypeStruct((n, IDX_HEADS * IDX_DIM), BF16),
            jax.ShapeDtypeStruct((bsz, seq // ATT_BLOCK, N_HEADS * HEAD_ROWS, ATT_BLOCK), BF16),
            jax.ShapeDtypeStruct((n, LANES), BF16),
            jax.ShapeDtypeStruct((n, LANES), BF16),
            jax.ShapeDtypeStruct((bsz, IDX_HEADS, seq), F32),
        ],
        compiler_params=pltpu.CompilerParams(
            dimension_semantics=("arbitrary",),
            vmem_limit_bytes=_vmem_limit(
                [((PROJ_ROWS, d), F32), ((N_ADA, d), F32), ((PROJ_ROWS, ATTN_DIM), BF16), ((PROJ_ROWS, ATTN_DIM), BF16),
                 ((PROJ_ROWS, IDX_HEADS * IDX_DIM), BF16), ((chunks_per_tile, N_HEADS * HEAD_ROWS, ATT_BLOCK), BF16),
                 ((PROJ_ROWS, LANES), BF16), ((PROJ_ROWS, LANES), BF16), ((IDX_HEADS, PROJ_ROWS), F32)],
                [(wqk.shape, BF16), (wvw_t.shape, BF16), (wki.shape, BF16)],
                [((PROJ_ROWS, d), BF16), ((PROJ_ROWS, wqk.shape[1]), F32), ((PROJ_ROWS, wki.shape[1]), F32),
                 ((wvw_t.shape[0], PROJ_ROWS), F32)])),
        name="attn_proj",
    )(x2d, mod, wqk, wvw_t, wki)


def _t5_bucket(n):
    max_exact = REL_BUCKETS // 2
    nf = jnp.maximum(n, 1).astype(F32)
    large = max_exact + jnp.floor(jnp.log(nf / max_exact) / math.log(REL_MAX_DIST / max_exact)
                                  * (REL_BUCKETS - max_exact)).astype(jnp.int32)
    large = jnp.minimum(large, REL_BUCKETS - 1)
    return jnp.where(n < max_exact, n, large)


def _far_bucket(first_dist, last_dist):
    n = np.arange(first_dist, last_dist + 1, dtype=np.float32)
    max_exact = REL_BUCKETS // 2
    large = max_exact + (np.log(n / np.float32(max_exact)) / np.float32(math.log(REL_MAX_DIST / max_exact))
                         * np.float32(REL_BUCKETS - max_exact)).astype(np.int32)
    buckets = np.where(n < max_exact, n.astype(np.int32), np.minimum(large, REL_BUCKETS - 1))
    assert buckets.min() == buckets.max(), "key chunks two or more blocks away must share one bias bucket"
    return int(buckets[0])


def _bias_kernel(rb_ref, o_ref, *, far_bucket):
    o = pl.program_id(0)
    row = lax.broadcasted_iota(jnp.int32, (ATT_BLOCK, ATT_BLOCK), 0)
    col = lax.broadcasted_iota(jnp.int32, (ATT_BLOCK, ATT_BLOCK), 1)
    dist = o * ATT_BLOCK + col - row
    bucket = _t5_bucket(jnp.maximum(dist, 0))
    for h in range(N_HEADS):
        tile = jnp.zeros((ATT_BLOCK, ATT_BLOCK), F32)
        for b in range(REL_BUCKETS):
            tile = jnp.where(bucket == b, rb_ref[b, h], tile)
        o_ref[0, h] = (tile - rb_ref[far_bucket, h]) * LOG2E


def _rel_bias_tiles(rel_bias, far_bucket):
    return pl.pallas_call(
        functools.partial(_bias_kernel, far_bucket=far_bucket),
        grid=(2,),
        in_specs=[pl.BlockSpec(memory_space=pltpu.SMEM)],
        out_specs=pl.BlockSpec((1, N_HEADS, ATT_BLOCK, ATT_BLOCK), lambda o: (o, 0, 0, 0)),
        out_shape=jax.ShapeDtypeStruct((2, N_HEADS, ATT_BLOCK, ATT_BLOCK), F32),
        compiler_params=pltpu.CompilerParams(
            dimension_semantics=("arbitrary",),
            vmem_limit_bytes=_vmem_limit([((N_HEADS, ATT_BLOCK, ATT_BLOCK), F32)], [],
                                         [((ATT_BLOCK, ATT_BLOCK), F32)] * 3)),
        name="rel_bias",
    )(rel_bias)


def _attn_kernel(q_ref, qi_ref, wit_ref, k_ref, vt_ref, kia_ref, kib_ref, bt_ref, o_ref,
                 sc_ref, zc_ref, ranked_ref, qh_ref, lg_ref, p_ref, pp_ref, st_ref, acc_ref, *, topk, seq):
    blk = ATT_BLOCK
    groups = blk // SUBLANES
    i = pl.program_id(1)
    nch = i + 1
    kf = float(topk)
    inf = jnp.inf

    def chunk_start(j):
        return j * blk if isinstance(j, int) else pl.multiple_of(j * blk, blk)

    def as_groups(x):
        return x.reshape(groups, SUBLANES, blk)

    def lanes8(v):
        return jnp.broadcast_to(v, (SUBLANES, blk))

    def colmin(x8):
        return jnp.min(x8, axis=0, keepdims=True)

    def colmax(x8):
        return jnp.max(x8, axis=0, keepdims=True)

    def colsum(x8):
        return jnp.sum(x8, axis=0, keepdims=True)

    wf = wit_ref[0] * (IDX_DIM ** -0.5)
    qi = qi_ref[0]
    row = lax.broadcasted_iota(jnp.int32, (blk, blk), 0)
    col = lax.broadcasted_iota(jnp.int32, (blk, blk), 1)
    causal = row <= col

    def chunk_scores(j):
        r0 = chunk_start(j)
        ka = kia_ref[0, pl.ds(r0, blk), :]
        kb = kib_ref[0, pl.ds(r0, blk), :]
        s = jnp.zeros((blk, blk), F32)
        for hp in range(IDX_HEADS // 2):
            qp = qi[:, hp * LANES:(hp + 1) * LANES]
            a0 = lax.dot_general(ka, qp, NT_DIMS, preferred_element_type=F32)
            a1 = lax.dot_general(kb, qp, NT_DIMS, preferred_element_type=F32)
            s = s + jnp.maximum(a0, 0.0) * wf[2 * hp:2 * hp + 1, :]
            s = s + jnp.maximum(a1, 0.0) * wf[2 * hp + 1:2 * hp + 2, :]
        return r0, s * (IDX_HEADS ** -0.5)

    def score_stats(j, s_lo, s_hi, stats):
        mn8, mx8, pos8, zer8 = stats
        hi3 = as_groups(s_hi)
        zer8 = zer8 + jnp.sum(jnp.where(hi3 == 0.0, 1.0, 0.0), axis=0)
        zc_ref[j] = zer8
        return (jnp.minimum(mn8, jnp.min(as_groups(s_lo), axis=0)),
                jnp.maximum(mx8, jnp.max(hi3, axis=0)),
                pos8 + jnp.sum(jnp.where(hi3 > 0.0, 1.0, 0.0), axis=0),
                zer8)

    def score_body(j, stats):
        r0, s = chunk_scores(j)
        sc_ref[pl.ds(r0, blk), :] = s
        return score_stats(j, s, s, stats)

    def score_run(first, count, stats):
        for c in range(count):
            stats = score_body(first + c, stats)
        return stats

    def score_diag(st):
        r_diag, s_diag = chunk_scores(i)
        s_diag_hi = jnp.where(causal, s_diag, -inf)
        sc_ref[pl.ds(r_diag, blk), :] = s_diag_hi
        return score_stats(i, jnp.where(causal, s_diag, inf), s_diag_hi, st)

    zeros8 = jnp.zeros((SUBLANES, blk), F32)
    stats = (jnp.full((SUBLANES, blk), inf, F32), jnp.full((SUBLANES, blk), -inf, F32), zeros8, zeros8)
    stats = lax.fori_loop(0, i // SCORE_RUN, lambda jq, st: score_run(SCORE_RUN * jq, SCORE_RUN, st), stats)
    left = i % SCORE_RUN
    mn8, mx8, pos8, zer8 = lax.switch(
        left, [functools.partial(lambda r, st: score_diag(score_run(i - r, r, st)), r) for r in range(SCORE_RUN)], stats)

    @pl.when(nch % 2 == 1)
    def _():
        sc_ref[pl.ds(chunk_start(nch), blk), :] = jnp.full((blk, blk), -inf, F32)

    npair = (nch + 1) // 2

    def count_gt(thr):
        t8 = lanes8(thr)

        def body(jp, accs):
            base = pl.multiple_of(jp * 2 * blk, 2 * blk)
            accs = list(accs)
            for s in range(2 * blk // COUNT_ROWS):
                x = sc_ref[pl.ds(base + s * COUNT_ROWS, COUNT_ROWS), :]
                for g in range(COUNT_ROWS // SUBLANES):
                    hit = jnp.where(x[g * SUBLANES:(g + 1) * SUBLANES] > t8, 1.0, 0.0)
                    accs[g % COUNT_CHAINS] = accs[g % COUNT_CHAINS] + hit
            return tuple(accs)

        accs = lax.fori_loop(0, npair, body, tuple(jnp.zeros((SUBLANES, blk), F32) for _ in range(COUNT_CHAINS)))
        return colsum(sum(accs[1:], accs[0]))

    def bisect(_, st):
        lo, hi, low, done = st
        mid = 0.5 * lo + 0.5 * hi
        c = count_gt(mid)
        live = done < 0.5
        up = jnp.logical_and(live, c >= kf)
        down = jnp.logical_and(live, c < kf)
        low = jnp.where(up, mid, low)
        lo = jnp.where(up, mid, lo)
        hi = jnp.where(down, mid, hi)
        done = jnp.where(c == kf, 1.0, done)
        return lo, hi, low, done

    tpos = i * blk + lax.broadcasted_iota(jnp.int32, (1, blk), 1)
    pos, zer = colsum(pos8), colsum(zer8)
    mn, mx = colmin(mn8), colmax(mx8)
    zero_tie = jnp.logical_and(pos < kf, pos + zer >= kf)
    pos_ge = pos >= kf
    done0 = jnp.where(jnp.logical_or(jnp.logical_or(tpos + 1 <= topk, zero_tie), pos == kf), 1.0, 0.0)
    low0 = jnp.where(jnp.logical_or(zero_tie, pos_ge), 0.0, -inf)
    lo0 = jnp.where(pos_ge, jnp.maximum(mn, 0.0), mn)
    hi0 = jnp.where(pos_ge, mx, jnp.minimum(mx, 0.0))
    tie0 = jnp.where(zero_tie, 0.0, inf)
    need0 = jnp.where(zero_tie, kf - pos, 0.0)

    def next_value_above(thr):
        t8 = lanes8(thr)

        def body(jp, accs):
            base = pl.multiple_of(jp * 2 * blk, 2 * blk)
            accs = list(accs)
            for s in range(2 * blk // COUNT_ROWS):
                x = sc_ref[pl.ds(base + s * COUNT_ROWS, COUNT_ROWS), :]
                for g in range(COUNT_ROWS // SUBLANES):
                    xg = x[g * SUBLANES:(g + 1) * SUBLANES]
                    accs[g % COUNT_CHAINS] = jnp.minimum(accs[g % COUNT_CHAINS], jnp.where(xg > t8, xg, inf))
            return tuple(accs)

        accs = lax.fori_loop(0, npair, body, tuple(jnp.full((SUBLANES, blk), inf, F32) for _ in range(COUNT_CHAINS)))
        return colmin(functools.reduce(jnp.minimum, accs))

    def climb_cond(st):
        return jnp.min(st[1]) < 0.5

    def climb_body(st):
        low, done, tie, need = st
        live = done < 0.5
        cand = next_value_above(low)
        cgt = count_gt(cand)
        found = jnp.logical_and(live, cgt < kf)
        tie = jnp.where(found, cand, tie)
        need = jnp.where(found, kf - cgt, need)
        low = jnp.where(live, cand, low)
        done = jnp.where(jnp.logical_and(live, cgt <= kf), 1.0, done)
        return low, done, tie, need

    def select(_):
        _, _, low, done = lax.fori_loop(0, BISECT_STEPS, bisect, (lo0, hi0, low0, done0))
        low, _, tie, need = lax.while_loop(climb_cond, climb_body, (low, done, tie0, need0))
        return low, tie, need

    low, tie, need = lax.cond(jnp.min(done0) < 0.5, select, lambda _: (low0, tie0, need0), 0)

    def plain_mask(_):
        low8 = lanes8(low)

        def body(j, _):
            r0 = chunk_start(j)
            x = as_groups(sc_ref[pl.ds(r0, blk), :])
            sc_ref[pl.ds(r0, blk), :] = jnp.where(x > low8[None], 0.0, -inf).reshape(blk, blk)
            return 0

        return lax.fori_loop(0, nch, body, 0)

    def ranked_chunk(j, tri, base):
        r0 = chunk_start(j)
        x = sc_ref[pl.ds(r0, blk), :]
        eq = x == tie
        rank = jnp.dot(tri, jnp.where(eq, 1.0, 0.0).astype(BF16), preferred_element_type=F32)
        sel = jnp.logical_or(x > low, jnp.logical_and(eq, rank + base <= need))
        sc_ref[pl.ds(r0, blk), :] = jnp.where(sel, 0.0, -inf)
        return rank[blk - 1:blk, :]

    def tie_mask(_):
        tri = jnp.where(row >= col, 1.0, 0.0).astype(BF16)
        lax.fori_loop(0, nch, lambda j, base: base + ranked_chunk(j, tri, base), jnp.zeros((1, blk), F32))
        return 0

    def zero_tie_mask(_):
        tri = jnp.where(row >= col, 1.0, 0.0).astype(BF16)

        def cut_body(j, ncut):
            return ncut + jnp.where(colsum(zc_ref[j]) < need, 1.0, 0.0)

        ncut = lax.fori_loop(0, nch, cut_body, jnp.zeros((1, blk), F32))
        cut = jnp.where(tie < inf, ncut, inf)

        def flag_body(j, _):
            ranked_ref[j] = 0
            return 0

        lax.fori_loop(0, nch, flag_body, 0)

        def cut_cond(pending):
            return jnp.min(pending) < inf

        def cut_chunk(pending):
            jf = jnp.min(pending)
            j = jf.astype(jnp.int32)
            base = jnp.where(j > 0, colsum(zc_ref[jnp.maximum(j - 1, 0)]), 0.0)
            ranked_chunk(j, tri, base)
            ranked_ref[j] = 1
            return jnp.where(pending == jf, inf, pending)

        lax.while_loop(cut_cond, cut_chunk, cut)
        low8, cut8 = lanes8(low), lanes8(cut)

        def body(j, _):
            @pl.when(ranked_ref[j] == 0)
            def _():
                r0 = chunk_start(j)
                x = as_groups(sc_ref[pl.ds(r0, blk), :])
                at_low = jnp.where(jnp.logical_and(cut8 < inf, cut8 > j.astype(F32)), 0.0, -inf)
                out = jnp.where(x > low8[None], 0.0, jnp.where(x == low8[None], at_low[None], -inf))
                sc_ref[pl.ds(r0, blk), :] = out.reshape(blk, blk)
            return 0

        return lax.fori_loop(0, nch, body, 0)

    tied = tie < inf
    mask_kind = jnp.where(jnp.max(jnp.where(tied, 1.0, 0.0)) < 0.5, 0,
                          jnp.where(jnp.max(jnp.where(jnp.logical_and(tied, tie != 0.0), 1.0, 0.0)) < 0.5, 1, 2))
    lax.switch(mask_kind, [plain_mask, zero_tie_mask, tie_mask], 0)

    q = q_ref[0]
    lane = lax.broadcasted_iota(jnp.int32, (blk, LANES), 1)
    for h in range(N_HEADS):
        hp = h // 2
        in_head = (lane < HEAD_DIM) if h % 2 == 0 else (lane >= HEAD_DIM)
        qh_ref[h] = jnp.where(in_head, q[:, hp * LANES:(hp + 1) * LANES], jnp.zeros((), BF16))

    def logits(j, h, mb, near):
        hp = h // 2
        kc = k_ref[0, pl.ds(chunk_start(j), blk), hp * LANES:(hp + 1) * LANES]
        lg = lax.dot_general(kc, qh_ref[h], NT_DIMS, preferred_element_type=F32) + mb
        return lg if near is None else lg + bt_ref[near, h]

    CMAX, RMAX, RESC = 0, 1, 2

    def stage_logits(j, near):
        mb = sc_ref[pl.ds(chunk_start(j), blk), :]
        cm8 = []
        for h in range(N_HEADS):
            lg = logits(j, h, mb, near)
            lg_ref[h] = lg
            cm8.append(jnp.max(as_groups(lg), axis=0))
        for h in range(N_HEADS):
            st_ref[CMAX, h] = lanes8(colmax(cm8[h]))

    def stage_exp():
        for h in range(N_HEADS):
            m_old = st_ref[RMAX, h]
            m_new = jnp.maximum(m_old, st_ref[CMAX, h])
            m_use = jnp.where(m_new == -inf, 0.0, m_new)
            p = jnp.exp2(as_groups(lg_ref[h]) - m_use[None])
            p_ref[h] = p.reshape(blk, blk).astype(BF16)
            st_ref[RMAX, h] = m_new
            st_ref[RESC, h] = jnp.exp2(m_old - m_use)

    def stage_pv(j):
        for h in range(N_HEADS):
            rows = slice(h * HEAD_ROWS, (h + 1) * HEAD_ROWS)
            pv = jnp.dot(vt_ref[0, j, rows, :], p_ref[h], preferred_element_type=F32)
            acc = acc_ref[rows, :].reshape(HEAD_ROWS // SUBLANES, SUBLANES, blk) * st_ref[RESC, h][None]
            acc_ref[rows, :] = acc.reshape(HEAD_ROWS, blk) + pv

    def pipelined(j, near_next):
        stage_exp()
        stage_logits(j + 1, near_next)
        stage_pv(j)

    def exact_attention():
        acc_ref[...] = jnp.zeros(acc_ref.shape, F32)
        st_ref[RMAX] = jnp.full((N_HEADS, SUBLANES, blk), -inf, F32)

        def last_chunks(first):
            for j in range(first, 0):
                pipelined(i + j, -(j + 1))
            stage_exp()
            stage_pv(i)

        @pl.when(i == 0)
        def _():
            stage_logits(0, 0)
            last_chunks(0)

        @pl.when(i == 1)
        def _():
            stage_logits(0, 1)
            last_chunks(-1)

        @pl.when(i >= 2)
        def _():
            stage_logits(0, None)

        def far_body(j, _):
            pipelined(j, None)
            return 0

        lax.fori_loop(0, jnp.maximum(i - 2, 0), far_body, 0)

        @pl.when(i >= 2)
        def _():
            last_chunks(-2)

    acc_ref[...] = jnp.zeros(acc_ref.shape, F32)
    st_ref[RMAX] = jnp.zeros((N_HEADS, SUBLANES, blk), F32)

    def fast_run(chunks):
        for c, (j, near) in enumerate(chunks):
            mb = sc_ref[pl.ds(chunk_start(j), blk), :]
            for h in range(N_HEADS):
                p = jnp.exp2(as_groups(logits(j, h, mb, near)) - st_ref[RMAX, h][None])
                pp_ref[c, h] = p.reshape(blk, blk).astype(BF16)
        for h in range(N_HEADS):
            rows = slice(h * HEAD_ROWS, (h + 1) * HEAD_ROWS)
            tot = acc_ref[rows, :]
            for c, (j, _) in enumerate(chunks):
                tot = tot + jnp.dot(vt_ref[0, j, rows, :], pp_ref[c, h], preferred_element_type=F32)
            den = tot[HEAD_DIM:HEAD_DIM + 1, :]
            up = lanes8(jnp.where(den > 0.0, jnp.floor(jnp.log2(den)), 0.0))
            scaled = tot.reshape(HEAD_ROWS // SUBLANES, SUBLANES, blk) * jnp.exp2(-up)[None]
            acc_ref[rows, :] = scaled.reshape(HEAD_ROWS, blk)
            st_ref[RMAX, h] = st_ref[RMAX, h] + up

    def far_run(jr, _):
        fast_run([(FAST_RUN * jr + c, None) for c in range(FAST_RUN)])
        return 0

    n_far = jnp.maximum(i - 1, 0)
    lax.fori_loop(0, n_far // FAST_RUN, far_run, 0)

    @pl.when(i == 0)
    def _():
        fast_run([(0, 0)])

    for left in range(FAST_RUN):
        @pl.when(jnp.logical_and(i >= 1, n_far % FAST_RUN == left))
        def _():
            last = [(i - 1 - left + c, None) for c in range(left)] + [(i - 1, 1), (i, 0)]
            for first in range(0, len(last), FAST_RUN):
                fast_run(last[first:first + FAST_RUN])

    acc = acc_ref[...]
    finite = jnp.min(jnp.where(jnp.isfinite(acc), 1.0, 0.0))
    dens = jnp.concatenate([acc[h * HEAD_ROWS + HEAD_DIM:h * HEAD_ROWS + HEAD_DIM + 1, :] for h in range(N_HEADS)], axis=0)
    usable = jnp.logical_and(finite > 0.5, jnp.min(dens) > 0.0)
    lax.cond(usable, lambda: None, exact_attention)

    heads = []
    for h in range(N_HEADS):
        r0 = h * HEAD_ROWS
        heads.append(acc_ref[r0:r0 + HEAD_DIM, :] / acc_ref[r0 + HEAD_DIM:r0 + HEAD_DIM + 1, :])
    o_ref[0] = jnp.concatenate(heads, axis=0).T.astype(BF16)


def _attention(q, qi, wit, k, vt, kia, kib, btiles, *, bsz, seq, topk):
    blk = ATT_BLOCK
    nblk = seq // blk
    return pl.pallas_call(
        functools.partial(_attn_kernel, topk=topk, seq=seq),
        grid=(bsz, nblk),
        in_specs=[
            pl.BlockSpec((1, blk, ATTN_DIM), lambda b, i: (b, i, 0)),
            pl.BlockSpec((1, blk, IDX_HEADS * IDX_DIM), lambda b, i: (b, i, 0)),
            pl.BlockSpec((1, IDX_HEADS, blk), lambda b, i: (b, 0, i)),
            pl.BlockSpec((1, seq, ATTN_DIM), lambda b, i: (b, 0, 0)),
            pl.BlockSpec((1, nblk, N_HEADS * HEAD_ROWS, blk), lambda b, i: (b, 0, 0, 0)),
            pl.BlockSpec((1, seq, LANES), lambda b, i: (b, 0, 0)),
            pl.BlockSpec((1, seq, LANES), lambda b, i: (b, 0, 0)),
            pl.BlockSpec(btiles.shape, lambda b, i: (0, 0, 0, 0)),
        ],
        out_specs=pl.BlockSpec((1, blk, ATTN_DIM), lambda b, i: (b, i, 0)),
        out_shape=jax.ShapeDtypeStruct((bsz, seq, ATTN_DIM), BF16),
        scratch_shapes=[
            pltpu.VMEM((seq + blk, blk), F32),
            pltpu.VMEM((nblk, SUBLANES, blk), F32),
            pltpu.SMEM((nblk,), jnp.int32),
            pltpu.VMEM((N_HEADS, blk, LANES), BF16),
            pltpu.VMEM((N_HEADS, blk, blk), F32),
            pltpu.VMEM((N_HEADS, blk, blk), BF16),
            pltpu.VMEM((FAST_RUN, N_HEADS, blk, blk), BF16),
            pltpu.VMEM((3, N_HEADS, SUBLANES, blk), F32),
            pltpu.VMEM((N_HEADS * HEAD_ROWS, blk), F32),
        ],
        compiler_params=pltpu.CompilerParams(
            dimension_semantics=("arbitrary", "arbitrary"),
            vmem_limit_bytes=_vmem_limit(
                [((blk, ATTN_DIM), BF16), ((blk, IDX_HEADS * IDX_DIM), BF16), ((IDX_HEADS, blk), F32),
                 ((seq, ATTN_DIM), BF16), ((nblk, N_HEADS * HEAD_ROWS, blk), BF16), ((seq, LANES), BF16),
                 ((seq, LANES), BF16), (btiles.shape, F32), ((blk, ATTN_DIM), BF16)],
                [((seq + blk, blk), F32), ((nblk, SUBLANES, blk), F32), ((N_HEADS, blk, LANES), BF16),
                 ((N_HEADS, blk, blk), F32), ((N_HEADS, blk, blk), BF16), ((FAST_RUN, N_HEADS, blk, blk), BF16),
                 ((3, N_HEADS, SUBLANES, blk), F32),
                 ((N_HEADS * HEAD_ROWS, blk), F32)],
                [((blk, blk), F32)] * 4 + [((blk, blk), BF16), ((ATTN_DIM, blk), F32), ((blk, ATTN_DIM), F32)],
                stays_in_hbm=((bsz, seq, ATTN_DIM), BF16))),
        name="attention",
    )(q, qi, wit, k, vt, kia, kib, btiles)


def _mix_kernel(x_ref, mod_ref, att_ref, wpg_ref, wpool_ref, ps_ref, wa_ref, wb_ref, wo_ref,
                lng_ref, lnb_ref, o_ref, pe_ref, mix_ref, *, alpha, mod_row, ln_row, tiles_per_seq):
    rows = x_ref.shape[0]
    i = pl.program_id(0)
    seq_tile = i % tiles_per_seq

    @pl.when(i == 0)
    def _():
        pe_ref[rows:, :] = jnp.zeros((POOL_HALO, POOL_DIM), F32)

    x = x_ref[...]
    sh = mod_ref[0, mod_row:mod_row + 1, :]
    sc = mod_ref[0, mod_row + 1:mod_row + 2, :]
    gate = mod_ref[0, mod_row + 2:mod_row + 3, :]
    u = (x * (1.0 + sc) + sh).astype(BF16)
    pg = jnp.dot(u, wpg_ref[...], preferred_element_type=F32)
    pe_ref[0:POOL_HALO, :] = jnp.where(seq_tile == 0, 0.0, pe_ref[rows:, :])
    pe_ref[POOL_HALO:, :] = pg[:, :POOL_DIM]
    t = seq_tile * rows + lax.broadcasted_iota(jnp.int32, (rows, 1), 0)
    for g, w in enumerate(POOL_WINDOWS):
        cols = slice(g * POOL_GROUP_DIM, (g + 1) * POOL_GROUP_DIM)
        cur = pe_ref[POOL_HALO:, cols]
        win = cur
        for back in range(1, w):
            win = win + pe_ref[POOL_HALO - back:POOL_HALO - back + rows, cols]
        cnt = jnp.minimum(t + 1, w).astype(F32)
        pooled = (win / cnt - cur).astype(BF16)
        mixed = jnp.dot(pooled, wpool_ref[g], preferred_element_type=F32)
        mix_ref[:, cols] = (mixed * ps_ref[:, cols]).astype(BF16)
    y_a = jnp.dot(mix_ref[...], wa_ref[...], preferred_element_type=F32)
    y_b = jnp.dot(att_ref[...], wb_ref[...], preferred_element_type=F32)
    d = x.shape[1]
    ga = pg[:, POOL_DIM:POOL_DIM + d]
    gb = pg[:, POOL_DIM + d:]
    merged = (jax.nn.sigmoid(ga) * y_a + jax.nn.sigmoid(gb) * y_b).astype(BF16)
    y = jnp.dot(merged, wo_ref[...], preferred_element_type=F32)
    z = alpha * x + gate * y
    o_ref[...] = _layer_norm(z, lng_ref[ln_row:ln_row + 1, :], lnb_ref[ln_row:ln_row + 1, :])


def _mix_out(x2d, mod, att2d, wpg, wpool, pool_scale, wa, wb, wo, ln_g, ln_b, *, seq, alpha, mod_row, ln_row):
    n, d = x2d.shape
    tiles_per_seq = seq // MIX_ROWS
    resident = dict(pipeline_mode=pl.Buffered(1))
    full = lambda a: pl.BlockSpec(a.shape, lambda i: (0,) * a.ndim, **resident)
    return pl.pallas_call(
        functools.partial(_mix_kernel, alpha=alpha, mod_row=mod_row, ln_row=ln_row, tiles_per_seq=tiles_per_seq),
        grid=(n // MIX_ROWS,),
        in_specs=[
            pl.BlockSpec((MIX_ROWS, d), lambda i: (i, 0)),
            pl.BlockSpec((1, N_ADA, d), lambda i: (i // tiles_per_seq, 0, 0)),
            pl.BlockSpec((MIX_ROWS, ATTN_DIM), lambda i: (i, 0)),
            full(wpg), full(wpool), full(pool_scale), full(wa), full(wb), full(wo),
            pl.BlockSpec(ln_g.shape, lambda i: (0, 0)),
            pl.BlockSpec(ln_b.shape, lambda i: (0, 0)),
        ],
        out_specs=pl.BlockSpec((MIX_ROWS, d), lambda i: (i, 0)),
        out_shape=jax.ShapeDtypeStruct((n, d), F32),
        scratch_shapes=[
            pltpu.VMEM((POOL_HALO + MIX_ROWS, POOL_DIM), F32),
            pltpu.VMEM((MIX_ROWS, POOL_DIM), BF16),
        ],
        compiler_params=pltpu.CompilerParams(
            dimension_semantics=("arbitrary",),
            vmem_limit_bytes=_vmem_limit(
                [((MIX_ROWS, d), F32), ((N_ADA, d), F32), ((MIX_ROWS, ATTN_DIM), BF16), (ln_g.shape, F32),
                 (ln_b.shape, F32), ((MIX_ROWS, d), F32)],
                [(wpg.shape, BF16), (wpool.shape, BF16), (pool_scale.shape, F32), (wa.shape, BF16), (wb.shape, BF16),
                 (wo.shape, BF16), ((POOL_HALO + MIX_ROWS, POOL_DIM), F32), ((MIX_ROWS, POOL_DIM), BF16)],
                [((MIX_ROWS, d), BF16), ((MIX_ROWS, wpg.shape[1]), F32)] + [((MIX_ROWS, d), F32)] * 5)),
        name="mix_out",
    )(x2d, mod, att2d, wpg, wpool, pool_scale, wa, wb, wo, ln_g, ln_b)


def kernel(x, c, w_ada, b_ada, ln_g, ln_b, ffn1_w_gate, ffn1_w_up, ffn1_w_down, w_in, w_pool, pool_scale,
           w_a, w_b, w_out, rel_bias, ffn2_w_gate, ffn2_w_up, ffn2_w_down):
    bsz, seq, d = x.shape
    depth = w_ada.shape[0]
    alpha = (2.0 * depth) ** 0.25
    topk = min(TOP_K, seq // 4)
    assert seq % FFN_ROWS == 0 and seq % PROJ_ROWS == 0 and seq % MIX_ROWS == 0 and seq % ATT_BLOCK == 0
    assert PROJ_ROWS % ATT_BLOCK == 0 and POOL_HALO >= max(POOL_WINDOWS) - 1
    far_bucket = _far_bucket(ATT_BLOCK + 1, max(seq - 1, ATT_BLOCK + 1))

    o_q = POOL_DIM
    o_k = o_q + ATTN_DIM
    o_v = o_k + ATTN_DIM
    o_qi = o_v + ATTN_DIM
    o_ki = o_qi + IDX_HEADS * IDX_DIM
    o_wi = o_ki + IDX_DIM
    o_ga = o_wi + IDX_HEADS

    btiles = _rel_bias_tiles(rel_bias, far_bucket)
    x2d = x.reshape(bsz * seq, d)
    for l in range(depth):
        wl = w_in[l]
        zeros_ki = jnp.zeros((d, LANES - IDX_DIM), wl.dtype)
        w_ki = wl[:, o_ki:o_wi]
        wqk = jnp.concatenate([wl[:, o_q:o_v], wl[:, o_qi:o_ki]], axis=1).astype(BF16)
        wvw_t = jnp.pad(jnp.concatenate([wl[:, o_v:o_qi], wl[:, o_wi:o_ga]], axis=1).T,
                        ((0, 2 * SUBLANES - IDX_HEADS), (0, 0))).astype(BF16)
        wki = jnp.concatenate([w_ki, zeros_ki, zeros_ki, w_ki], axis=1).astype(BF16)
        wpg = jnp.concatenate([wl[:, :POOL_DIM], wl[:, o_ga:]], axis=1).astype(BF16)

        mod = _ada_mod(c, w_ada[l], b_ada[l]).reshape(bsz, N_ADA, d)
        x2d = _ffn(x2d, mod, ln_g[l], ln_b[l], ffn1_w_gate[l].astype(BF16), ffn1_w_up[l].astype(BF16),
                   ffn1_w_down[l].astype(BF16), seq=seq, alpha=alpha, mod_row=0, ln_row=0)
        q, k, qi, vt, kia, kib, wit = _attn_proj(x2d, mod, wqk, wvw_t, wki, bsz=bsz, seq=seq, mod_row=3)
        att = _attention(q.reshape(bsz, seq, ATTN_DIM), qi.reshape(bsz, seq, IDX_HEADS * IDX_DIM), wit,
                         k.reshape(bsz, seq, ATTN_DIM), vt, kia.reshape(bsz, seq, LANES),
                         kib.reshape(bsz, seq, LANES), btiles, bsz=bsz, seq=seq, topk=topk)
        x2d = _mix_out(x2d, mod, att.reshape(bsz * seq, ATTN_DIM), wpg, w_pool[l].astype(BF16),
                       pool_scale[l].reshape(1, POOL_DIM), w_a[l].astype(BF16), w_b[l].astype(BF16),
                       w_out[l].astype(BF16), ln_g[l], ln_b[l], seq=seq, alpha=alpha, mod_row=3, ln_row=1)
        x2d = _ffn(x2d, mod, ln_g[l], ln_b[l], ffn2_w_gate[l].astype(BF16), ffn2_w_up[l].astype(BF16),
                   ffn2_w_down[l].astype(BF16), seq=seq, alpha=alpha, mod_row=6, ln_row=2)
    return x2d.reshape(bsz, seq, d)
```

```python
import functools
import math

import numpy as np
import jax
import jax.numpy as jnp
from jax import lax
from jax.experimental import pallas as pl
from jax.experimental.pallas import tpu as pltpu

POOL_WINDOWS = (2, 4, 8, 16)
POOL_GROUP_DIM = 128
POOL_DIM = len(POOL_WINDOWS) * POOL_GROUP_DIM
N_HEADS = 8
HEAD_DIM = 64
ATTN_DIM = N_HEADS * HEAD_DIM
HEAD_PAD = 16
HEAD_ROWS = HEAD_DIM + HEAD_PAD
IDX_HEADS = 8
IDX_DIM = 64
TOP_K = 256
REL_BUCKETS = 32
REL_MAX_DIST = 128
N_ADA = 9
LN_EPS = 1e-5
POOL_HALO = 16

LANES = 128
SUBLANES = 8
V7X_VMEM_BYTES = 64 * 1024 * 1024
FFN_ROWS = 512
FFN_COLS = 256
PROJ_ROWS = 512
MIX_ROWS = 512
ATT_BLOCK = 256
ADA_COLS = 1024
SCORE_RUN = 4
FAST_RUN = 4
COUNT_CHAINS = 4
COUNT_ROWS = 64
BISECT_STEPS = 14

LOG2E = math.log2(math.e)
BF16 = jnp.bfloat16
F32 = jnp.float32
NT_DIMS = (((1,), (1,)), ((), ()))


def _tile_bytes(shape, dtype):
    itemsize = jnp.dtype(dtype).itemsize
    sublanes = SUBLANES * (4 // itemsize)
    shape = (1,) * (2 - len(shape)) + tuple(shape)
    rows = -(-shape[-2] // sublanes) * sublanes
    cols = -(-shape[-1] // LANES) * LANES
    return math.prod(shape[:-2]) * rows * cols * itemsize


def _vmem_limit(pipelined, resident, temporaries, stays_in_hbm=None):
    need = (2 * sum(_tile_bytes(*b) for b in pipelined) + sum(_tile_bytes(*b) for b in resident)
            + sum(_tile_bytes(*b) for b in temporaries))
    if stays_in_hbm is not None:
        need = max(need, V7X_VMEM_BYTES - _tile_bytes(*stays_in_hbm))
    return need


def _layer_norm(z, g, b):
    mu = jnp.mean(z, axis=-1, keepdims=True)
    zc = z - mu
    var = jnp.mean(zc * zc, axis=-1, keepdims=True)
    return zc * lax.rsqrt(var + LN_EPS) * g + b


def _silu(a):
    return a * jax.nn.sigmoid(a)


def _ada_kernel(c_ref, w_ref, b_ref, o_ref):
    a = _silu(c_ref[...])
    o_ref[...] = jnp.dot(a, w_ref[...], preferred_element_type=F32) + b_ref[...]


def _ada_mod(c, w_ada, b_ada):
    bsz, d = c.shape
    n = w_ada.shape[1]
    rows = -(-bsz // SUBLANES) * SUBLANES
    c_pad = jnp.pad(c, ((0, rows - bsz), (0, 0)))
    out = pl.pallas_call(
        _ada_kernel,
        grid=(n // ADA_COLS,),
        in_specs=[
            pl.BlockSpec((rows, d), lambda j: (0, 0)),
            pl.BlockSpec((d, ADA_COLS), lambda j: (0, j)),
            pl.BlockSpec((1, ADA_COLS), lambda j: (0, j)),
        ],
        out_specs=pl.BlockSpec((rows, ADA_COLS), lambda j: (0, j)),
        out_shape=jax.ShapeDtypeStruct((rows, n), F32),
        compiler_params=pltpu.CompilerParams(
            dimension_semantics=("arbitrary",),
            vmem_limit_bytes=_vmem_limit([((rows, d), F32), ((d, ADA_COLS), F32), ((1, ADA_COLS), F32),
                                          ((rows, ADA_COLS), F32)], [], [((rows, ADA_COLS), F32)],
                                         stays_in_hbm=(w_ada.shape, F32))),
        name="ada_mod",
    )(c_pad, w_ada, b_ada.reshape(1, n))
    return out[:bsz]


def _ffn_kernel(x_ref, mod_ref, lng_ref, lnb_ref, wg_ref, wu_ref, wd_ref, o_ref, h_ref, *, alpha, mod_row, ln_row):
    x = x_ref[...]
    sh = mod_ref[0, mod_row:mod_row + 1, :]
    sc = mod_ref[0, mod_row + 1:mod_row + 2, :]
    gate = mod_ref[0, mod_row + 2:mod_row + 3, :]
    u = (x * (1.0 + sc) + sh).astype(BF16)
    d_ff = wg_ref.shape[1]
    for c in range(d_ff // FFN_COLS):
        sl = slice(c * FFN_COLS, (c + 1) * FFN_COLS)
        a = jnp.dot(u, wg_ref[:, sl], preferred_element_type=F32)
        b = jnp.dot(u, wu_ref[:, sl], preferred_element_type=F32)
        h_ref[:, sl] = (_silu(a) * b).astype(BF16)
    y = jnp.dot(h_ref[...], wd_ref[...], preferred_element_type=F32)
    z = alpha * x + (0.5 * gate) * y
    o_ref[...] = _layer_norm(z, lng_ref[ln_row:ln_row + 1, :], lnb_ref[ln_row:ln_row + 1, :])


def _ffn(x2d, mod, ln_g, ln_b, wg, wu, wd, *, seq, alpha, mod_row, ln_row):
    n, d = x2d.shape
    d_ff = wg.shape[1]
    tiles_per_seq = seq // FFN_ROWS
    resident = dict(pipeline_mode=pl.Buffered(1))
    return pl.pallas_call(
        functools.partial(_ffn_kernel, alpha=alpha, mod_row=mod_row, ln_row=ln_row),
        grid=(n // FFN_ROWS,),
        in_specs=[
            pl.BlockSpec((FFN_ROWS, d), lambda i: (i, 0)),
            pl.BlockSpec((1, N_ADA, d), lambda i: (i // tiles_per_seq, 0, 0)),
            pl.BlockSpec(ln_g.shape, lambda i: (0, 0)),
            pl.BlockSpec(ln_b.shape, lambda i: (0, 0)),
            pl.BlockSpec((d, d_ff), lambda i: (0, 0), **resident),
            pl.BlockSpec((d, d_ff), lambda i: (0, 0), **resident),
            pl.BlockSpec((d_ff, d), lambda i: (0, 0), **resident),
        ],
        out_specs=pl.BlockSpec((FFN_ROWS, d), lambda i: (i, 0)),
        out_shape=jax.ShapeDtypeStruct((n, d), F32),
        scratch_shapes=[pltpu.VMEM((FFN_ROWS, d_ff), BF16)],
        compiler_params=pltpu.CompilerParams(
            dimension_semantics=("arbitrary",),
            vmem_limit_bytes=_vmem_limit(
                [((FFN_ROWS, d), F32), ((N_ADA, d), F32), (ln_g.shape, F32), (ln_b.shape, F32), ((FFN_ROWS, d), F32)],
                [((d, d_ff), BF16), ((d, d_ff), BF16), ((d_ff, d), BF16), ((FFN_ROWS, d_ff), BF16)],
                [((FFN_ROWS, d), BF16), ((FFN_ROWS, FFN_COLS), F32), ((FFN_ROWS, FFN_COLS), F32),
                 ((FFN_ROWS, d), F32), ((FFN_ROWS, d), F32)])),
        name="ffn",
    )(x2d, mod, ln_g, ln_b, wg, wu, wd)


def _proj_kernel(x_ref, mod_ref, wqk_ref, wvw_ref, wki_ref,
                 q_ref, k_ref, qi_ref, vt_ref, kia_ref, kib_ref, wit_ref, *, mod_row):
    x = x_ref[...]
    sh = mod_ref[0, mod_row:mod_row + 1, :]
    sc = mod_ref[0, mod_row + 1:mod_row + 2, :]
    u = (x * (1.0 + sc) + sh).astype(BF16)
    qkq = jnp.dot(u, wqk_ref[...], preferred_element_type=F32)
    q_ref[...] = (qkq[:, :ATTN_DIM] * (HEAD_DIM ** -0.5 * LOG2E)).astype(BF16)
    k_ref[...] = qkq[:, ATTN_DIM:2 * ATTN_DIM].astype(BF16)
    qi_ref[...] = qkq[:, 2 * ATTN_DIM:].astype(BF16)
    kk = jnp.dot(u, wki_ref[...], preferred_element_type=F32)
    kia_ref[...] = kk[:, :LANES].astype(BF16)
    kib_ref[...] = kk[:, LANES:].astype(BF16)
    vw = lax.dot_general(wvw_ref[...], u, NT_DIMS, preferred_element_type=F32)
    vt = vw[:ATTN_DIM].astype(BF16)
    ones_rows = jnp.where(lax.broadcasted_iota(jnp.int32, (HEAD_PAD, ATT_BLOCK), 0) == 0, 1.0, 0.0).astype(BF16)
    for c in range(vt_ref.shape[1]):
        for h in range(N_HEADS):
            vt_ref[0, c, h * HEAD_ROWS:h * HEAD_ROWS + HEAD_DIM, :] = \
                vt[h * HEAD_DIM:(h + 1) * HEAD_DIM, c * ATT_BLOCK:(c + 1) * ATT_BLOCK]
            vt_ref[0, c, h * HEAD_ROWS + HEAD_DIM:(h + 1) * HEAD_ROWS, :] = ones_rows
    wit_ref[0] = vw[ATTN_DIM:ATTN_DIM + IDX_HEADS, :]


def _attn_proj(x2d, mod, wqk, wvw_t, wki, *, bsz, seq, mod_row):
    n, d = x2d.shape
    tiles_per_seq = seq // PROJ_ROWS
    chunks_per_tile = PROJ_ROWS // ATT_BLOCK
    resident = dict(pipeline_mode=pl.Buffered(1))
    row_spec = lambda cols: pl.BlockSpec((PROJ_ROWS, cols), lambda i: (i, 0))
    return pl.pallas_call(
        functools.partial(_proj_kernel, mod_row=mod_row),
        grid=(n // PROJ_ROWS,),
        in_specs=[
            pl.BlockSpec((PROJ_ROWS, d), lambda i: (i, 0)),
            pl.BlockSpec((1, N_ADA, d), lambda i: (i // tiles_per_seq, 0, 0)),
            pl.BlockSpec(wqk.shape, lambda i: (0, 0), **resident),
            pl.BlockSpec(wvw_t.shape, lambda i: (0, 0), **resident),
            pl.BlockSpec(wki.shape, lambda i: (0, 0), **resident),
        ],
        out_specs=[
            row_spec(ATTN_DIM), row_spec(ATTN_DIM), row_spec(IDX_HEADS * IDX_DIM),
            pl.BlockSpec((1, chunks_per_tile, N_HEADS * HEAD_ROWS, ATT_BLOCK),
                         lambda i: (i // tiles_per_seq, i % tiles_per_seq, 0, 0)),
            row_spec(LANES), row_spec(LANES),
            pl.BlockSpec((1, IDX_HEADS, PROJ_ROWS), lambda i: (i // tiles_per_seq, 0, i % tiles_per_seq)),
        ],
        out_shape=[
            jax.ShapeDtypeStruct((n, ATTN_DIM), BF16),
            jax.ShapeDtypeStruct((n, ATTN_DIM), BF16),
            jax.ShapeDtypeStruct((n, IDX_HEADS * IDX_DIM), BF16),
            jax.ShapeDtypeStruct((bsz, seq // ATT_BLOCK, N_HEADS * HEAD_ROWS, ATT_BLOCK), BF16),
            jax.ShapeDtypeStruct((n, LANES), BF16),
            jax.ShapeDtypeStruct((n, LANES), BF16),
            jax.ShapeDtypeStruct((bsz, IDX_HEADS, seq), F32),
        ],
        compiler_params=pltpu.CompilerParams(
            dimension_semantics=("arbitrary",),
            vmem_limit_bytes=_vmem_limit(
                [((PROJ_ROWS, d), F32), ((N_ADA, d), F32), ((PROJ_ROWS, ATTN_DIM), BF16), ((PROJ_ROWS, ATTN_DIM), BF16),
                 ((PROJ_ROWS, IDX_HEADS * IDX_DIM), BF16), ((chunks_per_tile, N_HEADS * HEAD_ROWS, ATT_BLOCK), BF16),
                 ((PROJ_ROWS, LANES), BF16), ((PROJ_ROWS, LANES), BF16), ((IDX_HEADS, PROJ_ROWS), F32)],
                [(wqk.shape, BF16), (wvw_t.shape, BF16), (wki.shape, BF16)],
                [((PROJ_ROWS, d), BF16), ((PROJ_ROWS, wqk.shape[1]), F32), ((PROJ_ROWS, wki.shape[1]), F32),
                 ((wvw_t.shape[0], PROJ_ROWS), F32)],
                stays_in_hbm=((n, ATTN_DIM), BF16))),
        name="attn_proj",
    )(x2d, mod, wqk, wvw_t, wki)


def _t5_bucket(n):
    max_exact = REL_BUCKETS // 2
    nf = jnp.maximum(n, 1).astype(F32)
    large = max_exact + jnp.floor(jnp.log(nf / max_exact) / math.log(REL_MAX_DIST / max_exact)
                                  * (REL_BUCKETS - max_exact)).astype(jnp.int32)
    large = jnp.minimum(large, REL_BUCKETS - 1)
    return jnp.where(n < max_exact, n, large)


def _far_bucket(first_dist, last_dist):
    n = np.arange(first_dist, last_dist + 1, dtype=np.float32)
    max_exact = REL_BUCKETS // 2
    large = max_exact + (np.log(n / np.float32(max_exact)) / np.float32(math.log(REL_MAX_DIST / max_exact))
                         * np.float32(REL_BUCKETS - max_exact)).astype(np.int32)
    buckets = np.where(n < max_exact, n.astype(np.int32), np.minimum(large, REL_BUCKETS - 1))
    assert buckets.min() == buckets.max(), "key chunks two or more blocks away must share one bias bucket"
    return int(buckets[0])


def _bias_kernel(rb_ref, o_ref, *, far_bucket):
    o = pl.program_id(0)
    row = lax.broadcasted_iota(jnp.int32, (ATT_BLOCK, ATT_BLOCK), 0)
    col = lax.broadcasted_iota(jnp.int32, (ATT_BLOCK, ATT_BLOCK), 1)
    dist = o * ATT_BLOCK + col - row
    bucket = _t5_bucket(jnp.maximum(dist, 0))
    for h in range(N_HEADS):
        tile = jnp.zeros((ATT_BLOCK, ATT_BLOCK), F32)
        for b in range(REL_BUCKETS):
            tile = jnp.where(bucket == b, rb_ref[b, h], tile)
        o_ref[0, h] = (tile - rb_ref[far_bucket, h]) * LOG2E


def _rel_bias_tiles(rel_bias, far_bucket):
    return pl.pallas_call(
        functools.partial(_bias_kernel, far_bucket=far_bucket),
        grid=(2,),
        in_specs=[pl.BlockSpec(memory_space=pltpu.SMEM)],
        out_specs=pl.BlockSpec((1, N_HEADS, ATT_BLOCK, ATT_BLOCK), lambda o: (o, 0, 0, 0)),
        out_shape=jax.ShapeDtypeStruct((2, N_HEADS, ATT_BLOCK, ATT_BLOCK), F32),
        compiler_params=pltpu.CompilerParams(
            dimension_semantics=("arbitrary",),
            vmem_limit_bytes=_vmem_limit([((N_HEADS, ATT_BLOCK, ATT_BLOCK), F32)], [],
                                         [((ATT_BLOCK, ATT_BLOCK), F32)] * 3)),
        name="rel_bias",
    )(rel_bias)


def _attn_kernel(q_ref, qi_ref, wit_ref, k_ref, vt_ref, kia_ref, kib_ref, bt_ref, o_ref,
                 sc_ref, zc_ref, ranked_ref, qh_ref, lg_ref, p_ref, pp_ref, st_ref, acc_ref, *, topk, seq):
    blk = ATT_BLOCK
    groups = blk // SUBLANES
    i = pl.program_id(1)
    nch = i + 1
    kf = float(topk)
    inf = jnp.inf

    def chunk_start(j):
        return j * blk if isinstance(j, int) else pl.multiple_of(j * blk, blk)

    def as_groups(x):
        return x.reshape(groups, SUBLANES, blk)

    def lanes8(v):
        return jnp.broadcast_to(v, (SUBLANES, blk))

    def colmin(x8):
        return jnp.min(x8, axis=0, keepdims=True)

    def colmax(x8):
        return jnp.max(x8, axis=0, keepdims=True)

    def colsum(x8):
        return jnp.sum(x8, axis=0, keepdims=True)

    wf = wit_ref[0] * (IDX_DIM ** -0.5)
    qi = qi_ref[0]
    row = lax.broadcasted_iota(jnp.int32, (blk, blk), 0)
    col = lax.broadcasted_iota(jnp.int32, (blk, blk), 1)
    causal = row <= col

    def chunk_scores(j):
        r0 = chunk_start(j)
        ka = kia_ref[0, pl.ds(r0, blk), :]
        kb = kib_ref[0, pl.ds(r0, blk), :]
        s = jnp.zeros((blk, blk), F32)
        for hp in range(IDX_HEADS // 2):
            qp = qi[:, hp * LANES:(hp + 1) * LANES]
            a0 = lax.dot_general(ka, qp, NT_DIMS, preferred_element_type=F32)
            a1 = lax.dot_general(kb, qp, NT_DIMS, preferred_element_type=F32)
            s = s + jnp.maximum(a0, 0.0) * wf[2 * hp:2 * hp + 1, :]
            s = s + jnp.maximum(a1, 0.0) * wf[2 * hp + 1:2 * hp + 2, :]
        return r0, s * (IDX_HEADS ** -0.5)

    def score_stats(j, s_lo, s_hi, stats):
        mn8, mx8, pos8, zer8 = stats
        hi3 = as_groups(s_hi)
        zer8 = zer8 + jnp.sum(jnp.where(hi3 == 0.0, 1.0, 0.0), axis=0)
        zc_ref[j] = zer8
        return (jnp.minimum(mn8, jnp.min(as_groups(s_lo), axis=0)),
                jnp.maximum(mx8, jnp.max(hi3, axis=0)),
                pos8 + jnp.sum(jnp.where(hi3 > 0.0, 1.0, 0.0), axis=0),
                zer8)

    def score_body(j, stats):
        r0, s = chunk_scores(j)
        sc_ref[pl.ds(r0, blk), :] = s
        return score_stats(j, s, s, stats)

    def score_run(first, count, stats):
        for c in range(count):
            stats = score_body(first + c, stats)
        return stats

    def score_diag(st):
        r_diag, s_diag = chunk_scores(i)
        s_diag_hi = jnp.where(causal, s_diag, -inf)
        sc_ref[pl.ds(r_diag, blk), :] = s_diag_hi
        return score_stats(i, jnp.where(causal, s_diag, inf), s_diag_hi, st)

    zeros8 = jnp.zeros((SUBLANES, blk), F32)
    stats = (jnp.full((SUBLANES, blk), inf, F32), jnp.full((SUBLANES, blk), -inf, F32), zeros8, zeros8)
    stats = lax.fori_loop(0, i // SCORE_RUN, lambda jq, st: score_run(SCORE_RUN * jq, SCORE_RUN, st), stats)
    left = i % SCORE_RUN
    mn8, mx8, pos8, zer8 = lax.switch(
        left, [functools.partial(lambda r, st: score_diag(score_run(i - r, r, st)), r) for r in range(SCORE_RUN)], stats)

    @pl.when(nch % 2 == 1)
    def _():
        sc_ref[pl.ds(chunk_start(nch), blk), :] = jnp.full((blk, blk), -inf, F32)

    npair = (nch + 1) // 2

    def count_gt(thr):
        t8 = lanes8(thr)

        def body(jp, accs):
            base = pl.multiple_of(jp * 2 * blk, 2 * blk)
            accs = list(accs)
            for s in range(2 * blk // COUNT_ROWS):
                x = sc_ref[pl.ds(base + s * COUNT_ROWS, COUNT_ROWS), :]
                for g in range(COUNT_ROWS // SUBLANES):
                    hit = jnp.where(x[g * SUBLANES:(g + 1) * SUBLANES] > t8, 1.0, 0.0)
                    accs[g % COUNT_CHAINS] = accs[g % COUNT_CHAINS] + hit
            return tuple(accs)

        accs = lax.fori_loop(0, npair, body, tuple(jnp.zeros((SUBLANES, blk), F32) for _ in range(COUNT_CHAINS)))
        return colsum(sum(accs[1:], accs[0]))

    def bisect(_, st):
        lo, hi, low, done = st
        mid = 0.5 * lo + 0.5 * hi
        c = count_gt(mid)
        live = done < 0.5
        up = jnp.logical_and(live, c >= kf)
        down = jnp.logical_and(live, c < kf)
        low = jnp.where(up, mid, low)
        lo = jnp.where(up, mid, lo)
        hi = jnp.where(down, mid, hi)
        done = jnp.where(c == kf, 1.0, done)
        return lo, hi, low, done

    tpos = i * blk + lax.broadcasted_iota(jnp.int32, (1, blk), 1)
    pos, zer = colsum(pos8), colsum(zer8)
    mn, mx = colmin(mn8), colmax(mx8)
    zero_tie = jnp.logical_and(pos < kf, pos + zer >= kf)
    pos_ge = pos >= kf
    done0 = jnp.where(jnp.logical_or(jnp.logical_or(tpos + 1 <= topk, zero_tie), pos == kf), 1.0, 0.0)
    low0 = jnp.where(jnp.logical_or(zero_tie, pos_ge), 0.0, -inf)
    lo0 = jnp.where(pos_ge, jnp.maximum(mn, 0.0), mn)
    hi0 = jnp.where(pos_ge, mx, jnp.minimum(mx, 0.0))
    tie0 = jnp.where(zero_tie, 0.0, inf)
    need0 = jnp.where(zero_tie, kf - pos, 0.0)

    def next_value_above(thr):
        t8 = lanes8(thr)

        def body(jp, accs):
            base = pl.multiple_of(jp * 2 * blk, 2 * blk)
            accs = list(accs)
            for s in range(2 * blk // COUNT_ROWS):
                x = sc_ref[pl.ds(base + s * COUNT_ROWS, COUNT_ROWS), :]
                for g in range(COUNT_ROWS // SUBLANES):
                    xg = x[g * SUBLANES:(g + 1) * SUBLANES]
                    accs[g % COUNT_CHAINS] = jnp.minimum(accs[g % COUNT_CHAINS], jnp.where(xg > t8, xg, inf))
            return tuple(accs)

        accs = lax.fori_loop(0, npair, body, tuple(jnp.full((SUBLANES, blk), inf, F32) for _ in range(COUNT_CHAINS)))
        return colmin(functools.reduce(jnp.minimum, accs))

    def climb_cond(st):
        return jnp.min(st[1]) < 0.5

    def climb_body(st):
        low, done, tie, need = st
        live = done < 0.5
        cand = next_value_above(low)
        cgt = count_gt(cand)
        found = jnp.logical_and(live, cgt < kf)
        tie = jnp.where(found, cand, tie)
        need = jnp.where(found, kf - cgt, need)
        low = jnp.where(live, cand, low)
        done = jnp.where(jnp.logical_and(live, cgt <= kf), 1.0, done)
        return low, done, tie, need

    def select(_):
        _, _, low, done = lax.fori_loop(0, BISECT_STEPS, bisect, (lo0, hi0, low0, done0))
        low, _, tie, need = lax.while_loop(climb_cond, climb_body, (low, done, tie0, need0))
        return low, tie, need

    low, tie, need = lax.cond(jnp.min(done0) < 0.5, select, lambda _: (low0, tie0, need0), 0)

    def plain_mask(_):
        low8 = lanes8(low)

        def body(j, _):
            r0 = chunk_start(j)
            x = as_groups(sc_ref[pl.ds(r0, blk), :])
            sc_ref[pl.ds(r0, blk), :] = jnp.where(x > low8[None], 0.0, -inf).reshape(blk, blk)
            return 0

        return lax.fori_loop(0, nch, body, 0)

    def ranked_chunk(j, tri, base):
        r0 = chunk_start(j)
        x = sc_ref[pl.ds(r0, blk), :]
        eq = x == tie
        rank = jnp.dot(tri, jnp.where(eq, 1.0, 0.0).astype(BF16), preferred_element_type=F32)
        sel = jnp.logical_or(x > low, jnp.logical_and(eq, rank + base <= need))
        sc_ref[pl.ds(r0, blk), :] = jnp.where(sel, 0.0, -inf)
        return rank[blk - 1:blk, :]

    def tie_mask(_):
        tri = jnp.where(row >= col, 1.0, 0.0).astype(BF16)
        lax.fori_loop(0, nch, lambda j, base: base + ranked_chunk(j, tri, base), jnp.zeros((1, blk), F32))
        return 0

    def zero_tie_mask(_):
        tri = jnp.where(row >= col, 1.0, 0.0).astype(BF16)

        def cut_body(j, ncut):
            return ncut + jnp.where(colsum(zc_ref[j]) < need, 1.0, 0.0)

        ncut = lax.fori_loop(0, nch, cut_body, jnp.zeros((1, blk), F32))
        cut = jnp.where(tie < inf, ncut, inf)

        def flag_body(j, _):
            ranked_ref[j] = 0
            return 0

        lax.fori_loop(0, nch, flag_body, 0)

        def cut_cond(pending):
            return jnp.min(pending) < inf

        def cut_chunk(pending):
            jf = jnp.min(pending)
            j = jf.astype(jnp.int32)
            base = jnp.where(j > 0, colsum(zc_ref[jnp.maximum(j - 1, 0)]), 0.0)
            ranked_chunk(j, tri, base)
            ranked_ref[j] = 1
            return jnp.where(pending == jf, inf, pending)

        lax.while_loop(cut_cond, cut_chunk, cut)
        low8, cut8 = lanes8(low), lanes8(cut)

        def body(j, _):
            @pl.when(ranked_ref[j] == 0)
            def _():
                r0 = chunk_start(j)
                x = as_groups(sc_ref[pl.ds(r0, blk), :])
                at_low = jnp.where(jnp.logical_and(cut8 < inf, cut8 > j.astype(F32)), 0.0, -inf)
                out = jnp.where(x > low8[None], 0.0, jnp.where(x == low8[None], at_low[None], -inf))
                sc_ref[pl.ds(r0, blk), :] = out.reshape(blk, blk)
            return 0

        return lax.fori_loop(0, nch, body, 0)

    tied = tie < inf
    mask_kind = jnp.where(jnp.max(jnp.where(tied, 1.0, 0.0)) < 0.5, 0,
                          jnp.where(jnp.max(jnp.where(jnp.logical_and(tied, tie != 0.0), 1.0, 0.0)) < 0.5, 1, 2))
    lax.switch(mask_kind, [plain_mask, zero_tie_mask, tie_mask], 0)

    q = q_ref[0]
    lane = lax.broadcasted_iota(jnp.int32, (blk, LANES), 1)
    for h in range(N_HEADS):
        hp = h // 2
        in_head = (lane < HEAD_DIM) if h % 2 == 0 else (lane >= HEAD_DIM)
        qh_ref[h] = jnp.where(in_head, q[:, hp * LANES:(hp + 1) * LANES], jnp.zeros((), BF16))

    def logits(j, h, mb, near):
        hp = h // 2
        kc = k_ref[0, pl.ds(chunk_start(j), blk), hp * LANES:(hp + 1) * LANES]
        lg = lax.dot_general(kc, qh_ref[h], NT_DIMS, preferred_element_type=F32) + mb
        return lg if near is None else lg + bt_ref[near, h]

    CMAX, RMAX, RESC = 0, 1, 2

    def stage_logits(j, near):
        mb = sc_ref[pl.ds(chunk_start(j), blk), :]
        cm8 = []
        for h in range(N_HEADS):
            lg = logits(j, h, mb, near)
            lg_ref[h] = lg
            cm8.append(jnp.max(as_groups(lg), axis=0))
        for h in range(N_HEADS):
            st_ref[CMAX, h] = lanes8(colmax(cm8[h]))

    def stage_exp():
        for h in range(N_HEADS):
            m_old = st_ref[RMAX, h]
            m_new = jnp.maximum(m_old, st_ref[CMAX, h])
            m_use = jnp.where(m_new == -inf, 0.0, m_new)
            p = jnp.exp2(as_groups(lg_ref[h]) - m_use[None])
            p_ref[h] = p.reshape(blk, blk).astype(BF16)
            st_ref[RMAX, h] = m_new
            st_ref[RESC, h] = jnp.exp2(m_old - m_use)

    def stage_pv(j):
        for h in range(N_HEADS):
            rows = slice(h * HEAD_ROWS, (h + 1) * HEAD_ROWS)
            pv = jnp.dot(vt_ref[0, j, rows, :], p_ref[h], preferred_element_type=F32)
            acc = acc_ref[rows, :].reshape(HEAD_ROWS // SUBLANES, SUBLANES, blk) * st_ref[RESC, h][None]
            acc_ref[rows, :] = acc.reshape(HEAD_ROWS, blk) + pv

    def pipelined(j, near_next):
        stage_exp()
        stage_logits(j + 1, near_next)
        stage_pv(j)

    def exact_attention():
        acc_ref[...] = jnp.zeros(acc_ref.shape, F32)
        st_ref[RMAX] = jnp.full((N_HEADS, SUBLANES, blk), -inf, F32)

        def last_chunks(first):
            for j in range(first, 0):
                pipelined(i + j, -(j + 1))
            stage_exp()
            stage_pv(i)

        @pl.when(i == 0)
        def _():
            stage_logits(0, 0)
            last_chunks(0)

        @pl.when(i == 1)
        def _():
            stage_logits(0, 1)
            last_chunks(-1)

        @pl.when(i >= 2)
        def _():
            stage_logits(0, None)

        def far_body(j, _):
            pipelined(j, None)
            return 0

        lax.fori_loop(0, jnp.maximum(i - 2, 0), far_body, 0)

        @pl.when(i >= 2)
        def _():
            last_chunks(-2)

    acc_ref[...] = jnp.zeros(acc_ref.shape, F32)
    st_ref[RMAX] = jnp.zeros((N_HEADS, SUBLANES, blk), F32)

    def fast_run(chunks):
        for c, (j, near) in enumerate(chunks):
            mb = sc_ref[pl.ds(chunk_start(j), blk), :]
            for h in range(N_HEADS):
                p = jnp.exp2(as_groups(logits(j, h, mb, near)) - st_ref[RMAX, h][None])
                pp_ref[c, h] = p.reshape(blk, blk).astype(BF16)
        for h in range(N_HEADS):
            rows = slice(h * HEAD_ROWS, (h + 1) * HEAD_ROWS)
            tot = acc_ref[rows, :]
            for c, (j, _) in enumerate(chunks):
                tot = tot + jnp.dot(vt_ref[0, j, rows, :], pp_ref[c, h], preferred_element_type=F32)
            den = tot[HEAD_DIM:HEAD_DIM + 1, :]
            up = lanes8(jnp.where(den > 0.0, jnp.floor(jnp.log2(den)), 0.0))
            scaled = tot.reshape(HEAD_ROWS // SUBLANES, SUBLANES, blk) * jnp.exp2(-up)[None]
            acc_ref[rows, :] = scaled.reshape(HEAD_ROWS, blk)
            st_ref[RMAX, h] = st_ref[RMAX, h] + up

    def far_run(jr, _):
        fast_run([(FAST_RUN * jr + c, None) for c in range(FAST_RUN)])
        return 0

    n_far = jnp.maximum(i - 1, 0)
    lax.fori_loop(0, n_far // FAST_RUN, far_run, 0)

    @pl.when(i == 0)
    def _():
        fast_run([(0, 0)])

    for left in range(FAST_RUN):
        @pl.when(jnp.logical_and(i >= 1, n_far % FAST_RUN == left))
        def _():
            last = [(i - 1 - left + c, None) for c in range(left)] + [(i - 1, 1), (i, 0)]
            for first in range(0, len(last), FAST_RUN):
                fast_run(last[first:first + FAST_RUN])

    acc = acc_ref[...]
    finite = jnp.min(jnp.where(jnp.isfinite(acc), 1.0, 0.0))
    dens = jnp.concatenate([acc[h * HEAD_ROWS + HEAD_DIM:h * HEAD_ROWS + HEAD_DIM + 1, :] for h in range(N_HEADS)], axis=0)
    usable = jnp.logical_and(finite > 0.5, jnp.min(dens) > 0.0)
    lax.cond(usable, lambda: None, exact_attention)

    heads = []
    for h in range(N_HEADS):
        r0 = h * HEAD_ROWS
        heads.append(acc_ref[r0:r0 + HEAD_DIM, :] / acc_ref[r0 + HEAD_DIM:r0 + HEAD_DIM + 1, :])
    o_ref[0] = jnp.concatenate(heads, axis=0).T.astype(BF16)


def _attention(q, qi, wit, k, vt, kia, kib, btiles, *, bsz, seq, topk):
    blk = ATT_BLOCK
    nblk = seq // blk
    return pl.pallas_call(
        functools.partial(_attn_kernel, topk=topk, seq=seq),
        grid=(bsz, nblk),
        in_specs=[
            pl.BlockSpec((1, blk, ATTN_DIM), lambda b, i: (b, i, 0)),
            pl.BlockSpec((1, blk, IDX_HEADS * IDX_DIM), lambda b, i: (b, i, 0)),
            pl.BlockSpec((1, IDX_HEADS, blk), lambda b, i: (b, 0, i)),
            pl.BlockSpec((1, seq, ATTN_DIM), lambda b, i: (b, 0, 0)),
            pl.BlockSpec((1, nblk, N_HEADS * HEAD_ROWS, blk), lambda b, i: (b, 0, 0, 0)),
            pl.BlockSpec((1, seq, LANES), lambda b, i: (b, 0, 0)),
            pl.BlockSpec((1, seq, LANES), lambda b, i: (b, 0, 0)),
            pl.BlockSpec(btiles.shape, lambda b, i: (0, 0, 0, 0)),
        ],
        out_specs=pl.BlockSpec((1, blk, ATTN_DIM), lambda b, i: (b, i, 0)),
        out_shape=jax.ShapeDtypeStruct((bsz, seq, ATTN_DIM), BF16),
        scratch_shapes=[
            pltpu.VMEM((seq + blk, blk), F32),
            pltpu.VMEM((nblk, SUBLANES, blk), F32),
            pltpu.SMEM((nblk,), jnp.int32),
            pltpu.VMEM((N_HEADS, blk, LANES), BF16),
            pltpu.VMEM((N_HEADS, blk, blk), F32),
            pltpu.VMEM((N_HEADS, blk, blk), BF16),
            pltpu.VMEM((FAST_RUN, N_HEADS, blk, blk), BF16),
            pltpu.VMEM((3, N_HEADS, SUBLANES, blk), F32),
            pltpu.VMEM((N_HEADS * HEAD_ROWS, blk), F32),
        ],
        compiler_params=pltpu.CompilerParams(
            dimension_semantics=("arbitrary", "arbitrary"),
            vmem_limit_bytes=_vmem_limit(
                [((blk, ATTN_DIM), BF16), ((blk, IDX_HEADS * IDX_DIM), BF16), ((IDX_HEADS, blk), F32),
                 ((seq, ATTN_DIM), BF16), ((nblk, N_HEADS * HEAD_ROWS, blk), BF16), ((seq, LANES), BF16),
                 ((seq, LANES), BF16), (btiles.shape, F32), ((blk, ATTN_DIM), BF16)],
                [((seq + blk, blk), F32), ((nblk, SUBLANES, blk), F32), ((N_HEADS, blk, LANES), BF16),
                 ((N_HEADS, blk, blk), F32), ((N_HEADS, blk, blk), BF16), ((FAST_RUN, N_HEADS, blk, blk), BF16),
                 ((3, N_HEADS, SUBLANES, blk), F32),
                 ((N_HEADS * HEAD_ROWS, blk), F32)],
                [((blk, blk), F32)] * 4 + [((blk, blk), BF16), ((ATTN_DIM, blk), F32), ((blk, ATTN_DIM), F32)],
                stays_in_hbm=((bsz, seq, ATTN_DIM), BF16))),
        name="attention",
    )(q, qi, wit, k, vt, kia, kib, btiles)


def _mix_kernel(x_ref, mod_ref, att_ref, wpg_ref, wpool_ref, ps_ref, wa_ref, wb_ref, wo_ref,
                lng_ref, lnb_ref, o_ref, pe_ref, mix_ref, *, alpha, mod_row, ln_row, tiles_per_seq):
    rows = x_ref.shape[0]
    i = pl.program_id(0)
    seq_tile = i % tiles_per_seq

    @pl.when(i == 0)
    def _():
        pe_ref[rows:, :] = jnp.zeros((POOL_HALO, POOL_DIM), F32)

    x = x_ref[...]
    sh = mod_ref[0, mod_row:mod_row + 1, :]
    sc = mod_ref[0, mod_row + 1:mod_row + 2, :]
    gate = mod_ref[0, mod_row + 2:mod_row + 3, :]
    u = (x * (1.0 + sc) + sh).astype(BF16)
    pg = jnp.dot(u, wpg_ref[...], preferred_element_type=F32)
    pe_ref[0:POOL_HALO, :] = jnp.where(seq_tile == 0, 0.0, pe_ref[rows:, :])
    pe_ref[POOL_HALO:, :] = pg[:, :POOL_DIM]
    t = seq_tile * rows + lax.broadcasted_iota(jnp.int32, (rows, 1), 0)
    for g, w in enumerate(POOL_WINDOWS):
        cols = slice(g * POOL_GROUP_DIM, (g + 1) * POOL_GROUP_DIM)
        cur = pe_ref[POOL_HALO:, cols]
        win = cur
        for back in range(1, w):
            win = win + pe_ref[POOL_HALO - back:POOL_HALO - back + rows, cols]
        cnt = jnp.minimum(t + 1, w).astype(F32)
        pooled = (win / cnt - cur).astype(BF16)
        mixed = jnp.dot(pooled, wpool_ref[g], preferred_element_type=F32)
        mix_ref[:, cols] = (mixed * ps_ref[:, cols]).astype(BF16)
    y_a = jnp.dot(mix_ref[...], wa_ref[...], preferred_element_type=F32)
    y_b = jnp.dot(att_ref[...], wb_ref[...], preferred_element_type=F32)
    d = x.shape[1]
    ga = pg[:, POOL_DIM:POOL_DIM + d]
    gb = pg[:, POOL_DIM + d:]
    merged = (jax.nn.sigmoid(ga) * y_a + jax.nn.sigmoid(gb) * y_b).astype(BF16)
    y = jnp.dot(merged, wo_ref[...], preferred_element_type=F32)
    z = alpha * x + gate * y
    o_ref[...] = _layer_norm(z, lng_ref[ln_row:ln_row + 1, :], lnb_ref[ln_row:ln_row + 1, :])


def _mix_out(x2d, mod, att2d, wpg, wpool, pool_scale, wa, wb, wo, ln_g, ln_b, *, seq, alpha, mod_row, ln_row):
    n, d = x2d.shape
    tiles_per_seq = seq // MIX_ROWS
    resident = dict(pipeline_mode=pl.Buffered(1))
    full = lambda a: pl.BlockSpec(a.shape, lambda i: (0,) * a.ndim, **resident)
    return pl.pallas_call(
        functools.partial(_mix_kernel, alpha=alpha, mod_row=mod_row, ln_row=ln_row, tiles_per_seq=tiles_per_seq),
        grid=(n // MIX_ROWS,),
        in_specs=[
            pl.BlockSpec((MIX_ROWS, d), lambda i: (i, 0)),
            pl.BlockSpec((1, N_ADA, d), lambda i: (i // tiles_per_seq, 0, 0)),
            pl.BlockSpec((MIX_ROWS, ATTN_DIM), lambda i: (i, 0)),
            full(wpg), full(wpool), full(pool_scale), full(wa), full(wb), full(wo),
            pl.BlockSpec(ln_g.shape, lambda i: (0, 0)),
            pl.BlockSpec(ln_b.shape, lambda i: (0, 0)),
        ],
        out_specs=pl.BlockSpec((MIX_ROWS, d), lambda i: (i, 0)),
        out_shape=jax.ShapeDtypeStruct((n, d), F32),
        scratch_shapes=[
            pltpu.VMEM((POOL_HALO + MIX_ROWS, POOL_DIM), F32),
            pltpu.VMEM((MIX_ROWS, POOL_DIM), BF16),
        ],
        compiler_params=pltpu.CompilerParams(
            dimension_semantics=("arbitrary",),
            vmem_limit_bytes=_vmem_limit(
                [((MIX_ROWS, d), F32), ((N_ADA, d), F32), ((MIX_ROWS, ATTN_DIM), BF16), (ln_g.shape, F32),
                 (ln_b.shape, F32), ((MIX_ROWS, d), F32)],
                [(wpg.shape, BF16), (wpool.shape, BF16), (pool_scale.shape, F32), (wa.shape, BF16), (wb.shape, BF16),
                 (wo.shape, BF16), ((POOL_HALO + MIX_ROWS, POOL_DIM), F32), ((MIX_ROWS, POOL_DIM), BF16)],
                [((MIX_ROWS, d), BF16), ((MIX_ROWS, wpg.shape[1]), F32)] + [((MIX_ROWS, d), F32)] * 5)),
        name="mix_out",
    )(x2d, mod, att2d, wpg, wpool, pool_scale, wa, wb, wo, ln_g, ln_b)


def kernel(x, c, w_ada, b_ada, ln_g, ln_b, ffn1_w_gate, ffn1_w_up, ffn1_w_down, w_in, w_pool, pool_scale,
           w_a, w_b, w_out, rel_bias, ffn2_w_gate, ffn2_w_up, ffn2_w_down):
    bsz, seq, d = x.shape
    depth = w_ada.shape[0]
    alpha = (2.0 * depth) ** 0.25
    topk = min(TOP_K, seq // 4)
    assert seq % FFN_ROWS == 0 and seq % PROJ_ROWS == 0 and seq % MIX_ROWS == 0 and seq % ATT_BLOCK == 0
    assert PROJ_ROWS % ATT_BLOCK == 0 and POOL_HALO >= max(POOL_WINDOWS) - 1
    far_bucket = _far_bucket(ATT_BLOCK + 1, max(seq - 1, ATT_BLOCK + 1))

    o_q = POOL_DIM
    o_k = o_q + ATTN_DIM
    o_v = o_k + ATTN_DIM
    o_qi = o_v + ATTN_DIM
    o_ki = o_qi + IDX_HEADS * IDX_DIM
    o_wi = o_ki + IDX_DIM
    o_ga = o_wi + IDX_HEADS

    btiles = _rel_bias_tiles(rel_bias, far_bucket)
    x2d = x.reshape(bsz * seq, d)
    for l in range(depth):
        wl = w_in[l]
        zeros_ki = jnp.zeros((d, LANES - IDX_DIM), wl.dtype)
        w_ki = wl[:, o_ki:o_wi]
        wqk = jnp.concatenate([wl[:, o_q:o_v], wl[:, o_qi:o_ki]], axis=1).astype(BF16)
        wvw_t = jnp.pad(jnp.concatenate([wl[:, o_v:o_qi], wl[:, o_wi:o_ga]], axis=1).T,
                        ((0, 2 * SUBLANES - IDX_HEADS), (0, 0))).astype(BF16)
        wki = jnp.concatenate([w_ki, zeros_ki, zeros_ki, w_ki], axis=1).astype(BF16)
        wpg = jnp.concatenate([wl[:, :POOL_DIM], wl[:, o_ga:]], axis=1).astype(BF16)

        mod = _ada_mod(c, w_ada[l], b_ada[l]).reshape(bsz, N_ADA, d)
        x2d = _ffn(x2d, mod, ln_g[l], ln_b[l], ffn1_w_gate[l].astype(BF16), ffn1_w_up[l].astype(BF16),
                   ffn1_w_down[l].astype(BF16), seq=seq, alpha=alpha, mod_row=0, ln_row=0)
        q, k, qi, vt, kia, kib, wit = _attn_proj(x2d, mod, wqk, wvw_t, wki, bsz=bsz, seq=seq, mod_row=3)
        att = _attention(q.reshape(bsz, seq, ATTN_DIM), qi.reshape(bsz, seq, IDX_HEADS * IDX_DIM), wit,
                         k.reshape(bsz, seq, ATTN_DIM), vt, kia.reshape(bsz, seq, LANES),
                         kib.reshape(bsz, seq, LANES), btiles, bsz=bsz, seq=seq, topk=topk)
        x2d = _mix_out(x2d, mod, att.reshape(bsz * seq, ATTN_DIM), wpg, w_pool[l].astype(BF16),
                       pool_scale[l].reshape(1, POOL_DIM), w_a[l].astype(BF16), w_b[l].astype(BF16),
                       w_out[l].astype(BF16), ln_g[l], ln_b[l], seq=seq, alpha=alpha, mod_row=3, ln_row=1)
        x2d = _ffn(x2d, mod, ln_g[l], ln_b[l], ffn2_w_gate[l].astype(BF16), ffn2_w_up[l].astype(BF16),
                   ffn2_w_down[l].astype(BF16), seq=seq, alpha=alpha, mod_row=6, ln_row=2)
    return x2d.reshape(bsz, seq, d)
```

```python
import functools
import math

import numpy as np
import jax
import jax.numpy as jnp
from jax import lax
from jax.experimental import pallas as pl
from jax.experimental.pallas import tpu as pltpu

POOL_WINDOWS = (2, 4, 8, 16)
POOL_GROUP_DIM = 128
POOL_DIM = len(POOL_WINDOWS) * POOL_GROUP_DIM
N_HEADS = 8
HEAD_DIM = 64
ATTN_DIM = N_HEADS * HEAD_DIM
HEAD_PAD = 16
HEAD_ROWS = HEAD_DIM + HEAD_PAD
IDX_HEADS = 8
IDX_DIM = 64
TOP_K = 256
REL_BUCKETS = 32
REL_MAX_DIST = 128
N_ADA = 9
LN_EPS = 1e-5
POOL_HALO = 16

LANES = 128
SUBLANES = 8
V7X_VMEM_BYTES = 64 * 1024 * 1024
FFN_ROWS = 512
FFN_COLS = 256
PROJ_ROWS = 512
MIX_ROWS = 512
ATT_BLOCK = 256
ADA_COLS = 1024
SCORE_RUN = 4
FAST_RUN = 4
COUNT_CHAINS = 4
COUNT_ROWS = 64
BISECT_STEPS = 14

LOG2E = math.log2(math.e)
BF16 = jnp.bfloat16
F32 = jnp.float32
NT_DIMS = (((1,), (1,)), ((), ()))


def _tile_bytes(shape, dtype):
    itemsize = jnp.dtype(dtype).itemsize
    sublanes = SUBLANES * (4 // itemsize)
    shape = (1,) * (2 - len(shape)) + tuple(shape)
    rows = -(-shape[-2] // sublanes) * sublanes
    cols = -(-shape[-1] // LANES) * LANES
    return math.prod(shape[:-2]) * rows * cols * itemsize


def _vmem_limit(pipelined, resident, temporaries, stays_in_hbm=None):
    need = (2 * sum(_tile_bytes(*b) for b in pipelined) + sum(_tile_bytes(*b) for b in resident)
            + sum(_tile_bytes(*b) for b in temporaries))
    if stays_in_hbm is not None:
        need = max(need, V7X_VMEM_BYTES - _tile_bytes(*stays_in_hbm))
    return need


def _layer_norm(z, g, b):
    mu = jnp.mean(z, axis=-1, keepdims=True)
    zc = z - mu
    var = jnp.mean(zc * zc, axis=-1, keepdims=True)
    return zc * lax.rsqrt(var + LN_EPS) * g + b


def _silu(a):
    return a * jax.nn.sigmoid(a)


def _ada_kernel(c_ref, w_ref, b_ref, o_ref):
    a = _silu(c_ref[...])
    o_ref[...] = jnp.dot(a, w_ref[...], preferred_element_type=F32) + b_ref[...]


def _ada_mod(c, w_ada, b_ada):
    bsz, d = c.shape
    n = w_ada.shape[1]
    rows = -(-bsz // SUBLANES) * SUBLANES
    c_pad = jnp.pad(c, ((0, rows - bsz), (0, 0)))
    out = pl.pallas_call(
        _ada_kernel,
        grid=(n // ADA_COLS,),
        in_specs=[
            pl.BlockSpec((rows, d), lambda j: (0, 0)),
            pl.BlockSpec((d, ADA_COLS), lambda j: (0, j)),
            pl.BlockSpec((1, ADA_COLS), lambda j: (0, j)),
        ],
        out_specs=pl.BlockSpec((rows, ADA_COLS), lambda j: (0, j)),
        out_shape=jax.ShapeDtypeStruct((rows, n), F32),
        compiler_params=pltpu.CompilerParams(
            dimension_semantics=("arbitrary",),
            vmem_limit_bytes=_vmem_limit([((rows, d), F32), ((d, ADA_COLS), F32), ((1, ADA_COLS), F32),
                                          ((rows, ADA_COLS), F32)], [], [((rows, ADA_COLS), F32)],
                                         stays_in_hbm=(w_ada.shape, F32))),
        name="ada_mod",
    )(c_pad, w_ada, b_ada.reshape(1, n))
    return out[:bsz]


def _ffn_kernel(x_ref, mod_ref, lng_ref, lnb_ref, wg_ref, wu_ref, wd_ref, o_ref, h_ref, *, alpha, mod_row, ln_row):
    x = x_ref[...]
    sh = mod_ref[0, mod_row:mod_row + 1, :]
    sc = mod_ref[0, mod_row + 1:mod_row + 2, :]
    gate = mod_ref[0, mod_row + 2:mod_row + 3, :]
    u = (x * (1.0 + sc) + sh).astype(BF16)
    d_ff = wg_ref.shape[1]
    for c in range(d_ff // FFN_COLS):
        sl = slice(c * FFN_COLS, (c + 1) * FFN_COLS)
        a = jnp.dot(u, wg_ref[:, sl], preferred_element_type=F32)
        b = jnp.dot(u, wu_ref[:, sl], preferred_element_type=F32)
        h_ref[:, sl] = (_silu(a) * b).astype(BF16)
    y = jnp.dot(h_ref[...], wd_ref[...], preferred_element_type=F32)
    z = alpha * x + (0.5 * gate) * y
    o_ref[...] = _layer_norm(z, lng_ref[ln_row:ln_row + 1, :], lnb_ref[ln_row:ln_row + 1, :])


def _ffn(x2d, mod, ln_g, ln_b, wg, wu, wd, *, seq, alpha, mod_row, ln_row):
    n, d = x2d.shape
    d_ff = wg.shape[1]
    tiles_per_seq = seq // FFN_ROWS
    resident = dict(pipeline_mode=pl.Buffered(1))
    return pl.pallas_call(
        functools.partial(_ffn_kernel, alpha=alpha, mod_row=mod_row, ln_row=ln_row),
        grid=(n // FFN_ROWS,),
        in_specs=[
            pl.BlockSpec((FFN_ROWS, d), lambda i: (i, 0)),
            pl.BlockSpec((1, N_ADA, d), lambda i: (i // tiles_per_seq, 0, 0)),
            pl.BlockSpec(ln_g.shape, lambda i: (0, 0)),
            pl.BlockSpec(ln_b.shape, lambda i: (0, 0)),
            pl.BlockSpec((d, d_ff), lambda i: (0, 0), **resident),
            pl.BlockSpec((d, d_ff), lambda i: (0, 0), **resident),
            pl.BlockSpec((d_ff, d), lambda i: (0, 0), **resident),
        ],
        out_specs=pl.BlockSpec((FFN_ROWS, d), lambda i: (i, 0)),
        out_shape=jax.ShapeDtypeStruct((n, d), F32),
        scratch_shapes=[pltpu.VMEM((FFN_ROWS, d_ff), BF16)],
        compiler_params=pltpu.CompilerParams(
            dimension_semantics=("arbitrary",),
            vmem_limit_bytes=_vmem_limit(
                [((FFN_ROWS, d), F32), ((N_ADA, d), F32), (ln_g.shape, F32), (ln_b.shape, F32), ((FFN_ROWS, d), F32)],
                [((d, d_ff), BF16), ((d, d_ff), BF16), ((d_ff, d), BF16), ((FFN_ROWS, d_ff), BF16)],
                [((FFN_ROWS, d), BF16), ((FFN_ROWS, FFN_COLS), F32), ((FFN_ROWS, FFN_COLS), F32),
                 ((FFN_ROWS, d), F32), ((FFN_ROWS, d), F32)])),
        name="ffn",
    )(x2d, mod, ln_g, ln_b, wg, wu, wd)


def _proj_kernel(x_ref, mod_ref, wqk_ref, wvw_ref, wki_ref,
                 q_ref, k_ref, qi_ref, vt_ref, kia_ref, kib_ref, wit_ref, *, mod_row):
    x = x_ref[...]
    sh = mod_ref[0, mod_row:mod_row + 1, :]
    sc = mod_ref[0, mod_row + 1:mod_row + 2, :]
    u = (x * (1.0 + sc) + sh).astype(BF16)
    qkq = jnp.dot(u, wqk_ref[...], preferred_element_type=F32)
    q_ref[...] = (qkq[:, :ATTN_DIM] * (HEAD_DIM ** -0.5 * LOG2E)).astype(BF16)
    k_ref[...] = qkq[:, ATTN_DIM:2 * ATTN_DIM].astype(BF16)
    qi_ref[...] = qkq[:, 2 * ATTN_DIM:].astype(BF16)
    kk = jnp.dot(u, wki_ref[...], preferred_element_type=F32)
    kia_ref[...] = kk[:, :LANES].astype(BF16)
    kib_ref[...] = kk[:, LANES:].astype(BF16)
    vw = lax.dot_general(wvw_ref[...], u, NT_DIMS, preferred_element_type=F32)
    vt = vw[:ATTN_DIM].astype(BF16)
    ones_rows = jnp.where(lax.broadcasted_iota(jnp.int32, (HEAD_PAD, ATT_BLOCK), 0) == 0, 1.0, 0.0).astype(BF16)
    for c in range(vt_ref.shape[1]):
        for h in range(N_HEADS):
            vt_ref[0, c, h * HEAD_ROWS:h * HEAD_ROWS + HEAD_DIM, :] = \
                vt[h * HEAD_DIM:(h + 1) * HEAD_DIM, c * ATT_BLOCK:(c + 1) * ATT_BLOCK]
            vt_ref[0, c, h * HEAD_ROWS + HEAD_DIM:(h + 1) * HEAD_ROWS, :] = ones_rows
    wit_ref[0] = vw[ATTN_DIM:ATTN_DIM + IDX_HEADS, :]


def _attn_proj(x2d, mod, wqk, wvw_t, wki, *, bsz, seq, mod_row):
    n, d = x2d.shape
    tiles_per_seq = seq // PROJ_ROWS
    chunks_per_tile = PROJ_ROWS // ATT_BLOCK
    resident = dict(pipeline_mode=pl.Buffered(1))
    row_spec = lambda cols: pl.BlockSpec((PROJ_ROWS, cols), lambda i: (i, 0))
    return pl.pallas_call(
        functools.partial(_proj_kernel, mod_row=mod_row),
        grid=(n // PROJ_ROWS,),
        in_specs=[
            pl.BlockSpec((PROJ_ROWS, d), lambda i: (i, 0)),
            pl.BlockSpec((1, N_ADA, d), lambda i: (i // tiles_per_seq, 0, 0)),
            pl.BlockSpec(wqk.shape, lambda i: (0, 0), **resident),
            pl.BlockSpec(wvw_t.shape, lambda i: (0, 0), **resident),
            pl.BlockSpec(wki.shape, lambda i: (0, 0), **resident),
        ],
        out_specs=[
            row_spec(ATTN_DIM), row_spec(ATTN_DIM), row_spec(IDX_HEADS * IDX_DIM),
            pl.BlockSpec((1, chunks_per_tile, N_HEADS * HEAD_ROWS, ATT_BLOCK),
                         lambda i: (i // tiles_per_seq, i % tiles_per_seq, 0, 0)),
            row_spec(LANES), row_spec(LANES),
            pl.BlockSpec((1, IDX_HEADS, PROJ_ROWS), lambda i: (i // tiles_per_seq, 0, i % tiles_per_seq)),
        ],
        out_shape=[
            jax.ShapeDtypeStruct((n, ATTN_DIM), BF16),
            jax.ShapeDtypeStruct((n, ATTN_DIM), BF16),
            jax.ShapeDtypeStruct((n, IDX_HEADS * IDX_DIM), BF16),
            jax.ShapeDtypeStruct((bsz, seq // ATT_BLOCK, N_HEADS * HEAD_ROWS, ATT_BLOCK), BF16),
            jax.ShapeDtypeStruct((n, LANES), BF16),
            jax.ShapeDtypeStruct((n, LANES), BF16),
            jax.ShapeDtypeStruct((bsz, IDX_HEADS, seq), F32),
        ],
        compiler_params=pltpu.CompilerParams(
            dimension_semantics=("arbitrary",),
            vmem_limit_bytes=_vmem_limit(
                [((PROJ_ROWS, d), F32), ((N_ADA, d), F32), ((PROJ_ROWS, ATTN_DIM), BF16), ((PROJ_ROWS, ATTN_DIM), BF16),
                 ((PROJ_ROWS, IDX_HEADS * IDX_DIM), BF16), ((chunks_per_tile, N_HEADS * HEAD_ROWS, ATT_BLOCK), BF16),
                 ((PROJ_ROWS, LANES), BF16), ((PROJ_ROWS, LANES), BF16), ((IDX_HEADS, PROJ_ROWS), F32)],
                [(wqk.shape, BF16), (wvw_t.shape, BF16), (wki.shape, BF16)],
                [((PROJ_ROWS, d), BF16), ((PROJ_ROWS, wqk.shape[1]), F32), ((PROJ_ROWS, wki.shape[1]), F32),
                 ((wvw_t.shape[0], PROJ_ROWS), F32)],
                stays_in_hbm=((n, ATTN_DIM), BF16))),
        name="attn_proj",
    )(x2d, mod, wqk, wvw_t, wki)


def _t5_bucket(n):
    max_exact = REL_BUCKETS // 2
    nf = jnp.maximum(n, 1).astype(F32)
    large = max_exact + jnp.floor(jnp.log(nf / max_exact) / math.log(REL_MAX_DIST / max_exact)
                                  * (REL_BUCKETS - max_exact)).astype(jnp.int32)
    large = jnp.minimum(large, REL_BUCKETS - 1)
    return jnp.where(n < max_exact, n, large)


def _far_bucket(first_dist, last_dist):
    n = np.arange(first_dist, last_dist + 1, dtype=np.float32)
    max_exact = REL_BUCKETS // 2
    large = max_exact + (np.log(n / np.float32(max_exact)) / np.float32(math.log(REL_MAX_DIST / max_exact))
                         * np.float32(REL_BUCKETS - max_exact)).astype(np.int32)
    buckets = np.where(n < max_exact, n.astype(np.int32), np.minimum(large, REL_BUCKETS - 1))
    assert buckets.min() == buckets.max(), "key chunks two or more blocks away must share one bias bucket"
    return int(buckets[0])


def _fill_bias_tiles(rb_ref, bt_ref, far_bucket):
    row = lax.broadcasted_iota(jnp.int32, (ATT_BLOCK, ATT_BLOCK), 0)
    col = lax.broadcasted_iota(jnp.int32, (ATT_BLOCK, ATT_BLOCK), 1)
    for o in range(2):
        bucket = _t5_bucket(jnp.maximum(o * ATT_BLOCK + col - row, 0))
        for h in range(N_HEADS):
            tile = jnp.zeros((ATT_BLOCK, ATT_BLOCK), F32)
            for b in range(REL_BUCKETS):
                tile = jnp.where(bucket == b, rb_ref[b, h], tile)
            bt_ref[o, h] = (tile - rb_ref[far_bucket, h]) * LOG2E


def _attn_kernel(q_ref, qi_ref, wit_ref, k_ref, vt_ref, kia_ref, kib_ref, rb_ref, o_ref,
                 bt_ref, sc_ref, zc_ref, ranked_ref, qh_ref, lg_ref, p_ref, pp_ref, st_ref, acc_ref, *,
                 topk, seq, far_bucket):
    blk = ATT_BLOCK
    groups = blk // SUBLANES
    i = pl.program_id(1)
    nch = i + 1
    kf = float(topk)
    inf = jnp.inf

    @pl.when(jnp.logical_and(pl.program_id(0) == 0, i == 0))
    def _():
        _fill_bias_tiles(rb_ref, bt_ref, far_bucket)

    def chunk_start(j):
        return j * blk if isinstance(j, int) else pl.multiple_of(j * blk, blk)

    def as_groups(x):
        return x.reshape(groups, SUBLANES, blk)

    def lanes8(v):
        return jnp.broadcast_to(v, (SUBLANES, blk))

    def colmin(x8):
        return jnp.min(x8, axis=0, keepdims=True)

    def colmax(x8):
        return jnp.max(x8, axis=0, keepdims=True)

    def colsum(x8):
        return jnp.sum(x8, axis=0, keepdims=True)

    wf = wit_ref[0] * (IDX_DIM ** -0.5)
    qi = qi_ref[0]
    row = lax.broadcasted_iota(jnp.int32, (blk, blk), 0)
    col = lax.broadcasted_iota(jnp.int32, (blk, blk), 1)
    causal = row <= col

    def chunk_scores(j):
        r0 = chunk_start(j)
        ka = kia_ref[0, pl.ds(r0, blk), :]
        kb = kib_ref[0, pl.ds(r0, blk), :]
        s = jnp.zeros((blk, blk), F32)
        for hp in range(IDX_HEADS // 2):
            qp = qi[:, hp * LANES:(hp + 1) * LANES]
            a0 = lax.dot_general(ka, qp, NT_DIMS, preferred_element_type=F32)
            a1 = lax.dot_general(kb, qp, NT_DIMS, preferred_element_type=F32)
            s = s + jnp.maximum(a0, 0.0) * wf[2 * hp:2 * hp + 1, :]
            s = s + jnp.maximum(a1, 0.0) * wf[2 * hp + 1:2 * hp + 2, :]
        return r0, s * (IDX_HEADS ** -0.5)

    def score_stats(j, s_lo, s_hi, stats):
        mn8, mx8, pos8, zer8 = stats
        hi3 = as_groups(s_hi)
        zer8 = zer8 + jnp.sum(jnp.where(hi3 == 0.0, 1.0, 0.0), axis=0)
        zc_ref[j] = zer8
        return (jnp.minimum(mn8, jnp.min(as_groups(s_lo), axis=0)),
                jnp.maximum(mx8, jnp.max(hi3, axis=0)),
                pos8 + jnp.sum(jnp.where(hi3 > 0.0, 1.0, 0.0), axis=0),
                zer8)

    def score_body(j, stats):
        r0, s = chunk_scores(j)
        sc_ref[pl.ds(r0, blk), :] = s
        return score_stats(j, s, s, stats)

    def score_run(first, count, stats):
        for c in range(count):
            stats = score_body(first + c, stats)
        return stats

    def score_diag(st):
        r_diag, s_diag = chunk_scores(i)
        s_diag_hi = jnp.where(causal, s_diag, -inf)
        sc_ref[pl.ds(r_diag, blk), :] = s_diag_hi
        return score_stats(i, jnp.where(causal, s_diag, inf), s_diag_hi, st)

    zeros8 = jnp.zeros((SUBLANES, blk), F32)
    stats = (jnp.full((SUBLANES, blk), inf, F32), jnp.full((SUBLANES, blk), -inf, F32), zeros8, zeros8)
    stats = lax.fori_loop(0, i // SCORE_RUN, lambda jq, st: score_run(SCORE_RUN * jq, SCORE_RUN, st), stats)
    left = i % SCORE_RUN
    mn8, mx8, pos8, zer8 = lax.switch(
        left, [functools.partial(lambda r, st: score_diag(score_run(i - r, r, st)), r) for r in range(SCORE_RUN)], stats)

    @pl.when(nch % 2 == 1)
    def _():
        sc_ref[pl.ds(chunk_start(nch), blk), :] = jnp.full((blk, blk), -inf, F32)

    npair = (nch + 1) // 2

    def count_gt(thr):
        t8 = lanes8(thr)

        def body(jp, accs):
            base = pl.multiple_of(jp * 2 * blk, 2 * blk)
            accs = list(accs)
            for s in range(2 * blk // COUNT_ROWS):
                x = sc_ref[pl.ds(base + s * COUNT_ROWS, COUNT_ROWS), :]
                for g in range(COUNT_ROWS // SUBLANES):
                    hit = jnp.where(x[g * SUBLANES:(g + 1) * SUBLANES] > t8, 1.0, 0.0)
                    accs[g % COUNT_CHAINS] = accs[g % COUNT_CHAINS] + hit
            return tuple(accs)

        accs = lax.fori_loop(0, npair, body, tuple(jnp.zeros((SUBLANES, blk), F32) for _ in range(COUNT_CHAINS)))
        return colsum(sum(accs[1:], accs[0]))

    def bisect(_, st):
        lo, hi, low, done = st
        mid = 0.5 * lo + 0.5 * hi
        c = count_gt(mid)
        live = done < 0.5
        up = jnp.logical_and(live, c >= kf)
        down = jnp.logical_and(live, c < kf)
        low = jnp.where(up, mid, low)
        lo = jnp.where(up, mid, lo)
        hi = jnp.where(down, mid, hi)
        done = jnp.where(c == kf, 1.0, done)
        return lo, hi, low, done

    tpos = i * blk + lax.broadcasted_iota(jnp.int32, (1, blk), 1)
    pos, zer = colsum(pos8), colsum(zer8)
    mn, mx = colmin(mn8), colmax(mx8)
    zero_tie = jnp.logical_and(pos < kf, pos + zer >= kf)
    pos_ge = pos >= kf
    done0 = jnp.where(jnp.logical_or(jnp.logical_or(tpos + 1 <= topk, zero_tie), pos == kf), 1.0, 0.0)
    low0 = jnp.where(jnp.logical_or(zero_tie, pos_ge), 0.0, -inf)
    lo0 = jnp.where(pos_ge, jnp.maximum(mn, 0.0), mn)
    hi0 = jnp.where(pos_ge, mx, jnp.minimum(mx, 0.0))
    tie0 = jnp.where(zero_tie, 0.0, inf)
    need0 = jnp.where(zero_tie, kf - pos, 0.0)

    def next_value_above(thr):
        t8 = lanes8(thr)

        def body(jp, accs):
            base = pl.multiple_of(jp * 2 * blk, 2 * blk)
            accs = list(accs)
            for s in range(2 * blk // COUNT_ROWS):
                x = sc_ref[pl.ds(base + s * COUNT_ROWS, COUNT_ROWS), :]
                for g in range(COUNT_ROWS // SUBLANES):
                    xg = x[g * SUBLANES:(g + 1) * SUBLANES]
                    accs[g % COUNT_CHAINS] = jnp.minimum(accs[g % COUNT_CHAINS], jnp.where(xg > t8, xg, inf))
            return tuple(accs)

        accs = lax.fori_loop(0, npair, body, tuple(jnp.full((SUBLANES, blk), inf, F32) for _ in range(COUNT_CHAINS)))
        return colmin(functools.reduce(jnp.minimum, accs))

    def climb_cond(st):
        return jnp.min(st[1]) < 0.5

    def climb_body(st):
        low, done, tie, need = st
        live = done < 0.5
        cand = next_value_above(low)
        cgt = count_gt(cand)
        found = jnp.logical_and(live, cgt < kf)
        tie = jnp.where(found, cand, tie)
        need = jnp.where(found, kf - cgt, need)
        low = jnp.where(live, cand, low)
        done = jnp.where(jnp.logical_and(live, cgt <= kf), 1.0, done)
        return low, done, tie, need

    def select(_):
        _, _, low, done = lax.fori_loop(0, BISECT_STEPS, bisect, (lo0, hi0, low0, done0))
        low, _, tie, need = lax.while_loop(climb_cond, climb_body, (low, done, tie0, need0))
        return low, tie, need

    low, tie, need = lax.cond(jnp.min(done0) < 0.5, select, lambda _: (low0, tie0, need0), 0)

    def plain_mask(_):
        low8 = lanes8(low)

        def body(j, _):
            r0 = chunk_start(j)
            x = as_groups(sc_ref[pl.ds(r0, blk), :])
            sc_ref[pl.ds(r0, blk), :] = jnp.where(x > low8[None], 0.0, -inf).reshape(blk, blk)
            return 0

        return lax.fori_loop(0, nch, body, 0)

    def ranked_chunk(j, tri, base):
        r0 = chunk_start(j)
        x = sc_ref[pl.ds(r0, blk), :]
        eq = x == tie
        rank = jnp.dot(tri, jnp.where(eq, 1.0, 0.0).astype(BF16), preferred_element_type=F32)
        sel = jnp.logical_or(x > low, jnp.logical_and(eq, rank + base <= need))
        sc_ref[pl.ds(r0, blk), :] = jnp.where(sel, 0.0, -inf)
        return rank[blk - 1:blk, :]

    def tie_mask(_):
        tri = jnp.where(row >= col, 1.0, 0.0).astype(BF16)
        lax.fori_loop(0, nch, lambda j, base: base + ranked_chunk(j, tri, base), jnp.zeros((1, blk), F32))
        return 0

    def zero_tie_mask(_):
        tri = jnp.where(row >= col, 1.0, 0.0).astype(BF16)

        def cut_body(j, ncut):
            return ncut + jnp.where(colsum(zc_ref[j]) < need, 1.0, 0.0)

        ncut = lax.fori_loop(0, nch, cut_body, jnp.zeros((1, blk), F32))
        cut = jnp.where(tie < inf, ncut, inf)

        def flag_body(j, _):
            ranked_ref[j] = 0
            return 0

        lax.fori_loop(0, nch, flag_body, 0)

        def cut_cond(pending):
            return jnp.min(pending) < inf

        def cut_chunk(pending):
            jf = jnp.min(pending)
            j = jf.astype(jnp.int32)
            base = jnp.where(j > 0, colsum(zc_ref[jnp.maximum(j - 1, 0)]), 0.0)
            ranked_chunk(j, tri, base)
            ranked_ref[j] = 1
            return jnp.where(pending == jf, inf, pending)

        lax.while_loop(cut_cond, cut_chunk, cut)
        low8, cut8 = lanes8(low), lanes8(cut)

        def body(j, _):
            @pl.when(ranked_ref[j] == 0)
            def _():
                r0 = chunk_start(j)
                x = as_groups(sc_ref[pl.ds(r0, blk), :])
                at_low = jnp.where(jnp.logical_and(cut8 < inf, cut8 > j.astype(F32)), 0.0, -inf)
                out = jnp.where(x > low8[None], 0.0, jnp.where(x == low8[None], at_low[None], -inf))
                sc_ref[pl.ds(r0, blk), :] = out.reshape(blk, blk)
            return 0

        return lax.fori_loop(0, nch, body, 0)

    tied = tie < inf
    mask_kind = jnp.where(jnp.max(jnp.where(tied, 1.0, 0.0)) < 0.5, 0,
                          jnp.where(jnp.max(jnp.where(jnp.logical_and(tied, tie != 0.0), 1.0, 0.0)) < 0.5, 1, 2))
    lax.switch(mask_kind, [plain_mask, zero_tie_mask, tie_mask], 0)

    q = q_ref[0]
    lane = lax.broadcasted_iota(jnp.int32, (blk, LANES), 1)
    for h in range(N_HEADS):
        hp = h // 2
        in_head = (lane < HEAD_DIM) if h % 2 == 0 else (lane >= HEAD_DIM)
        qh_ref[h] = jnp.where(in_head, q[:, hp * LANES:(hp + 1) * LANES], jnp.zeros((), BF16))

    def logits(j, h, mb, near):
        hp = h // 2
        kc = k_ref[0, pl.ds(chunk_start(j), blk), hp * LANES:(hp + 1) * LANES]
        lg = lax.dot_general(kc, qh_ref[h], NT_DIMS, preferred_element_type=F32) + mb
        return lg if near is None else lg + bt_ref[near, h]

    CMAX, RMAX, RESC = 0, 1, 2

    def stage_logits(j, near):
        mb = sc_ref[pl.ds(chunk_start(j), blk), :]
        cm8 = []
        for h in range(N_HEADS):
            lg = logits(j, h, mb, near)
            lg_ref[h] = lg
            cm8.append(jnp.max(as_groups(lg), axis=0))
        for h in range(N_HEADS):
            st_ref[CMAX, h] = lanes8(colmax(cm8[h]))

    def stage_exp():
        for h in range(N_HEADS):
            m_old = st_ref[RMAX, h]
            m_new = jnp.maximum(m_old, st_ref[CMAX, h])
            m_use = jnp.where(m_new == -inf, 0.0, m_new)
            p = jnp.exp2(as_groups(lg_ref[h]) - m_use[None])
            p_ref[h] = p.reshape(blk, blk).astype(BF16)
            st_ref[RMAX, h] = m_new
            st_ref[RESC, h] = jnp.exp2(m_old - m_use)

    def stage_pv(j):
        for h in range(N_HEADS):
            rows = slice(h * HEAD_ROWS, (h + 1) * HEAD_ROWS)
            pv = jnp.dot(vt_ref[0, j, rows, :], p_ref[h], preferred_element_type=F32)
            acc = acc_ref[rows, :].reshape(HEAD_ROWS // SUBLANES, SUBLANES, blk) * st_ref[RESC, h][None]
            acc_ref[rows, :] = acc.reshape(HEAD_ROWS, blk) + pv

    def pipelined(j, near_next):
        stage_exp()
        stage_logits(j + 1, near_next)
        stage_pv(j)

    def exact_attention():
        acc_ref[...] = jnp.zeros(acc_ref.shape, F32)
        st_ref[RMAX] = jnp.full((N_HEADS, SUBLANES, blk), -inf, F32)

        def last_chunks(first):
            for j in range(first, 0):
                pipelined(i + j, -(j + 1))
            stage_exp()
            stage_pv(i)

        @pl.when(i == 0)
        def _():
            stage_logits(0, 0)
            last_chunks(0)

        @pl.when(i == 1)
        def _():
            stage_logits(0, 1)
            last_chunks(-1)

        @pl.when(i >= 2)
        def _():
            stage_logits(0, None)

        def far_body(j, _):
            pipelined(j, None)
            return 0

        lax.fori_loop(0, jnp.maximum(i - 2, 0), far_body, 0)

        @pl.when(i >= 2)
        def _():
            last_chunks(-2)

    acc_ref[...] = jnp.zeros(acc_ref.shape, F32)
    st_ref[RMAX] = jnp.zeros((N_HEADS, SUBLANES, blk), F32)

    def fast_run(chunks):
        for c, (j, near) in enumerate(chunks):
            mb = sc_ref[pl.ds(chunk_start(j), blk), :]
            for h in range(N_HEADS):
                p = jnp.exp2(as_groups(logits(j, h, mb, near)) - st_ref[RMAX, h][None])
                pp_ref[c, h] = p.reshape(blk, blk).astype(BF16)
        for h in range(N_HEADS):
            rows = slice(h * HEAD_ROWS, (h + 1) * HEAD_ROWS)
            tot = acc_ref[rows, :]
            for c, (j, _) in enumerate(chunks):
                tot = tot + jnp.dot(vt_ref[0, j, rows, :], pp_ref[c, h], preferred_element_type=F32)
            den = tot[HEAD_DIM:HEAD_DIM + 1, :]
            up = lanes8(jnp.where(den > 0.0, jnp.floor(jnp.log2(den)), 0.0))
            scaled = tot.reshape(HEAD_ROWS // SUBLANES, SUBLANES, blk) * jnp.exp2(-up)[None]
            acc_ref[rows, :] = scaled.reshape(HEAD_ROWS, blk)
            st_ref[RMAX, h] = st_ref[RMAX, h] + up

    def far_run(jr, _):
        fast_run([(FAST_RUN * jr + c, None) for c in range(FAST_RUN)])
        return 0

    n_far = jnp.maximum(i - 1, 0)
    lax.fori_loop(0, n_far // FAST_RUN, far_run, 0)

    @pl.when(i == 0)
    def _():
        fast_run([(0, 0)])

    for left in range(FAST_RUN):
        @pl.when(jnp.logical_and(i >= 1, n_far % FAST_RUN == left))
        def _():
            last = [(i - 1 - left + c, None) for c in range(left)] + [(i - 1, 1), (i, 0)]
            for first in range(0, len(last), FAST_RUN):
                fast_run(last[first:first + FAST_RUN])

    acc = acc_ref[...]
    finite = jnp.min(jnp.where(jnp.isfinite(acc), 1.0, 0.0))
    dens = jnp.concatenate([acc[h * HEAD_ROWS + HEAD_DIM:h * HEAD_ROWS + HEAD_DIM + 1, :] for h in range(N_HEADS)], axis=0)
    usable = jnp.logical_and(finite > 0.5, jnp.min(dens) > 0.0)
    lax.cond(usable, lambda: None, exact_attention)

    heads = []
    for h in range(N_HEADS):
        r0 = h * HEAD_ROWS
        heads.append(acc_ref[r0:r0 + HEAD_DIM, :] / acc_ref[r0 + HEAD_DIM:r0 + HEAD_DIM + 1, :])
    o_ref[0] = jnp.concatenate(heads, axis=0).T.astype(BF16)


def _attention(q, qi, wit, k, vt, kia, kib, rel_bias, *, bsz, seq, topk, far_bucket):
    blk = ATT_BLOCK
    nblk = seq // blk
    bias_tiles = (2, N_HEADS, blk, blk)
    return pl.pallas_call(
        functools.partial(_attn_kernel, topk=topk, seq=seq, far_bucket=far_bucket),
        grid=(bsz, nblk),
        in_specs=[
            pl.BlockSpec((1, blk, ATTN_DIM), lambda b, i: (b, i, 0)),
            pl.BlockSpec((1, blk, IDX_HEADS * IDX_DIM), lambda b, i: (b, i, 0)),
            pl.BlockSpec((1, IDX_HEADS, blk), lambda b, i: (b, 0, i)),
            pl.BlockSpec((1, seq, ATTN_DIM), lambda b, i: (b, 0, 0)),
            pl.BlockSpec((1, nblk, N_HEADS * HEAD_ROWS, blk), lambda b, i: (b, 0, 0, 0)),
            pl.BlockSpec((1, seq, LANES), lambda b, i: (b, 0, 0)),
            pl.BlockSpec((1, seq, LANES), lambda b, i: (b, 0, 0)),
            pl.BlockSpec(memory_space=pltpu.SMEM),
        ],
        out_specs=pl.BlockSpec((1, blk, ATTN_DIM), lambda b, i: (b, i, 0)),
        out_shape=jax.ShapeDtypeStruct((bsz, seq, ATTN_DIM), BF16),
        scratch_shapes=[
            pltpu.VMEM(bias_tiles, F32),
            pltpu.VMEM((seq + blk, blk), F32),
            pltpu.VMEM((nblk, SUBLANES, blk), F32),
            pltpu.SMEM((nblk,), jnp.int32),
            pltpu.VMEM((N_HEADS, blk, LANES), BF16),
            pltpu.VMEM((N_HEADS, blk, blk), F32),
            pltpu.VMEM((N_HEADS, blk, blk), BF16),
            pltpu.VMEM((FAST_RUN, N_HEADS, blk, blk), BF16),
            pltpu.VMEM((3, N_HEADS, SUBLANES, blk), F32),
            pltpu.VMEM((N_HEADS * HEAD_ROWS, blk), F32),
        ],
        compiler_params=pltpu.CompilerParams(
            dimension_semantics=("arbitrary", "arbitrary"),
            vmem_limit_bytes=_vmem_limit(
                [((blk, ATTN_DIM), BF16), ((blk, IDX_HEADS * IDX_DIM), BF16), ((IDX_HEADS, blk), F32),
                 ((seq, ATTN_DIM), BF16), ((nblk, N_HEADS * HEAD_ROWS, blk), BF16), ((seq, LANES), BF16),
                 ((seq, LANES), BF16), ((blk, ATTN_DIM), BF16)],
                [(bias_tiles, F32), ((seq + blk, blk), F32), ((nblk, SUBLANES, blk), F32), ((N_HEADS, blk, LANES), BF16),
                 ((N_HEADS, blk, blk), F32), ((N_HEADS, blk, blk), BF16), ((FAST_RUN, N_HEADS, blk, blk), BF16),
                 ((3, N_HEADS, SUBLANES, blk), F32),
                 ((N_HEADS * HEAD_ROWS, blk), F32)],
                [((blk, blk), F32)] * 4 + [((blk, blk), BF16), ((ATTN_DIM, blk), F32), ((blk, ATTN_DIM), F32)],
                stays_in_hbm=((bsz, seq, ATTN_DIM), BF16))),
        name="attention",
    )(q, qi, wit, k, vt, kia, kib, rel_bias)


def _mix_kernel(x_ref, mod_ref, att_ref, wpg_ref, wpool_ref, ps_ref, wa_ref, wb_ref, wo_ref,
                lng_ref, lnb_ref, o_ref, pe_ref, mix_ref, *, alpha, mod_row, ln_row, tiles_per_seq):
    rows = x_ref.shape[0]
    i = pl.program_id(0)
    seq_tile = i % tiles_per_seq

    @pl.when(i == 0)
    def _():
        pe_ref[rows:, :] = jnp.zeros((POOL_HALO, POOL_DIM), F32)

    x = x_ref[...]
    sh = mod_ref[0, mod_row:mod_row + 1, :]
    sc = mod_ref[0, mod_row + 1:mod_row + 2, :]
    gate = mod_ref[0, mod_row + 2:mod_row + 3, :]
    u = (x * (1.0 + sc) + sh).astype(BF16)
    pg = jnp.dot(u, wpg_ref[...], preferred_element_type=F32)
    pe_ref[0:POOL_HALO, :] = jnp.where(seq_tile == 0, 0.0, pe_ref[rows:, :])
    pe_ref[POOL_HALO:, :] = pg[:, :POOL_DIM]
    t = seq_tile * rows + lax.broadcasted_iota(jnp.int32, (rows, 1), 0)
    for g, w in enumerate(POOL_WINDOWS):
        cols = slice(g * POOL_GROUP_DIM, (g + 1) * POOL_GROUP_DIM)
        cur = pe_ref[POOL_HALO:, cols]
        win = cur
        for back in range(1, w):
            win = win + pe_ref[POOL_HALO - back:POOL_HALO - back + rows, cols]
        cnt = jnp.minimum(t + 1, w).astype(F32)
        pooled = (win / cnt - cur).astype(BF16)
        mixed = jnp.dot(pooled, wpool_ref[g], preferred_element_type=F32)
        mix_ref[:, cols] = (mixed * ps_ref[:, cols]).astype(BF16)
    y_a = jnp.dot(mix_ref[...], wa_ref[...], preferred_element_type=F32)
    y_b = jnp.dot(att_ref[...], wb_ref[...], preferred_element_type=F32)
    d = x.shape[1]
    ga = pg[:, POOL_DIM:POOL_DIM + d]
    gb = pg[:, POOL_DIM + d:]
    merged = (jax.nn.sigmoid(ga) * y_a + jax.nn.sigmoid(gb) * y_b).astype(BF16)
    y = jnp.dot(merged, wo_ref[...], preferred_element_type=F32)
    z = alpha * x + gate * y
    o_ref[...] = _layer_norm(z, lng_ref[ln_row:ln_row + 1, :], lnb_ref[ln_row:ln_row + 1, :])


def _mix_out(x2d, mod, att2d, wpg, wpool, pool_scale, wa, wb, wo, ln_g, ln_b, *, seq, alpha, mod_row, ln_row):
    n, d = x2d.shape
    tiles_per_seq = seq // MIX_ROWS
    resident = dict(pipeline_mode=pl.Buffered(1))
    full = lambda a: pl.BlockSpec(a.shape, lambda i: (0,) * a.ndim, **resident)
    return pl.pallas_call(
        functools.partial(_mix_kernel, alpha=alpha, mod_row=mod_row, ln_row=ln_row, tiles_per_seq=tiles_per_seq),
        grid=(n // MIX_ROWS,),
        in_specs=[
            pl.BlockSpec((MIX_ROWS, d), lambda i: (i, 0)),
            pl.BlockSpec((1, N_ADA, d), lambda i: (i // tiles_per_seq, 0, 0)),
            pl.BlockSpec((MIX_ROWS, ATTN_DIM), lambda i: (i, 0)),
            full(wpg), full(wpool), full(pool_scale), full(wa), full(wb), full(wo),
            pl.BlockSpec(ln_g.shape, lambda i: (0, 0)),
            pl.BlockSpec(ln_b.shape, lambda i: (0, 0)),
        ],
        out_specs=pl.BlockSpec((MIX_ROWS, d), lambda i: (i, 0)),
        out_shape=jax.ShapeDtypeStruct((n, d), F32),
        scratch_shapes=[
            pltpu.VMEM((POOL_HALO + MIX_ROWS, POOL_DIM), F32),
            pltpu.VMEM((MIX_ROWS, POOL_DIM), BF16),
        ],
        compiler_params=pltpu.CompilerParams(
            dimension_semantics=("arbitrary",),
            vmem_limit_bytes=_vmem_limit(
                [((MIX_ROWS, d), F32), ((N_ADA, d), F32), ((MIX_ROWS, ATTN_DIM), BF16), (ln_g.shape, F32),
                 (ln_b.shape, F32), ((MIX_ROWS, d), F32)],
                [(wpg.shape, BF16), (wpool.shape, BF16), (pool_scale.shape, F32), (wa.shape, BF16), (wb.shape, BF16),
                 (wo.shape, BF16), ((POOL_HALO + MIX_ROWS, POOL_DIM), F32), ((MIX_ROWS, POOL_DIM), BF16)],
                [((MIX_ROWS, d), BF16), ((MIX_ROWS, wpg.shape[1]), F32)] + [((MIX_ROWS, d), F32)] * 5)),
        name="mix_out",
    )(x2d, mod, att2d, wpg, wpool, pool_scale, wa, wb, wo, ln_g, ln_b)


def kernel(x, c, w_ada, b_ada, ln_g, ln_b, ffn1_w_gate, ffn1_w_up, ffn1_w_down, w_in, w_pool, pool_scale,
           w_a, w_b, w_out, rel_bias, ffn2_w_gate, ffn2_w_up, ffn2_w_down):
    bsz, seq, d = x.shape
    depth = w_ada.shape[0]
    alpha = (2.0 * depth) ** 0.25
    topk = min(TOP_K, seq // 4)
    assert seq % FFN_ROWS == 0 and seq % PROJ_ROWS == 0 and seq % MIX_ROWS == 0 and seq % ATT_BLOCK == 0
    assert PROJ_ROWS % ATT_BLOCK == 0 and POOL_HALO >= max(POOL_WINDOWS) - 1
    far_bucket = _far_bucket(ATT_BLOCK + 1, max(seq - 1, ATT_BLOCK + 1))

    o_q = POOL_DIM
    o_k = o_q + ATTN_DIM
    o_v = o_k + ATTN_DIM
    o_qi = o_v + ATTN_DIM
    o_ki = o_qi + IDX_HEADS * IDX_DIM
    o_wi = o_ki + IDX_DIM
    o_ga = o_wi + IDX_HEADS

    x2d = x.reshape(bsz * seq, d)
    for l in range(depth):
        wl = w_in[l]
        zeros_ki = jnp.zeros((d, LANES - IDX_DIM), wl.dtype)
        w_ki = wl[:, o_ki:o_wi]
        wqk = jnp.concatenate([wl[:, o_q:o_v], wl[:, o_qi:o_ki]], axis=1).astype(BF16)
        wvw_t = jnp.pad(jnp.concatenate([wl[:, o_v:o_qi], wl[:, o_wi:o_ga]], axis=1).T,
                        ((0, 2 * SUBLANES - IDX_HEADS), (0, 0))).astype(BF16)
        wki = jnp.concatenate([w_ki, zeros_ki, zeros_ki, w_ki], axis=1).astype(BF16)
        wpg = jnp.concatenate([wl[:, :POOL_DIM], wl[:, o_ga:]], axis=1).astype(BF16)

        mod = _ada_mod(c, w_ada[l], b_ada[l]).reshape(bsz, N_ADA, d)
        x2d = _ffn(x2d, mod, ln_g[l], ln_b[l], ffn1_w_gate[l].astype(BF16), ffn1_w_up[l].astype(BF16),
                   ffn1_w_down[l].astype(BF16), seq=seq, alpha=alpha, mod_row=0, ln_row=0)
        q, k, qi, vt, kia, kib, wit = _attn_proj(x2d, mod, wqk, wvw_t, wki, bsz=bsz, seq=seq, mod_row=3)
        att = _attention(q.reshape(bsz, seq, ATTN_DIM), qi.reshape(bsz, seq, IDX_HEADS * IDX_DIM), wit,
                         k.reshape(bsz, seq, ATTN_DIM), vt, kia.reshape(bsz, seq, LANES),
                         kib.reshape(bsz, seq, LANES), rel_bias, bsz=bsz, seq=seq, topk=topk,
                         far_bucket=far_bucket)
        x2d = _mix_out(x2d, mod, att.reshape(bsz * seq, ATTN_DIM), wpg, w_pool[l].astype(BF16),
                       pool_scale[l].reshape(1, POOL_DIM), w_a[l].astype(BF16), w_b[l].astype(BF16),
                       w_out[l].astype(BF16), ln_g[l], ln_b[l], seq=seq, alpha=alpha, mod_row=3, ln_row=1)
        x2d = _ffn(x2d, mod, ln_g[l], ln_b[l], ffn2_w_gate[l].astype(BF16), ffn2_w_up[l].astype(BF16),
                   ffn2_w_down[l].astype(BF16), seq=seq, alpha=alpha, mod_row=6, ln_row=2)
    return x2d.reshape(bsz, seq, d)
```

```python
import functools
import math

import numpy as np
import jax
import jax.numpy as jnp
from jax import lax
from jax.experimental import pallas as pl
from jax.experimental.pallas import tpu as pltpu

POOL_WINDOWS = (2, 4, 8, 16)
POOL_GROUP_DIM = 128
POOL_DIM = len(POOL_WINDOWS) * POOL_GROUP_DIM
N_HEADS = 8
HEAD_DIM = 64
ATTN_DIM = N_HEADS * HEAD_DIM
HEAD_PAD = 16
HEAD_ROWS = HEAD_DIM + HEAD_PAD
IDX_HEADS = 8
IDX_DIM = 64
TOP_K = 256
REL_BUCKETS = 32
REL_MAX_DIST = 128
N_ADA = 9
LN_EPS = 1e-5
POOL_HALO = 16

LANES = 128
SUBLANES = 8
V7X_VMEM_BYTES = 64 * 1024 * 1024
FFN_ROWS = 512
FFN_COLS = 256
PROJ_ROWS = 512
MIX_ROWS = 512
ATT_BLOCK = 256
ADA_COLS = 1024
SCORE_RUN = 4
FAST_RUN = 4
COUNT_CHAINS = 4
COUNT_ROWS = 64
BISECT_STEPS = 14

LOG2E = math.log2(math.e)
BF16 = jnp.bfloat16
F32 = jnp.float32
NT_DIMS = (((1,), (1,)), ((), ()))


def _tile_bytes(shape, dtype):
    itemsize = jnp.dtype(dtype).itemsize
    sublanes = SUBLANES * (4 // itemsize)
    shape = (1,) * (2 - len(shape)) + tuple(shape)
    rows = -(-shape[-2] // sublanes) * sublanes
    cols = -(-shape[-1] // LANES) * LANES
    return math.prod(shape[:-2]) * rows * cols * itemsize


def _vmem_limit(pipelined, resident, temporaries, stays_in_hbm=None):
    need = (2 * sum(_tile_bytes(*b) for b in pipelined) + sum(_tile_bytes(*b) for b in resident)
            + sum(_tile_bytes(*b) for b in temporaries))
    if stays_in_hbm is not None:
        need = max(need, V7X_VMEM_BYTES - _tile_bytes(*stays_in_hbm))
    return need


def _layer_norm(z, g, b):
    mu = jnp.mean(z, axis=-1, keepdims=True)
    zc = z - mu
    var = jnp.mean(zc * zc, axis=-1, keepdims=True)
    return zc * lax.rsqrt(var + LN_EPS) * g + b


def _silu(a):
    return a * jax.nn.sigmoid(a)


def _ada_kernel(c_ref, w_ref, b_ref, o_ref):
    a = _silu(c_ref[...])
    o_ref[...] = jnp.dot(a, w_ref[...], preferred_element_type=F32) + b_ref[...]


def _ada_mod(c, w_ada, b_ada):
    bsz, d = c.shape
    n = w_ada.shape[1]
    rows = -(-bsz // SUBLANES) * SUBLANES
    c_pad = jnp.pad(c, ((0, rows - bsz), (0, 0)))
    out = pl.pallas_call(
        _ada_kernel,
        grid=(n // ADA_COLS,),
        in_specs=[
            pl.BlockSpec((rows, d), lambda j: (0, 0)),
            pl.BlockSpec((d, ADA_COLS), lambda j: (0, j)),
            pl.BlockSpec((1, ADA_COLS), lambda j: (0, j)),
        ],
        out_specs=pl.BlockSpec((rows, ADA_COLS), lambda j: (0, j)),
        out_shape=jax.ShapeDtypeStruct((rows, n), F32),
        compiler_params=pltpu.CompilerParams(
            dimension_semantics=("arbitrary",),
            vmem_limit_bytes=_vmem_limit([((rows, d), F32), ((d, ADA_COLS), F32), ((1, ADA_COLS), F32),
                                          ((rows, ADA_COLS), F32)], [], [((rows, ADA_COLS), F32)],
                                         stays_in_hbm=(w_ada.shape, F32))),
        name="ada_mod",
    )(c_pad, w_ada, b_ada.reshape(1, n))
    return out[:bsz]


def _ffn_kernel(x_ref, mod_ref, lng_ref, lnb_ref, wg_ref, wu_ref, wd_ref, o_ref, h_ref, *, alpha, mod_row, ln_row):
    x = x_ref[...]
    sh = mod_ref[0, mod_row:mod_row + 1, :]
    sc = mod_ref[0, mod_row + 1:mod_row + 2, :]
    gate = mod_ref[0, mod_row + 2:mod_row + 3, :]
    u = (x * (1.0 + sc) + sh).astype(BF16)
    d_ff = wg_ref.shape[1]
    for c in range(d_ff // FFN_COLS):
        sl = slice(c * FFN_COLS, (c + 1) * FFN_COLS)
        a = jnp.dot(u, wg_ref[:, sl], preferred_element_type=F32)
        b = jnp.dot(u, wu_ref[:, sl], preferred_element_type=F32)
        h_ref[:, sl] = (_silu(a) * b).astype(BF16)
    y = jnp.dot(h_ref[...], wd_ref[...], preferred_element_type=F32)
    z = alpha * x + (0.5 * gate) * y
    o_ref[...] = _layer_norm(z, lng_ref[ln_row:ln_row + 1, :], lnb_ref[ln_row:ln_row + 1, :])


def _ffn(x2d, mod, ln_g, ln_b, wg, wu, wd, *, seq, alpha, mod_row, ln_row):
    n, d = x2d.shape
    d_ff = wg.shape[1]
    tiles_per_seq = seq // FFN_ROWS
    resident = dict(pipeline_mode=pl.Buffered(1))
    return pl.pallas_call(
        functools.partial(_ffn_kernel, alpha=alpha, mod_row=mod_row, ln_row=ln_row),
        grid=(n // FFN_ROWS,),
        in_specs=[
            pl.BlockSpec((FFN_ROWS, d), lambda i: (i, 0)),
            pl.BlockSpec((1, N_ADA, d), lambda i: (i // tiles_per_seq, 0, 0)),
            pl.BlockSpec(ln_g.shape, lambda i: (0, 0)),
            pl.BlockSpec(ln_b.shape, lambda i: (0, 0)),
            pl.BlockSpec((d, d_ff), lambda i: (0, 0), **resident),
            pl.BlockSpec((d, d_ff), lambda i: (0, 0), **resident),
            pl.BlockSpec((d_ff, d), lambda i: (0, 0), **resident),
        ],
        out_specs=pl.BlockSpec((FFN_ROWS, d), lambda i: (i, 0)),
        out_shape=jax.ShapeDtypeStruct((n, d), F32),
        scratch_shapes=[pltpu.VMEM((FFN_ROWS, d_ff), BF16)],
        compiler_params=pltpu.CompilerParams(
            dimension_semantics=("arbitrary",),
            vmem_limit_bytes=_vmem_limit(
                [((FFN_ROWS, d), F32), ((N_ADA, d), F32), (ln_g.shape, F32), (ln_b.shape, F32), ((FFN_ROWS, d), F32)],
                [((d, d_ff), BF16), ((d, d_ff), BF16), ((d_ff, d), BF16), ((FFN_ROWS, d_ff), BF16)],
                [((FFN_ROWS, d), BF16), ((FFN_ROWS, FFN_COLS), F32), ((FFN_ROWS, FFN_COLS), F32),
                 ((FFN_ROWS, d), F32), ((FFN_ROWS, d), F32)])),
        name="ffn",
    )(x2d, mod, ln_g, ln_b, wg, wu, wd)


def _proj_kernel(x_ref, mod_ref, wqk_ref, wvw_ref, wki_ref,
                 q_ref, k_ref, qi_ref, vt_ref, kia_ref, kib_ref, wit_ref, wt_ref, *, mod_row):
    @pl.when(pl.program_id(0) == 0)
    def _():
        wt_ref[...] = wvw_ref[...].astype(F32).T[:wt_ref.shape[0]].astype(BF16)

    x = x_ref[...]
    sh = mod_ref[0, mod_row:mod_row + 1, :]
    sc = mod_ref[0, mod_row + 1:mod_row + 2, :]
    u = (x * (1.0 + sc) + sh).astype(BF16)
    qkq = jnp.dot(u, wqk_ref[...], preferred_element_type=F32)
    q_ref[...] = (qkq[:, :ATTN_DIM] * (HEAD_DIM ** -0.5 * LOG2E)).astype(BF16)
    k_ref[...] = qkq[:, ATTN_DIM:2 * ATTN_DIM].astype(BF16)
    qi_ref[...] = qkq[:, 2 * ATTN_DIM:].astype(BF16)
    kk = jnp.dot(u, wki_ref[...], preferred_element_type=F32)
    kia_ref[...] = kk[:, :LANES].astype(BF16)
    kib_ref[...] = kk[:, LANES:].astype(BF16)
    vw = lax.dot_general(wt_ref[...], u, NT_DIMS, preferred_element_type=F32)
    vt = vw[:ATTN_DIM].astype(BF16)
    ones_rows = jnp.where(lax.broadcasted_iota(jnp.int32, (HEAD_PAD, ATT_BLOCK), 0) == 0, 1.0, 0.0).astype(BF16)
    for c in range(vt_ref.shape[1]):
        for h in range(N_HEADS):
            vt_ref[0, c, h * HEAD_ROWS:h * HEAD_ROWS + HEAD_DIM, :] = \
                vt[h * HEAD_DIM:(h + 1) * HEAD_DIM, c * ATT_BLOCK:(c + 1) * ATT_BLOCK]
            vt_ref[0, c, h * HEAD_ROWS + HEAD_DIM:(h + 1) * HEAD_ROWS, :] = ones_rows
    wit_ref[0] = vw[ATTN_DIM:ATTN_DIM + IDX_HEADS, :]


def _attn_proj(x2d, mod, wqk, wvw_t, wki, *, bsz, seq, mod_row):
    n, d = x2d.shape
    tiles_per_seq = seq // PROJ_ROWS
    chunks_per_tile = PROJ_ROWS // ATT_BLOCK
    resident = dict(pipeline_mode=pl.Buffered(1))
    row_spec = lambda cols: pl.BlockSpec((PROJ_ROWS, cols), lambda i: (i, 0))
    return pl.pallas_call(
        functools.partial(_proj_kernel, mod_row=mod_row),
        grid=(n // PROJ_ROWS,),
        in_specs=[
            pl.BlockSpec((PROJ_ROWS, d), lambda i: (i, 0)),
            pl.BlockSpec((1, N_ADA, d), lambda i: (i // tiles_per_seq, 0, 0)),
            pl.BlockSpec(wqk.shape, lambda i: (0, 0), **resident),
            pl.BlockSpec(wvw_t.shape, lambda i: (0, 0), **resident),
            pl.BlockSpec(wki.shape, lambda i: (0, 0), **resident),
        ],
        out_specs=[
            row_spec(ATTN_DIM), row_spec(ATTN_DIM), row_spec(IDX_HEADS * IDX_DIM),
            pl.BlockSpec((1, chunks_per_tile, N_HEADS * HEAD_ROWS, ATT_BLOCK),
                         lambda i: (i // tiles_per_seq, i % tiles_per_seq, 0, 0)),
            row_spec(LANES), row_spec(LANES),
            pl.BlockSpec((1, IDX_HEADS, PROJ_ROWS), lambda i: (i // tiles_per_seq, 0, i % tiles_per_seq)),
        ],
        out_shape=[
            jax.ShapeDtypeStruct((n, ATTN_DIM), BF16),
            jax.ShapeDtypeStruct((n, ATTN_DIM), BF16),
            jax.ShapeDtypeStruct((n, IDX_HEADS * IDX_DIM), BF16),
            jax.ShapeDtypeStruct((bsz, seq // ATT_BLOCK, N_HEADS * HEAD_ROWS, ATT_BLOCK), BF16),
            jax.ShapeDtypeStruct((n, LANES), BF16),
            jax.ShapeDtypeStruct((n, LANES), BF16),
            jax.ShapeDtypeStruct((bsz, IDX_HEADS, seq), F32),
        ],
        scratch_shapes=[pltpu.VMEM((ATTN_DIM + 2 * SUBLANES, d), BF16)],
        compiler_params=pltpu.CompilerParams(
            dimension_semantics=("arbitrary",),
            vmem_limit_bytes=_vmem_limit(
                [((PROJ_ROWS, d), F32), ((N_ADA, d), F32), ((PROJ_ROWS, ATTN_DIM), BF16), ((PROJ_ROWS, ATTN_DIM), BF16),
                 ((PROJ_ROWS, IDX_HEADS * IDX_DIM), BF16), ((chunks_per_tile, N_HEADS * HEAD_ROWS, ATT_BLOCK), BF16),
                 ((PROJ_ROWS, LANES), BF16), ((PROJ_ROWS, LANES), BF16), ((IDX_HEADS, PROJ_ROWS), F32)],
                [(wqk.shape, BF16), (wvw_t.shape, BF16), (wki.shape, BF16), ((ATTN_DIM + 2 * SUBLANES, d), BF16)],
                [((PROJ_ROWS, d), BF16), ((PROJ_ROWS, wqk.shape[1]), F32), ((PROJ_ROWS, wki.shape[1]), F32),
                 ((wvw_t.shape[1], PROJ_ROWS), F32)],
                stays_in_hbm=((n, ATTN_DIM), BF16))),
        name="attn_proj",
    )(x2d, mod, wqk, wvw_t, wki)


def _t5_bucket(n):
    max_exact = REL_BUCKETS // 2
    nf = jnp.maximum(n, 1).astype(F32)
    large = max_exact + jnp.floor(jnp.log(nf / max_exact) / math.log(REL_MAX_DIST / max_exact)
                                  * (REL_BUCKETS - max_exact)).astype(jnp.int32)
    large = jnp.minimum(large, REL_BUCKETS - 1)
    return jnp.where(n < max_exact, n, large)


def _far_bucket(first_dist, last_dist):
    n = np.arange(first_dist, last_dist + 1, dtype=np.float32)
    max_exact = REL_BUCKETS // 2
    large = max_exact + (np.log(n / np.float32(max_exact)) / np.float32(math.log(REL_MAX_DIST / max_exact))
                         * np.float32(REL_BUCKETS - max_exact)).astype(np.int32)
    buckets = np.where(n < max_exact, n.astype(np.int32), np.minimum(large, REL_BUCKETS - 1))
    assert buckets.min() == buckets.max(), "key chunks two or more blocks away must share one bias bucket"
    return int(buckets[0])


def _bias_kernel(rb_ref, o_ref, *, far_bucket):
    o = pl.program_id(0)
    row = lax.broadcasted_iota(jnp.int32, (ATT_BLOCK, ATT_BLOCK), 0)
    col = lax.broadcasted_iota(jnp.int32, (ATT_BLOCK, ATT_BLOCK), 1)
    dist = o * ATT_BLOCK + col - row
    bucket = _t5_bucket(jnp.maximum(dist, 0))
    for h in range(N_HEADS):
        tile = jnp.zeros((ATT_BLOCK, ATT_BLOCK), F32)
        for b in range(REL_BUCKETS):
            tile = jnp.where(bucket == b, rb_ref[b, h], tile)
        o_ref[0, h] = (tile - rb_ref[far_bucket, h]) * LOG2E


def _rel_bias_tiles(rel_bias, far_bucket):
    return pl.pallas_call(
        functools.partial(_bias_kernel, far_bucket=far_bucket),
        grid=(2,),
        in_specs=[pl.BlockSpec(memory_space=pltpu.SMEM)],
        out_specs=pl.BlockSpec((1, N_HEADS, ATT_BLOCK, ATT_BLOCK), lambda o: (o, 0, 0, 0)),
        out_shape=jax.ShapeDtypeStruct((2, N_HEADS, ATT_BLOCK, ATT_BLOCK), F32),
        compiler_params=pltpu.CompilerParams(
            dimension_semantics=("arbitrary",),
            vmem_limit_bytes=_vmem_limit([((N_HEADS, ATT_BLOCK, ATT_BLOCK), F32)], [],
                                         [((ATT_BLOCK, ATT_BLOCK), F32)] * 3)),
        name="rel_bias",
    )(rel_bias)


def _attn_kernel(q_ref, qi_ref, wit_ref, k_ref, vt_ref, kia_ref, kib_ref, bt_ref, o_ref,
                 sc_ref, zc_ref, ranked_ref, qh_ref, lg_ref, p_ref, pp_ref, st_ref, acc_ref, *, topk, seq):
    blk = ATT_BLOCK
    groups = blk // SUBLANES
    i = pl.program_id(1)
    nch = i + 1
    kf = float(topk)
    inf = jnp.inf

    def chunk_start(j):
        return j * blk if isinstance(j, int) else pl.multiple_of(j * blk, blk)

    def as_groups(x):
        return x.reshape(groups, SUBLANES, blk)

    def lanes8(v):
        return jnp.broadcast_to(v, (SUBLANES, blk))

    def colmin(x8):
        return jnp.min(x8, axis=0, keepdims=True)

    def colmax(x8):
        return jnp.max(x8, axis=0, keepdims=True)

    def colsum(x8):
        return jnp.sum(x8, axis=0, keepdims=True)

    wf = wit_ref[0] * (IDX_DIM ** -0.5)
    qi = qi_ref[0]
    row = lax.broadcasted_iota(jnp.int32, (blk, blk), 0)
    col = lax.broadcasted_iota(jnp.int32, (blk, blk), 1)
    causal = row <= col

    def chunk_scores(j):
        r0 = chunk_start(j)
        ka = kia_ref[0, pl.ds(r0, blk), :]
        kb = kib_ref[0, pl.ds(r0, blk), :]
        s = jnp.zeros((blk, blk), F32)
        for hp in range(IDX_HEADS // 2):
            qp = qi[:, hp * LANES:(hp + 1) * LANES]
            a0 = lax.dot_general(ka, qp, NT_DIMS, preferred_element_type=F32)
            a1 = lax.dot_general(kb, qp, NT_DIMS, preferred_element_type=F32)
            s = s + jnp.maximum(a0, 0.0) * wf[2 * hp:2 * hp + 1, :]
            s = s + jnp.maximum(a1, 0.0) * wf[2 * hp + 1:2 * hp + 2, :]
        return r0, s * (IDX_HEADS ** -0.5)

    def score_stats(j, s_lo, s_hi, stats):
        mn8, mx8, pos8, zer8 = stats
        hi3 = as_groups(s_hi)
        zer8 = zer8 + jnp.sum(jnp.where(hi3 == 0.0, 1.0, 0.0), axis=0)
        zc_ref[j] = zer8
        return (jnp.minimum(mn8, jnp.min(as_groups(s_lo), axis=0)),
                jnp.maximum(mx8, jnp.max(hi3, axis=0)),
                pos8 + jnp.sum(jnp.where(hi3 > 0.0, 1.0, 0.0), axis=0),
                zer8)

    def score_body(j, stats):
        r0, s = chunk_scores(j)
        sc_ref[pl.ds(r0, blk), :] = s
        return score_stats(j, s, s, stats)

    def score_run(first, count, stats):
        for c in range(count):
            stats = score_body(first + c, stats)
        return stats

    def score_diag(st):
        r_diag, s_diag = chunk_scores(i)
        s_diag_hi = jnp.where(causal, s_diag, -inf)
        sc_ref[pl.ds(r_diag, blk), :] = s_diag_hi
        return score_stats(i, jnp.where(causal, s_diag, inf), s_diag_hi, st)

    zeros8 = jnp.zeros((SUBLANES, blk), F32)
    stats = (jnp.full((SUBLANES, blk), inf, F32), jnp.full((SUBLANES, blk), -inf, F32), zeros8, zeros8)
    stats = lax.fori_loop(0, i // SCORE_RUN, lambda jq, st: score_run(SCORE_RUN * jq, SCORE_RUN, st), stats)
    left = i % SCORE_RUN
    mn8, mx8, pos8, zer8 = lax.switch(
        left, [functools.partial(lambda r, st: score_diag(score_run(i - r, r, st)), r) for r in range(SCORE_RUN)], stats)

    @pl.when(nch % 2 == 1)
    def _():
        sc_ref[pl.ds(chunk_start(nch), blk), :] = jnp.full((blk, blk), -inf, F32)

    npair = (nch + 1) // 2

    def count_gt(thr):
        t8 = lanes8(thr)

        def body(jp, accs):
            base = pl.multiple_of(jp * 2 * blk, 2 * blk)
            accs = list(accs)
            for s in range(2 * blk // COUNT_ROWS):
                x = sc_ref[pl.ds(base + s * COUNT_ROWS, COUNT_ROWS), :]
                for g in range(COUNT_ROWS // SUBLANES):
                    hit = jnp.where(x[g * SUBLANES:(g + 1) * SUBLANES] > t8, 1.0, 0.0)
                    accs[g % COUNT_CHAINS] = accs[g % COUNT_CHAINS] + hit
            return tuple(accs)

        accs = lax.fori_loop(0, npair, body, tuple(jnp.zeros((SUBLANES, blk), F32) for _ in range(COUNT_CHAINS)))
        return colsum(sum(accs[1:], accs[0]))

    def bisect(_, st):
        lo, hi, low, done = st
        mid = 0.5 * lo + 0.5 * hi
        c = count_gt(mid)
        live = done < 0.5
        up = jnp.logical_and(live, c >= kf)
        down = jnp.logical_and(live, c < kf)
        low = jnp.where(up, mid, low)
        lo = jnp.where(up, mid, lo)
        hi = jnp.where(down, mid, hi)
        done = jnp.where(c == kf, 1.0, done)
        return lo, hi, low, done

    tpos = i * blk + lax.broadcasted_iota(jnp.int32, (1, blk), 1)
    pos, zer = colsum(pos8), colsum(zer8)
    mn, mx = colmin(mn8), colmax(mx8)
    zero_tie = jnp.logical_and(pos < kf, pos + zer >= kf)
    pos_ge = pos >= kf
    done0 = jnp.where(jnp.logical_or(jnp.logical_or(tpos + 1 <= topk, zero_tie), pos == kf), 1.0, 0.0)
    low0 = jnp.where(jnp.logical_or(zero_tie, pos_ge), 0.0, -inf)
    lo0 = jnp.where(pos_ge, jnp.maximum(mn, 0.0), mn)
    hi0 = jnp.where(pos_ge, mx, jnp.minimum(mx, 0.0))
    tie0 = jnp.where(zero_tie, 0.0, inf)
    need0 = jnp.where(zero_tie, kf - pos, 0.0)

    def next_value_above(thr):
        t8 = lanes8(thr)

        def body(jp, accs):
            base = pl.multiple_of(jp * 2 * blk, 2 * blk)
            accs = list(accs)
            for s in range(2 * blk // COUNT_ROWS):
                x = sc_ref[pl.ds(base + s * COUNT_ROWS, COUNT_ROWS), :]
                for g in range(COUNT_ROWS // SUBLANES):
                    xg = x[g * SUBLANES:(g + 1) * SUBLANES]
                    accs[g % COUNT_CHAINS] = jnp.minimum(accs[g % COUNT_CHAINS], jnp.where(xg > t8, xg, inf))
            return tuple(accs)

        accs = lax.fori_loop(0, npair, body, tuple(jnp.full((SUBLANES, blk), inf, F32) for _ in range(COUNT_CHAINS)))
        return colmin(functools.reduce(jnp.minimum, accs))

    def climb_cond(st):
        return jnp.min(st[1]) < 0.5

    def climb_body(st):
        low, done, tie, need = st
        live = done < 0.5
        cand = next_value_above(low)
        cgt = count_gt(cand)
        found = jnp.logical_and(live, cgt < kf)
        tie = jnp.where(found, cand, tie)
        need = jnp.where(found, kf - cgt, need)
        low = jnp.where(live, cand, low)
        done = jnp.where(jnp.logical_and(live, cgt <= kf), 1.0, done)
        return low, done, tie, need

    def select(_):
        _, _, low, done = lax.fori_loop(0, BISECT_STEPS, bisect, (lo0, hi0, low0, done0))
        low, _, tie, need = lax.while_loop(climb_cond, climb_body, (low, done, tie0, need0))
        return low, tie, need

    low, tie, need = lax.cond(jnp.min(done0) < 0.5, select, lambda _: (low0, tie0, need0), 0)

    def plain_mask(_):
        low8 = lanes8(low)

        def body(j, _):
            r0 = chunk_start(j)
            x = as_groups(sc_ref[pl.ds(r0, blk), :])
            sc_ref[pl.ds(r0, blk), :] = jnp.where(x > low8[None], 0.0, -inf).reshape(blk, blk)
            return 0

        return lax.fori_loop(0, nch, body, 0)

    def ranked_chunk(j, tri, base):
        r0 = chunk_start(j)
        x = sc_ref[pl.ds(r0, blk), :]
        eq = x == tie
        rank = jnp.dot(tri, jnp.where(eq, 1.0, 0.0).astype(BF16), preferred_element_type=F32)
        sel = jnp.logical_or(x > low, jnp.logical_and(eq, rank + base <= need))
        sc_ref[pl.ds(r0, blk), :] = jnp.where(sel, 0.0, -inf)
        return rank[blk - 1:blk, :]

    def tie_mask(_):
        tri = jnp.where(row >= col, 1.0, 0.0).astype(BF16)
        lax.fori_loop(0, nch, lambda j, base: base + ranked_chunk(j, tri, base), jnp.zeros((1, blk), F32))
        return 0

    def zero_tie_mask(_):
        tri = jnp.where(row >= col, 1.0, 0.0).astype(BF16)

        def cut_body(j, ncut):
            return ncut + jnp.where(colsum(zc_ref[j]) < need, 1.0, 0.0)

        ncut = lax.fori_loop(0, nch, cut_body, jnp.zeros((1, blk), F32))
        cut = jnp.where(tie < inf, ncut, inf)

        def flag_body(j, _):
            ranked_ref[j] = 0
            return 0

        lax.fori_loop(0, nch, flag_body, 0)

        def cut_cond(pending):
            return jnp.min(pending) < inf

        def cut_chunk(pending):
            jf = jnp.min(pending)
            j = jf.astype(jnp.int32)
            base = jnp.where(j > 0, colsum(zc_ref[jnp.maximum(j - 1, 0)]), 0.0)
            ranked_chunk(j, tri, base)
            ranked_ref[j] = 1
            return jnp.where(pending == jf, inf, pending)

        lax.while_loop(cut_cond, cut_chunk, cut)
        low8, cut8 = lanes8(low), lanes8(cut)

        def body(j, _):
            @pl.when(ranked_ref[j] == 0)
            def _():
                r0 = chunk_start(j)
                x = as_groups(sc_ref[pl.ds(r0, blk), :])
                at_low = jnp.where(jnp.logical_and(cut8 < inf, cut8 > j.astype(F32)), 0.0, -inf)
                out = jnp.where(x > low8[None], 0.0, jnp.where(x == low8[None], at_low[None], -inf))
                sc_ref[pl.ds(r0, blk), :] = out.reshape(blk, blk)
            return 0

        return lax.fori_loop(0, nch, body, 0)

    tied = tie < inf
    mask_kind = jnp.where(jnp.max(jnp.where(tied, 1.0, 0.0)) < 0.5, 0,
                          jnp.where(jnp.max(jnp.where(jnp.logical_and(tied, tie != 0.0), 1.0, 0.0)) < 0.5, 1, 2))
    lax.switch(mask_kind, [plain_mask, zero_tie_mask, tie_mask], 0)

    q = q_ref[0]
    lane = lax.broadcasted_iota(jnp.int32, (blk, LANES), 1)
    for h in range(N_HEADS):
        hp = h // 2
        in_head = (lane < HEAD_DIM) if h % 2 == 0 else (lane >= HEAD_DIM)
        qh_ref[h] = jnp.where(in_head, q[:, hp * LANES:(hp + 1) * LANES], jnp.zeros((), BF16))

    def logits(j, h, mb, near):
        hp = h // 2
        kc = k_ref[0, pl.ds(chunk_start(j), blk), hp * LANES:(hp + 1) * LANES]
        lg = lax.dot_general(kc, qh_ref[h], NT_DIMS, preferred_element_type=F32) + mb
        return lg if near is None else lg + bt_ref[near, h]

    CMAX, RMAX, RESC = 0, 1, 2

    def stage_logits(j, near):
        mb = sc_ref[pl.ds(chunk_start(j), blk), :]
        cm8 = []
        for h in range(N_HEADS):
            lg = logits(j, h, mb, near)
            lg_ref[h] = lg
            cm8.append(jnp.max(as_groups(lg), axis=0))
        for h in range(N_HEADS):
            st_ref[CMAX, h] = lanes8(colmax(cm8[h]))

    def stage_exp():
        for h in range(N_HEADS):
            m_old = st_ref[RMAX, h]
            m_new = jnp.maximum(m_old, st_ref[CMAX, h])
            m_use = jnp.where(m_new == -inf, 0.0, m_new)
            p = jnp.exp2(as_groups(lg_ref[h]) - m_use[None])
            p_ref[h] = p.reshape(blk, blk).astype(BF16)
            st_ref[RMAX, h] = m_new
            st_ref[RESC, h] = jnp.exp2(m_old - m_use)

    def stage_pv(j):
        for h in range(N_HEADS):
            rows = slice(h * HEAD_ROWS, (h + 1) * HEAD_ROWS)
            pv = jnp.dot(vt_ref[0, j, rows, :], p_ref[h], preferred_element_type=F32)
            acc = acc_ref[rows, :].reshape(HEAD_ROWS // SUBLANES, SUBLANES, blk) * st_ref[RESC, h][None]
            acc_ref[rows, :] = acc.reshape(HEAD_ROWS, blk) + pv

    def pipelined(j, near_next):
        stage_exp()
        stage_logits(j + 1, near_next)
        stage_pv(j)

    def exact_attention():
        acc_ref[...] = jnp.zeros(acc_ref.shape, F32)
        st_ref[RMAX] = jnp.full((N_HEADS, SUBLANES, blk), -inf, F32)

        def last_chunks(first):
            for j in range(first, 0):
                pipelined(i + j, -(j + 1))
            stage_exp()
            stage_pv(i)

        @pl.when(i == 0)
        def _():
            stage_logits(0, 0)
            last_chunks(0)

        @pl.when(i == 1)
        def _():
            stage_logits(0, 1)
            last_chunks(-1)

        @pl.when(i >= 2)
        def _():
            stage_logits(0, None)

        def far_body(j, _):
            pipelined(j, None)
            return 0

        lax.fori_loop(0, jnp.maximum(i - 2, 0), far_body, 0)

        @pl.when(i >= 2)
        def _():
            last_chunks(-2)

    acc_ref[...] = jnp.zeros(acc_ref.shape, F32)
    st_ref[RMAX] = jnp.zeros((N_HEADS, SUBLANES, blk), F32)

    def fast_run(chunks):
        for c, (j, near) in enumerate(chunks):
            mb = sc_ref[pl.ds(chunk_start(j), blk), :]
            for h in range(N_HEADS):
                p = jnp.exp2(as_groups(logits(j, h, mb, near)) - st_ref[RMAX, h][None])
                pp_ref[c, h] = p.reshape(blk, blk).astype(BF16)
        for h in range(N_HEADS):
            rows = slice(h * HEAD_ROWS, (h + 1) * HEAD_ROWS)
            tot = acc_ref[rows, :]
            for c, (j, _) in enumerate(chunks):
                tot = tot + jnp.dot(vt_ref[0, j, rows, :], pp_ref[c, h], preferred_element_type=F32)
            den = tot[HEAD_DIM:HEAD_DIM + 1, :]
            up = lanes8(jnp.where(den > 0.0, jnp.floor(jnp.log2(den)), 0.0))
            scaled = tot.reshape(HEAD_ROWS // SUBLANES, SUBLANES, blk) * jnp.exp2(-up)[None]
            acc_ref[rows, :] = scaled.reshape(HEAD_ROWS, blk)
            st_ref[RMAX, h] = st_ref[RMAX, h] + up

    def far_run(jr, _):
        fast_run([(FAST_RUN * jr + c, None) for c in range(FAST_RUN)])
        return 0

    n_far = jnp.maximum(i - 1, 0)
    lax.fori_loop(0, n_far // FAST_RUN, far_run, 0)

    @pl.when(i == 0)
    def _():
        fast_run([(0, 0)])

    for left in range(FAST_RUN):
        @pl.when(jnp.logical_and(i >= 1, n_far % FAST_RUN == left))
        def _():
            last = [(i - 1 - left + c, None) for c in range(left)] + [(i - 1, 1), (i, 0)]
            for first in range(0, len(last), FAST_RUN):
                fast_run(last[first:first + FAST_RUN])

    acc = acc_ref[...]
    finite = jnp.min(jnp.where(jnp.isfinite(acc), 1.0, 0.0))
    dens = jnp.concatenate([acc[h * HEAD_ROWS + HEAD_DIM:h * HEAD_ROWS + HEAD_DIM + 1, :] for h in range(N_HEADS)], axis=0)
    usable = jnp.logical_and(finite > 0.5, jnp.min(dens) > 0.0)
    lax.cond(usable, lambda: None, exact_attention)

    heads = []
    for h in range(N_HEADS):
        r0 = h * HEAD_ROWS
        heads.append(acc_ref[r0:r0 + HEAD_DIM, :] / acc_ref[r0 + HEAD_DIM:r0 + HEAD_DIM + 1, :])
    o_ref[0] = jnp.concatenate(heads, axis=0).T.astype(BF16)


def _attention(q, qi, wit, k, vt, kia, kib, btiles, *, bsz, seq, topk):
    blk = ATT_BLOCK
    nblk = seq // blk
    return pl.pallas_call(
        functools.partial(_attn_kernel, topk=topk, seq=seq),
        grid=(bsz, nblk),
        in_specs=[
            pl.BlockSpec((1, blk, ATTN_DIM), lambda b, i: (b, i, 0)),
            pl.BlockSpec((1, blk, IDX_HEADS * IDX_DIM), lambda b, i: (b, i, 0)),
            pl.BlockSpec((1, IDX_HEADS, blk), lambda b, i: (b, 0, i)),
            pl.BlockSpec((1, seq, ATTN_DIM), lambda b, i: (b, 0, 0)),
            pl.BlockSpec((1, nblk, N_HEADS * HEAD_ROWS, blk), lambda b, i: (b, 0, 0, 0)),
            pl.BlockSpec((1, seq, LANES), lambda b, i: (b, 0, 0)),
            pl.BlockSpec((1, seq, LANES), lambda b, i: (b, 0, 0)),
            pl.BlockSpec(btiles.shape, lambda b, i: (0, 0, 0, 0)),
        ],
        out_specs=pl.BlockSpec((1, blk, ATTN_DIM), lambda b, i: (b, i, 0)),
        out_shape=jax.ShapeDtypeStruct((bsz, seq, ATTN_DIM), BF16),
        scratch_shapes=[
            pltpu.VMEM((seq + blk, blk), F32),
            pltpu.VMEM((nblk, SUBLANES, blk), F32),
            pltpu.SMEM((nblk,), jnp.int32),
            pltpu.VMEM((N_HEADS, blk, LANES), BF16),
            pltpu.VMEM((N_HEADS, blk, blk), F32),
            pltpu.VMEM((N_HEADS, blk, blk), BF16),
            pltpu.VMEM((FAST_RUN, N_HEADS, blk, blk), BF16),
            pltpu.VMEM((3, N_HEADS, SUBLANES, blk), F32),
            pltpu.VMEM((N_HEADS * HEAD_ROWS, blk), F32),
        ],
        compiler_params=pltpu.CompilerParams(
            dimension_semantics=("arbitrary", "arbitrary"),
            vmem_limit_bytes=_vmem_limit(
                [((blk, ATTN_DIM), BF16), ((blk, IDX_HEADS * IDX_DIM), BF16), ((IDX_HEADS, blk), F32),
                 ((seq, ATTN_DIM), BF16), ((nblk, N_HEADS * HEAD_ROWS, blk), BF16), ((seq, LANES), BF16),
                 ((seq, LANES), BF16), (btiles.shape, F32), ((blk, ATTN_DIM), BF16)],
                [((seq + blk, blk), F32), ((nblk, SUBLANES, blk), F32), ((N_HEADS, blk, LANES), BF16),
                 ((N_HEADS, blk, blk), F32), ((N_HEADS, blk, blk), BF16), ((FAST_RUN, N_HEADS, blk, blk), BF16),
                 ((3, N_HEADS, SUBLANES, blk), F32),
                 ((N_HEADS * HEAD_ROWS, blk), F32)],
                [((blk, blk), F32)] * 4 + [((blk, blk), BF16), ((ATTN_DIM, blk), F32), ((blk, ATTN_DIM), F32)],
                stays_in_hbm=((bsz, seq, ATTN_DIM), BF16))),
        name="attention",
    )(q, qi, wit, k, vt, kia, kib, btiles)


def _mix_kernel(x_ref, mod_ref, att_ref, wpg_ref, wpool_ref, ps_ref, wa_ref, wb_ref, wo_ref,
                lng_ref, lnb_ref, o_ref, pe_ref, mix_ref, *, alpha, mod_row, ln_row, tiles_per_seq):
    rows = x_ref.shape[0]
    i = pl.program_id(0)
    seq_tile = i % tiles_per_seq

    @pl.when(i == 0)
    def _():
        pe_ref[rows:, :] = jnp.zeros((POOL_HALO, POOL_DIM), F32)

    x = x_ref[...]
    sh = mod_ref[0, mod_row:mod_row + 1, :]
    sc = mod_ref[0, mod_row + 1:mod_row + 2, :]
    gate = mod_ref[0, mod_row + 2:mod_row + 3, :]
    u = (x * (1.0 + sc) + sh).astype(BF16)
    pg = jnp.dot(u, wpg_ref[...], preferred_element_type=F32)
    pe_ref[0:POOL_HALO, :] = jnp.where(seq_tile == 0, 0.0, pe_ref[rows:, :])
    pe_ref[POOL_HALO:, :] = pg[:, :POOL_DIM]
    t = seq_tile * rows + lax.broadcasted_iota(jnp.int32, (rows, 1), 0)
    for g, w in enumerate(POOL_WINDOWS):
        cols = slice(g * POOL_GROUP_DIM, (g + 1) * POOL_GROUP_DIM)
        cur = pe_ref[POOL_HALO:, cols]
        win = cur
        for back in range(1, w):
            win = win + pe_ref[POOL_HALO - back:POOL_HALO - back + rows, cols]
        cnt = jnp.minimum(t + 1, w).astype(F32)
        pooled = (win / cnt - cur).astype(BF16)
        mixed = jnp.dot(pooled, wpool_ref[g], preferred_element_type=F32)
        mix_ref[:, cols] = (mixed * ps_ref[:, cols]).astype(BF16)
    y_a = jnp.dot(mix_ref[...], wa_ref[...], preferred_element_type=F32)
    y_b = jnp.dot(att_ref[...], wb_ref[...], preferred_element_type=F32)
    d = x.shape[1]
    ga = pg[:, POOL_DIM:POOL_DIM + d]
    gb = pg[:, POOL_DIM + d:]
    merged = (jax.nn.sigmoid(ga) * y_a + jax.nn.sigmoid(gb) * y_b).astype(BF16)
    y = jnp.dot(merged, wo_ref[...], preferred_element_type=F32)
    z = alpha * x + gate * y
    o_ref[...] = _layer_norm(z, lng_ref[ln_row:ln_row + 1, :], lnb_ref[ln_row:ln_row + 1, :])


def _mix_out(x2d, mod, att2d, wpg, wpool, pool_scale, wa, wb, wo, ln_g, ln_b, *, seq, alpha, mod_row, ln_row):
    n, d = x2d.shape
    tiles_per_seq = seq // MIX_ROWS
    resident = dict(pipeline_mode=pl.Buffered(1))
    full = lambda a: pl.BlockSpec(a.shape, lambda i: (0,) * a.ndim, **resident)
    return pl.pallas_call(
        functools.partial(_mix_kernel, alpha=alpha, mod_row=mod_row, ln_row=ln_row, tiles_per_seq=tiles_per_seq),
        grid=(n // MIX_ROWS,),
        in_specs=[
            pl.BlockSpec((MIX_ROWS, d), lambda i: (i, 0)),
            pl.BlockSpec((1, N_ADA, d), lambda i: (i // tiles_per_seq, 0, 0)),
            pl.BlockSpec((MIX_ROWS, ATTN_DIM), lambda i: (i, 0)),
            full(wpg), full(wpool), full(pool_scale), full(wa), full(wb), full(wo),
            pl.BlockSpec(ln_g.shape, lambda i: (0, 0)),
            pl.BlockSpec(ln_b.shape, lambda i: (0, 0)),
        ],
        out_specs=pl.BlockSpec((MIX_ROWS, d), lambda i: (i, 0)),
        out_shape=jax.ShapeDtypeStruct((n, d), F32),
        scratch_shapes=[
            pltpu.VMEM((POOL_HALO + MIX_ROWS, POOL_DIM), F32),
            pltpu.VMEM((MIX_ROWS, POOL_DIM), BF16),
        ],
        compiler_params=pltpu.CompilerParams(
            dimension_semantics=("arbitrary",),
            vmem_limit_bytes=_vmem_limit(
                [((MIX_ROWS, d), F32), ((N_ADA, d), F32), ((MIX_ROWS, ATTN_DIM), BF16), (ln_g.shape, F32),
                 (ln_b.shape, F32), ((MIX_ROWS, d), F32)],
                [(wpg.shape, BF16), (wpool.shape, BF16), (pool_scale.shape, F32), (wa.shape, BF16), (wb.shape, BF16),
                 (wo.shape, BF16), ((POOL_HALO + MIX_ROWS, POOL_DIM), F32), ((MIX_ROWS, POOL_DIM), BF16)],
                [((MIX_ROWS, d), BF16), ((MIX_ROWS, wpg.shape[1]), F32)] + [((MIX_ROWS, d), F32)] * 5)),
        name="mix_out",
    )(x2d, mod, att2d, wpg, wpool, pool_scale, wa, wb, wo, ln_g, ln_b)


def kernel(x, c, w_ada, b_ada, ln_g, ln_b, ffn1_w_gate, ffn1_w_up, ffn1_w_down, w_in, w_pool, pool_scale,
           w_a, w_b, w_out, rel_bias, ffn2_w_gate, ffn2_w_up, ffn2_w_down):
    bsz, seq, d = x.shape
    depth = w_ada.shape[0]
    alpha = (2.0 * depth) ** 0.25
    topk = min(TOP_K, seq // 4)
    assert seq % FFN_ROWS == 0 and seq % PROJ_ROWS == 0 and seq % MIX_ROWS == 0 and seq % ATT_BLOCK == 0
    assert PROJ_ROWS % ATT_BLOCK == 0 and POOL_HALO >= max(POOL_WINDOWS) - 1
    far_bucket = _far_bucket(ATT_BLOCK + 1, max(seq - 1, ATT_BLOCK + 1))

    o_q = POOL_DIM
    o_k = o_q + ATTN_DIM
    o_v = o_k + ATTN_DIM
    o_qi = o_v + ATTN_DIM
    o_ki = o_qi + IDX_HEADS * IDX_DIM
    o_wi = o_ki + IDX_DIM
    o_ga = o_wi + IDX_HEADS

    btiles = _rel_bias_tiles(rel_bias, far_bucket)
    x2d = x.reshape(bsz * seq, d)
    for l in range(depth):
        wl = w_in[l]
        zeros_ki = jnp.zeros((d, LANES - IDX_DIM), wl.dtype)
        w_ki = wl[:, o_ki:o_wi]
        wqk = jnp.concatenate([wl[:, o_q:o_v], wl[:, o_qi:o_ki]], axis=1).astype(BF16)
        wvw_t = jnp.pad(jnp.concatenate([wl[:, o_v:o_qi], wl[:, o_wi:o_ga]], axis=1),
                        ((0, 0), (0, LANES - IDX_HEADS))).astype(BF16)
        wki = jnp.concatenate([w_ki, zeros_ki, zeros_ki, w_ki], axis=1).astype(BF16)
        wpg = jnp.concatenate([wl[:, :POOL_DIM], wl[:, o_ga:]], axis=1).astype(BF16)

        mod = _ada_mod(c, w_ada[l], b_ada[l]).reshape(bsz, N_ADA, d)
        x2d = _ffn(x2d, mod, ln_g[l], ln_b[l], ffn1_w_gate[l].astype(BF16), ffn1_w_up[l].astype(BF16),
                   ffn1_w_down[l].astype(BF16), seq=seq, alpha=alpha, mod_row=0, ln_row=0)
        q, k, qi, vt, kia, kib, wit = _attn_proj(x2d, mod, wqk, wvw_t, wki, bsz=bsz, seq=seq, mod_row=3)
        att = _attention(q.reshape(bsz, seq, ATTN_DIM), qi.reshape(bsz, seq, IDX_HEADS * IDX_DIM), wit,
                         k.reshape(bsz, seq, ATTN_DIM), vt, kia.reshape(bsz, seq, LANES),
                         kib.reshape(bsz, seq, LANES), btiles, bsz=bsz, seq=seq, topk=topk)
        x2d = _mix_out(x2d, mod, att.reshape(bsz * seq, ATTN_DIM), wpg, w_pool[l].astype(BF16),
                       pool_scale[l].reshape(1, POOL_DIM), w_a[l].astype(BF16), w_b[l].astype(BF16),
                       w_out[l].astype(BF16), ln_g[l], ln_b[l], seq=seq, alpha=alpha, mod_row=3, ln_row=1)
        x2d = _ffn(x2d, mod, ln_g[l], ln_b[l], ffn2_w_gate[l].astype(BF16), ffn2_w_up[l].astype(BF16),
                   ffn2_w_down[l].astype(BF16), seq=seq, alpha=alpha, mod_row=6, ln_row=2)
    return x2d.reshape(bsz, seq, d)
```
